```python
import math
import jax
import jax.numpy as jnp
from jax import lax
import numpy as np

D_MODEL = 1024
BATCH = 8
SEQ = 2048
DEPTH = 2
DEC_BATCH = 128
DEC_SEQ = 4
PAST_LEN = 16384
PAGE_SIZE = 128

HEAD_DIM = 64
N_MIXERS = 4
HEADS = D_MODEL // (N_MIXERS * HEAD_DIM)
RET_DK = HEAD_DIM
RET_DV = HEAD_DIM
GLA_DK = HEAD_DIM // 2
GLA_DV = HEAD_DIM
GLA_GATE_RANK = 16
GLA_GATE_NORM = 16.0
HG_DK = HEAD_DIM
HG_DV = HEAD_DIM
GDN_DK = HEAD_DIM
GDN_DV = HEAD_DIM
CONV_W = 4
GDN_CONV_CH = HEADS * (2 * GDN_DK + GDN_DV)
MIX_WIDTH = HEADS * (RET_DV + GLA_DV + HG_DV + GDN_DV)
D_FF = ((8 * D_MODEL // 3 + 127) // 128) * 128
CHUNK = 64
ROPE_BASE = 10000.0
LN_EPS = 1e-5
RMS_EPS = 1e-6
ALPHA = (2.0 * DEPTH) ** 0.25
DEEPNORM_BETA = (8.0 * DEPTH) ** -0.25
N_MOD = 9
MIX_COLS = (
    HEADS * RET_DK, HEADS * RET_DK, HEADS * RET_DV, HEADS * RET_DV,
    HEADS * GLA_DK, HEADS * GLA_DK, HEADS * GLA_DV, GLA_GATE_RANK, HEADS * GLA_DV,
    HEADS * HG_DK, HEADS * HG_DK, HEADS * HG_DV, HEADS * HG_DV,
    GDN_CONV_CH, HEADS, HEADS, HEADS * GDN_DV,
)
IN_COLS = sum(MIX_COLS)

kernel_name = 'hybrid_ret_gla_hgrn2_gdn_decode_step'


def _layer_norm(x, g, b):
    xf = x.astype(jnp.float32)
    mu = jnp.mean(xf, axis=-1, keepdims=True)
    var = jnp.mean(jnp.square(xf - mu), axis=-1, keepdims=True)
    y = (xf - mu) * lax.rsqrt(var + LN_EPS)
    return (y * g.astype(jnp.float32) + b.astype(jnp.float32)).astype(x.dtype)


def _rms(x, g=None):
    xf = x.astype(jnp.float32)
    y = xf * lax.rsqrt(jnp.mean(jnp.square(xf), axis=-1, keepdims=True) + RMS_EPS)
    return y if g is None else y * g.astype(jnp.float32)


def _l2norm(x):
    xf = x.astype(jnp.float32)
    return xf * lax.rsqrt(jnp.sum(jnp.square(xf), axis=-1, keepdims=True) + RMS_EPS)


def _rotary(x, pos):
    half = x.shape[-1] // 2
    inv = ROPE_BASE ** (-jnp.arange(half, dtype=jnp.float32) / half)
    ang = pos[:, None] * inv[None, :]
    cos = jnp.cos(ang)[None, :, None, :]
    sin = jnp.sin(ang)[None, :, None, :]
    x1, x2 = x[..., :half], x[..., half:]
    return jnp.concatenate([x1 * cos - x2 * sin, x1 * sin + x2 * cos], axis=-1)


def _swiglu(h, wi, wo):
    a, b = jnp.split(h @ wi, 2, axis=-1)
    return (jax.nn.silu(a) * b) @ wo


def _split_cols(t):
    out, start = [], 0
    for w in MIX_COLS:
        out.append(t[..., start:start + w])
        start += w
    return out


def _chunking(T):
    c = min(CHUNK, T)
    return c, -(-T // c) * c


def _prep(t, tp, c):
    t = jnp.swapaxes(t, 1, 2).astype(jnp.float32)
    b, h, T, d = t.shape
    t = jnp.pad(t, ((0, 0), (0, 0), (0, tp - T), (0, 0)))
    return jnp.moveaxis(t.reshape(b, h, tp // c, c, d), 2, 0)


def _unchunk(o, T):
    nc, b, h, c, d = o.shape
    o = jnp.moveaxis(o, 0, 2).reshape(b, h, nc * c, d)[:, :, :T]
    return jnp.swapaxes(o, 1, 2)


def _gla_scan(q, k, v, g, s0):
    T = q.shape[1]
    c, tp = _chunking(T)
    xs = tuple(_prep(t, tp, c) for t in (q, k, v, g))
    causal = jnp.tril(jnp.ones((c, c), dtype=bool))[:, :, None]

    def step(S, inp):
        qc, kc, vc, gc = inp
        G = jnp.cumsum(gc, axis=2)
        diff = G[:, :, :, None, :] - G[:, :, None, :, :]
        decay = jnp.where(causal, jnp.exp(jnp.where(causal, diff, 0.0)), 0.0)
        att = jnp.einsum('bhtd,bhsd,bhtsd->bhts', qc, kc, decay)
        o = att @ vc + (qc * jnp.exp(G)) @ S
        gl = G[:, :, -1:, :]
        S = jnp.exp(gl[:, :, 0, :, None]) * S + jnp.einsum('bhsd,bhsv->bhdv', kc * jnp.exp(gl - G), vc)
        return S, o

    s, o = lax.scan(step, s0.astype(jnp.float32), xs)
    return _unchunk(o, T), s.astype(s0.dtype)


def _gdn_scan(q, k, v, beta, g, s0):
    T = q.shape[1]
    c, tp = _chunking(T)
    xs = tuple(_prep(t, tp, c) for t in (q, k, v, beta[..., None], g[..., None]))
    incl = jnp.tril(jnp.ones((c, c), dtype=bool))
    strict = jnp.tril(jnp.ones((c, c), dtype=bool), -1)
    eye = jnp.eye(c, dtype=jnp.float32)

    def step(S, inp):
        qc, kc, vc, bc, gc = inp
        G = jnp.cumsum(gc, axis=2)
        diff = G - jnp.swapaxes(G, -1, -2)
        L = jnp.where(incl, jnp.exp(jnp.where(incl, diff, 0.0)), 0.0)
        a = jnp.where(strict, bc * (kc @ jnp.swapaxes(kc, -1, -2)) * L, 0.0)
        tm = eye + a
        u = lax.linalg.triangular_solve(tm, bc * vc, left_side=True, lower=True, unit_diagonal=True)
        w = lax.linalg.triangular_solve(tm, bc * jnp.exp(G) * kc, left_side=True, lower=True, unit_diagonal=True)
        delta = u - w @ S
        o = ((qc @ jnp.swapaxes(kc, -1, -2)) * L) @ delta + (qc * jnp.exp(G)) @ S
        gl = G[:, :, -1:, :]
        S = jnp.exp(gl) * S + jnp.swapaxes(kc * jnp.exp(gl - G), -1, -2) @ delta
        return S, o

    s, o = lax.scan(step, s0.astype(jnp.float32), xs)
    return _unchunk(o, T), s.astype(s0.dtype)


def _causal_conv(u, buf, w):
    T = u.shape[1]
    up = jnp.concatenate([buf.astype(u.dtype), u], axis=1)
    out = sum(up[:, i:i + T] * w[i] for i in range(CONV_W))
    return out, up[:, -(CONV_W - 1):].astype(buf.dtype)


def _token_mixing(h, pos, lb_l, st, l, p):
    B, T, _ = h.shape
    f32 = jnp.float32
    ret_s, gla_s, hg_s, gdn_s, conv_s = st
    (rq, rk, rv, rg, aq, ak, av, alr, ag, hq, hf, hi, hg, dqkv, db, da, dg) = _split_cols(h @ p['w_in'][l])
    heads = lambda t: t.reshape(B, T, HEADS, -1)

    q = _rotary(heads(rq).astype(f32), pos)
    k = _rotary(heads(rk).astype(f32), pos) * RET_DK ** -0.5
    ret_decay = jnp.log(1.0 - 2.0 ** (-5.0 - jnp.arange(HEADS, dtype=f32)))
    g = jnp.broadcast_to(ret_decay[:, None], (B, T, HEADS, RET_DK))
    o_ret, ret_s = _gla_scan(q, k, heads(rv), g, ret_s)
    o_ret = _rms(o_ret) * jax.nn.silu(heads(rg).astype(f32))

    gk = jax.nn.log_sigmoid((alr @ p['gla_wg'][l] + p['gla_bg'][l]).astype(f32)) / GLA_GATE_NORM
    o_gla, gla_s = _gla_scan(heads(aq).astype(f32) * GLA_DK ** -0.5, heads(ak), heads(av), heads(gk), gla_s)
    o_gla = _rms(o_gla, p['gla_norm'][l]) * jax.nn.silu(heads(ag).astype(f32))

    zf = heads(hf.astype(f32))
    lbh = lb_l.reshape(HEADS, HG_DK)
    log_f = jnp.log(lbh + (1.0 - lbh) * jax.nn.sigmoid(zf))
    k_h = (1.0 - lbh) * jax.nn.sigmoid(-zf)
    q_h = jax.nn.silu(heads(hq).astype(f32)) * HG_DK ** -0.5
    o_hg, hg_s = _gla_scan(q_h, k_h, heads(hi), log_f, hg_s)
    o_hg = _rms(o_hg, p['hg_norm'][l]) * jax.nn.silu(heads(hg).astype(f32))

    u, conv_s = _causal_conv(dqkv, conv_s, p['gdn_conv'][l])
    u = jax.nn.silu(u)
    uq, uk, uv = jnp.split(u, [HEADS * GDN_DK, 2 * HEADS * GDN_DK], axis=-1)
    q_d = _l2norm(heads(uq)) * GDN_DK ** -0.5
    k_d = _l2norm(heads(uk))
    beta = jax.nn.sigmoid(db.astype(f32))
    g_d = -jnp.exp(p['gdn_a_log'][l].astype(f32)) * jax.nn.softplus(da.astype(f32) + p['gdn_dt_bias'][l].astype(f32))
    o_gdn, gdn_s = _gdn_scan(q_d, k_d, heads(uv), beta, g_d, gdn_s)
    o_gdn = _rms(o_gdn, p['gdn_norm'][l]) * jax.nn.silu(heads(dg).astype(f32))

    o = jnp.concatenate([t.reshape(B, T, -1) for t in (o_ret, o_gla, o_hg, o_gdn)], axis=-1).astype(h.dtype)
    return o @ p['w_out'][l], (ret_s, gla_s, hg_s, gdn_s, conv_s)


def _layer(x, c, pos, l, st, p, lb):
    mod = jax.nn.silu(c) @ p['ada_w'][l] + p['ada_b'][l]
    sh1, sc1, gt1, sh2, sc2, gt2, sh3, sc3, gt3 = jnp.split(mod[:, None, :], N_MOD, axis=-1)
    h = x * (1.0 + sc1) + sh1
    x = _layer_norm(ALPHA * x + 0.5 * (1.0 + gt1) * _swiglu(h, p['ffn1_wi'][l], p['ffn1_wo'][l]),
                    p['ln_g'][l, 0], p['ln_b'][l, 0])
    h = x * (1.0 + sc2) + sh2
    m, new_st = _token_mixing(h, pos, lb[l], st, l, p)
    x = _layer_norm(ALPHA * x + (1.0 + gt2) * m, p['ln_g'][l, 1], p['ln_b'][l, 1])
    h = x * (1.0 + sc3) + sh3
    x = _layer_norm(ALPHA * x + 0.5 * (1.0 + gt3) * _swiglu(h, p['ffn2_wi'][l], p['ffn2_wo'][l]),
                    p['ln_g'][l, 2], p['ln_b'][l, 2])
    return x, new_st


def _trunk(x, c, pos, states, p, lb):
    new = []
    for l in range(DEPTH):
        x, st = _layer(x, c, pos, l, tuple(s[l] for s in states), p, lb)
        new.append(st)
    return x, tuple(jnp.stack([n[i] for n in new]) for i in range(len(states)))


def setup_inputs(seed: int = 0) -> dict:
    key = jax.random.key(seed)
    ks = jax.random.split(key, 32)
    f32 = jnp.float32
    d = D_MODEL

    def nrm(k, shape, s):
        return jax.random.normal(k, shape, f32) * s

    u_a = jax.random.uniform(ks[26], (DEPTH, HEADS), f32, 1.0, 16.0)
    u_dt = jax.random.uniform(ks[27], (DEPTH, HEADS), f32)
    dt = jnp.exp(u_dt * (math.log(0.1) - math.log(0.001)) + math.log(0.001))
    return {
        'x_prompt': nrm(ks[0], (BATCH, SEQ, d), 1.0),
        'x_sample': nrm(ks[1], (DEC_BATCH, DEC_SEQ, d), 1.0),
        'state_ret': nrm(ks[2], (DEPTH, DEC_BATCH, HEADS, RET_DK, RET_DV), 0.5),
        'state_gla': nrm(ks[3], (DEPTH, DEC_BATCH, HEADS, GLA_DK, GLA_DV), 0.5),
        'state_hgrn': nrm(ks[4], (DEPTH, DEC_BATCH, HEADS, HG_DK, HG_DV), 0.5),
        'state_gdn': nrm(ks[5], (DEPTH, DEC_BATCH, HEADS, GDN_DK, GDN_DV), 0.5),
        'state_gdn_conv': nrm(ks[6], (DEPTH, DEC_BATCH, CONV_W - 1, GDN_CONV_CH), 1.0),
        'c_prompt': nrm(ks[7], (BATCH, d), 1.0),
        'c_sample': nrm(ks[8], (DEC_BATCH, d), 1.0),
        'ada_w': nrm(ks[9], (DEPTH, d, N_MOD * d), 0.1 * d ** -0.5),
        'ada_b': nrm(ks[10], (DEPTH, N_MOD * d), 0.02),
        'ln_g': 1.0 + nrm(ks[11], (DEPTH, 3, d), 0.02),
        'ln_b': nrm(ks[12], (DEPTH, 3, d), 0.02),
        'ffn1_wi': nrm(ks[13], (DEPTH, d, 2 * D_FF), d ** -0.5),
        'ffn1_wo': nrm(ks[14], (DEPTH, D_FF, d), D_FF ** -0.5 * DEEPNORM_BETA),
        'ffn2_wi': nrm(ks[15], (DEPTH, d, 2 * D_FF), d ** -0.5),
        'ffn2_wo': nrm(ks[16], (DEPTH, D_FF, d), D_FF ** -0.5 * DEEPNORM_BETA),
        'w_in': nrm(ks[17], (DEPTH, d, IN_COLS), d ** -0.5),
        'gla_wg': nrm(ks[18], (DEPTH, GLA_GATE_RANK, HEADS * GLA_DK), GLA_GATE_RANK ** -0.5),
        'gla_bg': nrm(ks[19], (DEPTH, HEADS * GLA_DK), 0.02),
        'hg_lb': nrm(ks[20], (DEPTH, HEADS * HG_DK), 0.5),
        'gdn_conv': nrm(ks[21], (DEPTH, CONV_W, GDN_CONV_CH), CONV_W ** -0.5),
        'gdn_a_log': jnp.log(u_a),
        'gdn_dt_bias': dt + jnp.log(-jnp.expm1(-dt)),
        'gla_norm': 1.0 + nrm(ks[22], (DEPTH, GLA_DV), 0.02),
        'hg_norm': 1.0 + nrm(ks[23], (DEPTH, HG_DV), 0.02),
        'gdn_norm': 1.0 + nrm(ks[24], (DEPTH, GDN_DV), 0.02),
        'w_out': nrm(ks[25], (DEPTH, MIX_WIDTH, d), MIX_WIDTH ** -0.5 * DEEPNORM_BETA),
    }


def reference(x_prompt, x_sample, state_ret, state_gla, state_hgrn, state_gdn, state_gdn_conv,
              c_prompt, c_sample, ada_w, ada_b, ln_g, ln_b, ffn1_wi, ffn1_wo, ffn2_wi, ffn2_wo,
              w_in, gla_wg, gla_bg, hg_lb, gdn_conv, gdn_a_log, gdn_dt_bias,
              gla_norm, hg_norm, gdn_norm, w_out):
    p = {'ada_w': ada_w, 'ada_b': ada_b, 'ln_g': ln_g, 'ln_b': ln_b,
         'ffn1_wi': ffn1_wi, 'ffn1_wo': ffn1_wo, 'ffn2_wi': ffn2_wi, 'ffn2_wo': ffn2_wo,
         'w_in': w_in, 'gla_wg': gla_wg, 'gla_bg': gla_bg, 'gdn_conv': gdn_conv,
         'gdn_a_log': gdn_a_log, 'gdn_dt_bias': gdn_dt_bias, 'gla_norm': gla_norm,
         'hg_norm': hg_norm, 'gdn_norm': gdn_norm, 'w_out': w_out}
    plb = jax.nn.softmax(hg_lb.astype(jnp.float32), axis=0)
    lb = jnp.cumsum(plb, axis=0) - plb[0]

    sample_states = (state_ret, state_gla, state_hgrn, state_gdn, state_gdn_conv)
    bp = x_prompt.shape[0]
    prompt_states = tuple(jnp.zeros((DEPTH, bp) + s.shape[2:], x_prompt.dtype) for s in sample_states)
    pos_p = jnp.arange(x_prompt.shape[1], dtype=jnp.float32)
    pos_s = PAST_LEN + jnp.arange(x_sample.shape[1], dtype=jnp.float32)

    y_prompt, (p_ret, p_gla, p_hg, p_gdn, p_conv) = _trunk(x_prompt, c_prompt, pos_p, prompt_states, p, lb)
    y_sample, (s_ret, s_gla, s_hg, s_gdn, s_conv) = _trunk(x_sample, c_sample, pos_s, sample_states, p, lb)
    return (y_prompt, y_sample, p_ret, p_gla, p_hg, p_gdn, p_conv, s_ret, s_gla, s_hg, s_gdn, s_conv)
```

```python
import functools
import math

import numpy as np
import jax
import jax.numpy as jnp
from jax import lax
from jax.experimental import pallas as pl
from jax.experimental.pallas import tpu as pltpu

F32 = jnp.float32
BF16 = jnp.bfloat16

HEAD_DIM = 64
HEADS = 4
GLA_DK = 32
GLA_GATE_NORM = 16.0
CONV_W = 4
CHUNK = 64
ROPE_BASE = 10000.0
LN_EPS = 1e-5
RMS_EPS = 1e-6
PAST_LEN = 16384
N_MOD = 9
HW = HEADS * HEAD_DIM

ROW_TILE = 512
PREP_TILE = 256
DEC_T = 8
DEC_NB = 8
FF_CHUNK = 512
V7X_VMEM_BYTES = 64 * 1024 * 1024
VMEM_LIMIT = V7X_VMEM_BYTES - 12 * 1024 * 1024

_SLABS = (("rq", 256), ("rk", 256), ("rv", 256), ("rg", 256),
          ("aq", 128), ("ak", 128), ("av", 256), ("ag", 256),
          ("hq", 256), ("hf", 256), ("hi", 256), ("hg", 256),
          ("dqkv", 768), ("dg", 256), ("small", 128))
_OFF = {}
_o = 0
for _n, _w in _SLABS:
    _OFF[_n] = (_o, _w)
    _o += _w
PROJ_W = _o
_SRC = {"rq": 0, "rk": 256, "rv": 512, "rg": 768, "aq": 1024, "ak": 1152, "av": 1280, "alr": 1536,
        "ag": 1552, "hq": 1808, "hf": 2064, "hi": 2320, "hg": 2576, "dqkv": 2832, "db": 3600,
        "da": 3604, "dg": 3608}
SMALL_ALR, SMALL_DB, SMALL_DA = 0, 16, 20


def _bf(x):
    return x.astype(BF16)


def _dot(a, b):
    return jnp.dot(a, b, preferred_element_type=F32)


def _dot_nt(a, b):
    return lax.dot_general(a, b, (((1,), (1,)), ((), ())), preferred_element_type=F32)


def _dot_tn(a, b):
    return lax.dot_general(a, b, (((0,), (0,)), ((), ())), preferred_element_type=F32)


def _split2(x):
    hi = x.astype(BF16)
    lo = (x - hi.astype(F32)).astype(BF16)
    return hi, lo


def _split3(x):
    hi = x.astype(BF16)
    r = x - hi.astype(F32)
    mid = r.astype(BF16)
    lo = (r - mid.astype(F32)).astype(BF16)
    return hi, mid, lo


def _dot_x01(x, m01):
    hi, lo = _split2(x)
    return _dot(hi, m01) + _dot(lo, m01)


def _dot3(x, y):
    xh, xl = _split2(x)
    yh, yl = _split2(y)
    return _dot(xh, yh) + _dot(xl, yh) + _dot(xh, yl)


def _silu(x):
    return x * jax.nn.sigmoid(x)


def _softplus(x):
    return jnp.maximum(x, 0.0) + jnp.log1p(jnp.exp(-jnp.abs(x)))


def _ln(y, g, b):
    mu = jnp.mean(y, axis=-1, keepdims=True)
    yc = y - mu
    var = jnp.mean(yc * yc, axis=-1, keepdims=True)
    return yc * lax.rsqrt(var + LN_EPS) * g + b


def _mod_rows(ref, tm):
    v = ref[...]
    v = v.reshape(v.shape[-2], v.shape[-1])
    nb, d = v.shape
    if nb == 1:
        return v
    return jnp.broadcast_to(v[:, None, :], (nb, tm // nb, d)).reshape(tm, d)


def _tile4(x):
    return jnp.concatenate([x, x, x, x], axis=0)


def _cumsum_rows(g):
    c = g.shape[0]
    row = lax.broadcasted_iota(jnp.int32, g.shape, 0)
    out = g
    sh = 1
    while sh < c:
        out = out + jnp.where(row >= sh, pltpu.roll(out, sh, 0), 0.0)
        sh *= 2
    return out


def _gref(G, m):
    c, w = G.shape
    if 2 * m >= 8:
        g3 = G.reshape(c // (2 * m), 2 * m, w)
        return jnp.broadcast_to(g3[:, m - 1:m, :], g3.shape).reshape(c, w)
    p = lax.broadcasted_iota(jnp.int32, G.shape, 0) % (2 * m)
    if m == 2:
        return jnp.where(p == 0, pltpu.roll(G, c - 1, 0),
                         jnp.where(p == 1, G, jnp.where(p == 2, pltpu.roll(G, 1, 0), pltpu.roll(G, 2, 0))))
    return jnp.where(p == 0, G, pltpu.roll(G, 1, 0))


def _decay_att(q, k, G, kmask, lvl_ref, levels, qmask=None):
    def rhs(x):
        return _bf(_tile4(x) * kmask) if qmask is None else _bf(_tile4(x))

    def lhs(x):
        return _bf(x) if qmask is None else _bf(_tile4(x) * qmask)

    att = _dot_nt(lhs(q), rhs(k)) * lvl_ref[len(levels)]
    for i, m in enumerate(levels):
        gr = _gref(G, m)
        ql = q * jnp.exp(jnp.minimum(G - gr, 0.0))
        kl = k * jnp.exp(jnp.minimum(gr - G, 0.0))
        att = att + _dot_nt(lhs(ql), rhs(kl)) * lvl_ref[i]
    return att


def _ret_gamma():
    return np.log(1.0 - 2.0 ** (-5.0 - np.arange(HEADS, dtype=np.float64)))


@functools.lru_cache(maxsize=None)
def _chunk_consts(dk):
    c = CHUNK
    h_att = np.arange(4 * c) // c
    s_att = np.arange(4 * c) % c
    t = np.arange(c)
    kmask = (h_att[:, None] == (np.arange(4 * dk) // dk)[None, :]).astype(np.float32)
    vmask = (h_att[:, None] == (np.arange(HW) // HEAD_DIM)[None, :]).astype(np.float32)
    smask = ((np.arange(HW) // HEAD_DIM)[:, None] == (np.arange(4 * dk) // dk)[None, :]).astype(np.float32)
    levels = (32, 16, 8, 4, 2, 1)
    lv = []
    for m in levels:
        same = (t[:, None] // (2 * m)) == (s_att[None, :] // (2 * m))
        lv.append(same & ((t[:, None] % (2 * m)) >= m) & ((s_att[None, :] % (2 * m)) < m))
    lv.append(t[:, None] == s_att[None, :])
    lvl = np.stack(lv).astype(np.float32)
    incl = (s_att[None, :] <= t[:, None]).astype(np.float32)
    strict = (s_att[None, :] < t[:, None]).astype(np.float32)
    eye = (s_att[None, :] == t[:, None]).astype(np.float32)
    fold = (np.arange(HW)[:, None] % HEAD_DIM == np.arange(HEAD_DIM)[None, :]).astype(np.float32)
    gam = _ret_gamma()
    gh = gam[h_att]
    ret_att = np.where(s_att[None, :] <= t[:, None], np.exp((t[:, None] - s_att[None, :]) * gh[None, :]), 0.0)
    gl = gam[np.arange(HW) // HEAD_DIM]
    ret_q = np.exp((t[:, None] + 1) * gl[None, :])
    ret_k = np.exp((c - 1 - t[:, None]) * gl[None, :])
    ret_s = np.exp(c * gl)[None, :]
    return dict(kmask=kmask, vmask=vmask, smask=smask, lvl=lvl, levels=levels, incl=incl, strict=strict,
                eye=eye, fold=fold, ret_att=ret_att.astype(np.float32), ret_q=ret_q.astype(np.float32),
                ret_k=ret_k.astype(np.float32), ret_s=ret_s.astype(np.float32))


@functools.lru_cache(maxsize=None)
def _dec_consts(dk, t_valid):
    n = HEADS * DEC_T
    hh = np.arange(n) // DEC_T
    tt = np.arange(n) % DEC_T
    qmask = (hh[:, None] == (np.arange(4 * dk) // dk)[None, :]).astype(np.float32)
    same_h = hh[:, None] == hh[None, :]
    levels = (2, 1)
    lv = []
    for m in levels:
        same = (tt[:, None] // (2 * m)) == (tt[None, :] // (2 * m))
        lv.append(same_h & same & ((tt[:, None] % (2 * m)) >= m) & ((tt[None, :] % (2 * m)) < m))
    lv.append(same_h & (tt[:, None] == tt[None, :]))
    lvl = np.stack(lv).astype(np.float32)
    incl = (same_h & (tt[None, :] <= tt[:, None])).astype(np.float32)
    strict = (same_h & (tt[None, :] < tt[:, None])).astype(np.float32)
    eye = np.eye(n, dtype=np.float32)
    rowvalid = (np.arange(DEC_T) < t_valid).astype(np.float32)[:, None]
    hmvalid = (tt < t_valid).astype(np.float32)[:, None]
    retg = (rowvalid * _ret_gamma()[np.arange(HW) // HEAD_DIM][None, :]).astype(np.float32)
    return dict(qmask=qmask, lvl=lvl, levels=levels, incl=incl, strict=strict, eye=eye,
                rowvalid=rowvalid, hmvalid=hmvalid, retg=retg)


def _bd_ones():
    h = np.arange(HW) // HEAD_DIM
    return (h[:, None] == h[None, :]).astype(np.float32)


def _expand_mat():
    e = np.zeros((128, 2 * HW), np.float32)
    for h in range(HEADS):
        e[SMALL_DB + h, h * HEAD_DIM:(h + 1) * HEAD_DIM] = 1.0
        e[SMALL_DA + h, HW + h * HEAD_DIM:HW + (h + 1) * HEAD_DIM] = 1.0
    return e


def _const_spec(a):
    nd = a.ndim
    return pl.BlockSpec(a.shape, lambda *_: (0,) * nd)


def _params(sem):
    return pltpu.CompilerParams(dimension_semantics=sem, vmem_limit_bytes=VMEM_LIMIT)


def _ada_kernel(c_ref, w_ref, b_ref, o_ref):
    c = c_ref[...]
    o_ref[0] = _dot(_bf(_silu(c)), _bf(w_ref[0])) + b_ref[0]


def _ada(c_all, ada_w, ada_b):
    nl, d, nd = ada_w.shape
    nb = c_all.shape[0]
    tn = d
    return pl.pallas_call(
        _ada_kernel,
        grid=(nl, nd // tn),
        in_specs=[pl.BlockSpec((nb, d), lambda l, j: (0, 0)),
                  pl.BlockSpec((1, d, tn), lambda l, j: (l, 0, j)),
                  pl.BlockSpec((1, 1, tn), lambda l, j: (l, 0, j))],
        out_specs=pl.BlockSpec((1, nb, tn), lambda l, j: (l, 0, j)),
        out_shape=jax.ShapeDtypeStruct((nl, nb, nd), F32),
        compiler_params=_params(("arbitrary", "arbitrary")),
        name="ada_mod",
    )(c_all, ada_w, ada_b.reshape(nl, 1, nd))


class _Rows:
    def __init__(self, batch, t, decode, tile=ROW_TILE):
        self.batch, self.t, self.decode = batch, t, decode
        self.rows = batch * t
        self.tm = min(tile, self.rows)
        assert self.rows % self.tm == 0
        if decode:
            assert self.tm % t == 0
        else:
            assert t % self.tm == 0
        self.steps = self.rows // self.tm


def _mod_spec(g, layer, k, d):
    if g.decode:
        nb = g.tm // g.t
        return pl.BlockSpec((1, 1, nb, d), lambda i: (layer, k, i, 0))
    per = g.t // g.tm
    return pl.BlockSpec((1, 1, 1, 1, d), lambda i: (layer, k, i // per, 0, 0))


def _ffn_kernel(x_ref, sh_ref, sc_ref, gt_ref, wi_ref, wo_ref, lng_ref, lnb_ref, o_ref, *, alpha, ln_idx, layer):
    x = x_ref[...]
    tm, d = x.shape
    dff = wo_ref.shape[1]
    sh, sc, gt = _mod_rows(sh_ref, tm), _mod_rows(sc_ref, tm), _mod_rows(gt_ref, tm)
    h = _bf(x * (1.0 + sc) + sh)
    acc = jnp.zeros((tm, d), F32)
    off = 0
    while off < dff:
        fc = min(FF_CHUNK, dff - off)
        a = _dot(h, wi_ref[0, :, off:off + fc])
        b = _dot(h, wi_ref[0, :, dff + off:dff + off + fc])
        acc = acc + _dot(_bf(_silu(a) * b), wo_ref[0, off:off + fc, :])
        off += fc
    y = alpha * x + 0.5 * (1.0 + gt) * acc
    o_ref[...] = _ln(y, lng_ref[layer, ln_idx:ln_idx + 1, :], lnb_ref[layer, ln_idx:ln_idx + 1, :])


def _ffn(g, x, mod, mods, wi, wo, ln_g, ln_b, layer, ln_idx, alpha):
    d = x.shape[1]
    dff = wo.shape[1]
    row = pl.BlockSpec((g.tm, d), lambda i: (i, 0))
    return pl.pallas_call(
        functools.partial(_ffn_kernel, alpha=alpha, ln_idx=ln_idx, layer=layer),
        grid=(g.steps,),
        in_specs=[row] + [_mod_spec(g, layer, k, d) for k in mods] + [
            pl.BlockSpec((1, d, 2 * dff), lambda i: (layer, 0, 0), pipeline_mode=pl.Buffered(1)),
            pl.BlockSpec((1, dff, d), lambda i: (layer, 0, 0), pipeline_mode=pl.Buffered(1)),
            _const_spec(ln_g), _const_spec(ln_b)],
        out_specs=row,
        out_shape=jax.ShapeDtypeStruct(x.shape, F32),
        compiler_params=_params(("arbitrary",)),
        name=f"ffn{ln_idx}",
    )(x, mod, mod, mod, wi, wo, ln_g, ln_b)


def _mix_out_kernel(x_ref, gt_ref, o0_ref, o1_ref, o2_ref, o3_ref, w_ref, lng_ref, lnb_ref, y_ref, *, alpha, layer,
                    decode):
    x = x_ref[...]
    tm, d = x.shape
    gt = _mod_rows(gt_ref, tm)
    acc = jnp.zeros((tm, d), F32)
    for mi, o_ref in enumerate((o0_ref, o1_ref, o2_ref, o3_ref)):
        if decode:
            o = o_ref[...]
            for h in range(HEADS):
                oh = o[:, h * DEC_T:(h + 1) * DEC_T, :].reshape(tm, HEAD_DIM)
                r0 = mi * HW + h * HEAD_DIM
                acc = acc + _dot(_bf(oh), w_ref[0, r0:r0 + HEAD_DIM, :])
        else:
            acc = acc + _dot(o_ref[...], w_ref[0, mi * HW:(mi + 1) * HW, :])
    y = alpha * x + (1.0 + gt) * acc
    y_ref[...] = _ln(y, lng_ref[layer, 1:2, :], lnb_ref[layer, 1:2, :])


def _mix_out(g, x, mod, outs, w_out, ln_g, ln_b, layer, alpha):
    d = x.shape[1]
    row = pl.BlockSpec((g.tm, d), lambda i: (i, 0))
    if g.decode:
        nb = g.tm // g.t
        ospec = pl.BlockSpec((nb, HEADS * DEC_T, HEAD_DIM), lambda i: (i, 0, 0))
    else:
        ospec = pl.BlockSpec((g.tm, HW), lambda i: (i, 0))
    return pl.pallas_call(
        functools.partial(_mix_out_kernel, alpha=alpha, layer=layer, decode=g.decode),
        grid=(g.steps,),
        in_specs=[row, _mod_spec(g, layer, 5, d), ospec, ospec, ospec, ospec,
                  pl.BlockSpec((1, w_out.shape[1], d), lambda i: (layer, 0, 0), pipeline_mode=pl.Buffered(1)),
                  _const_spec(ln_g), _const_spec(ln_b)],
        out_specs=row,
        out_shape=jax.ShapeDtypeStruct(x.shape, F32),
        compiler_params=_params(("arbitrary",)),
        name="mix_out",
    )(x, mod, *outs, w_out, ln_g, ln_b)


def _prep_kernel(*refs, layer, nlayers, g_t, tm, decode):
    (x_ref, sh_ref, sc_ref, w_ref, cos_ref, sin_ref, wg_ref, bg_ref, lb_ref, cw_ref, alog_ref, dtb_ref,
     bd_ref, ex_ref) = refs[:14]
    rest = refs[14:]
    if decode:
        h1_ref, h2_ref, h3_ref = rest[:3]
        outs = rest[3:]
        cbuf = None
    else:
        outs = rest[:-1]
        cbuf = rest[-1]

    x = x_ref[...]
    hb = _bf(x * (1.0 + _mod_rows(sc_ref, tm)) + _mod_rows(sh_ref, tm))

    def proj(name):
        o, w = _OFF[name]
        return _dot(hb, w_ref[0, :, o:o + w])

    bd = bd_ref[...]

    dest = {}
    if decode:
        nb = tm // g_t
        it = iter(outs)
        for mix, keys, widths in (("ret", ("q", "k"), (HW, HW)), ("gla", ("q", "k", "g"), (HEADS * GLA_DK,) * 3),
                                  ("hg", ("q", "k", "g"), (HW,) * 3), ("gdn", ("q", "k", "b", "g"), (HW,) * 4)):
            k_ref, v_ref, gate_ref = next(it), next(it), next(it)
            off = 0
            for kk, w in zip(keys, widths):
                dest[f"{mix}_{kk}"] = ("rows", k_ref, off)
                off += w
            dest[f"{mix}_v"] = ("heads", v_ref, 0)
            dest[f"{mix}_gate"] = ("heads", gate_ref, 0)
        dq_ref = next(it)
    else:
        ret_ref, gla_ref, hg_ref, gdn_ref, conv_ref = outs
        for dst, keys, widths in ((ret_ref, ("ret_q", "ret_k", "ret_v", "ret_gate"), (HW,) * 4),
                                  (gla_ref, ("gla_q", "gla_k", "gla_g", "gla_v", "gla_gate"),
                                   (HEADS * GLA_DK,) * 3 + (HW, HW)),
                                  (hg_ref, ("hg_q", "hg_k", "hg_g", "hg_v", "hg_gate"), (HW,) * 5),
                                  (gdn_ref, ("gdn_q", "gdn_k", "gdn_b", "gdn_g", "gdn_v", "gdn_gate"), (HW,) * 6)):
            off = 0
            for kk, w in zip(keys, widths):
                dest[kk] = ("rows", dst, off)
                off += w

    def put(name, val):
        kind, ref, off = dest[name]
        if kind == "rows":
            ref[:, off:off + val.shape[1]] = val
        else:
            for h in range(HEADS):
                ref[:, h] = val[:, h * HEAD_DIM:(h + 1) * HEAD_DIM].reshape(nb, g_t, HEAD_DIM)

    cos, sin = cos_ref[...], sin_ref[...]
    lane = lax.broadcasted_iota(jnp.int32, (tm, HW), 1)
    first_half = (lane % HEAD_DIM) < (HEAD_DIM // 2)

    def rope(v):
        sw = jnp.where(first_half, pltpu.roll(v, HW - HEAD_DIM // 2, 1), pltpu.roll(v, HEAD_DIM // 2, 1))
        return v * cos + sw * sin

    put("ret_q", rope(proj("rq")))
    put("ret_k", rope(proj("rk")) * (HEAD_DIM ** -0.5))
    put("ret_v", proj("rv"))
    put("ret_gate", _silu(proj("rg")))

    small = proj("small")
    gpre = _dot(_bf(small), wg_ref[0]) + bg_ref[layer:layer + 1, :]
    put("gla_q", proj("aq") * (GLA_DK ** -0.5))
    put("gla_k", proj("ak"))
    put("gla_g", -_softplus(-gpre) * (1.0 / GLA_GATE_NORM))
    put("gla_v", proj("av"))
    put("gla_gate", _silu(proj("ag")))

    rows = [lb_ref[i:i + 1, :] for i in range(nlayers)]
    mx = functools.reduce(jnp.maximum, rows)
    ex = [jnp.exp(r - mx) for r in rows]
    tot = functools.reduce(lambda a, b: a + b, ex)
    plb = [e / tot for e in ex]
    lb = functools.reduce(lambda a, b: a + b, plb[:layer + 1]) - plb[0]
    zf = proj("hf")
    put("hg_q", _silu(proj("hq")) * (HEAD_DIM ** -0.5))
    put("hg_k", (1.0 - lb) * jax.nn.sigmoid(-zf))
    put("hg_g", jnp.log(lb + (1.0 - lb) * jax.nn.sigmoid(zf)))
    put("hg_v", proj("hi"))
    put("hg_gate", _silu(proj("hg")))

    dq = proj("dqkv")
    cw = cw_ref[0]
    if decode:
        dq_ref[...] = dq
        tpos = lax.broadcasted_iota(jnp.int32, dq.shape, 0) % g_t
        conv = dq * cw[CONV_W - 1:CONV_W, :]
        for s, h_ref in ((1, h1_ref), (2, h2_ref), (3, h3_ref)):
            prev = jnp.where(tpos >= s, pltpu.roll(dq, s, 0), h_ref[...])
            conv = conv + prev * cw[CONV_W - 1 - s:CONV_W - s, :]
    else:
        i = pl.program_id(0)
        per = g_t // tm

        @pl.when(i % per == 0)
        def _():
            cbuf[0:8, :] = jnp.zeros((8, dq.shape[1]), F32)

        cbuf[8:8 + tm, :] = dq
        conv = cbuf[pl.ds(8, tm), :] * cw[CONV_W - 1:CONV_W, :]
        for s in range(1, CONV_W):
            conv = conv + cbuf[pl.ds(8 - s, tm), :] * cw[CONV_W - 1 - s:CONV_W - s, :]
        tail = cbuf[tm:tm + 8, :]
        cbuf[0:8, :] = tail

        @pl.when(i % per == per - 1)
        def _():
            conv_ref[0] = tail[8 - (CONV_W - 1):8, :]

    u = _silu(conv)
    uq, uk = u[:, 0:HW], u[:, HW:2 * HW]
    put("gdn_q", uq * lax.rsqrt(_dot_x01(uq * uq, bd) + RMS_EPS) * (HEAD_DIM ** -0.5))
    put("gdn_k", uk * lax.rsqrt(_dot_x01(uk * uk, bd) + RMS_EPS))
    put("gdn_v", u[:, 2 * HW:3 * HW])
    dbda = _dot_x01(small, ex_ref[...])
    put("gdn_b", jax.nn.sigmoid(dbda[:, 0:HW]))
    put("gdn_g", -jnp.exp(alog_ref[layer:layer + 1, :]) * _softplus(dbda[:, HW:2 * HW] + dtb_ref[layer:layer + 1, :]))
    put("gdn_gate", _silu(proj("dg")))


def _prep(g, x, mod, w_in_p, cos_t, sin_t, wg_p, gla_bg, hg_lb, gdn_conv, alog_e, dtb_e, hists, layer, nlayers):
    d = x.shape[1]
    bd = jnp.asarray(_bd_ones(), BF16)
    exm = jnp.asarray(_expand_mat(), BF16)
    row = pl.BlockSpec((g.tm, d), lambda i: (i, 0))
    if g.decode:
        tab = pl.BlockSpec((g.tm, HW), lambda i: (i, 0))
    else:
        per = g.t // g.tm
        tab = pl.BlockSpec((g.tm, HW), lambda i: (i % per, 0))
    in_specs = [row, _mod_spec(g, layer, 3, d), _mod_spec(g, layer, 4, d),
                pl.BlockSpec((1, d, PROJ_W), lambda i: (layer, 0, 0), pipeline_mode=pl.Buffered(1)),
                tab, tab,
                pl.BlockSpec((1,) + wg_p.shape[1:], lambda i: (layer, 0, 0)),
                _const_spec(gla_bg), _const_spec(hg_lb),
                pl.BlockSpec((1,) + gdn_conv.shape[1:], lambda i: (layer, 0, 0)),
                _const_spec(alog_e), _const_spec(dtb_e), _const_spec(bd), _const_spec(exm)]
    args = [x, mod, mod, w_in_p, cos_t, sin_t, wg_p, gla_bg, hg_lb, gdn_conv, alog_e, dtb_e, bd, exm]
    cch = gdn_conv.shape[2]

    def rows_out(w):
        return jax.ShapeDtypeStruct((g.rows, w), F32), pl.BlockSpec((g.tm, w), lambda i: (i, 0))

    if g.decode:
        nb = g.tm // g.t
        hspec = pl.BlockSpec((g.tm, cch), lambda i: (i, 0))
        in_specs += [hspec, hspec, hspec]
        args += list(hists)
        hm = (jax.ShapeDtypeStruct((g.batch, HEADS, g.t, HEAD_DIM), F32),
              pl.BlockSpec((nb, HEADS, g.t, HEAD_DIM), lambda i: (i, 0, 0, 0)))
        outs = []
        for kw in (2 * HW, 3 * HEADS * GLA_DK, 3 * HW, 4 * HW):
            outs += [rows_out(kw), hm, hm]
        outs.append(rows_out(cch))
        scratch = []
    else:
        outs = [rows_out(4 * HW), rows_out(3 * HEADS * GLA_DK + 2 * HW), rows_out(5 * HW), rows_out(6 * HW),
                (jax.ShapeDtypeStruct((g.batch, CONV_W - 1, cch), F32),
                 pl.BlockSpec((1, CONV_W - 1, cch), lambda i: (i // (g.t // g.tm), 0, 0)))]
        scratch = [pltpu.VMEM((g.tm + 8, cch), F32)]
    return pl.pallas_call(
        functools.partial(_prep_kernel, layer=layer, nlayers=nlayers, g_t=g.t, tm=g.tm, decode=g.decode),
        grid=(g.steps,),
        in_specs=in_specs,
        out_specs=[o[1] for o in outs],
        out_shape=[o[0] for o in outs],
        scratch_shapes=scratch,
        compiler_params=_params(("arbitrary",)),
        name="prep",
    )(*args)


def _finish_prompt(o, gate, normw, bd, o_ref, r0):
    ms = _dot_x01(o * o, bd) * (1.0 / HEAD_DIM)
    on = o * lax.rsqrt(ms + RMS_EPS)
    if normw is not None:
        on = on * normw
    o_ref[pl.ds(r0, CHUNK), :] = _bf(on * gate)


def _state_out(s_t, fold_ref, st_ref):
    fold = fold_ref[...]
    h1, h2, h3 = _split3(s_t)
    st_ref[0] = _dot_tn(h1, fold) + _dot_tn(h2, fold) + _dot_tn(h3, fold)


def _gla_prompt_kernel(f_ref, kmask_ref, vmask_ref, smask_ref, lvl_ref, fold_ref, bd_ref, nw_ref, ratt_ref, rq_ref,
                       rk_ref, rs_ref, o_ref, st_ref, s_scr, *, dk, nchunks, ret, levels, has_norm, layer):
    j = pl.program_id(1)
    dkk = HEADS * dk

    @pl.when(j == 0)
    def _():
        s_scr[...] = jnp.zeros(s_scr.shape, F32)

    kmask, vmask, smask, bd = kmask_ref[...], vmask_ref[...], smask_ref[...], bd_ref[...]
    normw = nw_ref[layer:layer + 1, :] if has_norm else None

    def chunk(ci, carry):
        r0 = pl.multiple_of(ci * CHUNK, CHUNK)
        rows = pl.ds(r0, CHUNK)
        q = f_ref[rows, 0:dkk]
        k = f_ref[rows, dkk:2 * dkk]
        s_t = s_scr[...]
        sb = _bf(s_t)
        if ret:
            v = f_ref[rows, 2 * dkk:2 * dkk + HW]
            gate = f_ref[rows, 2 * dkk + HW:2 * dkk + 2 * HW]
            att = _dot_nt(_bf(q), _bf(_tile4(k) * kmask)) * ratt_ref[...]
            qg = q * rq_ref[...]
            kd = k * rk_ref[...]
            sdec = rs_ref[...]
        else:
            g = f_ref[rows, 2 * dkk:3 * dkk]
            v = f_ref[rows, 3 * dkk:3 * dkk + HW]
            gate = f_ref[rows, 3 * dkk + HW:3 * dkk + 2 * HW]
            G = _cumsum_rows(g)
            att = _decay_att(q, k, G, kmask, lvl_ref, levels)
            qg = q * jnp.exp(G)
            gl = G[CHUNK - 1:CHUNK, :]
            kd = k * jnp.exp(gl - G)
            sdec = jnp.exp(gl)
        vb = _bf(v)
        o = _dot(_bf(att), _bf(_tile4(v) * vmask)) + _dot_nt(_bf(qg), sb)
        s_scr[...] = s_t * sdec + _dot_tn(vb, _bf(kd)) * smask
        _finish_prompt(o, gate, normw, bd, o_ref, r0)
        return carry

    lax.fori_loop(0, nchunks, chunk, 0)

    @pl.when(j == pl.num_programs(1) - 1)
    def _():
        _state_out(s_scr[...], fold_ref, st_ref)


def _gla_prompt(g, feat, dk, normw, layer, ret):
    c = _chunk_consts(dk)
    dkk = HEADS * dk
    blk = min(ROW_TILE, g.t)
    nchunks = blk // CHUNK
    per = g.t // blk
    consts = [jnp.asarray(c["kmask"]), jnp.asarray(c["vmask"]), jnp.asarray(c["smask"]), jnp.asarray(c["lvl"]),
              jnp.asarray(c["fold"], BF16), jnp.asarray(_bd_ones(), BF16),
              normw if normw is not None else jnp.ones((1, HW), F32),
              jnp.asarray(c["ret_att"]), jnp.asarray(c["ret_q"]), jnp.asarray(c["ret_k"]), jnp.asarray(c["ret_s"])]
    w = feat.shape[1]
    o, st = pl.pallas_call(
        functools.partial(_gla_prompt_kernel, dk=dk, nchunks=nchunks, ret=ret, levels=c["levels"],
                          has_norm=normw is not None, layer=layer),
        grid=(g.batch, per),
        in_specs=[pl.BlockSpec((blk, w), lambda b, j: (b * per + j, 0))] + [_const_spec(a) for a in consts],
        out_specs=[pl.BlockSpec((blk, HW), lambda b, j: (b * per + j, 0)),
                   pl.BlockSpec((1, dkk, HEAD_DIM), lambda b, j: (b, 0, 0))],
        out_shape=[jax.ShapeDtypeStruct((g.rows, HW), BF16),
                   jax.ShapeDtypeStruct((g.batch, dkk, HEAD_DIM), F32)],
        scratch_shapes=[pltpu.VMEM((HW, dkk), F32)],
        compiler_params=_params(("arbitrary", "arbitrary")),
        name="ret_prompt" if ret else f"gla_prompt_dk{dk}",
    )(feat, *consts)
    return o, st.reshape(g.batch, HEADS, dk, HEAD_DIM)


def _cat_prod(x, y, bdmask):
    return _dot3(x, _tile4(y) * bdmask)


def _gdn_prompt_kernel(f_ref, bdm_ref, smask_ref, incl_ref, strict_ref, eye_ref, fold_ref, bd_ref, nw_ref, o_ref,
                       st_ref, s_scr, *, nchunks, layer):
    j = pl.program_id(1)

    @pl.when(j == 0)
    def _():
        s_scr[...] = jnp.zeros(s_scr.shape, F32)

    bdm, smask, incl, strict, eye, bd = (bdm_ref[...], smask_ref[...], incl_ref[...], strict_ref[...], eye_ref[...],
                                         bd_ref[...])
    normw = nw_ref[layer:layer + 1, :]
    ones_c = jnp.ones((CHUNK, CHUNK), BF16)

    def chunk(ci, carry):
        r0 = pl.multiple_of(ci * CHUNK, CHUNK)
        rows = pl.ds(r0, CHUNK)
        q, k = f_ref[rows, 0:HW], f_ref[rows, HW:2 * HW]
        be, ge = f_ref[rows, 2 * HW:3 * HW], f_ref[rows, 3 * HW:4 * HW]
        v, gate = f_ref[rows, 4 * HW:5 * HW], f_ref[rows, 5 * HW:6 * HW]
        s_t = s_scr[...]
        sb = _bf(s_t)
        G = _cumsum_rows(ge)
        z1, z2, z3 = _split3(G * eye)
        grow = _dot(ones_c, z1) + _dot(ones_c, z2) + _dot(ones_c, z3)
        L = jnp.exp(jnp.minimum(G - grow, 0.0)) * incl
        kbd = _bf(_tile4(k) * bdm)
        qk = _dot_nt(_bf(q), kbd)
        kk = _dot_nt(_bf(k), kbd)
        a = be * kk * L * strict
        n = -a
        tinv = eye + n
        p = n
        for _ in range(5):
            p = _cat_prod(p, p, bdm)
            tinv = tinv + _cat_prod(tinv, p, bdm)
        eg = jnp.exp(G)
        u = _cat_prod(tinv, be * v, bdm)
        w = _cat_prod(tinv, be * eg * k, bdm)
        delta = u - _dot_nt(_bf(w), sb)
        o = _dot(_bf(qk * L), _bf(_tile4(delta) * bdm)) + _dot_nt(_bf(q * eg), sb)
        gl = G[CHUNK - 1:CHUNK, :]
        s_scr[...] = s_t * jnp.exp(gl) + _dot_tn(_bf(delta), _bf(k * jnp.exp(gl - G))) * smask
        _finish_prompt(o, gate, normw, bd, o_ref, r0)
        return carry

    lax.fori_loop(0, nchunks, chunk, 0)

    @pl.when(j == pl.num_programs(1) - 1)
    def _():
        _state_out(s_scr[...], fold_ref, st_ref)


def _gdn_prompt(g, feat, normw, layer):
    c = _chunk_consts(HEAD_DIM)
    blk = min(ROW_TILE, g.t)
    nchunks = blk // CHUNK
    per = g.t // blk
    consts = [jnp.asarray(c["vmask"]), jnp.asarray(c["smask"]), jnp.asarray(c["incl"]), jnp.asarray(c["strict"]),
              jnp.asarray(c["eye"]), jnp.asarray(c["fold"], BF16), jnp.asarray(_bd_ones(), BF16), normw]
    w = feat.shape[1]
    o, st = pl.pallas_call(
        functools.partial(_gdn_prompt_kernel, nchunks=nchunks, layer=layer),
        grid=(g.batch, per),
        in_specs=[pl.BlockSpec((blk, w), lambda b, j: (b * per + j, 0))] + [_const_spec(a) for a in consts],
        out_specs=[pl.BlockSpec((blk, HW), lambda b, j: (b * per + j, 0)),
                   pl.BlockSpec((1, HW, HEAD_DIM), lambda b, j: (b, 0, 0))],
        out_shape=[jax.ShapeDtypeStruct((g.rows, HW), BF16),
                   jax.ShapeDtypeStruct((g.batch, HW, HEAD_DIM), F32)],
        scratch_shapes=[pltpu.VMEM((HW, HW), F32)],
        compiler_params=_params(("arbitrary", "arbitrary")),
        name="gdn_prompt",
    )(feat, *consts)
    return o, st.reshape(g.batch, HEADS, HEAD_DIM, HEAD_DIM)


def _col_from_row(r):
    hi, mid, lo = (p.astype(F32) for p in _split3(r))
    n = r.shape[1]
    rid = lax.broadcasted_iota(jnp.int32, (8, n), 0)
    st = jnp.where(rid == 0, hi, jnp.where(rid == 1, mid, jnp.where(rid == 2, lo, 0.0)))
    return _dot_tn(_bf(st), jnp.ones((8, HEAD_DIM), BF16))


def _dec_finish(o, gate, normw, hmvalid):
    ms = jnp.mean(o * o, axis=-1, keepdims=True)
    on = o * lax.rsqrt(ms + RMS_EPS)
    if normw is not None:
        on = on * normw
    return on * gate * hmvalid


def _gla_decode_kernel(k_ref, v_ref, gate_ref, s_ref, qmask_ref, lvl_ref, rv_ref, hv_ref, retg_ref, nw_ref, o_ref,
                       so_ref, *, dk, ret, levels, has_norm, layer, nb):
    dkk = HEADS * dk
    qmask, rowvalid, hmvalid = qmask_ref[...], rv_ref[...], hv_ref[...]
    normw = nw_ref[layer:layer + 1, :] if has_norm else None

    def elem(e, carry):
        kf = k_ref[e]
        q = kf[:, 0:dkk] * rowvalid
        k = kf[:, dkk:2 * dkk] * rowvalid
        g = retg_ref[...] if ret else kf[:, 2 * dkk:3 * dkk] * rowvalid
        v = v_ref[e] * hmvalid
        s = s_ref[e]
        sb = _bf(s)
        G = _cumsum_rows(g)
        att = _decay_att(q, k, G, None, lvl_ref, levels, qmask=qmask)
        o = _dot(_bf(att), _bf(v)) + _dot(_bf(_tile4(q * jnp.exp(G)) * qmask), sb)
        gl = G[DEC_T - 1:DEC_T, :]
        kd = k * jnp.exp(gl - G)
        so_ref[e] = s * jnp.exp(_col_from_row(gl)) + _dot_tn(_bf(_tile4(kd) * qmask), _bf(v))
        o_ref[e] = _dec_finish(o, gate_ref[e], normw, hmvalid)
        return carry

    lax.fori_loop(0, nb, elem, 0)


def _gla_decode(batch, kfeat, vfeat, gatefeat, state, dk, normw, layer, ret, t_valid):
    c = _dec_consts(dk, t_valid)
    dkk = HEADS * dk
    nb = DEC_NB
    n = HEADS * DEC_T
    consts = [jnp.asarray(c["qmask"]), jnp.asarray(c["lvl"]), jnp.asarray(c["rowvalid"]), jnp.asarray(c["hmvalid"]),
              jnp.asarray(c["retg"]), normw if normw is not None else jnp.ones((1, HEAD_DIM), F32)]
    kw = kfeat.shape[-1]
    o, st = pl.pallas_call(
        functools.partial(_gla_decode_kernel, dk=dk, ret=ret, levels=c["levels"], has_norm=normw is not None,
                          layer=layer, nb=nb),
        grid=(batch // nb,),
        in_specs=[pl.BlockSpec((nb, DEC_T, kw), lambda i: (i, 0, 0)),
                  pl.BlockSpec((nb, n, HEAD_DIM), lambda i: (i, 0, 0)),
                  pl.BlockSpec((nb, n, HEAD_DIM), lambda i: (i, 0, 0)),
                  pl.BlockSpec((nb, dkk, HEAD_DIM), lambda i: (i, 0, 0))] + [_const_spec(a) for a in consts],
        out_specs=[pl.BlockSpec((nb, n, HEAD_DIM), lambda i: (i, 0, 0)),
                   pl.BlockSpec((nb, dkk, HEAD_DIM), lambda i: (i, 0, 0))],
        out_shape=[jax.ShapeDtypeStruct((batch, n, HEAD_DIM), F32),
                   jax.ShapeDtypeStruct((batch, dkk, HEAD_DIM), F32)],
        compiler_params=_params(("arbitrary",)),
        name="ret_decode" if ret else f"gla_decode_dk{dk}",
    )(kfeat, vfeat, gatefeat, state, *consts)
    return o, st


def _gdn_decode_kernel(k_ref, v_ref, gate_ref, s_ref, qmask_ref, incl_ref, strict_ref, eye_ref, rv_ref, hv_ref, nw_ref,
                       o_ref, so_ref, *, layer, nb):
    qmask, incl, strict, eye = qmask_ref[...], incl_ref[...], strict_ref[...], eye_ref[...]
    rowvalid, hmvalid = rv_ref[...], hv_ref[...]
    normw = nw_ref[layer:layer + 1, :]
    n = HEADS * DEC_T
    ones_w = jnp.full((HW, HEAD_DIM), 1.0 / HEAD_DIM, BF16)
    ones_r = jnp.full((n, HW), 1.0 / HEAD_DIM, BF16)

    def head_col(xe):
        h1, h2, h3 = _split3(_tile4(xe) * qmask)
        return _dot(h1, ones_w) + _dot(h2, ones_w) + _dot(h3, ones_w), (h1, h2, h3)

    def elem(e, carry):
        kf = k_ref[e]
        q = kf[:, 0:HW] * rowvalid
        k = kf[:, HW:2 * HW] * rowvalid
        be = kf[:, 2 * HW:3 * HW] * rowvalid
        ge = kf[:, 3 * HW:4 * HW] * rowvalid
        v = v_ref[e] * hmvalid
        s = s_ref[e]
        sb = _bf(s)
        G = _cumsum_rows(ge)
        gcol, (g1, g2, g3) = head_col(G)
        grow = _dot_nt(ones_r, g1) + _dot_nt(ones_r, g2) + _dot_nt(ones_r, g3)
        bcol, _ = head_col(be)
        L = jnp.exp(jnp.minimum(gcol[:, 0:n] - grow, 0.0)) * incl
        kt = _bf(_tile4(k))
        kk = _dot_nt(_bf(_tile4(k) * qmask), kt)
        qk = _dot_nt(_bf(_tile4(q) * qmask), kt)
        a = bcol[:, 0:n] * kk * L * strict
        a2 = _dot3(a, a)
        tinv = eye - a + a2 - _dot3(a2, a)
        eg = jnp.exp(G)
        u = _dot3(tinv, bcol * v)
        wrow = _dot3(tinv, _tile4(be * eg * k)) * qmask
        delta = u - _dot(_bf(wrow), sb)
        o = _dot(_bf(qk * L), _bf(delta)) + _dot(_bf(_tile4(q * eg) * qmask), sb)
        gl = G[DEC_T - 1:DEC_T, :]
        kd = k * jnp.exp(gl - G)
        so_ref[e] = s * jnp.exp(_col_from_row(gl)) + _dot_tn(_bf(_tile4(kd) * qmask), _bf(delta))
        o_ref[e] = _dec_finish(o, gate_ref[e], normw, hmvalid)
        return carry

    lax.fori_loop(0, nb, elem, 0)


def _gdn_decode(batch, kfeat, vfeat, gatefeat, state, normw, layer, t_valid):
    c = _dec_consts(HEAD_DIM, t_valid)
    nb = DEC_NB
    n = HEADS * DEC_T
    consts = [jnp.asarray(c["qmask"]), jnp.asarray(c["incl"]), jnp.asarray(c["strict"]), jnp.asarray(c["eye"]),
              jnp.asarray(c["rowvalid"]), jnp.asarray(c["hmvalid"]), normw]
    kw = kfeat.shape[-1]
    o, st = pl.pallas_call(
        functools.partial(_gdn_decode_kernel, layer=layer, nb=nb),
        grid=(batch // nb,),
        in_specs=[pl.BlockSpec((nb, DEC_T, kw), lambda i: (i, 0, 0)),
                  pl.BlockSpec((nb, n, HEAD_DIM), lambda i: (i, 0, 0)),
                  pl.BlockSpec((nb, n, HEAD_DIM), lambda i: (i, 0, 0)),
                  pl.BlockSpec((nb, HW, HEAD_DIM), lambda i: (i, 0, 0))] + [_const_spec(a) for a in consts],
        out_specs=[pl.BlockSpec((nb, n, HEAD_DIM), lambda i: (i, 0, 0)),
                   pl.BlockSpec((nb, HW, HEAD_DIM), lambda i: (i, 0, 0))],
        out_shape=[jax.ShapeDtypeStruct((batch, n, HEAD_DIM), F32),
                   jax.ShapeDtypeStruct((batch, HW, HEAD_DIM), F32)],
        compiler_params=_params(("arbitrary",)),
        name="gdn_decode",
    )(kfeat, vfeat, gatefeat, state, *consts)
    return o, st


def _rope_tables(pos):
    half = HEAD_DIM // 2
    inv = ROPE_BASE ** (-jnp.arange(half, dtype=F32) / half)
    ang = pos[:, None] * inv[None, :]
    cos, sin = jnp.cos(ang), jnp.sin(ang)
    cos_t = jnp.tile(jnp.concatenate([cos, cos], axis=1), (1, HEADS))
    sin_t = jnp.tile(jnp.concatenate([-sin, sin], axis=1), (1, HEADS))
    return cos_t, sin_t


def _relayout_w_in(w_in):
    parts = []
    for name, w in _SLABS:
        if name == "small":
            sm = jnp.concatenate([w_in[:, :, _SRC["alr"]:_SRC["alr"] + 16], w_in[:, :, _SRC["db"]:_SRC["db"] + 4],
                                  w_in[:, :, _SRC["da"]:_SRC["da"] + 4]], axis=2)
            parts.append(jnp.pad(sm, ((0, 0), (0, 0), (0, w - sm.shape[2]))))
        else:
            parts.append(w_in[:, :, _SRC[name]:_SRC[name] + w])
    return jnp.concatenate(parts, axis=2).astype(BF16)


def kernel(x_prompt, x_sample, state_ret, state_gla, state_hgrn, state_gdn, state_gdn_conv, c_prompt, c_sample, ada_w, ada_b, ln_g, ln_b, ffn1_wi, ffn1_wo, ffn2_wi, ffn2_wo, w_in, gla_wg, gla_bg, hg_lb, gdn_conv, gdn_a_log, gdn_dt_bias, gla_norm, hg_norm, gdn_norm, w_out):
    nl, d = ada_w.shape[0], ada_w.shape[1]
    bp, tp = x_prompt.shape[0], x_prompt.shape[1]
    bs, ts = x_sample.shape[0], x_sample.shape[1]
    alpha = (2.0 * nl) ** 0.25
    assert ts <= DEC_T and tp % CHUNK == 0 and ts >= CONV_W - 1

    wi1, wo1, wi2, wo2 = (w.astype(BF16) for w in (ffn1_wi, ffn1_wo, ffn2_wi, ffn2_wo))
    w_out_b = w_out.astype(BF16)
    w_in_p = _relayout_w_in(w_in)
    wg_p = jnp.pad(gla_wg, ((0, 0), (0, 128 - gla_wg.shape[1]), (0, 0))).astype(BF16)
    alog_e = jnp.repeat(gdn_a_log, HEAD_DIM, axis=1)
    dtb_e = jnp.repeat(gdn_dt_bias, HEAD_DIM, axis=1)
    gla_nw = jnp.tile(gla_norm, (1, HEADS))
    hg_nw = jnp.tile(hg_norm, (1, HEADS))
    gdn_nw = jnp.tile(gdn_norm, (1, HEADS))

    mod = _ada(jnp.concatenate([c_prompt, c_sample], axis=0), ada_w, ada_b)
    mod = mod.reshape(nl, bp + bs, N_MOD, d).transpose(0, 2, 1, 3)
    mod_p = mod[:, :, :bp].reshape(nl, N_MOD, bp, 1, d)
    mod_s = mod[:, :, bp:]

    gp = _Rows(bp, tp, decode=False)
    gs = _Rows(bs, DEC_T, decode=True)
    gp_prep = _Rows(bp, tp, decode=False, tile=PREP_TILE)
    gs_prep = _Rows(bs, DEC_T, decode=True, tile=PREP_TILE)
    cos_p, sin_p = _rope_tables(jnp.arange(tp, dtype=F32))
    cos_s, sin_s = _rope_tables(PAST_LEN + jnp.arange(DEC_T, dtype=F32))
    cos_s, sin_s = jnp.tile(cos_s, (bs, 1)), jnp.tile(sin_s, (bs, 1))

    xp = x_prompt.reshape(bp * tp, d)
    xs = jnp.pad(x_sample, ((0, 0), (0, DEC_T - ts), (0, 0))).reshape(bs * DEC_T, d)

    p_states, s_states = [], []
    for l in range(nl):
        xp = _ffn(gp, xp, mod_p, (0, 1, 2), wi1, wo1, ln_g, ln_b, l, 0, alpha)
        ret_f, gla_f, hg_f, gdn_f, p_conv = _prep(gp_prep, xp, mod_p, w_in_p, cos_p, sin_p, wg_p, gla_bg, hg_lb, gdn_conv,
                                                  alog_e, dtb_e, None, l, nl)
        o_ret, p_ret = _gla_prompt(gp, ret_f, HEAD_DIM, None, l, ret=True)
        o_gla, p_gla = _gla_prompt(gp, gla_f, GLA_DK, gla_nw, l, ret=False)
        o_hg, p_hg = _gla_prompt(gp, hg_f, HEAD_DIM, hg_nw, l, ret=False)
        o_gdn, p_gdn = _gdn_prompt(gp, gdn_f, gdn_nw, l)
        xp = _mix_out(gp, xp, mod_p, (o_ret, o_gla, o_hg, o_gdn), w_out_b, ln_g, ln_b, l, alpha)
        xp = _ffn(gp, xp, mod_p, (6, 7, 8), wi2, wo2, ln_g, ln_b, l, 2, alpha)
        p_states.append((p_ret, p_gla, p_hg, p_gdn, p_conv))

        xs = _ffn(gs, xs, mod_s, (0, 1, 2), wi1, wo1, ln_g, ln_b, l, 0, alpha)
        cs = state_gdn_conv[l]
        hists = []
        for s in range(1, CONV_W):
            hrow = jnp.pad(cs[:, CONV_W - 1 - s:, :], ((0, 0), (0, DEC_T - s), (0, 0)))
            hists.append(hrow.reshape(bs * DEC_T, -1))
        outs = _prep(gs_prep, xs, mod_s, w_in_p, cos_s, sin_s, wg_p, gla_bg, hg_lb, gdn_conv, alog_e, dtb_e, hists, l, nl)
        (ret_k, ret_v, ret_gt, gla_k, gla_v, gla_gt, hg_k, hg_v, hg_gt, gdn_k, gdn_v, gdn_gt, dq_raw) = outs

        def key3(a):
            return a.reshape(bs, DEC_T, a.shape[1])

        def hm3(a):
            return a.reshape(bs, HEADS * DEC_T, HEAD_DIM)

        def nat(st, dk):
            return st.reshape(bs, HEADS * dk, HEAD_DIM)

        o_ret, s_ret = _gla_decode(bs, key3(ret_k), hm3(ret_v), hm3(ret_gt), nat(state_ret[l], HEAD_DIM), HEAD_DIM,
                                   None, l, True, ts)
        o_gla, s_gla = _gla_decode(bs, key3(gla_k), hm3(gla_v), hm3(gla_gt), nat(state_gla[l], GLA_DK), GLA_DK,
                                   gla_norm, l, False, ts)
        o_hg, s_hg = _gla_decode(bs, key3(hg_k), hm3(hg_v), hm3(hg_gt), nat(state_hgrn[l], HEAD_DIM), HEAD_DIM,
                                 hg_norm, l, False, ts)
        o_gdn, s_gdn = _gdn_decode(bs, key3(gdn_k), hm3(gdn_v), hm3(gdn_gt), nat(state_gdn[l], HEAD_DIM), gdn_norm,
                                   l, ts)
        xs = _mix_out(gs, xs, mod_s, (o_ret, o_gla, o_hg, o_gdn), w_out_b, ln_g, ln_b, l, alpha)
        xs = _ffn(gs, xs, mod_s, (6, 7, 8), wi2, wo2, ln_g, ln_b, l, 2, alpha)
        s_conv = dq_raw.reshape(bs, DEC_T, -1)[:, ts - (CONV_W - 1):ts, :]
        s_states.append((s_ret.reshape(bs, HEADS, HEAD_DIM, HEAD_DIM), s_gla.reshape(bs, HEADS, GLA_DK, HEAD_DIM),
                         s_hg.reshape(bs, HEADS, HEAD_DIM, HEAD_DIM), s_gdn.reshape(bs, HEADS, HEAD_DIM, HEAD_DIM),
                         s_conv))

    y_prompt = xp.reshape(bp, tp, d)
    y_sample = xs.reshape(bs, DEC_T, d)[:, :ts, :]
    p_out = tuple(jnp.stack([st[i] for st in p_states]) for i in range(5))
    s_out = tuple(jnp.stack([st[i] for st in s_states]) for i in range(5))
    return (y_prompt, y_sample) + p_out + s_out
```

```python
import functools

import numpy as np
import jax
import jax.numpy as jnp
from jax import lax
from jax.experimental import pallas as pl
from jax.experimental.pallas import tpu as pltpu

F32 = jnp.float32
BF16 = jnp.bfloat16

HEAD_DIM = 64
HEADS = 4
GLA_DK = 32
GLA_GATE_NORM = 16.0
CONV_W = 4
CHUNK = 64
ROPE_BASE = 10000.0
LN_EPS = 1e-5
RMS_EPS = 1e-6
PAST_LEN = 16384
N_MOD = 9
HW = HEADS * HEAD_DIM

ROW_TILE = 512
PREP_TILE = 256
MIX_TILE = 512
DEC_T = 8
DEC_NB = 8
FF_CHUNK = 512
V7X_VMEM_BYTES = 64 * 1024 * 1024
VMEM_LIMIT = V7X_VMEM_BYTES - 12 * 1024 * 1024

_SLABS = (("rq", 256), ("rk", 256), ("rv", 256), ("rg", 256),
          ("aq", 128), ("ak", 128), ("av", 256), ("ag", 256),
          ("hq", 256), ("hf", 256), ("hi", 256), ("hg", 256),
          ("dqkv", 768), ("dg", 256), ("small", 128))
_OFF = {}
_o = 0
for _n, _w in _SLABS:
    _OFF[_n] = (_o, _w)
    _o += _w
PROJ_W = _o
_SRC = {"rq": 0, "rk": 256, "rv": 512, "rg": 768, "aq": 1024, "ak": 1152, "av": 1280, "alr": 1536,
        "ag": 1552, "hq": 1808, "hf": 2064, "hi": 2320, "hg": 2576, "dqkv": 2832, "db": 3600,
        "da": 3604, "dg": 3608}
SMALL_ALR, SMALL_DB, SMALL_DA = 0, 16, 20


def _bf(x):
    return x.astype(BF16)


def _dot(a, b):
    return jnp.dot(a, b, preferred_element_type=F32)


def _dot_nt(a, b):
    return lax.dot_general(a, b, (((1,), (1,)), ((), ())), preferred_element_type=F32)


def _dot_tn(a, b):
    return lax.dot_general(a, b, (((0,), (0,)), ((), ())), preferred_element_type=F32)


def _split2(x):
    hi = x.astype(BF16)
    lo = (x - hi.astype(F32)).astype(BF16)
    return hi, lo


def _split3(x):
    hi = x.astype(BF16)
    r = x - hi.astype(F32)
    mid = r.astype(BF16)
    lo = (r - mid.astype(F32)).astype(BF16)
    return hi, mid, lo


def _dot_x01(x, m01):
    hi, lo = _split2(x)
    return _dot(hi, m01) + _dot(lo, m01)


def _dot_exact01(m01, x):
    h1, h2, h3 = _split3(x)
    return _dot(m01, h1) + _dot(m01, h2) + _dot(m01, h3)


def _dot3(x, y):
    xh, xl = _split2(x)
    yh, yl = _split2(y)
    m = x.shape[0]
    top = _dot(jnp.concatenate([xh, xl], axis=0), yh)
    return top[:m] + top[m:] + _dot(xh, yl)


def _silu(x):
    return x * jax.nn.sigmoid(x)


def _softplus(x):
    return jnp.maximum(x, 0.0) + jnp.log1p(jnp.exp(-jnp.abs(x)))


def _ln(y, g, b):
    mu = jnp.mean(y, axis=-1, keepdims=True)
    yc = y - mu
    var = jnp.mean(yc * yc, axis=-1, keepdims=True)
    return yc * lax.rsqrt(var + LN_EPS) * g + b


def _mod_rows(ref, tm):
    v = ref[...]
    v = v.reshape(v.shape[-2], v.shape[-1])
    nb, d = v.shape
    if nb == 1:
        return v
    return jnp.broadcast_to(v[:, None, :], (nb, tm // nb, d)).reshape(tm, d)


def _tile4(x):
    return jnp.concatenate([x, x, x, x], axis=0)


def _per_block(x, const, period):
    n = x.shape[0] // period
    return (x.reshape(n, period, x.shape[1]) * const[None]).reshape(x.shape)


def _cumsum_rows(g, period):
    row = lax.broadcasted_iota(jnp.int32, g.shape, 0) % period
    out = g
    sh = 1
    while sh < period:
        out = out + jnp.where(row >= sh, pltpu.roll(out, sh, 0), 0.0)
        sh *= 2
    return out


def _block_row(G, period, r):
    c, w = G.shape
    g3 = G.reshape(c // period, period, w)
    return jnp.broadcast_to(g3[:, r:r + 1, :], g3.shape).reshape(c, w)


def _gref(G, m):
    c = G.shape[0]
    if 2 * m >= 8:
        return _block_row(G, 2 * m, m - 1)
    p = lax.broadcasted_iota(jnp.int32, G.shape, 0) % (2 * m)
    if m == 2:
        return jnp.where(p == 0, pltpu.roll(G, c - 1, 0),
                         jnp.where(p == 1, G, jnp.where(p == 2, pltpu.roll(G, 1, 0), pltpu.roll(G, 2, 0))))
    return jnp.where(p == 0, G, pltpu.roll(G, 1, 0))


def _decay_att(q, k, G, mm, lvl_ref, levels, period):
    att = _per_block(mm(q, k), lvl_ref[len(levels)], period)
    for i, m in enumerate(levels):
        gr = _gref(G, m)
        ql = q * jnp.exp(jnp.minimum(G - gr, 0.0))
        kl = k * jnp.exp(jnp.minimum(gr - G, 0.0))
        att = att + _per_block(mm(ql, kl), lvl_ref[i], period)
    return att


def _ret_gamma():
    return np.log(1.0 - 2.0 ** (-5.0 - np.arange(HEADS, dtype=np.float64)))


@functools.lru_cache(maxsize=None)
def _chunk_consts(dk):
    c = CHUNK
    h_att = np.arange(4 * c) // c
    s_att = np.arange(4 * c) % c
    t = np.arange(c)
    kmask = (h_att[:, None] == (np.arange(4 * dk) // dk)[None, :]).astype(np.float32)
    vmask = (h_att[:, None] == (np.arange(HW) // HEAD_DIM)[None, :]).astype(np.float32)
    smask = ((np.arange(HW) // HEAD_DIM)[:, None] == (np.arange(4 * dk) // dk)[None, :]).astype(np.float32)
    levels = (32, 16, 8, 4, 2, 1)
    lv = []
    for m in levels:
        same = (t[:, None] // (2 * m)) == (s_att[None, :] // (2 * m))
        lv.append(same & ((t[:, None] % (2 * m)) >= m) & ((s_att[None, :] % (2 * m)) < m))
    lv.append(t[:, None] == s_att[None, :])
    lvl = np.stack(lv).astype(np.float32)
    incl = (s_att[None, :] <= t[:, None]).astype(np.float32)
    strict = (s_att[None, :] < t[:, None]).astype(np.float32)
    eye = (s_att[None, :] == t[:, None]).astype(np.float32)
    fold = (np.arange(HW)[:, None] % HEAD_DIM == np.arange(HEAD_DIM)[None, :]).astype(np.float32)
    gam = _ret_gamma()
    gh = gam[h_att]
    ret_att = np.where(s_att[None, :] <= t[:, None], np.exp((t[:, None] - s_att[None, :]) * gh[None, :]), 0.0)
    gl = gam[np.arange(HW) // HEAD_DIM]
    ret_q = np.exp((t[:, None] + 1) * gl[None, :])
    ret_k = np.exp((c - 1 - t[:, None]) * gl[None, :])
    ret_s = np.exp(c * gl)[None, :]
    return dict(kmask=kmask, vmask=vmask, smask=smask, lvl=lvl, levels=levels, incl=incl, strict=strict,
                eye=eye, fold=fold, ret_att=ret_att.astype(np.float32), ret_q=ret_q.astype(np.float32),
                ret_k=ret_k.astype(np.float32), ret_s=ret_s.astype(np.float32))


@functools.lru_cache(maxsize=None)
def _dec_consts(dk, t_valid):
    n = DEC_NB * HEADS * DEC_T
    ee = np.arange(n) // (HEADS * DEC_T)
    hh = (np.arange(n) // DEC_T) % HEADS
    tt = np.arange(n) % DEC_T
    qmask = (hh[:, None] == (np.arange(4 * dk) // dk)[None, :]).astype(np.float32)
    same = (ee[:, None] == ee[None, :]) & (hh[:, None] == hh[None, :])
    levels = (2, 1)
    assert t_valid <= 4
    lv = []
    for m in levels:
        blk = (tt[:, None] // (2 * m)) == (tt[None, :] // (2 * m))
        lv.append(same & blk & ((tt[:, None] % (2 * m)) >= m) & ((tt[None, :] % (2 * m)) < m))
    lv.append(same & (tt[:, None] == tt[None, :]))
    lvl = np.stack(lv).astype(np.float32)
    incl = (same & (tt[None, :] <= tt[:, None])).astype(np.float32)
    strict = (same & (tt[None, :] < tt[:, None])).astype(np.float32)
    eye = np.eye(n, dtype=np.float32)
    rowvalid = np.tile((np.arange(DEC_T) < t_valid).astype(np.float32)[:, None], (DEC_NB, 1))
    hmvalid = (tt < t_valid).astype(np.float32)[:, None]
    retg = (rowvalid * _ret_gamma()[np.arange(HW) // HEAD_DIM][None, :]).astype(np.float32)
    return dict(qmask=qmask, lvl=lvl, levels=levels, incl=incl, strict=strict, eye=eye,
                rowvalid=rowvalid, hmvalid=hmvalid, retg=retg)


def _bd_ones():
    h = np.arange(HW) // HEAD_DIM
    return (h[:, None] == h[None, :]).astype(np.float32)


def _expand_mat():
    e = np.zeros((128, 2 * HW), np.float32)
    for h in range(HEADS):
        e[SMALL_DB + h, h * HEAD_DIM:(h + 1) * HEAD_DIM] = 1.0
        e[SMALL_DA + h, HW + h * HEAD_DIM:HW + (h + 1) * HEAD_DIM] = 1.0
    return e


def _const_spec(a):
    nd = a.ndim
    return pl.BlockSpec(a.shape, lambda *_: (0,) * nd)


def _params(sem):
    return pltpu.CompilerParams(dimension_semantics=sem, vmem_limit_bytes=VMEM_LIMIT)


def _ada_kernel(c_ref, w_ref, b_ref, o_ref):
    c = c_ref[...]
    o_ref[0] = _dot(_bf(_silu(c)), _bf(w_ref[0])) + b_ref[0]


def _ada(c_all, ada_w, ada_b):
    nl, d, nd = ada_w.shape
    nb = c_all.shape[0]
    tn = d
    return pl.pallas_call(
        _ada_kernel,
        grid=(nl, nd // tn),
        in_specs=[pl.BlockSpec((nb, d), lambda l, j: (0, 0)),
                  pl.BlockSpec((1, d, tn), lambda l, j: (l, 0, j)),
                  pl.BlockSpec((1, 1, tn), lambda l, j: (l, 0, j))],
        out_specs=pl.BlockSpec((1, nb, tn), lambda l, j: (l, 0, j)),
        out_shape=jax.ShapeDtypeStruct((nl, nb, nd), F32),
        compiler_params=_params(("arbitrary", "arbitrary")),
        name="ada_mod",
    )(c_all, ada_w, ada_b.reshape(nl, 1, nd))


class _Rows:
    def __init__(self, batch, t, decode, tile=ROW_TILE):
        self.batch, self.t, self.decode = batch, t, decode
        self.rows = batch * t
        self.tm = min(tile, self.rows)
        assert self.rows % self.tm == 0
        if decode:
            assert self.tm % t == 0
        else:
            assert t % self.tm == 0
        self.steps = self.rows // self.tm


def _mod_spec(g, layer, k, d):
    if g.decode:
        nb = g.tm // g.t
        return pl.BlockSpec((1, 1, nb, d), lambda i: (layer, k, i, 0))
    per = g.t // g.tm
    return pl.BlockSpec((1, 1, 1, 1, d), lambda i: (layer, k, i // per, 0, 0))


def _ffn_kernel(x_ref, sh_ref, sc_ref, gt_ref, wi_ref, wo_ref, lng_ref, lnb_ref, o_ref, *, alpha, ln_idx, layer):
    x = x_ref[...]
    tm, d = x.shape
    dff = wo_ref.shape[1]
    sh, sc, gt = _mod_rows(sh_ref, tm), _mod_rows(sc_ref, tm), _mod_rows(gt_ref, tm)
    h = _bf(x * (1.0 + sc) + sh)
    acc = jnp.zeros((tm, d), F32)
    off = 0
    while off < dff:
        fc = min(FF_CHUNK, dff - off)
        a = _dot(h, wi_ref[0, :, off:off + fc])
        b = _dot(h, wi_ref[0, :, dff + off:dff + off + fc])
        acc = acc + _dot(_bf(_silu(a) * b), wo_ref[0, off:off + fc, :])
        off += fc
    y = alpha * x + 0.5 * (1.0 + gt) * acc
    o_ref[...] = _ln(y, lng_ref[layer, ln_idx:ln_idx + 1, :], lnb_ref[layer, ln_idx:ln_idx + 1, :])


def _ffn(g, x, mod, mods, wi, wo, ln_g, ln_b, layer, ln_idx, alpha):
    d = x.shape[1]
    dff = wo.shape[1]
    row = pl.BlockSpec((g.tm, d), lambda i: (i, 0))
    return pl.pallas_call(
        functools.partial(_ffn_kernel, alpha=alpha, ln_idx=ln_idx, layer=layer),
        grid=(g.steps,),
        in_specs=[row] + [_mod_spec(g, layer, k, d) for k in mods] + [
            pl.BlockSpec((1, d, 2 * dff), lambda i: (layer, 0, 0), pipeline_mode=pl.Buffered(1)),
            pl.BlockSpec((1, dff, d), lambda i: (layer, 0, 0), pipeline_mode=pl.Buffered(1)),
            _const_spec(ln_g), _const_spec(ln_b)],
        out_specs=row,
        out_shape=jax.ShapeDtypeStruct(x.shape, F32),
        compiler_params=_params(("arbitrary",)),
        name=f"ffn{ln_idx}",
    )(x, mod, mod, mod, wi, wo, ln_g, ln_b)


def _mix_out_kernel(x_ref, gt_ref, o0_ref, o1_ref, o2_ref, o3_ref, w_ref, lng_ref, lnb_ref, y_ref, *, alpha, layer,
                    decode):
    x = x_ref[...]
    tm, d = x.shape
    gt = _mod_rows(gt_ref, tm)
    acc = jnp.zeros((tm, d), F32)
    for mi, o_ref in enumerate((o0_ref, o1_ref, o2_ref, o3_ref)):
        if decode:
            o = o_ref[...]
            for h in range(HEADS):
                oh = o[:, h * DEC_T:(h + 1) * DEC_T, :].reshape(tm, HEAD_DIM)
                r0 = mi * HW + h * HEAD_DIM
                acc = acc + _dot(_bf(oh), w_ref[0, r0:r0 + HEAD_DIM, :])
        else:
            acc = acc + _dot(o_ref[...], w_ref[0, mi * HW:(mi + 1) * HW, :])
    y = alpha * x + (1.0 + gt) * acc
    y_ref[...] = _ln(y, lng_ref[layer, 1:2, :], lnb_ref[layer, 1:2, :])


def _mix_out(g, x, mod, outs, w_out, ln_g, ln_b, layer, alpha):
    d = x.shape[1]
    row = pl.BlockSpec((g.tm, d), lambda i: (i, 0))
    if g.decode:
        nb = g.tm // g.t
        ospec = pl.BlockSpec((nb, HEADS * DEC_T, HEAD_DIM), lambda i: (i, 0, 0))
    else:
        ospec = pl.BlockSpec((g.tm, HW), lambda i: (i, 0))
    return pl.pallas_call(
        functools.partial(_mix_out_kernel, alpha=alpha, layer=layer, decode=g.decode),
        grid=(g.steps,),
        in_specs=[row, _mod_spec(g, layer, 5, d), ospec, ospec, ospec, ospec,
                  pl.BlockSpec((1, w_out.shape[1], d), lambda i: (layer, 0, 0), pipeline_mode=pl.Buffered(1)),
                  _const_spec(ln_g), _const_spec(ln_b)],
        out_specs=row,
        out_shape=jax.ShapeDtypeStruct(x.shape, F32),
        compiler_params=_params(("arbitrary",)),
        name="mix_out",
    )(x, mod, *outs, w_out, ln_g, ln_b)


def _prep_kernel(*refs, layer, nlayers, g_t, tm, decode):
    (x_ref, sh_ref, sc_ref, w_ref, cos_ref, sin_ref, wg_ref, bg_ref, lb_ref, cw_ref, alog_ref, dtb_ref,
     bd_ref, ex_ref) = refs[:14]
    rest = refs[14:]
    if decode:
        h1_ref, h2_ref, h3_ref = rest[:3]
        outs = rest[3:]
        cbuf = None
    else:
        outs = rest[:-1]
        cbuf = rest[-1]

    x = x_ref[...]
    hb = _bf(x * (1.0 + _mod_rows(sc_ref, tm)) + _mod_rows(sh_ref, tm))

    def proj(name):
        o, w = _OFF[name]
        return _dot(hb, w_ref[0, :, o:o + w])

    bd = bd_ref[...]

    dest = {}
    if decode:
        nb = tm // g_t
        it = iter(outs)
        for mix, keys, widths in (("ret", ("q", "k"), (HW, HW)), ("gla", ("q", "k", "g"), (HEADS * GLA_DK,) * 3),
                                  ("hg", ("q", "k", "g"), (HW,) * 3), ("gdn", ("q", "k", "b", "g"), (HW,) * 4)):
            k_ref, v_ref, gate_ref = next(it), next(it), next(it)
            off = 0
            for kk, w in zip(keys, widths):
                dest[f"{mix}_{kk}"] = ("rows", k_ref, off)
                off += w
            dest[f"{mix}_v"] = ("heads", v_ref, 0)
            dest[f"{mix}_gate"] = ("heads", gate_ref, 0)
        dq_ref = next(it)
    else:
        ret_ref, gla_ref, hg_ref, gdn_ref, conv_ref = outs
        for dst, keys, widths in ((ret_ref, ("ret_q", "ret_k", "ret_v", "ret_gate"), (HW,) * 4),
                                  (gla_ref, ("gla_q", "gla_k", "gla_g", "gla_v", "gla_gate"),
                                   (HEADS * GLA_DK,) * 3 + (HW, HW)),
                                  (hg_ref, ("hg_q", "hg_k", "hg_g", "hg_v", "hg_gate"), (HW,) * 5),
                                  (gdn_ref, ("gdn_q", "gdn_k", "gdn_b", "gdn_g", "gdn_v", "gdn_gate"), (HW,) * 6)):
            off = 0
            for kk, w in zip(keys, widths):
                dest[kk] = ("rows", dst, off)
                off += w

    def put(name, val):
        kind, ref, off = dest[name]
        if kind == "rows":
            ref[:, off:off + val.shape[1]] = val
        else:
            for h in range(HEADS):
                ref[:, h] = val[:, h * HEAD_DIM:(h + 1) * HEAD_DIM].reshape(nb, g_t, HEAD_DIM)

    cos, sin = cos_ref[...], sin_ref[...]
    lane = lax.broadcasted_iota(jnp.int32, (tm, HW), 1)
    first_half = (lane % HEAD_DIM) < (HEAD_DIM // 2)

    def rope(v):
        sw = jnp.where(first_half, pltpu.roll(v, HW - HEAD_DIM // 2, 1), pltpu.roll(v, HEAD_DIM // 2, 1))
        return v * cos + sw * sin

    put("ret_q", rope(proj("rq")))
    put("ret_k", rope(proj("rk")) * (HEAD_DIM ** -0.5))
    put("ret_v", proj("rv"))
    put("ret_gate", _silu(proj("rg")))

    small = proj("small")
    gpre = _dot(_bf(small), wg_ref[0]) + bg_ref[layer:layer + 1, :]
    put("gla_q", proj("aq") * (GLA_DK ** -0.5))
    put("gla_k", proj("ak"))
    put("gla_g", -_softplus(-gpre) * (1.0 / GLA_GATE_NORM))
    put("gla_v", proj("av"))
    put("gla_gate", _silu(proj("ag")))

    rows = [lb_ref[i:i + 1, :] for i in range(nlayers)]
    mx = functools.reduce(jnp.maximum, rows)
    ex = [jnp.exp(r - mx) for r in rows]
    tot = functools.reduce(lambda a, b: a + b, ex)
    plb = [e / tot for e in ex]
    lb = functools.reduce(lambda a, b: a + b, plb[:layer + 1]) - plb[0]
    zf = proj("hf")
    put("hg_q", _silu(proj("hq")) * (HEAD_DIM ** -0.5))
    put("hg_k", (1.0 - lb) * jax.nn.sigmoid(-zf))
    put("hg_g", jnp.log(lb + (1.0 - lb) * jax.nn.sigmoid(zf)))
    put("hg_v", proj("hi"))
    put("hg_gate", _silu(proj("hg")))

    dq = proj("dqkv")
    cw = cw_ref[0]
    if decode:
        dq_ref[...] = dq
        tpos = lax.broadcasted_iota(jnp.int32, dq.shape, 0) % g_t
        conv = dq * cw[CONV_W - 1:CONV_W, :]
        for s, h_ref in ((1, h1_ref), (2, h2_ref), (3, h3_ref)):
            prev = jnp.where(tpos >= s, pltpu.roll(dq, s, 0), h_ref[...])
            conv = conv + prev * cw[CONV_W - 1 - s:CONV_W - s, :]
    else:
        i = pl.program_id(0)
        per = g_t // tm

        @pl.when(i % per == 0)
        def _():
            cbuf[0:8, :] = jnp.zeros((8, dq.shape[1]), F32)

        cbuf[8:8 + tm, :] = dq
        conv = cbuf[pl.ds(8, tm), :] * cw[CONV_W - 1:CONV_W, :]
        for s in range(1, CONV_W):
            conv = conv + cbuf[pl.ds(8 - s, tm), :] * cw[CONV_W - 1 - s:CONV_W - s, :]
        tail = cbuf[tm:tm + 8, :]
        cbuf[0:8, :] = tail

        @pl.when(i % per == per - 1)
        def _():
            conv_ref[0] = tail[8 - (CONV_W - 1):8, :]

    u = _silu(conv)
    uq, uk = u[:, 0:HW], u[:, HW:2 * HW]
    put("gdn_q", uq * lax.rsqrt(_dot_x01(uq * uq, bd) + RMS_EPS) * (HEAD_DIM ** -0.5))
    put("gdn_k", uk * lax.rsqrt(_dot_x01(uk * uk, bd) + RMS_EPS))
    put("gdn_v", u[:, 2 * HW:3 * HW])
    dbda = _dot_x01(small, ex_ref[...])
    put("gdn_b", jax.nn.sigmoid(dbda[:, 0:HW]))
    put("gdn_g", -jnp.exp(alog_ref[layer:layer + 1, :]) * _softplus(dbda[:, HW:2 * HW] + dtb_ref[layer:layer + 1, :]))
    put("gdn_gate", _silu(proj("dg")))


def _prep(g, x, mod, w_in_p, cos_t, sin_t, wg_p, gla_bg, hg_lb, gdn_conv, alog_e, dtb_e, hists, layer, nlayers):
    d = x.shape[1]
    bd = jnp.asarray(_bd_ones(), BF16)
    exm = jnp.asarray(_expand_mat(), BF16)
    row = pl.BlockSpec((g.tm, d), lambda i: (i, 0))
    if g.decode:
        tab = pl.BlockSpec((g.tm, HW), lambda i: (i, 0))
    else:
        per = g.t // g.tm
        tab = pl.BlockSpec((g.tm, HW), lambda i: (i % per, 0))
    in_specs = [row, _mod_spec(g, layer, 3, d), _mod_spec(g, layer, 4, d),
                pl.BlockSpec((1, d, PROJ_W), lambda i: (layer, 0, 0), pipeline_mode=pl.Buffered(1)),
                tab, tab,
                pl.BlockSpec((1,) + wg_p.shape[1:], lambda i: (layer, 0, 0)),
                _const_spec(gla_bg), _const_spec(hg_lb),
                pl.BlockSpec((1,) + gdn_conv.shape[1:], lambda i: (layer, 0, 0)),
                _const_spec(alog_e), _const_spec(dtb_e), _const_spec(bd), _const_spec(exm)]
    args = [x, mod, mod, w_in_p, cos_t, sin_t, wg_p, gla_bg, hg_lb, gdn_conv, alog_e, dtb_e, bd, exm]
    cch = gdn_conv.shape[2]

    def rows_out(w):
        return jax.ShapeDtypeStruct((g.rows, w), F32), pl.BlockSpec((g.tm, w), lambda i: (i, 0))

    if g.decode:
        nb = g.tm // g.t
        hspec = pl.BlockSpec((g.tm, cch), lambda i: (i, 0))
        in_specs += [hspec, hspec, hspec]
        args += list(hists)
        hm = (jax.ShapeDtypeStruct((g.batch, HEADS, g.t, HEAD_DIM), F32),
              pl.BlockSpec((nb, HEADS, g.t, HEAD_DIM), lambda i: (i, 0, 0, 0)))
        outs = []
        for kw in (2 * HW, 3 * HEADS * GLA_DK, 3 * HW, 4 * HW):
            outs += [rows_out(kw), hm, hm]
        outs.append(rows_out(cch))
        scratch = []
    else:
        outs = [rows_out(4 * HW), rows_out(3 * HEADS * GLA_DK + 2 * HW), rows_out(5 * HW), rows_out(6 * HW),
                (jax.ShapeDtypeStruct((g.batch, CONV_W - 1, cch), F32),
                 pl.BlockSpec((1, CONV_W - 1, cch), lambda i: (i // (g.t // g.tm), 0, 0)))]
        scratch = [pltpu.VMEM((g.tm + 8, cch), F32)]
    return pl.pallas_call(
        functools.partial(_prep_kernel, layer=layer, nlayers=nlayers, g_t=g.t, tm=g.tm, decode=g.decode),
        grid=(g.steps,),
        in_specs=in_specs,
        out_specs=[o[1] for o in outs],
        out_shape=[o[0] for o in outs],
        scratch_shapes=scratch,
        compiler_params=_params(("arbitrary",)),
        name="prep",
    )(*args)


def _chunk(x, c):
    return x[c * CHUNK:(c + 1) * CHUNK]


def _per_chunk(fn, nch):
    return jnp.concatenate([fn(c) for c in range(nch)], axis=0)


def _finish_prompt(o, gate, normw, bd, o_ref):
    ms = _dot_x01(o * o, bd) * (1.0 / HEAD_DIM)
    on = o * lax.rsqrt(ms + RMS_EPS)
    if normw is not None:
        on = on * normw
    o_ref[...] = _bf(on * gate)


def _state_out(s_t, fold_ref, st_ref):
    fold = fold_ref[...]
    h1, h2, h3 = _split3(s_t)
    st_ref[0] = _dot_tn(h1, fold) + _dot_tn(h2, fold) + _dot_tn(h3, fold)


def _gla_prompt_kernel(f_ref, kmask_ref, vmask_ref, smask_ref, lvl_ref, fold_ref, bd_ref, nw_ref, ratt_ref, rq_ref,
                       rk_ref, rs_ref, o_ref, st_ref, s_scr, *, dk, nch, ret, levels, has_norm, layer):
    j = pl.program_id(1)
    dkk = HEADS * dk

    @pl.when(j == 0)
    def _():
        s_scr[...] = jnp.zeros(s_scr.shape, F32)

    kmask, vmask, smask, bd = kmask_ref[...], vmask_ref[...], smask_ref[...], bd_ref[...]
    normw = nw_ref[layer:layer + 1, :] if has_norm else None

    def mm(x, y):
        xb, yb = _bf(x), _bf(y)
        return _per_chunk(lambda c: _dot_nt(_chunk(xb, c), _tile4(_chunk(yb, c)) * kmask), nch)

    q = f_ref[:, 0:dkk]
    k = f_ref[:, dkk:2 * dkk]
    if ret:
        v = f_ref[:, 2 * dkk:2 * dkk + HW]
        gate = f_ref[:, 2 * dkk + HW:2 * dkk + 2 * HW]
        att = _per_block(mm(q, k), ratt_ref[...], CHUNK)
        qg = _per_block(q, rq_ref[...], CHUNK)
        kd = _per_block(k, rk_ref[...], CHUNK)
        sdec = [rs_ref[...]] * nch
    else:
        g = f_ref[:, 2 * dkk:3 * dkk]
        v = f_ref[:, 3 * dkk:3 * dkk + HW]
        gate = f_ref[:, 3 * dkk + HW:3 * dkk + 2 * HW]
        G = _cumsum_rows(g, CHUNK)
        att = _decay_att(q, k, G, mm, lvl_ref, levels, CHUNK)
        qg = q * jnp.exp(G)
        glb = _block_row(G, CHUNK, CHUNK - 1)
        kd = k * jnp.exp(glb - G)
        sdec = [jnp.exp(G[(c + 1) * CHUNK - 1:(c + 1) * CHUNK, :]) for c in range(nch)]
    vb, qgb, kdb, attb = _bf(v), _bf(qg), _bf(kd), _bf(att)
    o_intra = _per_chunk(lambda c: _dot(_chunk(attb, c), _tile4(_chunk(vb, c)) * vmask), nch)
    upd = [_dot_tn(_chunk(vb, c), _chunk(kdb, c)) * smask for c in range(nch)]

    s = s_scr[...]
    o_inter = []
    for c in range(nch):
        o_inter.append(_dot_nt(_chunk(qgb, c), _bf(s)))
        s = s * sdec[c] + upd[c]
    s_scr[...] = s
    _finish_prompt(o_intra + jnp.concatenate(o_inter, axis=0), gate, normw, bd, o_ref)

    @pl.when(j == pl.num_programs(1) - 1)
    def _():
        _state_out(s, fold_ref, st_ref)


def _gla_prompt(g, feat, dk, normw, layer, ret):
    c = _chunk_consts(dk)
    dkk = HEADS * dk
    blk = min(MIX_TILE, g.t)
    nch = blk // CHUNK
    per = g.t // blk
    consts = [jnp.asarray(c["kmask"], BF16), jnp.asarray(c["vmask"], BF16), jnp.asarray(c["smask"]),
              jnp.asarray(c["lvl"]), jnp.asarray(c["fold"], BF16), jnp.asarray(_bd_ones(), BF16),
              normw if normw is not None else jnp.ones((1, HW), F32),
              jnp.asarray(c["ret_att"]), jnp.asarray(c["ret_q"]), jnp.asarray(c["ret_k"]), jnp.asarray(c["ret_s"])]
    w = feat.shape[1]
    o, st = pl.pallas_call(
        functools.partial(_gla_prompt_kernel, dk=dk, nch=nch, ret=ret, levels=c["levels"],
                          has_norm=normw is not None, layer=layer),
        grid=(g.batch, per),
        in_specs=[pl.BlockSpec((blk, w), lambda b, j: (b * per + j, 0))] + [_const_spec(a) for a in consts],
        out_specs=[pl.BlockSpec((blk, HW), lambda b, j: (b * per + j, 0)),
                   pl.BlockSpec((1, dkk, HEAD_DIM), lambda b, j: (b, 0, 0))],
        out_shape=[jax.ShapeDtypeStruct((g.rows, HW), BF16),
                   jax.ShapeDtypeStruct((g.batch, dkk, HEAD_DIM), F32)],
        scratch_shapes=[pltpu.VMEM((HW, dkk), F32)],
        compiler_params=_params(("arbitrary", "arbitrary")),
        name="ret_prompt" if ret else f"gla_prompt_dk{dk}",
    )(feat, *consts)
    return o, st.reshape(g.batch, HEADS, dk, HEAD_DIM)


def _gdn_prompt_kernel(f_ref, bdm_ref, smask_ref, incl_ref, strict_ref, eye_ref, fold_ref, bd_ref, nw_ref, o_ref,
                       st_ref, s_scr, *, nch, layer):
    j = pl.program_id(1)

    @pl.when(j == 0)
    def _():
        s_scr[...] = jnp.zeros(s_scr.shape, F32)

    bdm, smask, incl, strict, eye, bd = (bdm_ref[...], smask_ref[...], incl_ref[...], strict_ref[...], eye_ref[...],
                                         bd_ref[...])
    normw = nw_ref[layer:layer + 1, :]
    ones_c = jnp.ones((CHUNK, CHUNK), BF16)

    def bdw(yb, c):
        return _tile4(_chunk(yb, c)) * bdm

    def prod1(x, y):
        xb, yb = _bf(x), _bf(y)
        return _per_chunk(lambda c: _dot(_chunk(xb, c), bdw(yb, c)), nch)

    def prod3(x, y):
        xh, xl = _split2(x)
        yh, yl = _split2(y)

        def one(c):
            top = _dot(jnp.concatenate([_chunk(xh, c), _chunk(xl, c)], axis=0), bdw(yh, c))
            return top[:CHUNK] + top[CHUNK:] + _dot(_chunk(xh, c), bdw(yl, c))

        return _per_chunk(one, nch)

    q, k = f_ref[:, 0:HW], f_ref[:, HW:2 * HW]
    be, ge = f_ref[:, 2 * HW:3 * HW], f_ref[:, 3 * HW:4 * HW]
    v, gate = f_ref[:, 4 * HW:5 * HW], f_ref[:, 5 * HW:6 * HW]
    G = _cumsum_rows(ge, CHUNK)
    zd = _per_block(G, eye, CHUNK)
    grow = _per_chunk(lambda c: _dot_exact01(ones_c, _chunk(zd, c)), nch)
    L = _per_block(jnp.exp(jnp.minimum(G - grow, 0.0)), incl, CHUNK)
    qb, kb = _bf(q), _bf(k)

    def qkk(c):
        return _dot_nt(jnp.concatenate([_chunk(qb, c), _chunk(kb, c)], axis=0), bdw(kb, c))

    qkk_all = [qkk(c) for c in range(nch)]
    qk = jnp.concatenate([m[:CHUNK] for m in qkk_all], axis=0)
    kk = jnp.concatenate([m[CHUNK:] for m in qkk_all], axis=0)
    a = _per_block(be * kk * L, strict, CHUNK)
    eye_b = _per_block(jnp.ones_like(a), eye, CHUNK)
    n = -a
    tinv = eye_b + n
    p = n
    for _ in range(5):
        p = prod1(p, p)
        tinv = tinv + prod1(tinv, p)
    resid = eye_b - tinv - prod3(a, tinv)
    tinv = tinv + prod1(tinv, resid)
    eg = jnp.exp(G)
    u = prod3(tinv, be * v)
    w = prod3(tinv, be * eg * k)
    glb = _block_row(G, CHUNK, CHUNK - 1)
    wb, qeb, qklb = _bf(w), _bf(q * eg), _bf(qk * L)
    kdb = _bf(k * jnp.exp(glb - G))
    sdec = [jnp.exp(G[(c + 1) * CHUNK - 1:(c + 1) * CHUNK, :]) for c in range(nch)]

    s = s_scr[...]
    outs = []
    for c in range(nch):
        ws = _dot_nt(jnp.concatenate([_chunk(wb, c), _chunk(qeb, c)], axis=0), _bf(s))
        db = _bf(_chunk(u, c) - ws[:CHUNK])
        outs.append(_dot(_chunk(qklb, c), _tile4(db) * bdm) + ws[CHUNK:])
        s = s * sdec[c] + _dot_tn(db, _chunk(kdb, c)) * smask
    s_scr[...] = s
    _finish_prompt(jnp.concatenate(outs, axis=0), gate, normw, bd, o_ref)

    @pl.when(j == pl.num_programs(1) - 1)
    def _():
        _state_out(s, fold_ref, st_ref)


def _gdn_prompt(g, feat, normw, layer):
    c = _chunk_consts(HEAD_DIM)
    blk = min(MIX_TILE, g.t)
    nch = blk // CHUNK
    per = g.t // blk
    consts = [jnp.asarray(c["vmask"], BF16), jnp.asarray(c["smask"]), jnp.asarray(c["incl"]), jnp.asarray(c["strict"]),
              jnp.asarray(c["eye"]), jnp.asarray(c["fold"], BF16), jnp.asarray(_bd_ones(), BF16), normw]
    w = feat.shape[1]
    o, st = pl.pallas_call(
        functools.partial(_gdn_prompt_kernel, nch=nch, layer=layer),
        grid=(g.batch, per),
        in_specs=[pl.BlockSpec((blk, w), lambda b, j: (b * per + j, 0))] + [_const_spec(a) for a in consts],
        out_specs=[pl.BlockSpec((blk, HW), lambda b, j: (b * per + j, 0)),
                   pl.BlockSpec((1, HW, HEAD_DIM), lambda b, j: (b, 0, 0))],
        out_shape=[jax.ShapeDtypeStruct((g.rows, HW), BF16),
                   jax.ShapeDtypeStruct((g.batch, HW, HEAD_DIM), F32)],
        scratch_shapes=[pltpu.VMEM((HW, HW), F32)],
        compiler_params=_params(("arbitrary", "arbitrary")),
        name="gdn_prompt",
    )(feat, *consts)
    return o, st.reshape(g.batch, HEADS, HEAD_DIM, HEAD_DIM)


def _seq_tile(x, nb):
    w = x.shape[1]
    return jnp.broadcast_to(x.reshape(nb, 1, DEC_T, w), (nb, HEADS, DEC_T, w)).reshape(nb * HEADS * DEC_T, w)


def _gl_stack(glb):
    hi, mid, lo = (p.astype(F32) for p in _split3(glb))
    rid = lax.broadcasted_iota(jnp.int32, glb.shape, 0) % DEC_T
    return jnp.where(rid == 0, hi, jnp.where(rid == 1, mid, jnp.where(rid == 2, lo, 0.0)))


def _dec_finish(o, gate, normw, hmvalid):
    ms = jnp.mean(o * o, axis=-1, keepdims=True)
    on = o * lax.rsqrt(ms + RMS_EPS)
    if normw is not None:
        on = on * normw
    return on * gate * hmvalid


def _gla_decode_kernel(k_ref, v_ref, gate_ref, s_ref, qmask_ref, lvl_ref, rv_ref, hv_ref, retg_ref, nw_ref, o_ref,
                       so_ref, *, dk, ret, levels, has_norm, layer, nb):
    dkk = HEADS * dk
    n = HEADS * DEC_T
    qmask, rowvalid, hmvalid = qmask_ref[...], rv_ref[...], hv_ref[...]
    normw = nw_ref[layer:layer + 1, :] if has_norm else None
    kf = k_ref[...].reshape(nb * DEC_T, k_ref.shape[2])
    q = kf[:, 0:dkk] * rowvalid
    k = kf[:, dkk:2 * dkk] * rowvalid
    g = retg_ref[...] if ret else kf[:, 2 * dkk:3 * dkk] * rowvalid
    v = v_ref[...].reshape(nb * n, HEAD_DIM) * hmvalid
    vb = _bf(v)
    G = _cumsum_rows(g, DEC_T)

    def mm(x, y):
        return _dot_nt(_bf(_seq_tile(x, nb) * qmask), _bf(_seq_tile(y, nb)))

    att = _decay_att(q, k, G, mm, lvl_ref, levels, nb * n)
    o = _dot(_bf(att), vb)
    qgt = _bf(_seq_tile(q * jnp.exp(G), nb) * qmask)
    glb = _block_row(G, DEC_T, DEC_T - 1)
    kdt = _bf(_seq_tile(k * jnp.exp(glb - G), nb) * qmask)
    gst = _gl_stack(glb)
    ones8 = jnp.ones((DEC_T, HEAD_DIM), BF16)
    o_inter = []
    for e in range(nb):
        s = s_ref[0, e]
        o_inter.append(_dot(qgt[e * n:(e + 1) * n], _bf(s)))
        glcol = _dot_tn(_bf(gst[e * DEC_T:(e + 1) * DEC_T]), ones8)
        so_ref[e] = s * jnp.exp(glcol) + _dot_tn(kdt[e * n:(e + 1) * n], vb[e * n:(e + 1) * n])
    o = o + jnp.concatenate(o_inter, axis=0)
    fin = _dec_finish(o, gate_ref[...].reshape(nb * n, HEAD_DIM), normw, hmvalid)
    o_ref[...] = fin.reshape(nb, n, HEAD_DIM)


def _gla_decode(batch, kfeat, vfeat, gatefeat, state, dk, normw, layer, ret, t_valid):
    c = _dec_consts(dk, t_valid)
    dkk = HEADS * dk
    nb = DEC_NB
    n = HEADS * DEC_T
    consts = [jnp.asarray(c["qmask"]), jnp.asarray(c["lvl"]), jnp.asarray(c["rowvalid"]), jnp.asarray(c["hmvalid"]),
              jnp.asarray(c["retg"]), normw if normw is not None else jnp.ones((1, HEAD_DIM), F32)]
    kw = kfeat.shape[-1]
    o, st = pl.pallas_call(
        functools.partial(_gla_decode_kernel, dk=dk, ret=ret, levels=c["levels"], has_norm=normw is not None,
                          layer=layer, nb=nb),
        grid=(batch // nb,),
        in_specs=[pl.BlockSpec((nb, DEC_T, kw), lambda i: (i, 0, 0)),
                  pl.BlockSpec((nb, n, HEAD_DIM), lambda i: (i, 0, 0)),
                  pl.BlockSpec((nb, n, HEAD_DIM), lambda i: (i, 0, 0)),
                  pl.BlockSpec((1, nb, dkk, HEAD_DIM), lambda i: (layer, i, 0, 0))] + [_const_spec(a) for a in consts],
        out_specs=[pl.BlockSpec((nb, n, HEAD_DIM), lambda i: (i, 0, 0)),
                   pl.BlockSpec((nb, dkk, HEAD_DIM), lambda i: (i, 0, 0))],
        out_shape=[jax.ShapeDtypeStruct((batch, n, HEAD_DIM), F32),
                   jax.ShapeDtypeStruct((batch, dkk, HEAD_DIM), F32)],
        compiler_params=_params(("arbitrary",)),
        name="ret_decode" if ret else f"gla_decode_dk{dk}",
    )(kfeat, vfeat, gatefeat, state, *consts)
    return o, st


def _gdn_decode_kernel(k_ref, v_ref, gate_ref, s_ref, qmask_ref, incl_ref, strict_ref, eye_ref, rv_ref, hv_ref, nw_ref,
                       o_ref, so_ref, *, layer, nb, t_valid):
    qmask, incl, strict, eye = qmask_ref[...], incl_ref[...], strict_ref[...], eye_ref[...]
    rowvalid, hmvalid = rv_ref[...], hv_ref[...]
    normw = nw_ref[layer:layer + 1, :]
    n = HEADS * DEC_T
    nn = nb * n
    ones_w = jnp.full((HW, nn), 1.0 / HEAD_DIM, BF16)
    ones_sq = jnp.ones((nn, nn), BF16)

    kf = k_ref[...].reshape(nb * DEC_T, k_ref.shape[2])
    q = kf[:, 0:HW] * rowvalid
    k = kf[:, HW:2 * HW] * rowvalid
    be = kf[:, 2 * HW:3 * HW] * rowvalid
    ge = kf[:, 3 * HW:4 * HW] * rowvalid
    v = v_ref[...].reshape(nn, HEAD_DIM) * hmvalid
    G = _cumsum_rows(ge, DEC_T)

    def head_col(xe):
        h1, h2, h3 = _split3(_seq_tile(xe, nb) * qmask)
        return _dot(h1, ones_w) + _dot(h2, ones_w) + _dot(h3, ones_w)

    gcol = head_col(G)
    bcol = head_col(be)
    grow = _dot_exact01(ones_sq, gcol * eye)
    L = jnp.exp(jnp.minimum(gcol - grow, 0.0)) * incl
    kt = _bf(_seq_tile(k, nb))
    kk = _dot_nt(_bf(_seq_tile(k, nb) * qmask), kt)
    qk = _dot_nt(_bf(_seq_tile(q, nb) * qmask), kt)
    a = bcol * kk * L * strict
    tinv = eye - a
    p, cnt = a, 2
    while cnt < t_valid:
        p = _dot3(p, p)
        tinv = tinv + _dot3(tinv, p)
        cnt *= 2
    eg = jnp.exp(G)
    u = _dot3(tinv, bcol[:, 0:HEAD_DIM] * v)
    wrow = _dot3(tinv, _seq_tile(be * eg * k, nb)) * qmask
    wb = _bf(wrow)
    qgt = _bf(_seq_tile(q * eg, nb) * qmask)
    glb = _block_row(G, DEC_T, DEC_T - 1)
    kdt = _bf(_seq_tile(k * jnp.exp(glb - G), nb) * qmask)
    gst = _gl_stack(glb)
    ones8 = jnp.ones((DEC_T, HEAD_DIM), BF16)

    states = [s_ref[0, e] for e in range(nb)]
    ws = [_dot(jnp.concatenate([wb[e * n:(e + 1) * n], qgt[e * n:(e + 1) * n]], axis=0), _bf(states[e]))
          for e in range(nb)]
    delta = u - jnp.concatenate([m[:n] for m in ws], axis=0)
    db = _bf(delta)
    o = _dot(_bf(qk * L), db) + jnp.concatenate([m[n:] for m in ws], axis=0)
    for e in range(nb):
        glcol = _dot_tn(_bf(gst[e * DEC_T:(e + 1) * DEC_T]), ones8)
        so_ref[e] = states[e] * jnp.exp(glcol) + _dot_tn(kdt[e * n:(e + 1) * n], db[e * n:(e + 1) * n])
    fin = _dec_finish(o, gate_ref[...].reshape(nn, HEAD_DIM), normw, hmvalid)
    o_ref[...] = fin.reshape(nb, n, HEAD_DIM)


def _gdn_decode(batch, kfeat, vfeat, gatefeat, state, normw, layer, t_valid):
    c = _dec_consts(HEAD_DIM, t_valid)
    nb = DEC_NB
    n = HEADS * DEC_T
    consts = [jnp.asarray(c["qmask"]), jnp.asarray(c["incl"]), jnp.asarray(c["strict"]), jnp.asarray(c["eye"]),
              jnp.asarray(c["rowvalid"]), jnp.asarray(c["hmvalid"]), normw]
    kw = kfeat.shape[-1]
    o, st = pl.pallas_call(
        functools.partial(_gdn_decode_kernel, layer=layer, nb=nb, t_valid=t_valid),
        grid=(batch // nb,),
        in_specs=[pl.BlockSpec((nb, DEC_T, kw), lambda i: (i, 0, 0)),
                  pl.BlockSpec((nb, n, HEAD_DIM), lambda i: (i, 0, 0)),
                  pl.BlockSpec((nb, n, HEAD_DIM), lambda i: (i, 0, 0)),
                  pl.BlockSpec((1, nb, HW, HEAD_DIM), lambda i: (layer, i, 0, 0))] + [_const_spec(a) for a in consts],
        out_specs=[pl.BlockSpec((nb, n, HEAD_DIM), lambda i: (i, 0, 0)),
                   pl.BlockSpec((nb, HW, HEAD_DIM), lambda i: (i, 0, 0))],
        out_shape=[jax.ShapeDtypeStruct((batch, n, HEAD_DIM), F32),
                   jax.ShapeDtypeStruct((batch, HW, HEAD_DIM), F32)],
        compiler_params=_params(("arbitrary",)),
        name="gdn_decode",
    )(kfeat, vfeat, gatefeat, state, *consts)
    return o, st


def _rope_tables(pos):
    half = HEAD_DIM // 2
    inv = ROPE_BASE ** (-jnp.arange(half, dtype=F32) / half)
    ang = pos[:, None] * inv[None, :]
    cos, sin = jnp.cos(ang), jnp.sin(ang)
    cos_t = jnp.tile(jnp.concatenate([cos, cos], axis=1), (1, HEADS))
    sin_t = jnp.tile(jnp.concatenate([-sin, sin], axis=1), (1, HEADS))
    return cos_t, sin_t


def _relayout_w_in(w_in):
    parts = []
    for name, w in _SLABS:
        if name == "small":
            sm = jnp.concatenate([w_in[:, :, _SRC["alr"]:_SRC["alr"] + 16], w_in[:, :, _SRC["db"]:_SRC["db"] + 4],
                                  w_in[:, :, _SRC["da"]:_SRC["da"] + 4]], axis=2)
            parts.append(jnp.pad(sm, ((0, 0), (0, 0), (0, w - sm.shape[2]))))
        else:
            parts.append(w_in[:, :, _SRC[name]:_SRC[name] + w])
    return jnp.concatenate(parts, axis=2).astype(BF16)


def kernel(x_prompt, x_sample, state_ret, state_gla, state_hgrn, state_gdn, state_gdn_conv, c_prompt, c_sample, ada_w, ada_b, ln_g, ln_b, ffn1_wi, ffn1_wo, ffn2_wi, ffn2_wo, w_in, gla_wg, gla_bg, hg_lb, gdn_conv, gdn_a_log, gdn_dt_bias, gla_norm, hg_norm, gdn_norm, w_out):
    nl, d = ada_w.shape[0], ada_w.shape[1]
    bp, tp = x_prompt.shape[0], x_prompt.shape[1]
    bs, ts = x_sample.shape[0], x_sample.shape[1]
    alpha = (2.0 * nl) ** 0.25
    assert ts <= DEC_T and tp % CHUNK == 0 and ts >= CONV_W - 1 and bs % DEC_NB == 0

    wi1, wo1, wi2, wo2 = (w.astype(BF16) for w in (ffn1_wi, ffn1_wo, ffn2_wi, ffn2_wo))
    w_out_b = w_out.astype(BF16)
    w_in_p = _relayout_w_in(w_in)
    wg_p = jnp.pad(gla_wg, ((0, 0), (0, 128 - gla_wg.shape[1]), (0, 0))).astype(BF16)
    alog_e = jnp.repeat(gdn_a_log, HEAD_DIM, axis=1)
    dtb_e = jnp.repeat(gdn_dt_bias, HEAD_DIM, axis=1)
    gla_nw = jnp.tile(gla_norm, (1, HEADS))
    hg_nw = jnp.tile(hg_norm, (1, HEADS))
    gdn_nw = jnp.tile(gdn_norm, (1, HEADS))

    mod = _ada(jnp.concatenate([c_prompt, c_sample], axis=0), ada_w, ada_b)
    mod = mod.reshape(nl, bp + bs, N_MOD, d).transpose(0, 2, 1, 3)
    mod_p = mod[:, :, :bp].reshape(nl, N_MOD, bp, 1, d)
    mod_s = mod[:, :, bp:]

    gp = _Rows(bp, tp, decode=False)
    gs = _Rows(bs, DEC_T, decode=True)
    gp_prep = _Rows(bp, tp, decode=False, tile=PREP_TILE)
    gs_prep = _Rows(bs, DEC_T, decode=True, tile=PREP_TILE)
    cos_p, sin_p = _rope_tables(jnp.arange(tp, dtype=F32))
    cos_s, sin_s = _rope_tables(PAST_LEN + jnp.arange(DEC_T, dtype=F32))
    cos_s, sin_s = jnp.tile(cos_s, (bs, 1)), jnp.tile(sin_s, (bs, 1))

    xp = x_prompt.reshape(bp * tp, d)
    xs = jnp.pad(x_sample, ((0, 0), (0, DEC_T - ts), (0, 0))).reshape(bs * DEC_T, d)

    st_ret = state_ret.reshape(nl, bs, HW, HEAD_DIM)
    st_gla = state_gla.reshape(nl, bs, HEADS * GLA_DK, HEAD_DIM)
    st_hg = state_hgrn.reshape(nl, bs, HW, HEAD_DIM)
    st_gdn = state_gdn.reshape(nl, bs, HW, HEAD_DIM)

    p_states, s_states = [], []
    for l in range(nl):
        xp = _ffn(gp, xp, mod_p, (0, 1, 2), wi1, wo1, ln_g, ln_b, l, 0, alpha)
        ret_f, gla_f, hg_f, gdn_f, p_conv = _prep(gp_prep, xp, mod_p, w_in_p, cos_p, sin_p, wg_p, gla_bg, hg_lb,
                                                  gdn_conv, alog_e, dtb_e, None, l, nl)
        o_ret, p_ret = _gla_prompt(gp, ret_f, HEAD_DIM, None, l, ret=True)
        o_gla, p_gla = _gla_prompt(gp, gla_f, GLA_DK, gla_nw, l, ret=False)
        o_hg, p_hg = _gla_prompt(gp, hg_f, HEAD_DIM, hg_nw, l, ret=False)
        o_gdn, p_gdn = _gdn_prompt(gp, gdn_f, gdn_nw, l)
        xp = _mix_out(gp, xp, mod_p, (o_ret, o_gla, o_hg, o_gdn), w_out_b, ln_g, ln_b, l, alpha)
        xp = _ffn(gp, xp, mod_p, (6, 7, 8), wi2, wo2, ln_g, ln_b, l, 2, alpha)
        p_states.append((p_ret, p_gla, p_hg, p_gdn, p_conv))

        xs = _ffn(gs, xs, mod_s, (0, 1, 2), wi1, wo1, ln_g, ln_b, l, 0, alpha)
        cs = state_gdn_conv[l]
        hists = []
        for s in range(1, CONV_W):
            hrow = jnp.pad(cs[:, CONV_W - 1 - s:, :], ((0, 0), (0, DEC_T - s), (0, 0)))
            hists.append(hrow.reshape(bs * DEC_T, -1))
        outs = _prep(gs_prep, xs, mod_s, w_in_p, cos_s, sin_s, wg_p, gla_bg, hg_lb, gdn_conv, alog_e, dtb_e, hists,
                     l, nl)
        (ret_k, ret_v, ret_gt, gla_k, gla_v, gla_gt, hg_k, hg_v, hg_gt, gdn_k, gdn_v, gdn_gt, dq_raw) = outs

        def key3(a):
            return a.reshape(bs, DEC_T, a.shape[1])

        def hm3(a):
            return a.reshape(bs, HEADS * DEC_T, HEAD_DIM)

        o_ret, s_ret = _gla_decode(bs, key3(ret_k), hm3(ret_v), hm3(ret_gt), st_ret, HEAD_DIM, None, l, True, ts)
        o_gla, s_gla = _gla_decode(bs, key3(gla_k), hm3(gla_v), hm3(gla_gt), st_gla, GLA_DK, gla_norm, l, False, ts)
        o_hg, s_hg = _gla_decode(bs, key3(hg_k), hm3(hg_v), hm3(hg_gt), st_hg, HEAD_DIM, hg_norm, l, False, ts)
        o_gdn, s_gdn = _gdn_decode(bs, key3(gdn_k), hm3(gdn_v), hm3(gdn_gt), st_gdn, gdn_norm, l, ts)
        xs = _mix_out(gs, xs, mod_s, (o_ret, o_gla, o_hg, o_gdn), w_out_b, ln_g, ln_b, l, alpha)
        xs = _ffn(gs, xs, mod_s, (6, 7, 8), wi2, wo2, ln_g, ln_b, l, 2, alpha)
        s_conv = dq_raw.reshape(bs, DEC_T, -1)[:, ts - (CONV_W - 1):ts, :]
        s_states.append((s_ret.reshape(bs, HEADS, HEAD_DIM, HEAD_DIM), s_gla.reshape(bs, HEADS, GLA_DK, HEAD_DIM),
                         s_hg.reshape(bs, HEADS, HEAD_DIM, HEAD_DIM), s_gdn.reshape(bs, HEADS, HEAD_DIM, HEAD_DIM),
                         s_conv))

    y_prompt = xp.reshape(bp, tp, d)
    y_sample = xs.reshape(bs, DEC_T, d)[:, :ts, :]
    p_out = tuple(jnp.stack([st[i] for st in p_states]) for i in range(5))
    s_out = tuple(jnp.stack([st[i] for st in s_states]) for i in range(5))
    return (y_prompt, y_sample) + p_out + s_out
```

```python
import functools

import numpy as np
import jax
import jax.numpy as jnp
from jax import lax
from jax.experimental import pallas as pl
from jax.experimental.pallas import tpu as pltpu

F32 = jnp.float32
BF16 = jnp.bfloat16

HEAD_DIM = 64
HEADS = 4
GLA_DK = 32
GLA_GATE_NORM = 16.0
CONV_W = 4
CHUNK = 64
ROPE_BASE = 10000.0
LN_EPS = 1e-5
RMS_EPS = 1e-6
PAST_LEN = 16384
N_MOD = 9
HW = HEADS * HEAD_DIM

ROW_TILE = 512
PREP_TILE = 256
MIX_TILE = 512
DEC_T = 8
DEC_NB = 8
FF_CHUNK = 512
V7X_VMEM_BYTES = 64 * 1024 * 1024
VMEM_LIMIT = V7X_VMEM_BYTES - 12 * 1024 * 1024

_SLABS = (("rq", 256), ("rk", 256), ("rv", 256), ("rg", 256),
          ("aq", 128), ("ak", 128), ("av", 256), ("ag", 256),
          ("hq", 256), ("hf", 256), ("hi", 256), ("hg", 256),
          ("dqkv", 768), ("dg", 256), ("small", 128))
_OFF = {}
_o = 0
for _n, _w in _SLABS:
    _OFF[_n] = (_o, _w)
    _o += _w
PROJ_W = _o
_SRC = {"rq": 0, "rk": 256, "rv": 512, "rg": 768, "aq": 1024, "ak": 1152, "av": 1280, "alr": 1536,
        "ag": 1552, "hq": 1808, "hf": 2064, "hi": 2320, "hg": 2576, "dqkv": 2832, "db": 3600,
        "da": 3604, "dg": 3608}
SMALL_ALR, SMALL_DB, SMALL_DA = 0, 16, 20


def _bf(x):
    return x.astype(BF16)


def _dot(a, b):
    return jnp.dot(a, b, preferred_element_type=F32)


def _dot_nt(a, b):
    return lax.dot_general(a, b, (((1,), (1,)), ((), ())), preferred_element_type=F32)


def _dot_tn(a, b):
    return lax.dot_general(a, b, (((0,), (0,)), ((), ())), preferred_element_type=F32)


def _split2(x):
    hi = x.astype(BF16)
    lo = (x - hi.astype(F32)).astype(BF16)
    return hi, lo


def _split3(x):
    hi = x.astype(BF16)
    r = x - hi.astype(F32)
    mid = r.astype(BF16)
    lo = (r - mid.astype(F32)).astype(BF16)
    return hi, mid, lo


def _dot_x01(x, m01):
    hi, lo = _split2(x)
    return _dot(hi, m01) + _dot(lo, m01)


def _dot_exact01(m01, x):
    h1, h2, h3 = _split3(x)
    return _dot(m01, h1) + _dot(m01, h2) + _dot(m01, h3)


def _dot3(x, y):
    xh, xl = _split2(x)
    yh, yl = _split2(y)
    m = x.shape[0]
    top = _dot(jnp.concatenate([xh, xl], axis=0), yh)
    return top[:m] + top[m:] + _dot(xh, yl)


def _silu(x):
    return x * jax.nn.sigmoid(x)


def _softplus(x):
    return jnp.maximum(x, 0.0) + jnp.log1p(jnp.exp(-jnp.abs(x)))


def _ln(y, g, b):
    mu = jnp.mean(y, axis=-1, keepdims=True)
    yc = y - mu
    var = jnp.mean(yc * yc, axis=-1, keepdims=True)
    return yc * lax.rsqrt(var + LN_EPS) * g + b


def _mod_rows(ref, tm):
    v = ref[...]
    v = v.reshape(v.shape[-2], v.shape[-1])
    nb, d = v.shape
    if nb == 1:
        return v
    return jnp.broadcast_to(v[:, None, :], (nb, tm // nb, d)).reshape(tm, d)


def _tile4(x):
    return jnp.concatenate([x, x, x, x], axis=0)


def _per_block(x, const, period):
    n = x.shape[0] // period
    return (x.reshape(n, period, x.shape[1]) * const[None]).reshape(x.shape)


def _cumsum_rows(g, period):
    row = lax.broadcasted_iota(jnp.int32, g.shape, 0) % period
    out = g
    sh = 1
    while sh < period:
        out = out + jnp.where(row >= sh, pltpu.roll(out, sh, 0), 0.0)
        sh *= 2
    return out


def _block_row(G, period, r):
    c, w = G.shape
    g3 = G.reshape(c // period, period, w)
    return jnp.broadcast_to(g3[:, r:r + 1, :], g3.shape).reshape(c, w)


def _gref(G, m):
    c = G.shape[0]
    if 2 * m >= 8:
        return _block_row(G, 2 * m, m - 1)
    p = lax.broadcasted_iota(jnp.int32, G.shape, 0) % (2 * m)
    if m == 2:
        return jnp.where(p == 0, pltpu.roll(G, c - 1, 0),
                         jnp.where(p == 1, G, jnp.where(p == 2, pltpu.roll(G, 1, 0), pltpu.roll(G, 2, 0))))
    return jnp.where(p == 0, G, pltpu.roll(G, 1, 0))


def _decay_att(q, k, G, mm, lvl_ref, levels, period):
    att = _per_block(mm(q, k), lvl_ref[len(levels)], period)
    for i, m in enumerate(levels):
        gr = _gref(G, m)
        ql = q * jnp.exp(jnp.minimum(G - gr, 0.0))
        kl = k * jnp.exp(jnp.minimum(gr - G, 0.0))
        att = att + _per_block(mm(ql, kl), lvl_ref[i], period)
    return att


def _ret_gamma():
    return np.log(1.0 - 2.0 ** (-5.0 - np.arange(HEADS, dtype=np.float64)))


@functools.lru_cache(maxsize=None)
def _chunk_consts(dk):
    c = CHUNK
    h_att = np.arange(4 * c) // c
    s_att = np.arange(4 * c) % c
    t = np.arange(c)
    kmask = (h_att[:, None] == (np.arange(4 * dk) // dk)[None, :]).astype(np.float32)
    vmask = (h_att[:, None] == (np.arange(HW) // HEAD_DIM)[None, :]).astype(np.float32)
    smask = ((np.arange(HW) // HEAD_DIM)[:, None] == (np.arange(4 * dk) // dk)[None, :]).astype(np.float32)
    levels = (32, 16, 8, 4, 2, 1)
    lv = []
    for m in levels:
        same = (t[:, None] // (2 * m)) == (s_att[None, :] // (2 * m))
        lv.append(same & ((t[:, None] % (2 * m)) >= m) & ((s_att[None, :] % (2 * m)) < m))
    lv.append(t[:, None] == s_att[None, :])
    lvl = np.stack(lv).astype(np.float32)
    incl = (s_att[None, :] <= t[:, None]).astype(np.float32)
    strict = (s_att[None, :] < t[:, None]).astype(np.float32)
    eye = (s_att[None, :] == t[:, None]).astype(np.float32)
    fold = (np.arange(HW)[:, None] % HEAD_DIM == np.arange(HEAD_DIM)[None, :]).astype(np.float32)
    gam = _ret_gamma()
    gh = gam[h_att]
    ret_att = np.where(s_att[None, :] <= t[:, None], np.exp((t[:, None] - s_att[None, :]) * gh[None, :]), 0.0)
    gl = gam[np.arange(HW) // HEAD_DIM]
    ret_q = np.exp((t[:, None] + 1) * gl[None, :])
    ret_k = np.exp((c - 1 - t[:, None]) * gl[None, :])
    ret_s = np.exp(c * gl)[None, :]
    return dict(kmask=kmask, vmask=vmask, smask=smask, lvl=lvl, levels=levels, incl=incl, strict=strict,
                eye=eye, fold=fold, ret_att=ret_att.astype(np.float32), ret_q=ret_q.astype(np.float32),
                ret_k=ret_k.astype(np.float32), ret_s=ret_s.astype(np.float32))


@functools.lru_cache(maxsize=None)
def _dec_consts(dk, t_valid):
    n = DEC_NB * HEADS * DEC_T
    ee = np.arange(n) // (HEADS * DEC_T)
    hh = (np.arange(n) // DEC_T) % HEADS
    tt = np.arange(n) % DEC_T
    qmask = (hh[:, None] == (np.arange(4 * dk) // dk)[None, :]).astype(np.float32)
    same = (ee[:, None] == ee[None, :]) & (hh[:, None] == hh[None, :])
    levels = (2, 1)
    assert t_valid <= 4
    lv = []
    for m in levels:
        blk = (tt[:, None] // (2 * m)) == (tt[None, :] // (2 * m))
        lv.append(same & blk & ((tt[:, None] % (2 * m)) >= m) & ((tt[None, :] % (2 * m)) < m))
    lv.append(same & (tt[:, None] == tt[None, :]))
    lvl = np.stack(lv).astype(np.float32)
    incl = (same & (tt[None, :] <= tt[:, None])).astype(np.float32)
    strict = (same & (tt[None, :] < tt[:, None])).astype(np.float32)
    eye = np.eye(n, dtype=np.float32)
    rowvalid = np.tile((np.arange(DEC_T) < t_valid).astype(np.float32)[:, None], (DEC_NB, 1))
    hmvalid = (tt < t_valid).astype(np.float32)[:, None]
    retg = (rowvalid * _ret_gamma()[np.arange(HW) // HEAD_DIM][None, :]).astype(np.float32)
    return dict(qmask=qmask, lvl=lvl, levels=levels, incl=incl, strict=strict, eye=eye,
                rowvalid=rowvalid, hmvalid=hmvalid, retg=retg)


def _bd_ones():
    h = np.arange(HW) // HEAD_DIM
    return (h[:, None] == h[None, :]).astype(np.float32)


def _expand_mat():
    e = np.zeros((128, 2 * HW), np.float32)
    for h in range(HEADS):
        e[SMALL_DB + h, h * HEAD_DIM:(h + 1) * HEAD_DIM] = 1.0
        e[SMALL_DA + h, HW + h * HEAD_DIM:HW + (h + 1) * HEAD_DIM] = 1.0
    return e


def _const_spec(a):
    nd = a.ndim
    return pl.BlockSpec(a.shape, lambda *_: (0,) * nd)


def _params(sem):
    return pltpu.CompilerParams(dimension_semantics=sem, vmem_limit_bytes=VMEM_LIMIT)


def _ada_kernel(c_ref, w_ref, b_ref, o_ref):
    c = c_ref[...]
    o_ref[0] = _dot(_bf(_silu(c)), _bf(w_ref[0])) + b_ref[0]


def _ada(c_all, ada_w, ada_b):
    nl, d, nd = ada_w.shape
    nb = c_all.shape[0]
    tn = d
    return pl.pallas_call(
        _ada_kernel,
        grid=(nl, nd // tn),
        in_specs=[pl.BlockSpec((nb, d), lambda l, j: (0, 0)),
                  pl.BlockSpec((1, d, tn), lambda l, j: (l, 0, j)),
                  pl.BlockSpec((1, 1, tn), lambda l, j: (l, 0, j))],
        out_specs=pl.BlockSpec((1, nb, tn), lambda l, j: (l, 0, j)),
        out_shape=jax.ShapeDtypeStruct((nl, nb, nd), F32),
        compiler_params=_params(("arbitrary", "arbitrary")),
        name="ada_mod",
    )(c_all, ada_w, ada_b.reshape(nl, 1, nd))


class _Rows:
    def __init__(self, batch, t, decode, tile=ROW_TILE):
        self.batch, self.t, self.decode = batch, t, decode
        self.rows = batch * t
        self.tm = min(tile, self.rows)
        assert self.rows % self.tm == 0
        if decode:
            assert self.tm % t == 0
        else:
            assert t % self.tm == 0
        self.steps = self.rows // self.tm


def _mod_spec(g, layer, k, d):
    if g.decode:
        nb = g.tm // g.t
        return pl.BlockSpec((1, 1, nb, d), lambda i: (layer, k, i, 0))
    per = g.t // g.tm
    return pl.BlockSpec((1, 1, 1, 1, d), lambda i: (layer, k, i // per, 0, 0))


def _ffn_kernel(x_ref, sh_ref, sc_ref, gt_ref, wi_ref, wo_ref, lng_ref, lnb_ref, o_ref, *, alpha, ln_idx, layer):
    x = x_ref[...]
    tm, d = x.shape
    dff = wo_ref.shape[1]
    sh, sc, gt = _mod_rows(sh_ref, tm), _mod_rows(sc_ref, tm), _mod_rows(gt_ref, tm)
    h = _bf(x * (1.0 + sc) + sh)
    acc = jnp.zeros((tm, d), F32)
    off = 0
    while off < dff:
        fc = min(FF_CHUNK, dff - off)
        a = _dot(h, wi_ref[0, :, off:off + fc])
        b = _dot(h, wi_ref[0, :, dff + off:dff + off + fc])
        acc = acc + _dot(_bf(_silu(a) * b), wo_ref[0, off:off + fc, :])
        off += fc
    y = alpha * x + 0.5 * (1.0 + gt) * acc
    o_ref[...] = _ln(y, lng_ref[layer, ln_idx:ln_idx + 1, :], lnb_ref[layer, ln_idx:ln_idx + 1, :])


def _ffn(g, x, mod, mods, wi, wo, ln_g, ln_b, layer, ln_idx, alpha):
    d = x.shape[1]
    dff = wo.shape[1]
    row = pl.BlockSpec((g.tm, d), lambda i: (i, 0))
    return pl.pallas_call(
        functools.partial(_ffn_kernel, alpha=alpha, ln_idx=ln_idx, layer=layer),
        grid=(g.steps,),
        in_specs=[row] + [_mod_spec(g, layer, k, d) for k in mods] + [
            pl.BlockSpec((1, d, 2 * dff), lambda i: (layer, 0, 0), pipeline_mode=pl.Buffered(1)),
            pl.BlockSpec((1, dff, d), lambda i: (layer, 0, 0), pipeline_mode=pl.Buffered(1)),
            _const_spec(ln_g), _const_spec(ln_b)],
        out_specs=row,
        out_shape=jax.ShapeDtypeStruct(x.shape, F32),
        compiler_params=_params(("arbitrary",)),
        name=f"ffn{ln_idx}",
    )(x, mod, mod, mod, wi, wo, ln_g, ln_b)


def _mix_out_kernel(x_ref, gt_ref, o0_ref, o1_ref, o2_ref, o3_ref, w_ref, lng_ref, lnb_ref, y_ref, *, alpha, layer,
                    decode):
    x = x_ref[...]
    tm, d = x.shape
    gt = _mod_rows(gt_ref, tm)
    acc = jnp.zeros((tm, d), F32)
    for mi, o_ref in enumerate((o0_ref, o1_ref, o2_ref, o3_ref)):
        if decode:
            o = o_ref[...]
            for h in range(HEADS):
                oh = o[:, h * DEC_T:(h + 1) * DEC_T, :].reshape(tm, HEAD_DIM)
                r0 = mi * HW + h * HEAD_DIM
                acc = acc + _dot(_bf(oh), w_ref[0, r0:r0 + HEAD_DIM, :])
        else:
            acc = acc + _dot(o_ref[...], w_ref[0, mi * HW:(mi + 1) * HW, :])
    y = alpha * x + (1.0 + gt) * acc
    y_ref[...] = _ln(y, lng_ref[layer, 1:2, :], lnb_ref[layer, 1:2, :])


def _mix_out(g, x, mod, outs, w_out, ln_g, ln_b, layer, alpha):
    d = x.shape[1]
    row = pl.BlockSpec((g.tm, d), lambda i: (i, 0))
    if g.decode:
        nb = g.tm // g.t
        ospec = pl.BlockSpec((nb, HEADS * DEC_T, HEAD_DIM), lambda i: (i, 0, 0))
    else:
        ospec = pl.BlockSpec((g.tm, HW), lambda i: (i, 0))
    return pl.pallas_call(
        functools.partial(_mix_out_kernel, alpha=alpha, layer=layer, decode=g.decode),
        grid=(g.steps,),
        in_specs=[row, _mod_spec(g, layer, 5, d), ospec, ospec, ospec, ospec,
                  pl.BlockSpec((1, w_out.shape[1], d), lambda i: (layer, 0, 0), pipeline_mode=pl.Buffered(1)),
                  _const_spec(ln_g), _const_spec(ln_b)],
        out_specs=row,
        out_shape=jax.ShapeDtypeStruct(x.shape, F32),
        compiler_params=_params(("arbitrary",)),
        name="mix_out",
    )(x, mod, *outs, w_out, ln_g, ln_b)


def _prep_kernel(*refs, layer, nlayers, g_t, tm, decode):
    (x_ref, sh_ref, sc_ref, w_ref, cos_ref, sin_ref, wg_ref, bg_ref, lb_ref, cw_ref, alog_ref, dtb_ref,
     bd_ref, ex_ref) = refs[:14]
    rest = refs[14:]
    if decode:
        h1_ref, h2_ref, h3_ref = rest[:3]
        outs = rest[3:]
        cbuf = None
    else:
        outs = rest[:-1]
        cbuf = rest[-1]

    x = x_ref[...]
    hb = _bf(x * (1.0 + _mod_rows(sc_ref, tm)) + _mod_rows(sh_ref, tm))

    def proj(name):
        o, w = _OFF[name]
        return _dot(hb, w_ref[0, :, o:o + w])

    bd = bd_ref[...]

    dest = {}
    if decode:
        nb = tm // g_t
        it = iter(outs)
        for mix, keys, widths in (("ret", ("q", "k"), (HW, HW)), ("gla", ("q", "k", "g"), (HEADS * GLA_DK,) * 3),
                                  ("hg", ("q", "k", "g"), (HW,) * 3), ("gdn", ("q", "k", "b", "g"), (HW,) * 4)):
            k_ref, v_ref, gate_ref = next(it), next(it), next(it)
            off = 0
            for kk, w in zip(keys, widths):
                dest[f"{mix}_{kk}"] = ("rows", k_ref, off)
                off += w
            dest[f"{mix}_v"] = ("heads", v_ref, 0)
            dest[f"{mix}_gate"] = ("heads", gate_ref, 0)
        dq_ref = next(it)
    else:
        ret_ref, gla_ref, hg_ref, gdn_ref, conv_ref = outs
        for dst, keys, widths in ((ret_ref, ("ret_q", "ret_k", "ret_v", "ret_gate"), (HW,) * 4),
                                  (gla_ref, ("gla_q", "gla_k", "gla_g", "gla_v", "gla_gate"),
                                   (HEADS * GLA_DK,) * 3 + (HW, HW)),
                                  (hg_ref, ("hg_q", "hg_k", "hg_g", "hg_v", "hg_gate"), (HW,) * 5),
                                  (gdn_ref, ("gdn_q", "gdn_k", "gdn_b", "gdn_g", "gdn_v", "gdn_gate"), (HW,) * 6)):
            off = 0
            for kk, w in zip(keys, widths):
                dest[kk] = ("rows", dst, off)
                off += w

    def put(name, val):
        kind, ref, off = dest[name]
        if kind == "rows":
            ref[:, off:off + val.shape[1]] = val
        else:
            for h in range(HEADS):
                ref[:, h] = val[:, h * HEAD_DIM:(h + 1) * HEAD_DIM].reshape(nb, g_t, HEAD_DIM)

    cos, sin = cos_ref[...], sin_ref[...]
    lane = lax.broadcasted_iota(jnp.int32, (tm, HW), 1)
    first_half = (lane % HEAD_DIM) < (HEAD_DIM // 2)

    def rope(v):
        sw = jnp.where(first_half, pltpu.roll(v, HW - HEAD_DIM // 2, 1), pltpu.roll(v, HEAD_DIM // 2, 1))
        return v * cos + sw * sin

    put("ret_q", rope(proj("rq")))
    put("ret_k", rope(proj("rk")) * (HEAD_DIM ** -0.5))
    put("ret_v", proj("rv"))
    put("ret_gate", _silu(proj("rg")))

    small = proj("small")
    gpre = _dot(_bf(small), wg_ref[0]) + bg_ref[layer:layer + 1, :]
    put("gla_q", proj("aq") * (GLA_DK ** -0.5))
    put("gla_k", proj("ak"))
    put("gla_g", -_softplus(-gpre) * (1.0 / GLA_GATE_NORM))
    put("gla_v", proj("av"))
    put("gla_gate", _silu(proj("ag")))

    rows = [lb_ref[i:i + 1, :] for i in range(nlayers)]
    mx = functools.reduce(jnp.maximum, rows)
    ex = [jnp.exp(r - mx) for r in rows]
    tot = functools.reduce(lambda a, b: a + b, ex)
    plb = [e / tot for e in ex]
    lb = functools.reduce(lambda a, b: a + b, plb[:layer + 1]) - plb[0]
    zf = proj("hf")
    put("hg_q", _silu(proj("hq")) * (HEAD_DIM ** -0.5))
    put("hg_k", (1.0 - lb) * jax.nn.sigmoid(-zf))
    put("hg_g", jnp.log(lb + (1.0 - lb) * jax.nn.sigmoid(zf)))
    put("hg_v", proj("hi"))
    put("hg_gate", _silu(proj("hg")))

    dq = proj("dqkv")
    cw = cw_ref[0]
    if decode:
        dq_ref[...] = dq
        tpos = lax.broadcasted_iota(jnp.int32, dq.shape, 0) % g_t
        conv = dq * cw[CONV_W - 1:CONV_W, :]
        for s, h_ref in ((1, h1_ref), (2, h2_ref), (3, h3_ref)):
            prev = jnp.where(tpos >= s, pltpu.roll(dq, s, 0), h_ref[...])
            conv = conv + prev * cw[CONV_W - 1 - s:CONV_W - s, :]
    else:
        i = pl.program_id(0)
        per = g_t // tm

        @pl.when(i % per == 0)
        def _():
            cbuf[0:8, :] = jnp.zeros((8, dq.shape[1]), F32)

        cbuf[8:8 + tm, :] = dq
        conv = cbuf[pl.ds(8, tm), :] * cw[CONV_W - 1:CONV_W, :]
        for s in range(1, CONV_W):
            conv = conv + cbuf[pl.ds(8 - s, tm), :] * cw[CONV_W - 1 - s:CONV_W - s, :]
        tail = cbuf[tm:tm + 8, :]
        cbuf[0:8, :] = tail

        @pl.when(i % per == per - 1)
        def _():
            conv_ref[0] = tail[8 - (CONV_W - 1):8, :]

    u = _silu(conv)
    uq, uk = u[:, 0:HW], u[:, HW:2 * HW]
    put("gdn_q", uq * lax.rsqrt(_dot_x01(uq * uq, bd) + RMS_EPS) * (HEAD_DIM ** -0.5))
    put("gdn_k", uk * lax.rsqrt(_dot_x01(uk * uk, bd) + RMS_EPS))
    put("gdn_v", u[:, 2 * HW:3 * HW])
    dbda = _dot_x01(small, ex_ref[...])
    put("gdn_b", jax.nn.sigmoid(dbda[:, 0:HW]))
    put("gdn_g", -jnp.exp(alog_ref[layer:layer + 1, :]) * _softplus(dbda[:, HW:2 * HW] + dtb_ref[layer:layer + 1, :]))
    put("gdn_gate", _silu(proj("dg")))


def _prep(g, x, mod, w_in_p, cos_t, sin_t, wg_p, gla_bg, hg_lb, gdn_conv, alog_e, dtb_e, hists, layer, nlayers):
    d = x.shape[1]
    bd = jnp.asarray(_bd_ones(), BF16)
    exm = jnp.asarray(_expand_mat(), BF16)
    row = pl.BlockSpec((g.tm, d), lambda i: (i, 0))
    if g.decode:
        tab = pl.BlockSpec((g.tm, HW), lambda i: (i, 0))
    else:
        per = g.t // g.tm
        tab = pl.BlockSpec((g.tm, HW), lambda i: (i % per, 0))
    in_specs = [row, _mod_spec(g, layer, 3, d), _mod_spec(g, layer, 4, d),
                pl.BlockSpec((1, d, PROJ_W), lambda i: (layer, 0, 0), pipeline_mode=pl.Buffered(1)),
                tab, tab,
                pl.BlockSpec((1,) + wg_p.shape[1:], lambda i: (layer, 0, 0)),
                _const_spec(gla_bg), _const_spec(hg_lb),
                pl.BlockSpec((1,) + gdn_conv.shape[1:], lambda i: (layer, 0, 0)),
                _const_spec(alog_e), _const_spec(dtb_e), _const_spec(bd), _const_spec(exm)]
    args = [x, mod, mod, w_in_p, cos_t, sin_t, wg_p, gla_bg, hg_lb, gdn_conv, alog_e, dtb_e, bd, exm]
    cch = gdn_conv.shape[2]

    def rows_out(w):
        return jax.ShapeDtypeStruct((g.rows, w), F32), pl.BlockSpec((g.tm, w), lambda i: (i, 0))

    if g.decode:
        nb = g.tm // g.t
        hspec = pl.BlockSpec((g.tm, cch), lambda i: (i, 0))
        in_specs += [hspec, hspec, hspec]
        args += list(hists)
        hm = (jax.ShapeDtypeStruct((g.batch, HEADS, g.t, HEAD_DIM), F32),
              pl.BlockSpec((nb, HEADS, g.t, HEAD_DIM), lambda i: (i, 0, 0, 0)))
        outs = []
        for kw in (2 * HW, 3 * HEADS * GLA_DK, 3 * HW, 4 * HW):
            outs += [rows_out(kw), hm, hm]
        outs.append(rows_out(cch))
        scratch = []
    else:
        outs = [rows_out(4 * HW), rows_out(3 * HEADS * GLA_DK + 2 * HW), rows_out(5 * HW), rows_out(6 * HW),
                (jax.ShapeDtypeStruct((g.batch, CONV_W - 1, cch), F32),
                 pl.BlockSpec((1, CONV_W - 1, cch), lambda i: (i // (g.t // g.tm), 0, 0)))]
        scratch = [pltpu.VMEM((g.tm + 8, cch), F32)]
    return pl.pallas_call(
        functools.partial(_prep_kernel, layer=layer, nlayers=nlayers, g_t=g.t, tm=g.tm, decode=g.decode),
        grid=(g.steps,),
        in_specs=in_specs,
        out_specs=[o[1] for o in outs],
        out_shape=[o[0] for o in outs],
        scratch_shapes=scratch,
        compiler_params=_params(("arbitrary",)),
        name="prep",
    )(*args)


def _chunk(x, c):
    return x[c * CHUNK:(c + 1) * CHUNK]


def _per_chunk(fn, nch):
    return jnp.concatenate([fn(c) for c in range(nch)], axis=0)


def _finish_prompt(o, gate, normw, bd, o_ref):
    ms = _dot_x01(o * o, bd) * (1.0 / HEAD_DIM)
    on = o * lax.rsqrt(ms + RMS_EPS)
    if normw is not None:
        on = on * normw
    o_ref[...] = _bf(on * gate)


def _state_out(s_t, fold_ref, st_ref):
    fold = fold_ref[...]
    h1, h2, h3 = _split3(s_t)
    st_ref[0] = _dot_tn(h1, fold) + _dot_tn(h2, fold) + _dot_tn(h3, fold)


def _gla_prompt_kernel(f_ref, kmask_ref, vmask_ref, smask_ref, lvl_ref, fold_ref, bd_ref, nw_ref, ratt_ref, rq_ref,
                       rk_ref, rs_ref, o_ref, st_ref, s_scr, *, dk, nch, ret, levels, has_norm, layer):
    j = pl.program_id(1)
    dkk = HEADS * dk

    @pl.when(j == 0)
    def _():
        s_scr[...] = jnp.zeros(s_scr.shape, F32)

    kmask, vmask, smask, bd = kmask_ref[...], vmask_ref[...], smask_ref[...], bd_ref[...]
    normw = nw_ref[layer:layer + 1, :] if has_norm else None

    def mm(x, y):
        xb, yb = _bf(x), _bf(y)
        return _per_chunk(lambda c: _dot_nt(_chunk(xb, c), _tile4(_chunk(yb, c)) * kmask), nch)

    q = f_ref[:, 0:dkk]
    k = f_ref[:, dkk:2 * dkk]
    if ret:
        v = f_ref[:, 2 * dkk:2 * dkk + HW]
        gate = f_ref[:, 2 * dkk + HW:2 * dkk + 2 * HW]
        att = _per_block(mm(q, k), ratt_ref[...], CHUNK)
        qg = _per_block(q, rq_ref[...], CHUNK)
        kd = _per_block(k, rk_ref[...], CHUNK)
        sdec = [rs_ref[...]] * nch
    else:
        g = f_ref[:, 2 * dkk:3 * dkk]
        v = f_ref[:, 3 * dkk:3 * dkk + HW]
        gate = f_ref[:, 3 * dkk + HW:3 * dkk + 2 * HW]
        G = _cumsum_rows(g, CHUNK)
        att = _decay_att(q, k, G, mm, lvl_ref, levels, CHUNK)
        qg = q * jnp.exp(G)
        glb = _block_row(G, CHUNK, CHUNK - 1)
        kd = k * jnp.exp(glb - G)
        sdec = [jnp.exp(G[(c + 1) * CHUNK - 1:(c + 1) * CHUNK, :]) for c in range(nch)]
    vb, qgb, kdb, attb = _bf(v), _bf(qg), _bf(kd), _bf(att)
    o_intra = _per_chunk(lambda c: _dot(_chunk(attb, c), _tile4(_chunk(vb, c)) * vmask), nch)
    upd = [_dot_tn(_chunk(vb, c), _chunk(kdb, c)) * smask for c in range(nch)]

    s = s_scr[...]
    o_inter = []
    for c in range(nch):
        o_inter.append(_dot_nt(_chunk(qgb, c), _bf(s)))
        s = s * sdec[c] + upd[c]
    s_scr[...] = s
    _finish_prompt(o_intra + jnp.concatenate(o_inter, axis=0), gate, normw, bd, o_ref)

    @pl.when(j == pl.num_programs(1) - 1)
    def _():
        _state_out(s, fold_ref, st_ref)


def _gla_prompt(g, feat, dk, normw, layer, ret):
    c = _chunk_consts(dk)
    dkk = HEADS * dk
    blk = min(MIX_TILE, g.t)
    nch = blk // CHUNK
    per = g.t // blk
    consts = [jnp.asarray(c["kmask"], BF16), jnp.asarray(c["vmask"], BF16), jnp.asarray(c["smask"]),
              jnp.asarray(c["lvl"]), jnp.asarray(c["fold"], BF16), jnp.asarray(_bd_ones(), BF16),
              normw if normw is not None else jnp.ones((1, HW), F32),
              jnp.asarray(c["ret_att"]), jnp.asarray(c["ret_q"]), jnp.asarray(c["ret_k"]), jnp.asarray(c["ret_s"])]
    w = feat.shape[1]
    o, st = pl.pallas_call(
        functools.partial(_gla_prompt_kernel, dk=dk, nch=nch, ret=ret, levels=c["levels"],
                          has_norm=normw is not None, layer=layer),
        grid=(g.batch, per),
        in_specs=[pl.BlockSpec((blk, w), lambda b, j: (b * per + j, 0))] + [_const_spec(a) for a in consts],
        out_specs=[pl.BlockSpec((blk, HW), lambda b, j: (b * per + j, 0)),
                   pl.BlockSpec((1, dkk, HEAD_DIM), lambda b, j: (b, 0, 0))],
        out_shape=[jax.ShapeDtypeStruct((g.rows, HW), BF16),
                   jax.ShapeDtypeStruct((g.batch, dkk, HEAD_DIM), F32)],
        scratch_shapes=[pltpu.VMEM((HW, dkk), F32)],
        compiler_params=_params(("arbitrary", "arbitrary")),
        name="ret_prompt" if ret else f"gla_prompt_dk{dk}",
    )(feat, *consts)
    return o, st.reshape(g.batch, HEADS, dk, HEAD_DIM)


def _gdn_prompt_kernel(f_ref, bdm_ref, smask_ref, incl_ref, strict_ref, eye_ref, fold_ref, bd_ref, nw_ref, o_ref,
                       st_ref, s_scr, *, nch, layer):
    j = pl.program_id(1)

    @pl.when(j == 0)
    def _():
        s_scr[...] = jnp.zeros(s_scr.shape, F32)

    bdm, smask, incl, strict, eye, bd = (bdm_ref[...], smask_ref[...], incl_ref[...], strict_ref[...], eye_ref[...],
                                         bd_ref[...])
    normw = nw_ref[layer:layer + 1, :]
    ones_c = jnp.ones((CHUNK, CHUNK), BF16)

    def bdw(yb, c):
        return _tile4(_chunk(yb, c)) * bdm

    def prod1(x, y):
        xb, yb = _bf(x), _bf(y)
        return _per_chunk(lambda c: _dot(_chunk(xb, c), bdw(yb, c)), nch)

    def prod1_pair(x1, x2, y):
        x1b, x2b, yb = _bf(x1), _bf(x2), _bf(y)
        both = [_dot(jnp.concatenate([_chunk(x1b, c), _chunk(x2b, c)], axis=0), bdw(yb, c)) for c in range(nch)]
        return (jnp.concatenate([m[:CHUNK] for m in both], axis=0),
                jnp.concatenate([m[CHUNK:] for m in both], axis=0))

    def prod2(x, y):
        xh, xl = _split2(x)
        yb = _bf(y)

        def one(c):
            top = _dot(jnp.concatenate([_chunk(xh, c), _chunk(xl, c)], axis=0), bdw(yb, c))
            return top[:CHUNK] + top[CHUNK:]

        return _per_chunk(one, nch)

    def prod3(x, y):
        xh, xl = _split2(x)
        yh, yl = _split2(y)

        def one(c):
            top = _dot(jnp.concatenate([_chunk(xh, c), _chunk(xl, c)], axis=0), bdw(yh, c))
            return top[:CHUNK] + top[CHUNK:] + _dot(_chunk(xh, c), bdw(yl, c))

        return _per_chunk(one, nch)

    q, k = f_ref[:, 0:HW], f_ref[:, HW:2 * HW]
    be, ge = f_ref[:, 2 * HW:3 * HW], f_ref[:, 3 * HW:4 * HW]
    v, gate = f_ref[:, 4 * HW:5 * HW], f_ref[:, 5 * HW:6 * HW]
    G = _cumsum_rows(ge, CHUNK)
    zd = _per_block(G, eye, CHUNK)
    grow = _per_chunk(lambda c: _dot_exact01(ones_c, _chunk(zd, c)), nch)
    L = _per_block(jnp.exp(jnp.minimum(G - grow, 0.0)), incl, CHUNK)
    qb, kb = _bf(q), _bf(k)

    def qkk(c):
        return _dot_nt(jnp.concatenate([_chunk(qb, c), _chunk(kb, c)], axis=0), bdw(kb, c))

    qkk_all = [qkk(c) for c in range(nch)]
    qk = jnp.concatenate([m[:CHUNK] for m in qkk_all], axis=0)
    kk = jnp.concatenate([m[CHUNK:] for m in qkk_all], axis=0)
    a = _per_block(be * kk * L, strict, CHUNK)
    eye_b = _per_block(jnp.ones_like(a), eye, CHUNK)
    p = -a
    tinv = eye_b + p
    for i in range(5):
        if i == 0:
            p = prod1(p, p)
        else:
            p, dt = prod1_pair(p, tinv, p)
            tinv = tinv + dt
    tinv = tinv + prod1(tinv, p)
    resid = eye_b - tinv - prod3(a, tinv)
    tinv = tinv + prod1(tinv, resid)
    eg = jnp.exp(G)
    u = prod2(tinv, be * v)
    w = prod2(tinv, be * eg * k)
    glb = _block_row(G, CHUNK, CHUNK - 1)
    wb, qeb, qklb = _bf(w), _bf(q * eg), _bf(qk * L)
    kdb = _bf(k * jnp.exp(glb - G))
    sdec = [jnp.exp(G[(c + 1) * CHUNK - 1:(c + 1) * CHUNK, :]) for c in range(nch)]

    s = s_scr[...]
    outs = []
    for c in range(nch):
        ws = _dot_nt(jnp.concatenate([_chunk(wb, c), _chunk(qeb, c)], axis=0), _bf(s))
        db = _bf(_chunk(u, c) - ws[:CHUNK])
        outs.append(_dot(_chunk(qklb, c), _tile4(db) * bdm) + ws[CHUNK:])
        s = s * sdec[c] + _dot_tn(db, _chunk(kdb, c)) * smask
    s_scr[...] = s
    _finish_prompt(jnp.concatenate(outs, axis=0), gate, normw, bd, o_ref)

    @pl.when(j == pl.num_programs(1) - 1)
    def _():
        _state_out(s, fold_ref, st_ref)


def _gdn_prompt(g, feat, normw, layer):
    c = _chunk_consts(HEAD_DIM)
    blk = min(MIX_TILE, g.t)
    nch = blk // CHUNK
    per = g.t // blk
    consts = [jnp.asarray(c["vmask"], BF16), jnp.asarray(c["smask"]), jnp.asarray(c["incl"]), jnp.asarray(c["strict"]),
              jnp.asarray(c["eye"]), jnp.asarray(c["fold"], BF16), jnp.asarray(_bd_ones(), BF16), normw]
    w = feat.shape[1]
    o, st = pl.pallas_call(
        functools.partial(_gdn_prompt_kernel, nch=nch, layer=layer),
        grid=(g.batch, per),
        in_specs=[pl.BlockSpec((blk, w), lambda b, j: (b * per + j, 0))] + [_const_spec(a) for a in consts],
        out_specs=[pl.BlockSpec((blk, HW), lambda b, j: (b * per + j, 0)),
                   pl.BlockSpec((1, HW, HEAD_DIM), lambda b, j: (b, 0, 0))],
        out_shape=[jax.ShapeDtypeStruct((g.rows, HW), BF16),
                   jax.ShapeDtypeStruct((g.batch, HW, HEAD_DIM), F32)],
        scratch_shapes=[pltpu.VMEM((HW, HW), F32)],
        compiler_params=_params(("arbitrary", "arbitrary")),
        name="gdn_prompt",
    )(feat, *consts)
    return o, st.reshape(g.batch, HEADS, HEAD_DIM, HEAD_DIM)


def _seq_tile(x, nb):
    w = x.shape[1]
    return jnp.broadcast_to(x.reshape(nb, 1, DEC_T, w), (nb, HEADS, DEC_T, w)).reshape(nb * HEADS * DEC_T, w)


def _gl_stack(glb):
    hi, mid, lo = (p.astype(F32) for p in _split3(glb))
    rid = lax.broadcasted_iota(jnp.int32, glb.shape, 0) % DEC_T
    return jnp.where(rid == 0, hi, jnp.where(rid == 1, mid, jnp.where(rid == 2, lo, 0.0)))


def _dec_finish(o, gate, normw, hmvalid):
    ms = jnp.mean(o * o, axis=-1, keepdims=True)
    on = o * lax.rsqrt(ms + RMS_EPS)
    if normw is not None:
        on = on * normw
    return on * gate * hmvalid


def _state_io(refs, layer):
    if layer == 0:
        return refs
    prev_ref, rest = refs[0], refs[1:]
    rest[-1][0:layer] = prev_ref[...]
    return rest


def _gla_decode_kernel(k_ref, v_ref, gate_ref, s_ref, *refs, dk, ret, levels, has_norm, layer, nb):
    qmask_ref, lvl_ref, rv_ref, hv_ref, retg_ref, nw_ref, o_ref, so_ref = _state_io(refs, layer)
    dkk = HEADS * dk
    n = HEADS * DEC_T
    qmask, rowvalid, hmvalid = qmask_ref[...], rv_ref[...], hv_ref[...]
    normw = nw_ref[layer:layer + 1, :] if has_norm else None
    kf = k_ref[...].reshape(nb * DEC_T, k_ref.shape[2])
    q = kf[:, 0:dkk] * rowvalid
    k = kf[:, dkk:2 * dkk] * rowvalid
    g = retg_ref[...] if ret else kf[:, 2 * dkk:3 * dkk] * rowvalid
    v = v_ref[...].reshape(nb * n, HEAD_DIM) * hmvalid
    vb = _bf(v)
    G = _cumsum_rows(g, DEC_T)

    def mm(x, y):
        return _dot_nt(_bf(_seq_tile(x, nb) * qmask), _bf(_seq_tile(y, nb)))

    att = _decay_att(q, k, G, mm, lvl_ref, levels, nb * n)
    o = _dot(_bf(att), vb)
    qgt = _bf(_seq_tile(q * jnp.exp(G), nb) * qmask)
    glb = _block_row(G, DEC_T, DEC_T - 1)
    kdt = _bf(_seq_tile(k * jnp.exp(glb - G), nb) * qmask)
    gst = _gl_stack(glb)
    ones8 = jnp.ones((DEC_T, HEAD_DIM), BF16)
    o_inter = []
    for e in range(nb):
        s = s_ref[0, e].reshape(dkk, HEAD_DIM)
        o_inter.append(_dot(qgt[e * n:(e + 1) * n], _bf(s)))
        glcol = _dot_tn(_bf(gst[e * DEC_T:(e + 1) * DEC_T]), ones8)
        s_new = s * jnp.exp(glcol) + _dot_tn(kdt[e * n:(e + 1) * n], vb[e * n:(e + 1) * n])
        so_ref[layer, e] = s_new.reshape(HEADS, dk, HEAD_DIM)
    o = o + jnp.concatenate(o_inter, axis=0)
    fin = _dec_finish(o, gate_ref[...].reshape(nb * n, HEAD_DIM), normw, hmvalid)
    o_ref[...] = fin.reshape(nb, n, HEAD_DIM)


def _decode_call(body, name, batch, kfeat, vfeat, gatefeat, state, prev, consts, dk, layer):
    nb = DEC_NB
    n = HEADS * DEC_T
    kw = kfeat.shape[-1]
    hm = pl.BlockSpec((nb, n, HEAD_DIM), lambda i: (i, 0, 0))

    def st_spec(nl):
        return pl.BlockSpec((nl, nb, HEADS, dk, HEAD_DIM), lambda i: (0, i, 0, 0, 0))

    in_specs = [pl.BlockSpec((nb, DEC_T, kw), lambda i: (i, 0, 0)), hm, hm,
                pl.BlockSpec((1, nb, HEADS, dk, HEAD_DIM), lambda i: (layer, i, 0, 0, 0))]
    args = [kfeat, vfeat, gatefeat, state]
    if layer > 0:
        in_specs.append(st_spec(layer))
        args.append(prev)
    return pl.pallas_call(
        body,
        grid=(batch // nb,),
        in_specs=in_specs + [_const_spec(a) for a in consts],
        out_specs=[hm, st_spec(layer + 1)],
        out_shape=[jax.ShapeDtypeStruct((batch, n, HEAD_DIM), F32),
                   jax.ShapeDtypeStruct((layer + 1, batch, HEADS, dk, HEAD_DIM), F32)],
        compiler_params=_params(("arbitrary",)),
        name=name,
    )(*args, *consts)


def _gla_decode(batch, kfeat, vfeat, gatefeat, state, prev, dk, normw, layer, ret, t_valid):
    c = _dec_consts(dk, t_valid)
    consts = [jnp.asarray(c["qmask"]), jnp.asarray(c["lvl"]), jnp.asarray(c["rowvalid"]), jnp.asarray(c["hmvalid"]),
              jnp.asarray(c["retg"]), normw if normw is not None else jnp.ones((1, HEAD_DIM), F32)]
    body = functools.partial(_gla_decode_kernel, dk=dk, ret=ret, levels=c["levels"], has_norm=normw is not None,
                             layer=layer, nb=DEC_NB)
    return _decode_call(body, "ret_decode" if ret else f"gla_decode_dk{dk}", batch, kfeat, vfeat, gatefeat, state,
                        prev, consts, dk, layer)


def _gdn_decode_kernel(k_ref, v_ref, gate_ref, s_ref, *refs, layer, nb, t_valid):
    (qmask_ref, incl_ref, strict_ref, eye_ref, rv_ref, hv_ref, nw_ref, o_ref, so_ref) = _state_io(refs, layer)
    qmask, incl, strict, eye = qmask_ref[...], incl_ref[...], strict_ref[...], eye_ref[...]
    rowvalid, hmvalid = rv_ref[...], hv_ref[...]
    normw = nw_ref[layer:layer + 1, :]
    n = HEADS * DEC_T
    nn = nb * n
    ones_w = jnp.full((HW, nn), 1.0 / HEAD_DIM, BF16)
    ones_sq = jnp.ones((nn, nn), BF16)

    kf = k_ref[...].reshape(nb * DEC_T, k_ref.shape[2])
    q = kf[:, 0:HW] * rowvalid
    k = kf[:, HW:2 * HW] * rowvalid
    be = kf[:, 2 * HW:3 * HW] * rowvalid
    ge = kf[:, 3 * HW:4 * HW] * rowvalid
    v = v_ref[...].reshape(nn, HEAD_DIM) * hmvalid
    G = _cumsum_rows(ge, DEC_T)

    def head_col(xe):
        h1, h2, h3 = _split3(_seq_tile(xe, nb) * qmask)
        return _dot(h1, ones_w) + _dot(h2, ones_w) + _dot(h3, ones_w)

    gcol = head_col(G)
    bcol = head_col(be)
    grow = _dot_exact01(ones_sq, gcol * eye)
    L = jnp.exp(jnp.minimum(gcol - grow, 0.0)) * incl
    kt = _bf(_seq_tile(k, nb))
    kk = _dot_nt(_bf(_seq_tile(k, nb) * qmask), kt)
    qk = _dot_nt(_bf(_seq_tile(q, nb) * qmask), kt)
    a = bcol * kk * L * strict
    tinv = eye - a
    p, cnt = a, 2
    while cnt < t_valid:
        p = _dot3(p, p)
        tinv = tinv + _dot3(tinv, p)
        cnt *= 2
    eg = jnp.exp(G)
    u = _dot3(tinv, bcol[:, 0:HEAD_DIM] * v)
    wrow = _dot3(tinv, _seq_tile(be * eg * k, nb)) * qmask
    wb = _bf(wrow)
    qgt = _bf(_seq_tile(q * eg, nb) * qmask)
    glb = _block_row(G, DEC_T, DEC_T - 1)
    kdt = _bf(_seq_tile(k * jnp.exp(glb - G), nb) * qmask)
    gst = _gl_stack(glb)
    ones8 = jnp.ones((DEC_T, HEAD_DIM), BF16)

    states = [s_ref[0, e].reshape(HW, HEAD_DIM) for e in range(nb)]
    ws = [_dot(jnp.concatenate([wb[e * n:(e + 1) * n], qgt[e * n:(e + 1) * n]], axis=0), _bf(states[e]))
          for e in range(nb)]
    delta = u - jnp.concatenate([m[:n] for m in ws], axis=0)
    db = _bf(delta)
    o = _dot(_bf(qk * L), db) + jnp.concatenate([m[n:] for m in ws], axis=0)
    for e in range(nb):
        glcol = _dot_tn(_bf(gst[e * DEC_T:(e + 1) * DEC_T]), ones8)
        s_new = states[e] * jnp.exp(glcol) + _dot_tn(kdt[e * n:(e + 1) * n], db[e * n:(e + 1) * n])
        so_ref[layer, e] = s_new.reshape(HEADS, HEAD_DIM, HEAD_DIM)
    fin = _dec_finish(o, gate_ref[...].reshape(nn, HEAD_DIM), normw, hmvalid)
    o_ref[...] = fin.reshape(nb, n, HEAD_DIM)


def _gdn_decode(batch, kfeat, vfeat, gatefeat, state, prev, normw, layer, t_valid):
    c = _dec_consts(HEAD_DIM, t_valid)
    consts = [jnp.asarray(c["qmask"]), jnp.asarray(c["incl"]), jnp.asarray(c["strict"]), jnp.asarray(c["eye"]),
              jnp.asarray(c["rowvalid"]), jnp.asarray(c["hmvalid"]), normw]
    body = functools.partial(_gdn_decode_kernel, layer=layer, nb=DEC_NB, t_valid=t_valid)
    return _decode_call(body, "gdn_decode", batch, kfeat, vfeat, gatefeat, state, prev, consts, HEAD_DIM, layer)


def _rope_tables(pos):
    half = HEAD_DIM // 2
    inv = ROPE_BASE ** (-jnp.arange(half, dtype=F32) / half)
    ang = pos[:, None] * inv[None, :]
    cos, sin = jnp.cos(ang), jnp.sin(ang)
    cos_t = jnp.tile(jnp.concatenate([cos, cos], axis=1), (1, HEADS))
    sin_t = jnp.tile(jnp.concatenate([-sin, sin], axis=1), (1, HEADS))
    return cos_t, sin_t


def _relayout_w_in(w_in):
    parts = []
    for name, w in _SLABS:
        if name == "small":
            sm = jnp.concatenate([w_in[:, :, _SRC["alr"]:_SRC["alr"] + 16], w_in[:, :, _SRC["db"]:_SRC["db"] + 4],
                                  w_in[:, :, _SRC["da"]:_SRC["da"] + 4]], axis=2)
            parts.append(jnp.pad(sm, ((0, 0), (0, 0), (0, w - sm.shape[2]))))
        else:
            parts.append(w_in[:, :, _SRC[name]:_SRC[name] + w])
    return jnp.concatenate(parts, axis=2).astype(BF16)


def kernel(x_prompt, x_sample, state_ret, state_gla, state_hgrn, state_gdn, state_gdn_conv, c_prompt, c_sample, ada_w, ada_b, ln_g, ln_b, ffn1_wi, ffn1_wo, ffn2_wi, ffn2_wo, w_in, gla_wg, gla_bg, hg_lb, gdn_conv, gdn_a_log, gdn_dt_bias, gla_norm, hg_norm, gdn_norm, w_out):
    nl, d = ada_w.shape[0], ada_w.shape[1]
    bp, tp = x_prompt.shape[0], x_prompt.shape[1]
    bs, ts = x_sample.shape[0], x_sample.shape[1]
    alpha = (2.0 * nl) ** 0.25
    assert ts <= DEC_T and tp % CHUNK == 0 and ts >= CONV_W - 1 and bs % DEC_NB == 0

    wi1, wo1, wi2, wo2 = (w.astype(BF16) for w in (ffn1_wi, ffn1_wo, ffn2_wi, ffn2_wo))
    w_out_b = w_out.astype(BF16)
    w_in_p = _relayout_w_in(w_in)
    wg_p = jnp.pad(gla_wg, ((0, 0), (0, 128 - gla_wg.shape[1]), (0, 0))).astype(BF16)
    alog_e = jnp.repeat(gdn_a_log, HEAD_DIM, axis=1)
    dtb_e = jnp.repeat(gdn_dt_bias, HEAD_DIM, axis=1)
    gla_nw = jnp.tile(gla_norm, (1, HEADS))
    hg_nw = jnp.tile(hg_norm, (1, HEADS))
    gdn_nw = jnp.tile(gdn_norm, (1, HEADS))

    mod = _ada(jnp.concatenate([c_prompt, c_sample], axis=0), ada_w, ada_b)
    mod = mod.reshape(nl, bp + bs, N_MOD, d).transpose(0, 2, 1, 3)
    mod_p = mod[:, :, :bp].reshape(nl, N_MOD, bp, 1, d)
    mod_s = mod[:, :, bp:]

    gp = _Rows(bp, tp, decode=False)
    gs = _Rows(bs, DEC_T, decode=True)
    gp_prep = _Rows(bp, tp, decode=False, tile=PREP_TILE)
    gs_prep = _Rows(bs, DEC_T, decode=True, tile=PREP_TILE)
    cos_p, sin_p = _rope_tables(jnp.arange(tp, dtype=F32))
    cos_s, sin_s = _rope_tables(PAST_LEN + jnp.arange(DEC_T, dtype=F32))
    cos_s, sin_s = jnp.tile(cos_s, (bs, 1)), jnp.tile(sin_s, (bs, 1))

    xp = x_prompt.reshape(bp * tp, d)
    xs = jnp.pad(x_sample, ((0, 0), (0, DEC_T - ts), (0, 0))).reshape(bs * DEC_T, d)

    p_states, s_convs = [], []
    s_ret = s_gla = s_hg = s_gdn = None
    for l in range(nl):
        xp = _ffn(gp, xp, mod_p, (0, 1, 2), wi1, wo1, ln_g, ln_b, l, 0, alpha)
        ret_f, gla_f, hg_f, gdn_f, p_conv = _prep(gp_prep, xp, mod_p, w_in_p, cos_p, sin_p, wg_p, gla_bg, hg_lb,
                                                  gdn_conv, alog_e, dtb_e, None, l, nl)
        o_ret, p_ret = _gla_prompt(gp, ret_f, HEAD_DIM, None, l, ret=True)
        o_gla, p_gla = _gla_prompt(gp, gla_f, GLA_DK, gla_nw, l, ret=False)
        o_hg, p_hg = _gla_prompt(gp, hg_f, HEAD_DIM, hg_nw, l, ret=False)
        o_gdn, p_gdn = _gdn_prompt(gp, gdn_f, gdn_nw, l)
        xp = _mix_out(gp, xp, mod_p, (o_ret, o_gla, o_hg, o_gdn), w_out_b, ln_g, ln_b, l, alpha)
        xp = _ffn(gp, xp, mod_p, (6, 7, 8), wi2, wo2, ln_g, ln_b, l, 2, alpha)
        p_states.append((p_ret, p_gla, p_hg, p_gdn, p_conv))

        xs = _ffn(gs, xs, mod_s, (0, 1, 2), wi1, wo1, ln_g, ln_b, l, 0, alpha)
        cs = state_gdn_conv[l]
        hists = []
        for s in range(1, CONV_W):
            hrow = jnp.pad(cs[:, CONV_W - 1 - s:, :], ((0, 0), (0, DEC_T - s), (0, 0)))
            hists.append(hrow.reshape(bs * DEC_T, -1))
        outs = _prep(gs_prep, xs, mod_s, w_in_p, cos_s, sin_s, wg_p, gla_bg, hg_lb, gdn_conv, alog_e, dtb_e, hists,
                     l, nl)
        (ret_k, ret_v, ret_gt, gla_k, gla_v, gla_gt, hg_k, hg_v, hg_gt, gdn_k, gdn_v, gdn_gt, dq_raw) = outs

        def key3(a):
            return a.reshape(bs, DEC_T, a.shape[1])

        def hm3(a):
            return a.reshape(bs, HEADS * DEC_T, HEAD_DIM)

        o_ret, s_ret = _gla_decode(bs, key3(ret_k), hm3(ret_v), hm3(ret_gt), state_ret, s_ret, HEAD_DIM, None, l,
                                   True, ts)
        o_gla, s_gla = _gla_decode(bs, key3(gla_k), hm3(gla_v), hm3(gla_gt), state_gla, s_gla, GLA_DK, gla_norm, l,
                                   False, ts)
        o_hg, s_hg = _gla_decode(bs, key3(hg_k), hm3(hg_v), hm3(hg_gt), state_hgrn, s_hg, HEAD_DIM, hg_norm, l,
                                 False, ts)
        o_gdn, s_gdn = _gdn_decode(bs, key3(gdn_k), hm3(gdn_v), hm3(gdn_gt), state_gdn, s_gdn, gdn_norm, l, ts)
        xs = _mix_out(gs, xs, mod_s, (o_ret, o_gla, o_hg, o_gdn), w_out_b, ln_g, ln_b, l, alpha)
        xs = _ffn(gs, xs, mod_s, (6, 7, 8), wi2, wo2, ln_g, ln_b, l, 2, alpha)
        s_convs.append(dq_raw.reshape(bs, DEC_T, -1)[:, ts - (CONV_W - 1):ts, :])

    y_prompt = xp.reshape(bp, tp, d)
    y_sample = xs.reshape(bs, DEC_T, d)[:, :ts, :]
    p_out = tuple(jnp.stack([st[i] for st in p_states]) for i in range(5))
    return (y_prompt, y_sample) + p_out + (s_ret, s_gla, s_hg, s_gdn, jnp.stack(s_convs))
```

```python
import functools

import numpy as np
import jax
import jax.numpy as jnp
from jax import lax
from jax.experimental import pallas as pl
from jax.experimental.pallas import tpu as pltpu

F32 = jnp.float32
BF16 = jnp.bfloat16

HEAD_DIM = 64
HEADS = 4
GLA_DK = 32
GLA_GATE_NORM = 16.0
CONV_W = 4
CHUNK = 64
ROPE_BASE = 10000.0
LN_EPS = 1e-5
RMS_EPS = 1e-6
PAST_LEN = 16384
N_MOD = 9
HW = HEADS * HEAD_DIM

ROW_TILE = 512
PREP_TILE = 256
MIX_TILE = 512
DEC_DCHUNK = 8
FF_CHUNK = 512
V7X_VMEM_BYTES = 64 * 1024 * 1024
VMEM_LIMIT = V7X_VMEM_BYTES - 12 * 1024 * 1024

_SLABS = (("rq", 256), ("rk", 256), ("rv", 256), ("rg", 256),
          ("aq", 128), ("ak", 128), ("av", 256), ("ag", 256),
          ("hq", 256), ("hf", 256), ("hi", 256), ("hg", 256),
          ("dqkv", 768), ("dg", 256), ("small", 128))
_OFF = {}
_o = 0
for _n, _w in _SLABS:
    _OFF[_n] = (_o, _w)
    _o += _w
PROJ_W = _o
_SRC = {"rq": 0, "rk": 256, "rv": 512, "rg": 768, "aq": 1024, "ak": 1152, "av": 1280, "alr": 1536,
        "ag": 1552, "hq": 1808, "hf": 2064, "hi": 2320, "hg": 2576, "dqkv": 2832, "db": 3600,
        "da": 3604, "dg": 3608}
SMALL_ALR, SMALL_DB, SMALL_DA = 0, 16, 20


def _bf(x):
    return x.astype(BF16)


def _dot(a, b):
    return jnp.dot(a, b, preferred_element_type=F32)


def _dot_nt(a, b):
    return lax.dot_general(a, b, (((1,), (1,)), ((), ())), preferred_element_type=F32)


def _dot_tn(a, b):
    return lax.dot_general(a, b, (((0,), (0,)), ((), ())), preferred_element_type=F32)


def _split2(x):
    hi = x.astype(BF16)
    lo = (x - hi.astype(F32)).astype(BF16)
    return hi, lo


def _split3(x):
    hi = x.astype(BF16)
    r = x - hi.astype(F32)
    mid = r.astype(BF16)
    lo = (r - mid.astype(F32)).astype(BF16)
    return hi, mid, lo


def _dot_x01(x, m01):
    hi, lo = _split2(x)
    return _dot(hi, m01) + _dot(lo, m01)


def _dot_exact01(m01, x):
    h1, h2, h3 = _split3(x)
    return _dot(m01, h1) + _dot(m01, h2) + _dot(m01, h3)


def _silu(x):
    return x * jax.nn.sigmoid(x)


def _softplus(x):
    return jnp.maximum(x, 0.0) + jnp.log1p(jnp.exp(-jnp.abs(x)))


def _ln(y, g, b):
    mu = jnp.mean(y, axis=-1, keepdims=True)
    yc = y - mu
    var = jnp.mean(yc * yc, axis=-1, keepdims=True)
    return yc * lax.rsqrt(var + LN_EPS) * g + b


def _mod_rows(ref, tm):
    v = ref[...]
    v = v.reshape(v.shape[-2], v.shape[-1])
    nb = v.shape[0]
    if nb == 1:
        return v
    return jnp.concatenate([v] * (tm // nb), axis=0)


def _tile4(x):
    return jnp.concatenate([x, x, x, x], axis=0)


def _per_block(x, const, period):
    n = x.shape[0] // period
    return (x.reshape(n, period, x.shape[1]) * const[None]).reshape(x.shape)


def _cumsum_rows(g, period):
    row = lax.broadcasted_iota(jnp.int32, g.shape, 0) % period
    out = g
    sh = 1
    while sh < period:
        out = out + jnp.where(row >= sh, pltpu.roll(out, sh, 0), 0.0)
        sh *= 2
    return out


def _block_row(G, period, r):
    c, w = G.shape
    g3 = G.reshape(c // period, period, w)
    return jnp.broadcast_to(g3[:, r:r + 1, :], g3.shape).reshape(c, w)


def _gref(G, m):
    c = G.shape[0]
    if 2 * m >= 8:
        return _block_row(G, 2 * m, m - 1)
    p = lax.broadcasted_iota(jnp.int32, G.shape, 0) % (2 * m)
    if m == 2:
        return jnp.where(p == 0, pltpu.roll(G, c - 1, 0),
                         jnp.where(p == 1, G, jnp.where(p == 2, pltpu.roll(G, 1, 0), pltpu.roll(G, 2, 0))))
    return jnp.where(p == 0, G, pltpu.roll(G, 1, 0))


def _decay_att(q, k, G, mm, lvl_ref, levels, period):
    att = _per_block(mm(q, k), lvl_ref[len(levels)], period)
    for i, m in enumerate(levels):
        gr = _gref(G, m)
        ql = q * jnp.exp(jnp.minimum(G - gr, 0.0))
        kl = k * jnp.exp(jnp.minimum(gr - G, 0.0))
        att = att + _per_block(mm(ql, kl), lvl_ref[i], period)
    return att


def _ret_gamma():
    return np.log(1.0 - 2.0 ** (-5.0 - np.arange(HEADS, dtype=np.float64)))


@functools.lru_cache(maxsize=None)
def _chunk_consts(dk):
    c = CHUNK
    h_att = np.arange(4 * c) // c
    s_att = np.arange(4 * c) % c
    t = np.arange(c)
    kmask = (h_att[:, None] == (np.arange(4 * dk) // dk)[None, :]).astype(np.float32)
    vmask = (h_att[:, None] == (np.arange(HW) // HEAD_DIM)[None, :]).astype(np.float32)
    smask = ((np.arange(HW) // HEAD_DIM)[:, None] == (np.arange(4 * dk) // dk)[None, :]).astype(np.float32)
    levels = (32, 16, 8, 4, 2, 1)
    lv = []
    for m in levels:
        same = (t[:, None] // (2 * m)) == (s_att[None, :] // (2 * m))
        lv.append(same & ((t[:, None] % (2 * m)) >= m) & ((s_att[None, :] % (2 * m)) < m))
    lv.append(t[:, None] == s_att[None, :])
    lvl = np.stack(lv).astype(np.float32)
    incl = (s_att[None, :] <= t[:, None]).astype(np.float32)
    strict = (s_att[None, :] < t[:, None]).astype(np.float32)
    eye = (s_att[None, :] == t[:, None]).astype(np.float32)
    fold = (np.arange(HW)[:, None] % HEAD_DIM == np.arange(HEAD_DIM)[None, :]).astype(np.float32)
    gam = _ret_gamma()
    gh = gam[h_att]
    ret_att = np.where(s_att[None, :] <= t[:, None], np.exp((t[:, None] - s_att[None, :]) * gh[None, :]), 0.0)
    gl = gam[np.arange(HW) // HEAD_DIM]
    ret_q = np.exp((t[:, None] + 1) * gl[None, :])
    ret_k = np.exp((c - 1 - t[:, None]) * gl[None, :])
    ret_s = np.exp(c * gl)[None, :]
    return dict(kmask=kmask, vmask=vmask, smask=smask, lvl=lvl, levels=levels, incl=incl, strict=strict,
                eye=eye, fold=fold, ret_att=ret_att.astype(np.float32), ret_q=ret_q.astype(np.float32),
                ret_k=ret_k.astype(np.float32), ret_s=ret_s.astype(np.float32))


def _bd_ones():
    h = np.arange(HW) // HEAD_DIM
    return (h[:, None] == h[None, :]).astype(np.float32)


def _expand_mat():
    e = np.zeros((128, 2 * HW), np.float32)
    for h in range(HEADS):
        e[SMALL_DB + h, h * HEAD_DIM:(h + 1) * HEAD_DIM] = 1.0
        e[SMALL_DA + h, HW + h * HEAD_DIM:HW + (h + 1) * HEAD_DIM] = 1.0
    return e


def _const_spec(a):
    nd = a.ndim
    return pl.BlockSpec(a.shape, lambda *_: (0,) * nd)


def _params(sem):
    return pltpu.CompilerParams(dimension_semantics=sem, vmem_limit_bytes=VMEM_LIMIT)


def _ada_kernel(c_ref, w_ref, b_ref, o_ref):
    c = c_ref[...]
    o_ref[0] = _dot(_bf(_silu(c)), _bf(w_ref[0])) + b_ref[0]


def _ada(c_all, ada_w, ada_b):
    nl, d, nd = ada_w.shape
    nb = c_all.shape[0]
    tn = d
    return pl.pallas_call(
        _ada_kernel,
        grid=(nl, nd // tn),
        in_specs=[pl.BlockSpec((nb, d), lambda l, j: (0, 0)),
                  pl.BlockSpec((1, d, tn), lambda l, j: (l, 0, j)),
                  pl.BlockSpec((1, 1, tn), lambda l, j: (l, 0, j))],
        out_specs=pl.BlockSpec((1, nb, tn), lambda l, j: (l, 0, j)),
        out_shape=jax.ShapeDtypeStruct((nl, nb, nd), F32),
        compiler_params=_params(("arbitrary", "arbitrary")),
        name="ada_mod",
    )(c_all, ada_w, ada_b.reshape(nl, 1, nd))


class _Rows:
    def __init__(self, batch, t, decode, tile=ROW_TILE):
        self.batch, self.t, self.decode = batch, t, decode
        self.rows = batch * t
        self.tm = min(tile, self.rows)
        assert self.rows % self.tm == 0
        if decode:
            assert self.tm % batch == 0
        else:
            assert t % self.tm == 0
        self.steps = self.rows // self.tm


def _mod_spec(g, layer, k, d):
    if g.decode:
        return pl.BlockSpec((1, 1, g.batch, d), lambda i: (layer, k, 0, 0))
    per = g.t // g.tm
    return pl.BlockSpec((1, 1, 1, 1, d), lambda i: (layer, k, i // per, 0, 0))


def _ffn_kernel(x_ref, sh_ref, sc_ref, gt_ref, wi_ref, wo_ref, lng_ref, lnb_ref, o_ref, *, alpha, ln_idx, layer):
    x = x_ref[...]
    tm, d = x.shape
    dff = wo_ref.shape[1]
    sh, sc, gt = _mod_rows(sh_ref, tm), _mod_rows(sc_ref, tm), _mod_rows(gt_ref, tm)
    h = _bf(x * (1.0 + sc) + sh)
    acc = jnp.zeros((tm, d), F32)
    off = 0
    while off < dff:
        fc = min(FF_CHUNK, dff - off)
        a = _dot(h, wi_ref[0, :, off:off + fc])
        b = _dot(h, wi_ref[0, :, dff + off:dff + off + fc])
        acc = acc + _dot(_bf(_silu(a) * b), wo_ref[0, off:off + fc, :])
        off += fc
    y = alpha * x + 0.5 * (1.0 + gt) * acc
    o_ref[...] = _ln(y, lng_ref[layer, ln_idx:ln_idx + 1, :], lnb_ref[layer, ln_idx:ln_idx + 1, :])


def _ffn(g, x, mod, mods, wi, wo, ln_g, ln_b, layer, ln_idx, alpha):
    d = x.shape[1]
    dff = wo.shape[1]
    row = pl.BlockSpec((g.tm, d), lambda i: (i, 0))
    return pl.pallas_call(
        functools.partial(_ffn_kernel, alpha=alpha, ln_idx=ln_idx, layer=layer),
        grid=(g.steps,),
        in_specs=[row] + [_mod_spec(g, layer, k, d) for k in mods] + [
            pl.BlockSpec((1, d, 2 * dff), lambda i: (layer, 0, 0), pipeline_mode=pl.Buffered(1)),
            pl.BlockSpec((1, dff, d), lambda i: (layer, 0, 0), pipeline_mode=pl.Buffered(1)),
            _const_spec(ln_g), _const_spec(ln_b)],
        out_specs=row,
        out_shape=jax.ShapeDtypeStruct(x.shape, F32),
        compiler_params=_params(("arbitrary",)),
        name=f"ffn{ln_idx}",
    )(x, mod, mod, mod, wi, wo, ln_g, ln_b)


def _mix_out_kernel(x_ref, gt_ref, o0_ref, o1_ref, o2_ref, o3_ref, w_ref, lng_ref, lnb_ref, y_ref, *, alpha, layer,
                    decode, batch):
    x = x_ref[...]
    tm, d = x.shape
    gt = _mod_rows(gt_ref, tm)
    if decode:
        accs = [jnp.zeros((batch, d), F32) for _ in range(tm // batch)]
        for mi, o_ref in enumerate((o0_ref, o1_ref, o2_ref, o3_ref)):
            for h in range(HEADS):
                r0 = mi * HW + h * HEAD_DIM
                wh = w_ref[0, r0:r0 + HEAD_DIM, :]
                for t in range(tm // batch):
                    accs[t] = accs[t] + _dot_tn(_bf(o_ref[h, t]), wh)
        acc = jnp.concatenate(accs, axis=0)
    else:
        acc = jnp.zeros((tm, d), F32)
        for mi, o_ref in enumerate((o0_ref, o1_ref, o2_ref, o3_ref)):
            acc = acc + _dot(o_ref[...], w_ref[0, mi * HW:(mi + 1) * HW, :])
    y = alpha * x + (1.0 + gt) * acc
    y_ref[...] = _ln(y, lng_ref[layer, 1:2, :], lnb_ref[layer, 1:2, :])


def _mix_out(g, x, mod, outs, w_out, ln_g, ln_b, layer, alpha):
    d = x.shape[1]
    row = pl.BlockSpec((g.tm, d), lambda i: (i, 0))
    if g.decode:
        assert g.steps == 1
        ospec = _const_spec(outs[0])
    else:
        ospec = pl.BlockSpec((g.tm, HW), lambda i: (i, 0))
    return pl.pallas_call(
        functools.partial(_mix_out_kernel, alpha=alpha, layer=layer, decode=g.decode, batch=g.batch),
        grid=(g.steps,),
        in_specs=[row, _mod_spec(g, layer, 5, d), ospec, ospec, ospec, ospec,
                  pl.BlockSpec((1, w_out.shape[1], d), lambda i: (layer, 0, 0), pipeline_mode=pl.Buffered(1)),
                  _const_spec(ln_g), _const_spec(ln_b)],
        out_specs=row,
        out_shape=jax.ShapeDtypeStruct(x.shape, F32),
        compiler_params=_params(("arbitrary",)),
        name="mix_out",
    )(x, mod, *outs, w_out, ln_g, ln_b)


def _prep_kernel(*refs, layer, nlayers, g_t, tm, decode, batch):
    (x_ref, sh_ref, sc_ref, w_ref, cos_ref, sin_ref, wg_ref, bg_ref, lb_ref, cw_ref, alog_ref, dtb_ref,
     bd_ref, ex_ref) = refs[:14]
    rest = refs[14:]
    if decode:
        cs_ref, alogc_ref, dtbc_ref = rest[:3]
        outs = rest[3:]
        cbuf = None
    else:
        outs = rest[:-1]
        cbuf = rest[-1]

    x = x_ref[...]
    hb = _bf(x * (1.0 + _mod_rows(sc_ref, tm)) + _mod_rows(sh_ref, tm))

    def proj(name):
        o, w = _OFF[name]
        return _dot(hb, w_ref[0, :, o:o + w])

    bd = bd_ref[...]

    dest = {}
    if decode:
        it = iter(outs)
        for mix, keys in (("ret", ("q", "k", "v", "gate")), ("gla", ("q", "k", "g", "v", "gate")),
                          ("hg", ("q", "k", "g", "v", "gate")), ("gdn", ("q", "k", "v", "gate"))):
            for kk in keys:
                dest[f"{mix}_{kk}"] = ("heads", next(it), 0)
        bgo_ref, convo_ref = next(it), next(it)
    else:
        ret_ref, gla_ref, hg_ref, gdn_ref, conv_ref = outs
        for dst, keys, widths in ((ret_ref, ("ret_q", "ret_k", "ret_v", "ret_gate"), (HW,) * 4),
                                  (gla_ref, ("gla_q", "gla_k", "gla_g", "gla_v", "gla_gate"),
                                   (HEADS * GLA_DK,) * 3 + (HW, HW)),
                                  (hg_ref, ("hg_q", "hg_k", "hg_g", "hg_v", "hg_gate"), (HW,) * 5),
                                  (gdn_ref, ("gdn_q", "gdn_k", "gdn_b", "gdn_g", "gdn_v", "gdn_gate"), (HW,) * 6)):
            off = 0
            for kk, w in zip(keys, widths):
                dest[kk] = ("rows", dst, off)
                off += w

    def put(name, val):
        kind, ref, off = dest[name]
        if kind == "rows":
            ref[:, off:off + val.shape[1]] = val
        else:
            dh = val.shape[1] // HEADS
            for t in range(tm // batch):
                vt = val[t * batch:(t + 1) * batch].T
                for h in range(HEADS):
                    ref[h, t] = vt[h * dh:(h + 1) * dh]

    cos, sin = cos_ref[...], sin_ref[...]
    lane = lax.broadcasted_iota(jnp.int32, (tm, HW), 1)
    first_half = (lane % HEAD_DIM) < (HEAD_DIM // 2)

    def rope(v):
        sw = jnp.where(first_half, pltpu.roll(v, HW - HEAD_DIM // 2, 1), pltpu.roll(v, HEAD_DIM // 2, 1))
        return v * cos + sw * sin

    put("ret_q", rope(proj("rq")))
    put("ret_k", rope(proj("rk")) * (HEAD_DIM ** -0.5))
    put("ret_v", proj("rv"))
    put("ret_gate", _silu(proj("rg")))

    small = proj("small")
    gpre = _dot(_bf(small), wg_ref[0]) + bg_ref[layer:layer + 1, :]
    put("gla_q", proj("aq") * (GLA_DK ** -0.5))
    put("gla_k", proj("ak"))
    put("gla_g", -_softplus(-gpre) * (1.0 / GLA_GATE_NORM))
    put("gla_v", proj("av"))
    put("gla_gate", _silu(proj("ag")))

    rows = [lb_ref[i:i + 1, :] for i in range(nlayers)]
    mx = functools.reduce(jnp.maximum, rows)
    ex = [jnp.exp(r - mx) for r in rows]
    tot = functools.reduce(lambda a, b: a + b, ex)
    plb = [e / tot for e in ex]
    lb = functools.reduce(lambda a, b: a + b, plb[:layer + 1]) - plb[0]
    zf = proj("hf")
    put("hg_q", _silu(proj("hq")) * (HEAD_DIM ** -0.5))
    put("hg_k", (1.0 - lb) * jax.nn.sigmoid(-zf))
    put("hg_g", jnp.log(lb + (1.0 - lb) * jax.nn.sigmoid(zf)))
    put("hg_v", proj("hi"))
    put("hg_gate", _silu(proj("hg")))

    dq = proj("dqkv")
    cw = cw_ref[0]
    if decode:
        cs = cs_ref[0]
        nt = tm // batch
        blocks = [dq[t * batch:(t + 1) * batch] for t in range(nt)]
        convs = []
        for t in range(nt):
            acc = blocks[t] * cw[CONV_W - 1:CONV_W, :]
            for s in range(1, CONV_W):
                prev = blocks[t - s] if t >= s else cs[CONV_W - 1 - s + t]
                acc = acc + prev * cw[CONV_W - 1 - s:CONV_W - s, :]
            convs.append(acc)
        conv = jnp.concatenate(convs, axis=0)
        convo_ref[...] = dq[(nt - (CONV_W - 1)) * batch:].reshape(CONV_W - 1, batch, dq.shape[1])
    else:
        i = pl.program_id(0)
        per = g_t // tm

        @pl.when(i % per == 0)
        def _():
            cbuf[0:8, :] = jnp.zeros((8, dq.shape[1]), F32)

        cbuf[8:8 + tm, :] = dq
        conv = cbuf[pl.ds(8, tm), :] * cw[CONV_W - 1:CONV_W, :]
        for s in range(1, CONV_W):
            conv = conv + cbuf[pl.ds(8 - s, tm), :] * cw[CONV_W - 1 - s:CONV_W - s, :]
        tail = cbuf[tm:tm + 8, :]
        cbuf[0:8, :] = tail

        @pl.when(i % per == per - 1)
        def _():
            conv_ref[0] = tail[8 - (CONV_W - 1):8, :]

    u = _silu(conv)
    uq, uk = u[:, 0:HW], u[:, HW:2 * HW]
    put("gdn_q", uq * lax.rsqrt(_dot_x01(uq * uq, bd) + RMS_EPS) * (HEAD_DIM ** -0.5))
    put("gdn_k", uk * lax.rsqrt(_dot_x01(uk * uk, bd) + RMS_EPS))
    put("gdn_v", u[:, 2 * HW:3 * HW])
    put("gdn_gate", _silu(proj("dg")))
    if decode:
        for t in range(tm // batch):
            smt = small[t * batch:(t + 1) * batch].T
            beta = jax.nn.sigmoid(smt[SMALL_DB:SMALL_DB + HEADS])
            gd = -jnp.exp(alogc_ref[layer]) * _softplus(smt[SMALL_DA:SMALL_DA + HEADS] + dtbc_ref[layer])
            for h in range(HEADS):
                bgo_ref[h, t, 0:1, :] = beta[h:h + 1]
                bgo_ref[h, t, 1:2, :] = gd[h:h + 1]
    else:
        dbda = _dot_x01(small, ex_ref[...])
        put("gdn_b", jax.nn.sigmoid(dbda[:, 0:HW]))
        put("gdn_g", -jnp.exp(alog_ref[layer:layer + 1, :]) * _softplus(dbda[:, HW:2 * HW] + dtb_ref[layer:layer + 1, :]))


def _prep(g, x, mod, w_in_p, cos_t, sin_t, wg_p, gla_bg, hg_lb, gdn_conv, alog_e, dtb_e, dec, layer, nlayers):
    d = x.shape[1]
    bd = jnp.asarray(_bd_ones(), BF16)
    exm = jnp.asarray(_expand_mat(), BF16)
    row = pl.BlockSpec((g.tm, d), lambda i: (i, 0))
    if g.decode:
        tab = pl.BlockSpec((g.tm, HW), lambda i: (i, 0))
    else:
        per = g.t // g.tm
        tab = pl.BlockSpec((g.tm, HW), lambda i: (i % per, 0))
    in_specs = [row, _mod_spec(g, layer, 3, d), _mod_spec(g, layer, 4, d),
                pl.BlockSpec((1, d, PROJ_W), lambda i: (layer, 0, 0), pipeline_mode=pl.Buffered(1)),
                tab, tab,
                pl.BlockSpec((1,) + wg_p.shape[1:], lambda i: (layer, 0, 0)),
                _const_spec(gla_bg), _const_spec(hg_lb),
                pl.BlockSpec((1,) + gdn_conv.shape[1:], lambda i: (layer, 0, 0)),
                _const_spec(alog_e), _const_spec(dtb_e), _const_spec(bd), _const_spec(exm)]
    args = [x, mod, mod, w_in_p, cos_t, sin_t, wg_p, gla_bg, hg_lb, gdn_conv, alog_e, dtb_e, bd, exm]
    cch = gdn_conv.shape[2]

    def rows_out(w):
        return jax.ShapeDtypeStruct((g.rows, w), F32), pl.BlockSpec((g.tm, w), lambda i: (i, 0))

    def whole(shape):
        return jax.ShapeDtypeStruct(shape, F32), pl.BlockSpec(shape, lambda i: (0,) * len(shape))

    if g.decode:
        assert g.steps == 1 and g.t >= CONV_W - 1
        cs, alog_c, dtb_c = dec
        in_specs += [pl.BlockSpec((1,) + cs.shape[1:], lambda i: (layer, 0, 0, 0)), _const_spec(alog_c),
                     _const_spec(dtb_c)]
        args += [cs, alog_c, dtb_c]
        outs = []
        for dhs in ((HEAD_DIM,) * 4, (GLA_DK,) * 3 + (HEAD_DIM,) * 2, (HEAD_DIM,) * 5, (HEAD_DIM,) * 4):
            outs += [whole((HEADS, g.t, dh, g.batch)) for dh in dhs]
        outs += [whole((HEADS, g.t, 2, g.batch)), whole((CONV_W - 1, g.batch, cch))]
        scratch = []
    else:
        outs = [rows_out(4 * HW), rows_out(3 * HEADS * GLA_DK + 2 * HW), rows_out(5 * HW), rows_out(6 * HW),
                (jax.ShapeDtypeStruct((g.batch, CONV_W - 1, cch), F32),
                 pl.BlockSpec((1, CONV_W - 1, cch), lambda i: (i // (g.t // g.tm), 0, 0)))]
        scratch = [pltpu.VMEM((g.tm + 8, cch), F32)]
    return pl.pallas_call(
        functools.partial(_prep_kernel, layer=layer, nlayers=nlayers, g_t=g.t, tm=g.tm, decode=g.decode,
                          batch=g.batch),
        grid=(g.steps,),
        in_specs=in_specs,
        out_specs=[o[1] for o in outs],
        out_shape=[o[0] for o in outs],
        scratch_shapes=scratch,
        compiler_params=_params(("arbitrary",)),
        name="prep",
    )(*args)


def _chunk(x, c):
    return x[c * CHUNK:(c + 1) * CHUNK]


def _per_chunk(fn, nch):
    return jnp.concatenate([fn(c) for c in range(nch)], axis=0)


def _finish_prompt(o, gate, normw, bd, o_ref):
    ms = _dot_x01(o * o, bd) * (1.0 / HEAD_DIM)
    on = o * lax.rsqrt(ms + RMS_EPS)
    if normw is not None:
        on = on * normw
    o_ref[...] = _bf(on * gate)


def _state_out(s_t, fold_ref, st_ref):
    fold = fold_ref[...]
    h1, h2, h3 = _split3(s_t)
    st_ref[0] = _dot_tn(h1, fold) + _dot_tn(h2, fold) + _dot_tn(h3, fold)


def _gla_prompt_kernel(f_ref, kmask_ref, vmask_ref, smask_ref, lvl_ref, fold_ref, bd_ref, nw_ref, ratt_ref, rq_ref,
                       rk_ref, rs_ref, o_ref, st_ref, s_scr, *, dk, nch, ret, levels, has_norm, layer):
    j = pl.program_id(1)
    dkk = HEADS * dk

    @pl.when(j == 0)
    def _():
        s_scr[...] = jnp.zeros(s_scr.shape, F32)

    kmask, vmask, smask, bd = kmask_ref[...], vmask_ref[...], smask_ref[...], bd_ref[...]
    normw = nw_ref[layer:layer + 1, :] if has_norm else None

    def mm(x, y):
        xb, yb = _bf(x), _bf(y)
        return _per_chunk(lambda c: _dot_nt(_chunk(xb, c), _tile4(_chunk(yb, c)) * kmask), nch)

    q = f_ref[:, 0:dkk]
    k = f_ref[:, dkk:2 * dkk]
    if ret:
        v = f_ref[:, 2 * dkk:2 * dkk + HW]
        gate = f_ref[:, 2 * dkk + HW:2 * dkk + 2 * HW]
        att = _per_block(mm(q, k), ratt_ref[...], CHUNK)
        qg = _per_block(q, rq_ref[...], CHUNK)
        kd = _per_block(k, rk_ref[...], CHUNK)
        sdec = [rs_ref[...]] * nch
    else:
        g = f_ref[:, 2 * dkk:3 * dkk]
        v = f_ref[:, 3 * dkk:3 * dkk + HW]
        gate = f_ref[:, 3 * dkk + HW:3 * dkk + 2 * HW]
        G = _cumsum_rows(g, CHUNK)
        att = _decay_att(q, k, G, mm, lvl_ref, levels, CHUNK)
        qg = q * jnp.exp(G)
        glb = _block_row(G, CHUNK, CHUNK - 1)
        kd = k * jnp.exp(glb - G)
        sdec = [jnp.exp(G[(c + 1) * CHUNK - 1:(c + 1) * CHUNK, :]) for c in range(nch)]
    vb, qgb, kdb, attb = _bf(v), _bf(qg), _bf(kd), _bf(att)
    o_intra = _per_chunk(lambda c: _dot(_chunk(attb, c), _tile4(_chunk(vb, c)) * vmask), nch)
    upd = [_dot_tn(_chunk(vb, c), _chunk(kdb, c)) * smask for c in range(nch)]

    s = s_scr[...]
    o_inter = []
    for c in range(nch):
        o_inter.append(_dot_nt(_chunk(qgb, c), _bf(s)))
        s = s * sdec[c] + upd[c]
    s_scr[...] = s
    _finish_prompt(o_intra + jnp.concatenate(o_inter, axis=0), gate, normw, bd, o_ref)

    @pl.when(j == pl.num_programs(1) - 1)
    def _():
        _state_out(s, fold_ref, st_ref)


def _gla_prompt(g, feat, dk, normw, layer, ret):
    c = _chunk_consts(dk)
    dkk = HEADS * dk
    blk = min(MIX_TILE, g.t)
    nch = blk // CHUNK
    per = g.t // blk
    consts = [jnp.asarray(c["kmask"], BF16), jnp.asarray(c["vmask"], BF16), jnp.asarray(c["smask"]),
              jnp.asarray(c["lvl"]), jnp.asarray(c["fold"], BF16), jnp.asarray(_bd_ones(), BF16),
              normw if normw is not None else jnp.ones((1, HW), F32),
              jnp.asarray(c["ret_att"]), jnp.asarray(c["ret_q"]), jnp.asarray(c["ret_k"]), jnp.asarray(c["ret_s"])]
    w = feat.shape[1]
    o, st = pl.pallas_call(
        functools.partial(_gla_prompt_kernel, dk=dk, nch=nch, ret=ret, levels=c["levels"],
                          has_norm=normw is not None, layer=layer),
        grid=(g.batch, per),
        in_specs=[pl.BlockSpec((blk, w), lambda b, j: (b * per + j, 0))] + [_const_spec(a) for a in consts],
        out_specs=[pl.BlockSpec((blk, HW), lambda b, j: (b * per + j, 0)),
                   pl.BlockSpec((1, dkk, HEAD_DIM), lambda b, j: (b, 0, 0))],
        out_shape=[jax.ShapeDtypeStruct((g.rows, HW), BF16),
                   jax.ShapeDtypeStruct((g.batch, dkk, HEAD_DIM), F32)],
        scratch_shapes=[pltpu.VMEM((HW, dkk), F32)],
        compiler_params=_params(("arbitrary", "arbitrary")),
        name="ret_prompt" if ret else f"gla_prompt_dk{dk}",
    )(feat, *consts)
    return o, st.reshape(g.batch, HEADS, dk, HEAD_DIM)


def _gdn_prompt_kernel(f_ref, bdm_ref, smask_ref, incl_ref, strict_ref, eye_ref, fold_ref, bd_ref, nw_ref, o_ref,
                       st_ref, s_scr, *, nch, layer):
    j = pl.program_id(1)

    @pl.when(j == 0)
    def _():
        s_scr[...] = jnp.zeros(s_scr.shape, F32)

    bdm, smask, incl, strict, eye, bd = (bdm_ref[...], smask_ref[...], incl_ref[...], strict_ref[...], eye_ref[...],
                                         bd_ref[...])
    normw = nw_ref[layer:layer + 1, :]
    ones_c = jnp.ones((CHUNK, CHUNK), BF16)

    def bdw(yb, c):
        return _tile4(_chunk(yb, c)) * bdm

    def prod1(x, y):
        xb, yb = _bf(x), _bf(y)
        return _per_chunk(lambda c: _dot(_chunk(xb, c), bdw(yb, c)), nch)

    def prod1_pair(x1, x2, y):
        x1b, x2b, yb = _bf(x1), _bf(x2), _bf(y)
        both = [_dot(jnp.concatenate([_chunk(x1b, c), _chunk(x2b, c)], axis=0), bdw(yb, c)) for c in range(nch)]
        return (jnp.concatenate([m[:CHUNK] for m in both], axis=0),
                jnp.concatenate([m[CHUNK:] for m in both], axis=0))

    def prod2(x, y):
        xh, xl = _split2(x)
        yb = _bf(y)

        def one(c):
            top = _dot(jnp.concatenate([_chunk(xh, c), _chunk(xl, c)], axis=0), bdw(yb, c))
            return top[:CHUNK] + top[CHUNK:]

        return _per_chunk(one, nch)

    def prod3(x, y):
        xh, xl = _split2(x)
        yh, yl = _split2(y)

        def one(c):
            top = _dot(jnp.concatenate([_chunk(xh, c), _chunk(xl, c)], axis=0), bdw(yh, c))
            return top[:CHUNK] + top[CHUNK:] + _dot(_chunk(xh, c), bdw(yl, c))

        return _per_chunk(one, nch)

    q, k = f_ref[:, 0:HW], f_ref[:, HW:2 * HW]
    be, ge = f_ref[:, 2 * HW:3 * HW], f_ref[:, 3 * HW:4 * HW]
    v, gate = f_ref[:, 4 * HW:5 * HW], f_ref[:, 5 * HW:6 * HW]
    G = _cumsum_rows(ge, CHUNK)
    zd = _per_block(G, eye, CHUNK)
    grow = _per_chunk(lambda c: _dot_exact01(ones_c, _chunk(zd, c)), nch)
    L = _per_block(jnp.exp(jnp.minimum(G - grow, 0.0)), incl, CHUNK)
    qb, kb = _bf(q), _bf(k)

    def qkk(c):
        return _dot_nt(jnp.concatenate([_chunk(qb, c), _chunk(kb, c)], axis=0), bdw(kb, c))

    qkk_all = [qkk(c) for c in range(nch)]
    qk = jnp.concatenate([m[:CHUNK] for m in qkk_all], axis=0)
    kk = jnp.concatenate([m[CHUNK:] for m in qkk_all], axis=0)
    a = _per_block(be * kk * L, strict, CHUNK)
    eye_b = _per_block(jnp.ones_like(a), eye, CHUNK)
    p = -a
    tinv = eye_b + p
    for i in range(5):
        if i == 0:
            p = prod1(p, p)
        else:
            p, dt = prod1_pair(p, tinv, p)
            tinv = tinv + dt
    tinv = tinv + prod1(tinv, p)
    resid = eye_b - tinv - prod3(a, tinv)
    tinv = tinv + prod1(tinv, resid)
    eg = jnp.exp(G)
    u = prod2(tinv, be * v)
    w = prod2(tinv, be * eg * k)
    glb = _block_row(G, CHUNK, CHUNK - 1)
    wb, qeb, qklb = _bf(w), _bf(q * eg), _bf(qk * L)
    kdb = _bf(k * jnp.exp(glb - G))
    sdec = [jnp.exp(G[(c + 1) * CHUNK - 1:(c + 1) * CHUNK, :]) for c in range(nch)]

    s = s_scr[...]
    outs = []
    for c in range(nch):
        ws = _dot_nt(jnp.concatenate([_chunk(wb, c), _chunk(qeb, c)], axis=0), _bf(s))
        db = _bf(_chunk(u, c) - ws[:CHUNK])
        outs.append(_dot(_chunk(qklb, c), _tile4(db) * bdm) + ws[CHUNK:])
        s = s * sdec[c] + _dot_tn(db, _chunk(kdb, c)) * smask
    s_scr[...] = s
    _finish_prompt(jnp.concatenate(outs, axis=0), gate, normw, bd, o_ref)

    @pl.when(j == pl.num_programs(1) - 1)
    def _():
        _state_out(s, fold_ref, st_ref)


def _gdn_prompt(g, feat, normw, layer):
    c = _chunk_consts(HEAD_DIM)
    blk = min(MIX_TILE, g.t)
    nch = blk // CHUNK
    per = g.t // blk
    consts = [jnp.asarray(c["vmask"], BF16), jnp.asarray(c["smask"]), jnp.asarray(c["incl"]), jnp.asarray(c["strict"]),
              jnp.asarray(c["eye"]), jnp.asarray(c["fold"], BF16), jnp.asarray(_bd_ones(), BF16), normw]
    w = feat.shape[1]
    o, st = pl.pallas_call(
        functools.partial(_gdn_prompt_kernel, nch=nch, layer=layer),
        grid=(g.batch, per),
        in_specs=[pl.BlockSpec((blk, w), lambda b, j: (b * per + j, 0))] + [_const_spec(a) for a in consts],
        out_specs=[pl.BlockSpec((blk, HW), lambda b, j: (b * per + j, 0)),
                   pl.BlockSpec((1, HW, HEAD_DIM), lambda b, j: (b, 0, 0))],
        out_shape=[jax.ShapeDtypeStruct((g.rows, HW), BF16),
                   jax.ShapeDtypeStruct((g.batch, HW, HEAD_DIM), F32)],
        scratch_shapes=[pltpu.VMEM((HW, HW), F32)],
        compiler_params=_params(("arbitrary", "arbitrary")),
        name="gdn_prompt",
    )(feat, *consts)
    return o, st.reshape(g.batch, HEADS, HEAD_DIM, HEAD_DIM)


def _state_io(refs, layer):
    if layer == 0:
        return refs
    prev_ref, rest = refs[0], refs[1:]
    rest[-1][0:layer] = prev_ref[...]
    return rest


def _dec_finish(o, gate, normw):
    ms = jnp.mean(o * o, axis=0, keepdims=True)
    on = o * lax.rsqrt(ms + RMS_EPS)
    if normw is not None:
        on = on * normw
    return on * gate


def _gla_decode_kernel(q_ref, k_ref, g_ref, v_ref, gate_ref, s_ref, *refs, nt, dk, ret, has_norm, layer):
    nw_ref, o_ref, so_ref = _state_io(refs, layer)
    normw = nw_ref[layer] if has_norm else None
    qs = [q_ref[0, t] for t in range(nt)]
    ks = [k_ref[0, t] for t in range(nt)]
    vs = [v_ref[0, t] for t in range(nt)]
    decay = [jnp.exp(g_ref[0, t]) for t in range(nt)]
    o_acc = [jnp.zeros(vs[0].shape, F32) for _ in range(nt)]
    for dc in range(dk // DEC_DCHUNK):
        r = slice(dc * DEC_DCHUNK, (dc + 1) * DEC_DCHUNK)
        sc = s_ref[0, 0, r]
        for t in range(nt):
            a = decay[t][None] if ret else decay[t][r][:, None, :]
            sc = sc * a + ks[t][r][:, None, :] * vs[t][None, :, :]
            o_acc[t] = o_acc[t] + jnp.sum(qs[t][r][:, None, :] * sc, axis=0)
        so_ref[layer, 0, r] = sc
    for t in range(nt):
        o_ref[0, t] = _dec_finish(o_acc[t], gate_ref[0, t], normw)


def _gdn_decode_kernel(q_ref, k_ref, v_ref, gate_ref, bg_ref, s_ref, *refs, nt, layer):
    nw_ref, o_ref, so_ref = _state_io(refs, layer)
    normw = nw_ref[layer]
    chunks = [slice(dc * DEC_DCHUNK, (dc + 1) * DEC_DCHUNK) for dc in range(HEAD_DIM // DEC_DCHUNK)]
    for t in range(nt):
        q, k, v = q_ref[0, t], k_ref[0, t], v_ref[0, t]
        beta = bg_ref[0, t, 0:1, :]
        ag = jnp.exp(bg_ref[0, t, 1:2, :])

        def state(r):
            return s_ref[0, 0, r] if t == 0 else so_ref[layer, 0, r]

        ks = jnp.zeros(v.shape, F32)
        for r in chunks:
            ks = ks + jnp.sum(k[r][:, None, :] * state(r), axis=0)
        delta = beta * (v - ag * ks)
        o = jnp.zeros(v.shape, F32)
        for r in chunks:
            sc = state(r) * ag[None] + k[r][:, None, :] * delta[None, :, :]
            so_ref[layer, 0, r] = sc
            o = o + jnp.sum(q[r][:, None, :] * sc, axis=0)
        o_ref[0, t] = _dec_finish(o, gate_ref[0, t], normw)


def _decode_call(body, name, feats, state_t, prev, extra, dk, layer):
    nt, batch = feats[0].shape[1], feats[0].shape[3]

    def head_spec(a):
        return pl.BlockSpec((1,) + a.shape[1:], lambda h: (h,) + (0,) * (a.ndim - 1))

    def st_spec(nl):
        return pl.BlockSpec((nl, 1, dk, HEAD_DIM, batch), lambda h: (0, h, 0, 0, 0))

    in_specs = [head_spec(a) for a in feats]
    in_specs.append(pl.BlockSpec((1, 1, dk, HEAD_DIM, batch), lambda h: (layer, h, 0, 0, 0)))
    args = list(feats) + [state_t]
    if layer > 0:
        in_specs.append(st_spec(layer))
        args.append(prev)
    o_shape = (HEADS, nt, HEAD_DIM, batch)
    return pl.pallas_call(
        body,
        grid=(HEADS,),
        in_specs=in_specs + [_const_spec(a) for a in extra],
        out_specs=[pl.BlockSpec((1,) + o_shape[1:], lambda h: (h, 0, 0, 0)), st_spec(layer + 1)],
        out_shape=[jax.ShapeDtypeStruct(o_shape, F32),
                   jax.ShapeDtypeStruct((layer + 1, HEADS, dk, HEAD_DIM, batch), F32)],
        compiler_params=_params(("arbitrary",)),
        name=name,
    )(*args, *extra)


def _gla_decode(q, k, g, v, gate, state_t, prev, dk, normw_col, layer, ret):
    nt = q.shape[1]
    body = functools.partial(_gla_decode_kernel, nt=nt, dk=dk, ret=ret, has_norm=normw_col is not None, layer=layer)
    nw = normw_col if normw_col is not None else jnp.ones((layer + 1, HEAD_DIM, 1), F32)
    return _decode_call(body, "ret_decode" if ret else f"gla_decode_dk{dk}", [q, k, g, v, gate], state_t, prev, [nw],
                        dk, layer)


def _gdn_decode(q, k, v, gate, bg, state_t, prev, normw_col, layer):
    body = functools.partial(_gdn_decode_kernel, nt=q.shape[1], layer=layer)
    return _decode_call(body, "gdn_decode", [q, k, v, gate, bg], state_t, prev, [normw_col], HEAD_DIM, layer)


def _rope_tables(pos):
    half = HEAD_DIM // 2
    inv = ROPE_BASE ** (-jnp.arange(half, dtype=F32) / half)
    ang = pos[:, None] * inv[None, :]
    cos, sin = jnp.cos(ang), jnp.sin(ang)
    cos_t = jnp.tile(jnp.concatenate([cos, cos], axis=1), (1, HEADS))
    sin_t = jnp.tile(jnp.concatenate([-sin, sin], axis=1), (1, HEADS))
    return cos_t, sin_t


def _relayout_w_in(w_in):
    parts = []
    for name, w in _SLABS:
        if name == "small":
            sm = jnp.concatenate([w_in[:, :, _SRC["alr"]:_SRC["alr"] + 16], w_in[:, :, _SRC["db"]:_SRC["db"] + 4],
                                  w_in[:, :, _SRC["da"]:_SRC["da"] + 4]], axis=2)
            parts.append(jnp.pad(sm, ((0, 0), (0, 0), (0, w - sm.shape[2]))))
        else:
            parts.append(w_in[:, :, _SRC[name]:_SRC[name] + w])
    return jnp.concatenate(parts, axis=2).astype(BF16)


def kernel(x_prompt, x_sample, state_ret, state_gla, state_hgrn, state_gdn, state_gdn_conv, c_prompt, c_sample, ada_w, ada_b, ln_g, ln_b, ffn1_wi, ffn1_wo, ffn2_wi, ffn2_wo, w_in, gla_wg, gla_bg, hg_lb, gdn_conv, gdn_a_log, gdn_dt_bias, gla_norm, hg_norm, gdn_norm, w_out):
    nl, d = ada_w.shape[0], ada_w.shape[1]
    bp, tp = x_prompt.shape[0], x_prompt.shape[1]
    bs, ts = x_sample.shape[0], x_sample.shape[1]
    alpha = (2.0 * nl) ** 0.25
    assert tp % CHUNK == 0 and ts >= CONV_W - 1

    wi1, wo1, wi2, wo2 = (w.astype(BF16) for w in (ffn1_wi, ffn1_wo, ffn2_wi, ffn2_wo))
    w_out_b = w_out.astype(BF16)
    w_in_p = _relayout_w_in(w_in)
    wg_p = jnp.pad(gla_wg, ((0, 0), (0, 128 - gla_wg.shape[1]), (0, 0))).astype(BF16)
    alog_e = jnp.repeat(gdn_a_log, HEAD_DIM, axis=1)
    dtb_e = jnp.repeat(gdn_dt_bias, HEAD_DIM, axis=1)
    gla_nw = jnp.tile(gla_norm, (1, HEADS))
    hg_nw = jnp.tile(hg_norm, (1, HEADS))
    gdn_nw = jnp.tile(gdn_norm, (1, HEADS))

    mod = _ada(jnp.concatenate([c_prompt, c_sample], axis=0), ada_w, ada_b)
    mod = mod.reshape(nl, bp + bs, N_MOD, d).transpose(0, 2, 1, 3)
    mod_p = mod[:, :, :bp].reshape(nl, N_MOD, bp, 1, d)
    mod_s = mod[:, :, bp:]

    gp = _Rows(bp, tp, decode=False)
    gp_prep = _Rows(bp, tp, decode=False, tile=PREP_TILE)
    gs = _Rows(bs, ts, decode=True)
    assert gs.steps == 1
    cos_p, sin_p = _rope_tables(jnp.arange(tp, dtype=F32))
    cos_s, sin_s = _rope_tables(PAST_LEN + jnp.arange(ts, dtype=F32))
    cos_s, sin_s = jnp.repeat(cos_s, bs, axis=0), jnp.repeat(sin_s, bs, axis=0)

    xp = x_prompt.reshape(bp * tp, d)
    xs = jnp.transpose(x_sample, (1, 0, 2)).reshape(ts * bs, d)

    st_ret, st_gla, st_hg, st_gdn = (jnp.transpose(s, (0, 2, 3, 4, 1))
                                     for s in (state_ret, state_gla, state_hgrn, state_gdn))
    conv_t = jnp.transpose(state_gdn_conv, (0, 2, 1, 3))
    dec_extra = (conv_t, gdn_a_log[:, :, None], gdn_dt_bias[:, :, None])
    ret_g = jnp.asarray(np.broadcast_to(_ret_gamma().astype(np.float32)[:, None, None, None], (HEADS, ts, 1, bs)))
    gla_nc, hg_nc, gdn_nc = gla_norm[:, :, None], hg_norm[:, :, None], gdn_norm[:, :, None]

    p_states, s_convs = [], []
    s_ret = s_gla = s_hg = s_gdn = None
    for l in range(nl):
        xp = _ffn(gp, xp, mod_p, (0, 1, 2), wi1, wo1, ln_g, ln_b, l, 0, alpha)
        ret_f, gla_f, hg_f, gdn_f, p_conv = _prep(gp_prep, xp, mod_p, w_in_p, cos_p, sin_p, wg_p, gla_bg, hg_lb,
                                                  gdn_conv, alog_e, dtb_e, None, l, nl)
        o_ret, p_ret = _gla_prompt(gp, ret_f, HEAD_DIM, None, l, ret=True)
        o_gla, p_gla = _gla_prompt(gp, gla_f, GLA_DK, gla_nw, l, ret=False)
        o_hg, p_hg = _gla_prompt(gp, hg_f, HEAD_DIM, hg_nw, l, ret=False)
        o_gdn, p_gdn = _gdn_prompt(gp, gdn_f, gdn_nw, l)
        xp = _mix_out(gp, xp, mod_p, (o_ret, o_gla, o_hg, o_gdn), w_out_b, ln_g, ln_b, l, alpha)
        xp = _ffn(gp, xp, mod_p, (6, 7, 8), wi2, wo2, ln_g, ln_b, l, 2, alpha)
        p_states.append((p_ret, p_gla, p_hg, p_gdn, p_conv))

        xs = _ffn(gs, xs, mod_s, (0, 1, 2), wi1, wo1, ln_g, ln_b, l, 0, alpha)
        (rq, rk, rv, rgt, aq, ak, ag, av, agt, hq, hk, hgg, hv, hgt, dq, dk_, dv, dgt, dbg, conv_new) = _prep(
            gs, xs, mod_s, w_in_p, cos_s, sin_s, wg_p, gla_bg, hg_lb, gdn_conv, alog_e, dtb_e, dec_extra, l, nl)
        o_ret, s_ret = _gla_decode(rq, rk, ret_g, rv, rgt, st_ret, s_ret, HEAD_DIM, None, l, True)
        o_gla, s_gla = _gla_decode(aq, ak, ag, av, agt, st_gla, s_gla, GLA_DK, gla_nc, l, False)
        o_hg, s_hg = _gla_decode(hq, hk, hgg, hv, hgt, st_hg, s_hg, HEAD_DIM, hg_nc, l, False)
        o_gdn, s_gdn = _gdn_decode(dq, dk_, dv, dgt, dbg, st_gdn, s_gdn, gdn_nc, l)
        xs = _mix_out(gs, xs, mod_s, (o_ret, o_gla, o_hg, o_gdn), w_out_b, ln_g, ln_b, l, alpha)
        xs = _ffn(gs, xs, mod_s, (6, 7, 8), wi2, wo2, ln_g, ln_b, l, 2, alpha)
        s_convs.append(conv_new)

    y_prompt = xp.reshape(bp, tp, d)
    y_sample = jnp.transpose(xs.reshape(ts, bs, d), (1, 0, 2))
    p_out = tuple(jnp.stack([st[i] for st in p_states]) for i in range(5))
    s_out = tuple(jnp.transpose(s, (0, 4, 1, 2, 3)) for s in (s_ret, s_gla, s_hg, s_gdn))
    return (y_prompt, y_sample) + p_out + s_out + (jnp.transpose(jnp.stack(s_convs), (0, 2, 1, 3)),)
```

```python
import functools

import numpy as np
import jax
import jax.numpy as jnp
from jax import lax
from jax.experimental import pallas as pl
from jax.experimental.pallas import tpu as pltpu

F32 = jnp.float32
BF16 = jnp.bfloat16

HEAD_DIM = 64
HEADS = 4
GLA_DK = 32
GLA_GATE_NORM = 16.0
CONV_W = 4
CHUNK = 64
ROPE_BASE = 10000.0
LN_EPS = 1e-5
RMS_EPS = 1e-6
PAST_LEN = 16384
N_MOD = 9
HW = HEADS * HEAD_DIM

ROW_TILE = 512
PREP_TILE = 256
MIX_TILE = 1024
DEC_DCHUNK = 8
FF_CHUNK = 512
V7X_VMEM_BYTES = 64 * 1024 * 1024
VMEM_LIMIT = V7X_VMEM_BYTES - 12 * 1024 * 1024

_SLABS = (("rq", 256), ("rk", 256), ("rv", 256), ("rg", 256),
          ("aq", 128), ("ak", 128), ("av", 256), ("ag", 256),
          ("hq", 256), ("hf", 256), ("hi", 256), ("hg", 256),
          ("dqkv", 768), ("dg", 256), ("small", 128))
_OFF = {}
_o = 0
for _n, _w in _SLABS:
    _OFF[_n] = (_o, _w)
    _o += _w
PROJ_W = _o
_SRC = {"rq": 0, "rk": 256, "rv": 512, "rg": 768, "aq": 1024, "ak": 1152, "av": 1280, "alr": 1536,
        "ag": 1552, "hq": 1808, "hf": 2064, "hi": 2320, "hg": 2576, "dqkv": 2832, "db": 3600,
        "da": 3604, "dg": 3608}
SMALL_ALR, SMALL_DB, SMALL_DA = 0, 16, 20


def _bf(x):
    return x.astype(BF16)


def _dot(a, b):
    return jnp.dot(a, b, preferred_element_type=F32)


def _dot_nt(a, b):
    return lax.dot_general(a, b, (((1,), (1,)), ((), ())), preferred_element_type=F32)


def _dot_tn(a, b):
    return lax.dot_general(a, b, (((0,), (0,)), ((), ())), preferred_element_type=F32)


def _split2(x):
    hi = x.astype(BF16)
    lo = (x - hi.astype(F32)).astype(BF16)
    return hi, lo


def _split3(x):
    hi = x.astype(BF16)
    r = x - hi.astype(F32)
    mid = r.astype(BF16)
    lo = (r - mid.astype(F32)).astype(BF16)
    return hi, mid, lo


def _dot_x01(x, m01):
    hi, lo = _split2(x)
    return _dot(hi, m01) + _dot(lo, m01)


def _dot_exact01(m01, x):
    h1, h2, h3 = _split3(x)
    return _dot(m01, h1) + _dot(m01, h2) + _dot(m01, h3)


def _silu(x):
    return x * jax.nn.sigmoid(x)


def _softplus(x):
    return jnp.maximum(x, 0.0) + jnp.log1p(jnp.exp(-jnp.abs(x)))


def _ln(y, g, b):
    mu = jnp.mean(y, axis=-1, keepdims=True)
    yc = y - mu
    var = jnp.mean(yc * yc, axis=-1, keepdims=True)
    return yc * lax.rsqrt(var + LN_EPS) * g + b


def _mod_rows(ref, tm):
    v = ref[...]
    v = v.reshape(v.shape[-2], v.shape[-1])
    nb = v.shape[0]
    if nb == 1:
        return v
    return jnp.concatenate([v] * (tm // nb), axis=0)


def _tile4(x):
    return jnp.concatenate([x, x, x, x], axis=0)


def _per_block(x, const, period):
    n = x.shape[0] // period
    return (x.reshape(n, period, x.shape[1]) * const[None]).reshape(x.shape)


def _cumsum_rows(g, period):
    row = lax.broadcasted_iota(jnp.int32, g.shape, 0) % period
    out = g
    sh = 1
    while sh < period:
        out = out + jnp.where(row >= sh, pltpu.roll(out, sh, 0), 0.0)
        sh *= 2
    return out


def _block_row(G, period, r):
    c, w = G.shape
    g3 = G.reshape(c // period, period, w)
    return jnp.broadcast_to(g3[:, r:r + 1, :], g3.shape).reshape(c, w)


def _gref(G, m):
    c = G.shape[0]
    if 2 * m >= 8:
        return _block_row(G, 2 * m, m - 1)
    p = lax.broadcasted_iota(jnp.int32, G.shape, 0) % (2 * m)
    if m == 2:
        return jnp.where(p == 0, pltpu.roll(G, c - 1, 0),
                         jnp.where(p == 1, G, jnp.where(p == 2, pltpu.roll(G, 1, 0), pltpu.roll(G, 2, 0))))
    return jnp.where(p == 0, G, pltpu.roll(G, 1, 0))


def _decay_att(q, k, G, mm, diag, lvl_ref, levels, period):
    att = _per_block(diag(q, k), lvl_ref[len(levels)], period)
    for i, m in enumerate(levels):
        if m == 1:
            ql = q * jnp.exp(jnp.minimum(G - pltpu.roll(G, 1, 0), 0.0))
            att = att + _per_block(diag(ql, pltpu.roll(k, 1, 0)), lvl_ref[i], period)
            continue
        gr = _gref(G, m)
        ql = q * jnp.exp(jnp.minimum(G - gr, 0.0))
        kl = k * jnp.exp(jnp.minimum(gr - G, 0.0))
        att = att + _per_block(mm(ql, kl), lvl_ref[i], period)
    return att


def _ret_gamma():
    return np.log(1.0 - 2.0 ** (-5.0 - np.arange(HEADS, dtype=np.float64)))


@functools.lru_cache(maxsize=None)
def _chunk_consts(dk):
    c = CHUNK
    h_att = np.arange(4 * c) // c
    s_att = np.arange(4 * c) % c
    t = np.arange(c)
    kmask = (h_att[:, None] == (np.arange(4 * dk) // dk)[None, :]).astype(np.float32)
    vmask = (h_att[:, None] == (np.arange(HW) // HEAD_DIM)[None, :]).astype(np.float32)
    smask = ((np.arange(HW) // HEAD_DIM)[:, None] == (np.arange(4 * dk) // dk)[None, :]).astype(np.float32)
    levels = (32, 16, 8, 4, 2, 1)
    lv = []
    for m in levels:
        same = (t[:, None] // (2 * m)) == (s_att[None, :] // (2 * m))
        lv.append(same & ((t[:, None] % (2 * m)) >= m) & ((s_att[None, :] % (2 * m)) < m))
    lv.append(t[:, None] == s_att[None, :])
    lvl = np.stack(lv).astype(np.float32)
    incl = (s_att[None, :] <= t[:, None]).astype(np.float32)
    strict = (s_att[None, :] < t[:, None]).astype(np.float32)
    eye = (s_att[None, :] == t[:, None]).astype(np.float32)
    fold = (np.arange(HW)[:, None] % HEAD_DIM == np.arange(HEAD_DIM)[None, :]).astype(np.float32)
    gam = _ret_gamma()
    gh = gam[h_att]
    ret_att = np.where(s_att[None, :] <= t[:, None], np.exp((t[:, None] - s_att[None, :]) * gh[None, :]), 0.0)
    gl = gam[np.arange(HW) // HEAD_DIM]
    ret_q = np.exp((t[:, None] + 1) * gl[None, :])
    ret_k = np.exp((c - 1 - t[:, None]) * gl[None, :])
    ret_s = np.exp(c * gl)[None, :]
    return dict(kmask=kmask, vmask=vmask, smask=smask, lvl=lvl, levels=levels, incl=incl, strict=strict,
                eye=eye, fold=fold, ret_att=ret_att.astype(np.float32), ret_q=ret_q.astype(np.float32),
                ret_k=ret_k.astype(np.float32), ret_s=ret_s.astype(np.float32))


def _bd_ones():
    h = np.arange(HW) // HEAD_DIM
    return (h[:, None] == h[None, :]).astype(np.float32)


def _expand_mat():
    e = np.zeros((128, 2 * HW), np.float32)
    for h in range(HEADS):
        e[SMALL_DB + h, h * HEAD_DIM:(h + 1) * HEAD_DIM] = 1.0
        e[SMALL_DA + h, HW + h * HEAD_DIM:HW + (h + 1) * HEAD_DIM] = 1.0
    return e


def _const_spec(a):
    nd = a.ndim
    return pl.BlockSpec(a.shape, lambda *_: (0,) * nd)


def _params(sem):
    return pltpu.CompilerParams(dimension_semantics=sem, vmem_limit_bytes=VMEM_LIMIT)


def _ada_kernel(c_ref, w_ref, b_ref, o_ref):
    c = c_ref[...]
    o_ref[0] = _dot(_bf(_silu(c)), _bf(w_ref[0])) + b_ref[0]


def _ada(c_all, ada_w, ada_b):
    nl, d, nd = ada_w.shape
    nb = c_all.shape[0]
    tn = d
    return pl.pallas_call(
        _ada_kernel,
        grid=(nl, nd // tn),
        in_specs=[pl.BlockSpec((nb, d), lambda l, j: (0, 0)),
                  pl.BlockSpec((1, d, tn), lambda l, j: (l, 0, j)),
                  pl.BlockSpec((1, 1, tn), lambda l, j: (l, 0, j))],
        out_specs=pl.BlockSpec((1, nb, tn), lambda l, j: (l, 0, j)),
        out_shape=jax.ShapeDtypeStruct((nl, nb, nd), F32),
        compiler_params=_params(("arbitrary", "arbitrary")),
        name="ada_mod",
    )(c_all, ada_w, ada_b.reshape(nl, 1, nd))


class _Rows:
    def __init__(self, batch, t, decode, tile=ROW_TILE):
        self.batch, self.t, self.decode = batch, t, decode
        self.rows = batch * t
        self.tm = min(tile, self.rows)
        assert self.rows % self.tm == 0
        if decode:
            assert self.tm % batch == 0
        else:
            assert t % self.tm == 0
        self.steps = self.rows // self.tm


def _mod_spec(g, layer, k, d):
    if g.decode:
        return pl.BlockSpec((1, 1, g.batch, d), lambda i: (layer, k, 0, 0))
    per = g.t // g.tm
    return pl.BlockSpec((1, 1, 1, 1, d), lambda i: (layer, k, i // per, 0, 0))


def _ffn_block(x, sh_ref, sc_ref, gt_ref, wi_ref, wo_ref, lng_ref, lnb_ref, *, alpha, ln_idx, layer):
    tm, d = x.shape
    dff = wo_ref.shape[1]
    sh, sc, gt = _mod_rows(sh_ref, tm), _mod_rows(sc_ref, tm), _mod_rows(gt_ref, tm)
    h = _bf(x * (1.0 + sc) + sh)
    acc = jnp.zeros((tm, d), F32)
    off = 0
    while off < dff:
        fc = min(FF_CHUNK, dff - off)
        a = _dot(h, wi_ref[0, :, off:off + fc])
        b = _dot(h, wi_ref[0, :, dff + off:dff + off + fc])
        acc = acc + _dot(_bf(_silu(a) * b), wo_ref[0, off:off + fc, :])
        off += fc
    y = alpha * x + 0.5 * (1.0 + gt) * acc
    return _ln(y, lng_ref[layer, ln_idx:ln_idx + 1, :], lnb_ref[layer, ln_idx:ln_idx + 1, :])


def _mix_block(x, gt_ref, o_refs, w_ref, lng_ref, lnb_ref, *, alpha, layer, decode, batch):
    tm, d = x.shape
    gt = _mod_rows(gt_ref, tm)
    if decode:
        accs = [jnp.zeros((batch, d), F32) for _ in range(tm // batch)]
        for mi, o_ref in enumerate(o_refs):
            for h in range(HEADS):
                r0 = mi * HW + h * HEAD_DIM
                wh = w_ref[0, r0:r0 + HEAD_DIM, :]
                for t in range(tm // batch):
                    accs[t] = accs[t] + _dot_tn(_bf(o_ref[h, t]), wh)
        acc = jnp.concatenate(accs, axis=0)
    else:
        acc = jnp.zeros((tm, d), F32)
        for mi, o_ref in enumerate(o_refs):
            acc = acc + _dot(o_ref[...], w_ref[0, mi * HW:(mi + 1) * HW, :])
    y = alpha * x + (1.0 + gt) * acc
    return _ln(y, lng_ref[layer, 1:2, :], lnb_ref[layer, 1:2, :])


def _ffn_kernel(x_ref, sh_ref, sc_ref, gt_ref, wi_ref, wo_ref, lng_ref, lnb_ref, o_ref, **kw):
    o_ref[...] = _ffn_block(x_ref[...], sh_ref, sc_ref, gt_ref, wi_ref, wo_ref, lng_ref, lnb_ref, **kw)


def _mix_ffn_kernel(x_ref, gtm_ref, o0_ref, o1_ref, o2_ref, o3_ref, w_ref, sh_ref, sc_ref, gt_ref, wi_ref, wo_ref,
                    lng_ref, lnb_ref, y_ref, *, alpha, layer, decode, batch):
    x1 = _mix_block(x_ref[...], gtm_ref, (o0_ref, o1_ref, o2_ref, o3_ref), w_ref, lng_ref, lnb_ref, alpha=alpha,
                    layer=layer, decode=decode, batch=batch)
    y_ref[...] = _ffn_block(x1, sh_ref, sc_ref, gt_ref, wi_ref, wo_ref, lng_ref, lnb_ref, alpha=alpha, ln_idx=2,
                            layer=layer)


def _ffn_specs(g, layer, mods, d, dff, ln_g, ln_b):
    return [_mod_spec(g, layer, k, d) for k in mods] + [
        pl.BlockSpec((1, d, 2 * dff), lambda i: (layer, 0, 0), pipeline_mode=pl.Buffered(1)),
        pl.BlockSpec((1, dff, d), lambda i: (layer, 0, 0), pipeline_mode=pl.Buffered(1)),
        _const_spec(ln_g), _const_spec(ln_b)]


def _ffn(g, x, mod, wi, wo, ln_g, ln_b, layer, alpha):
    d = x.shape[1]
    row = pl.BlockSpec((g.tm, d), lambda i: (i, 0))
    return pl.pallas_call(
        functools.partial(_ffn_kernel, alpha=alpha, ln_idx=0, layer=layer),
        grid=(g.steps,),
        in_specs=[row] + _ffn_specs(g, layer, (0, 1, 2), d, wo.shape[1], ln_g, ln_b),
        out_specs=row,
        out_shape=jax.ShapeDtypeStruct(x.shape, F32),
        compiler_params=_params(("arbitrary",)),
        name="ffn0",
    )(x, mod, mod, mod, wi, wo, ln_g, ln_b)


def _mix_ffn(g, x, mod, outs, w_out, wi, wo, ln_g, ln_b, layer, alpha):
    d = x.shape[1]
    row = pl.BlockSpec((g.tm, d), lambda i: (i, 0))
    if g.decode:
        assert g.steps == 1
        ospec = _const_spec(outs[0])
    else:
        ospec = pl.BlockSpec((g.tm, HW), lambda i: (i, 0))
    return pl.pallas_call(
        functools.partial(_mix_ffn_kernel, alpha=alpha, layer=layer, decode=g.decode, batch=g.batch),
        grid=(g.steps,),
        in_specs=[row, _mod_spec(g, layer, 5, d), ospec, ospec, ospec, ospec,
                  pl.BlockSpec((1, w_out.shape[1], d), lambda i: (layer, 0, 0), pipeline_mode=pl.Buffered(1))]
        + _ffn_specs(g, layer, (6, 7, 8), d, wo.shape[1], ln_g, ln_b),
        out_specs=row,
        out_shape=jax.ShapeDtypeStruct(x.shape, F32),
        compiler_params=_params(("arbitrary",)),
        name="mix_ffn2",
    )(x, mod, *outs, w_out, mod, mod, mod, wi, wo, ln_g, ln_b)


def _prep_kernel(*refs, layer, nlayers, g_t, tm, decode, batch):
    (x_ref, sh_ref, sc_ref, w_ref, cos_ref, sin_ref, wg_ref, bg_ref, lb_ref, cw_ref, alog_ref, dtb_ref,
     bd_ref, ex_ref) = refs[:14]
    rest = refs[14:]
    if decode:
        cs_ref, alogc_ref, dtbc_ref = rest[:3]
        outs = rest[3:]
        cbuf = None
    else:
        outs = rest[:-1]
        cbuf = rest[-1]

    x = x_ref[...]
    hb = _bf(x * (1.0 + _mod_rows(sc_ref, tm)) + _mod_rows(sh_ref, tm))

    def proj(name):
        o, w = _OFF[name]
        return _dot(hb, w_ref[0, :, o:o + w])

    bd = bd_ref[...]

    dest = {}
    if decode:
        it = iter(outs)
        for mix, keys in (("ret", ("q", "k", "v", "gate")), ("gla", ("q", "k", "g", "v", "gate")),
                          ("hg", ("q", "k", "g", "v", "gate")), ("gdn", ("q", "k", "v", "gate"))):
            for kk in keys:
                dest[f"{mix}_{kk}"] = ("heads", next(it), 0)
        bgo_ref, convo_ref = next(it), next(it)
    else:
        ret_ref, gla_ref, hg_ref, gdn_ref, conv_ref = outs
        for dst, keys, widths in ((ret_ref, ("ret_q", "ret_k", "ret_v", "ret_gate"), (HW,) * 4),
                                  (gla_ref, ("gla_q", "gla_k", "gla_g", "gla_v", "gla_gate"),
                                   (HEADS * GLA_DK,) * 3 + (HW, HW)),
                                  (hg_ref, ("hg_q", "hg_k", "hg_g", "hg_v", "hg_gate"), (HW,) * 5),
                                  (gdn_ref, ("gdn_q", "gdn_k", "gdn_b", "gdn_g", "gdn_v", "gdn_gate"), (HW,) * 6)):
            off = 0
            for kk, w in zip(keys, widths):
                dest[kk] = ("rows", dst, off)
                off += w

    def put(name, val):
        kind, ref, off = dest[name]
        if kind == "rows":
            ref[:, off:off + val.shape[1]] = val
        else:
            dh = val.shape[1] // HEADS
            for t in range(tm // batch):
                vt = val[t * batch:(t + 1) * batch].T
                for h in range(HEADS):
                    ref[h, t] = vt[h * dh:(h + 1) * dh]

    small = proj("small")

    dq = proj("dqkv")
    cw = cw_ref[0]
    if decode:
        cs = cs_ref[0]
        nt = tm // batch
        blocks = [dq[t * batch:(t + 1) * batch] for t in range(nt)]
        convs = []
        for t in range(nt):
            acc = blocks[t] * cw[CONV_W - 1:CONV_W, :]
            for s in range(1, CONV_W):
                prev = blocks[t - s] if t >= s else cs[CONV_W - 1 - s + t]
                acc = acc + prev * cw[CONV_W - 1 - s:CONV_W - s, :]
            convs.append(acc)
        conv = jnp.concatenate(convs, axis=0)
        convo_ref[...] = dq[(nt - (CONV_W - 1)) * batch:].reshape(CONV_W - 1, batch, dq.shape[1])
    else:
        i = pl.program_id(0)
        per = g_t // tm

        @pl.when(i % per == 0)
        def _():
            cbuf[0:8, :] = jnp.zeros((8, dq.shape[1]), F32)

        cbuf[8:8 + tm, :] = dq
        conv = cbuf[pl.ds(8, tm), :] * cw[CONV_W - 1:CONV_W, :]
        for s in range(1, CONV_W):
            conv = conv + cbuf[pl.ds(8 - s, tm), :] * cw[CONV_W - 1 - s:CONV_W - s, :]
        tail = cbuf[tm:tm + 8, :]
        cbuf[0:8, :] = tail

        @pl.when(i % per == per - 1)
        def _():
            conv_ref[0] = tail[8 - (CONV_W - 1):8, :]

    u = _silu(conv)
    uq, uk = u[:, 0:HW], u[:, HW:2 * HW]
    put("gdn_q", uq * lax.rsqrt(_dot_x01(uq * uq, bd) + RMS_EPS) * (HEAD_DIM ** -0.5))
    put("gdn_k", uk * lax.rsqrt(_dot_x01(uk * uk, bd) + RMS_EPS))
    put("gdn_v", u[:, 2 * HW:3 * HW])
    put("gdn_gate", _silu(proj("dg")))
    if decode:
        for t in range(tm // batch):
            smt = small[t * batch:(t + 1) * batch].T
            beta = jax.nn.sigmoid(smt[SMALL_DB:SMALL_DB + HEADS])
            gd = -jnp.exp(alogc_ref[layer]) * _softplus(smt[SMALL_DA:SMALL_DA + HEADS] + dtbc_ref[layer])
            for h in range(HEADS):
                bgo_ref[h, t, 0:1, :] = beta[h:h + 1]
                bgo_ref[h, t, 1:2, :] = gd[h:h + 1]
    else:
        dbda = _dot_x01(small, ex_ref[...])
        put("gdn_b", jax.nn.sigmoid(dbda[:, 0:HW]))
        put("gdn_g", -jnp.exp(alog_ref[layer:layer + 1, :]) * _softplus(dbda[:, HW:2 * HW] + dtb_ref[layer:layer + 1, :]))

    rows = [lb_ref[i:i + 1, :] for i in range(nlayers)]
    mx = functools.reduce(jnp.maximum, rows)
    ex = [jnp.exp(r - mx) for r in rows]
    tot = functools.reduce(lambda a, b: a + b, ex)
    plb = [e / tot for e in ex]
    lb = functools.reduce(lambda a, b: a + b, plb[:layer + 1]) - plb[0]
    zf = proj("hf")
    put("hg_q", _silu(proj("hq")) * (HEAD_DIM ** -0.5))
    put("hg_k", (1.0 - lb) * jax.nn.sigmoid(-zf))
    put("hg_g", jnp.log(lb + (1.0 - lb) * jax.nn.sigmoid(zf)))
    put("hg_v", proj("hi"))
    put("hg_gate", _silu(proj("hg")))

    gpre = _dot(_bf(small), wg_ref[0]) + bg_ref[layer:layer + 1, :]
    put("gla_q", proj("aq") * (GLA_DK ** -0.5))
    put("gla_k", proj("ak"))
    put("gla_g", -_softplus(-gpre) * (1.0 / GLA_GATE_NORM))
    put("gla_v", proj("av"))
    put("gla_gate", _silu(proj("ag")))

    cos, sin = cos_ref[...], sin_ref[...]
    lane = lax.broadcasted_iota(jnp.int32, (tm, HW), 1)
    first_half = (lane % HEAD_DIM) < (HEAD_DIM // 2)

    def rope(v):
        sw = jnp.where(first_half, pltpu.roll(v, HW - HEAD_DIM // 2, 1), pltpu.roll(v, HEAD_DIM // 2, 1))
        return v * cos + sw * sin

    put("ret_q", rope(proj("rq")))
    put("ret_k", rope(proj("rk")) * (HEAD_DIM ** -0.5))
    put("ret_v", proj("rv"))
    put("ret_gate", _silu(proj("rg")))


def _prep(g, x, mod, w_in_p, cos_t, sin_t, wg_p, gla_bg, hg_lb, gdn_conv, alog_e, dtb_e, dec, layer, nlayers):
    d = x.shape[1]
    bd = jnp.asarray(_bd_ones(), BF16)
    exm = jnp.asarray(_expand_mat(), BF16)
    row = pl.BlockSpec((g.tm, d), lambda i: (i, 0))
    if g.decode:
        tab = pl.BlockSpec((g.tm, HW), lambda i: (i, 0))
    else:
        per = g.t // g.tm
        tab = pl.BlockSpec((g.tm, HW), lambda i: (i % per, 0))
    in_specs = [row, _mod_spec(g, layer, 3, d), _mod_spec(g, layer, 4, d),
                pl.BlockSpec((1, d, PROJ_W), lambda i: (layer, 0, 0), pipeline_mode=pl.Buffered(1)),
                tab, tab,
                pl.BlockSpec((1,) + wg_p.shape[1:], lambda i: (layer, 0, 0)),
                _const_spec(gla_bg), _const_spec(hg_lb),
                pl.BlockSpec((1,) + gdn_conv.shape[1:], lambda i: (layer, 0, 0)),
                _const_spec(alog_e), _const_spec(dtb_e), _const_spec(bd), _const_spec(exm)]
    args = [x, mod, mod, w_in_p, cos_t, sin_t, wg_p, gla_bg, hg_lb, gdn_conv, alog_e, dtb_e, bd, exm]
    cch = gdn_conv.shape[2]

    def rows_out(w):
        return jax.ShapeDtypeStruct((g.rows, w), F32), pl.BlockSpec((g.tm, w), lambda i: (i, 0))

    def whole(shape):
        return jax.ShapeDtypeStruct(shape, F32), pl.BlockSpec(shape, lambda i: (0,) * len(shape))

    if g.decode:
        assert g.steps == 1 and g.t >= CONV_W - 1
        cs, alog_c, dtb_c = dec
        in_specs += [pl.BlockSpec((1,) + cs.shape[1:], lambda i: (layer, 0, 0, 0)), _const_spec(alog_c),
                     _const_spec(dtb_c)]
        args += [cs, alog_c, dtb_c]
        outs = []
        for dhs in ((HEAD_DIM,) * 4, (GLA_DK,) * 3 + (HEAD_DIM,) * 2, (HEAD_DIM,) * 5, (HEAD_DIM,) * 4):
            outs += [whole((HEADS, g.t, dh, g.batch)) for dh in dhs]
        outs += [whole((HEADS, g.t, 2, g.batch)), whole((CONV_W - 1, g.batch, cch))]
        scratch = []
    else:
        outs = [rows_out(4 * HW), rows_out(3 * HEADS * GLA_DK + 2 * HW), rows_out(5 * HW), rows_out(6 * HW),
                (jax.ShapeDtypeStruct((g.batch, CONV_W - 1, cch), F32),
                 pl.BlockSpec((1, CONV_W - 1, cch), lambda i: (i // (g.t // g.tm), 0, 0)))]
        scratch = [pltpu.VMEM((g.tm + 8, cch), F32)]
    return pl.pallas_call(
        functools.partial(_prep_kernel, layer=layer, nlayers=nlayers, g_t=g.t, tm=g.tm, decode=g.decode,
                          batch=g.batch),
        grid=(g.steps,),
        in_specs=in_specs,
        out_specs=[o[1] for o in outs],
        out_shape=[o[0] for o in outs],
        scratch_shapes=scratch,
        compiler_params=_params(("arbitrary",)),
        name="prep",
    )(*args)


def _chunk(x, c):
    return x[c * CHUNK:(c + 1) * CHUNK]


def _per_chunk(fn, nch):
    return jnp.concatenate([fn(c) for c in range(nch)], axis=0)


def _finish_prompt(o, gate, normw, bd, o_ref):
    ms = _dot_x01(o * o, bd) * (1.0 / HEAD_DIM)
    on = o * lax.rsqrt(ms + RMS_EPS)
    if normw is not None:
        on = on * normw
    o_ref[...] = _bf(on * gate)


def _state_out(s_t, fold_ref, st_ref):
    fold = fold_ref[...]
    h1, h2, h3 = _split3(s_t)
    st_ref[0] = _dot_tn(h1, fold) + _dot_tn(h2, fold) + _dot_tn(h3, fold)


def _gla_prompt_kernel(f_ref, kmask_ref, vmask_ref, smask_ref, lvl_ref, fold_ref, bd_ref, nw_ref, ratt_ref, rq_ref,
                       rk_ref, rs_ref, o_ref, st_ref, s_scr, *, dk, nch, ret, levels, has_norm, layer):
    j = pl.program_id(1)
    dkk = HEADS * dk

    @pl.when(j == 0)
    def _():
        s_scr[...] = jnp.zeros(s_scr.shape, F32)

    kmask, vmask, smask, bd = kmask_ref[...], vmask_ref[...], smask_ref[...], bd_ref[...]
    normw = nw_ref[layer:layer + 1, :] if has_norm else None

    def mm(x, y):
        xb, yb = _bf(x), _bf(y)
        return _per_chunk(lambda c: _dot_nt(_chunk(xb, c), _tile4(_chunk(yb, c)) * kmask), nch)

    q = f_ref[:, 0:dkk]
    k = f_ref[:, dkk:2 * dkk]
    if ret:
        v = f_ref[:, 2 * dkk:2 * dkk + HW]
        gate = f_ref[:, 2 * dkk + HW:2 * dkk + 2 * HW]
        att = _per_block(mm(q, k), ratt_ref[...], CHUNK)
        qg = _per_block(q, rq_ref[...], CHUNK)
        kd = _per_block(k, rk_ref[...], CHUNK)
        sdec = [rs_ref[...]] * nch
    else:
        g = f_ref[:, 2 * dkk:3 * dkk]
        v = f_ref[:, 3 * dkk:3 * dkk + HW]
        gate = f_ref[:, 3 * dkk + HW:3 * dkk + 2 * HW]
        G = _cumsum_rows(g, CHUNK)
        att = _decay_att(q, k, G, mm, lambda x, y: _dot_nt(_bf(x * y), kmask), lvl_ref, levels, CHUNK)
        qg = q * jnp.exp(G)
        glb = _block_row(G, CHUNK, CHUNK - 1)
        kd = k * jnp.exp(glb - G)
        sdec = [jnp.exp(G[(c + 1) * CHUNK - 1:(c + 1) * CHUNK, :]) for c in range(nch)]
    vb, qgb, kdb, attb = _bf(v), _bf(qg), _bf(kd), _bf(att)
    o_intra = _per_chunk(lambda c: _dot(_chunk(attb, c), _tile4(_chunk(vb, c)) * vmask), nch)
    upd = [_dot_tn(_chunk(vb, c), _chunk(kdb, c)) * smask for c in range(nch)]

    s = s_scr[...]
    o_inter = []
    for c in range(nch):
        o_inter.append(_dot_nt(_chunk(qgb, c), _bf(s)))
        s = s * sdec[c] + upd[c]
    s_scr[...] = s
    _finish_prompt(o_intra + jnp.concatenate(o_inter, axis=0), gate, normw, bd, o_ref)

    @pl.when(j == pl.num_programs(1) - 1)
    def _():
        _state_out(s, fold_ref, st_ref)


def _gla_prompt(g, feat, dk, normw, layer, ret):
    c = _chunk_consts(dk)
    dkk = HEADS * dk
    blk = min(MIX_TILE, g.t)
    nch = blk // CHUNK
    per = g.t // blk
    consts = [jnp.asarray(c["kmask"], BF16), jnp.asarray(c["vmask"], BF16), jnp.asarray(c["smask"]),
              jnp.asarray(c["lvl"]), jnp.asarray(c["fold"], BF16), jnp.asarray(_bd_ones(), BF16),
              normw if normw is not None else jnp.ones((1, HW), F32),
              jnp.asarray(c["ret_att"]), jnp.asarray(c["ret_q"]), jnp.asarray(c["ret_k"]), jnp.asarray(c["ret_s"])]
    w = feat.shape[1]
    o, st = pl.pallas_call(
        functools.partial(_gla_prompt_kernel, dk=dk, nch=nch, ret=ret, levels=c["levels"],
                          has_norm=normw is not None, layer=layer),
        grid=(g.batch, per),
        in_specs=[pl.BlockSpec((blk, w), lambda b, j: (b * per + j, 0))] + [_const_spec(a) for a in consts],
        out_specs=[pl.BlockSpec((blk, HW), lambda b, j: (b * per + j, 0)),
                   pl.BlockSpec((1, dkk, HEAD_DIM), lambda b, j: (b, 0, 0))],
        out_shape=[jax.ShapeDtypeStruct((g.rows, HW), BF16),
                   jax.ShapeDtypeStruct((g.batch, dkk, HEAD_DIM), F32)],
        scratch_shapes=[pltpu.VMEM((HW, dkk), F32)],
        compiler_params=_params(("arbitrary", "arbitrary")),
        name="ret_prompt" if ret else f"gla_prompt_dk{dk}",
    )(feat, *consts)
    return o, st.reshape(g.batch, HEADS, dk, HEAD_DIM)


def _gdn_prompt_kernel(f_ref, bdm_ref, smask_ref, incl_ref, strict_ref, eye_ref, fold_ref, bd_ref, nw_ref, o_ref,
                       st_ref, s_scr, *, nch, layer):
    j = pl.program_id(1)

    @pl.when(j == 0)
    def _():
        s_scr[...] = jnp.zeros(s_scr.shape, F32)

    bdm, smask, incl, strict, eye, bd = (bdm_ref[...], smask_ref[...], incl_ref[...], strict_ref[...], eye_ref[...],
                                         bd_ref[...])
    normw = nw_ref[layer:layer + 1, :]
    ones_c = jnp.ones((CHUNK, CHUNK), BF16)

    def bdw(yb, c):
        return _tile4(_chunk(yb, c)) * bdm

    def prod1(x, y):
        xb, yb = _bf(x), _bf(y)
        return _per_chunk(lambda c: _dot(_chunk(xb, c), bdw(yb, c)), nch)

    def prod1_pair(x1, x2, y):
        x1b, x2b, yb = _bf(x1), _bf(x2), _bf(y)
        both = [_dot(jnp.concatenate([_chunk(x1b, c), _chunk(x2b, c)], axis=0), bdw(yb, c)) for c in range(nch)]
        return (jnp.concatenate([m[:CHUNK] for m in both], axis=0),
                jnp.concatenate([m[CHUNK:] for m in both], axis=0))

    def prod2(x, y):
        xh, xl = _split2(x)
        yb = _bf(y)

        def one(c):
            top = _dot(jnp.concatenate([_chunk(xh, c), _chunk(xl, c)], axis=0), bdw(yb, c))
            return top[:CHUNK] + top[CHUNK:]

        return _per_chunk(one, nch)

    def prod3(x, y):
        xh, xl = _split2(x)
        yh, yl = _split2(y)

        def one(c):
            top = _dot(jnp.concatenate([_chunk(xh, c), _chunk(xl, c)], axis=0), bdw(yh, c))
            return top[:CHUNK] + top[CHUNK:] + _dot(_chunk(xh, c), bdw(yl, c))

        return _per_chunk(one, nch)

    q, k = f_ref[:, 0:HW], f_ref[:, HW:2 * HW]
    be, ge = f_ref[:, 2 * HW:3 * HW], f_ref[:, 3 * HW:4 * HW]
    v, gate = f_ref[:, 4 * HW:5 * HW], f_ref[:, 5 * HW:6 * HW]
    G = _cumsum_rows(ge, CHUNK)
    zd = _per_block(G, eye, CHUNK)
    grow = _per_chunk(lambda c: _dot_exact01(ones_c, _chunk(zd, c)), nch)
    L = _per_block(jnp.exp(jnp.minimum(G - grow, 0.0)), incl, CHUNK)
    qb, kb = _bf(q), _bf(k)

    def qkk(c):
        return _dot_nt(jnp.concatenate([_chunk(qb, c), _chunk(kb, c)], axis=0), bdw(kb, c))

    qkk_all = [qkk(c) for c in range(nch)]
    qk = jnp.concatenate([m[:CHUNK] for m in qkk_all], axis=0)
    kk = jnp.concatenate([m[CHUNK:] for m in qkk_all], axis=0)
    a = _per_block(be * kk * L, strict, CHUNK)
    eye_b = _per_block(jnp.ones_like(a), eye, CHUNK)
    p = -a
    tinv = eye_b + p
    for i in range(5):
        if i == 0:
            p = prod1(p, p)
        else:
            p, dt = prod1_pair(p, tinv, p)
            tinv = tinv + dt
    tinv = tinv + prod1(tinv, p)
    resid = eye_b - tinv - prod3(a, tinv)
    tinv = tinv + prod1(tinv, resid)
    eg = jnp.exp(G)
    u = prod2(tinv, be * v)
    w = prod2(tinv, be * eg * k)
    glb = _block_row(G, CHUNK, CHUNK - 1)
    wb, qeb, qklb = _bf(w), _bf(q * eg), _bf(qk * L)
    kdb = _bf(k * jnp.exp(glb - G))
    sdec = [jnp.exp(G[(c + 1) * CHUNK - 1:(c + 1) * CHUNK, :]) for c in range(nch)]

    ub = _bf(u)
    s_add = [_dot_tn(_chunk(ub, c), _chunk(kdb, c)) * smask for c in range(nch)]
    s_mul = [_bf(_dot_tn(_chunk(wb, c), _chunk(kdb, c)) * smask) for c in range(nch)]

    s = s_scr[...]
    snaps = []
    for c in range(nch):
        sb = _bf(s)
        snaps.append(sb)
        s = s * sdec[c] + s_add[c] - _dot(sb, s_mul[c])
    s_scr[...] = s

    outs = []
    for c in range(nch):
        ws = _dot_nt(jnp.concatenate([_chunk(wb, c), _chunk(qeb, c)], axis=0), snaps[c])
        db = _bf(_chunk(u, c) - ws[:CHUNK])
        outs.append(_dot(_chunk(qklb, c), _tile4(db) * bdm) + ws[CHUNK:])
    _finish_prompt(jnp.concatenate(outs, axis=0), gate, normw, bd, o_ref)

    @pl.when(j == pl.num_programs(1) - 1)
    def _():
        _state_out(s, fold_ref, st_ref)


def _gdn_prompt(g, feat, normw, layer):
    c = _chunk_consts(HEAD_DIM)
    blk = min(MIX_TILE, g.t)
    nch = blk // CHUNK
    per = g.t // blk
    consts = [jnp.asarray(c["vmask"], BF16), jnp.asarray(c["smask"]), jnp.asarray(c["incl"]), jnp.asarray(c["strict"]),
              jnp.asarray(c["eye"]), jnp.asarray(c["fold"], BF16), jnp.asarray(_bd_ones(), BF16), normw]
    w = feat.shape[1]
    o, st = pl.pallas_call(
        functools.partial(_gdn_prompt_kernel, nch=nch, layer=layer),
        grid=(g.batch, per),
        in_specs=[pl.BlockSpec((blk, w), lambda b, j: (b * per + j, 0))] + [_const_spec(a) for a in consts],
        out_specs=[pl.BlockSpec((blk, HW), lambda b, j: (b * per + j, 0)),
                   pl.BlockSpec((1, HW, HEAD_DIM), lambda b, j: (b, 0, 0))],
        out_shape=[jax.ShapeDtypeStruct((g.rows, HW), BF16),
                   jax.ShapeDtypeStruct((g.batch, HW, HEAD_DIM), F32)],
        scratch_shapes=[pltpu.VMEM((HW, HW), F32)],
        compiler_params=_params(("arbitrary", "arbitrary")),
        name="gdn_prompt",
    )(feat, *consts)
    return o, st.reshape(g.batch, HEADS, HEAD_DIM, HEAD_DIM)


def _state_io(refs, layer):
    if layer == 0:
        return refs
    prev_ref, rest = refs[0], refs[1:]
    rest[-1][0:layer] = prev_ref[...]
    return rest


def _dec_finish(o, gate, normw):
    ms = jnp.mean(o * o, axis=0, keepdims=True)
    on = o * lax.rsqrt(ms + RMS_EPS)
    if normw is not None:
        on = on * normw
    return on * gate


def _gla_decode_kernel(q_ref, k_ref, g_ref, v_ref, gate_ref, s_ref, *refs, nt, dk, ret, has_norm, layer):
    nw_ref, o_ref, so_ref = _state_io(refs, layer)
    normw = nw_ref[layer] if has_norm else None
    qs = [q_ref[0, t] for t in range(nt)]
    ks = [k_ref[0, t] for t in range(nt)]
    vs = [v_ref[0, t] for t in range(nt)]
    decay = [jnp.exp(g_ref[0, t]) for t in range(nt)]
    o_acc = [jnp.zeros(vs[0].shape, F32) for _ in range(nt)]
    for dc in range(dk // DEC_DCHUNK):
        r = slice(dc * DEC_DCHUNK, (dc + 1) * DEC_DCHUNK)
        sc = s_ref[0, 0, r]
        for t in range(nt):
            a = decay[t][None] if ret else decay[t][r][:, None, :]
            sc = sc * a + ks[t][r][:, None, :] * vs[t][None, :, :]
            o_acc[t] = o_acc[t] + jnp.sum(qs[t][r][:, None, :] * sc, axis=0)
        so_ref[layer, 0, r] = sc
    for t in range(nt):
        o_ref[0, t] = _dec_finish(o_acc[t], gate_ref[0, t], normw)


def _gdn_decode_kernel(q_ref, k_ref, v_ref, gate_ref, bg_ref, s_ref, *refs, nt, layer):
    nw_ref, o_ref, so_ref = _state_io(refs, layer)
    normw = nw_ref[layer]
    chunks = [slice(dc * DEC_DCHUNK, (dc + 1) * DEC_DCHUNK) for dc in range(HEAD_DIM // DEC_DCHUNK)]
    for t in range(nt):
        q, k, v = q_ref[0, t], k_ref[0, t], v_ref[0, t]
        beta = bg_ref[0, t, 0:1, :]
        ag = jnp.exp(bg_ref[0, t, 1:2, :])

        def state(r):
            return s_ref[0, 0, r] if t == 0 else so_ref[layer, 0, r]

        ks = jnp.zeros(v.shape, F32)
        for r in chunks:
            ks = ks + jnp.sum(k[r][:, None, :] * state(r), axis=0)
        delta = beta * (v - ag * ks)
        o = jnp.zeros(v.shape, F32)
        for r in chunks:
            sc = state(r) * ag[None] + k[r][:, None, :] * delta[None, :, :]
            so_ref[layer, 0, r] = sc
            o = o + jnp.sum(q[r][:, None, :] * sc, axis=0)
        o_ref[0, t] = _dec_finish(o, gate_ref[0, t], normw)


def _decode_call(body, name, feats, state_t, prev, extra, dk, layer):
    nt, batch = feats[0].shape[1], feats[0].shape[3]

    def head_spec(a):
        return pl.BlockSpec((1,) + a.shape[1:], lambda h: (h,) + (0,) * (a.ndim - 1))

    def st_spec(nl):
        return pl.BlockSpec((nl, 1, dk, HEAD_DIM, batch), lambda h: (0, h, 0, 0, 0))

    in_specs = [head_spec(a) for a in feats]
    in_specs.append(pl.BlockSpec((1, 1, dk, HEAD_DIM, batch), lambda h: (layer, h, 0, 0, 0)))
    args = list(feats) + [state_t]
    if layer > 0:
        in_specs.append(st_spec(layer))
        args.append(prev)
    o_shape = (HEADS, nt, HEAD_DIM, batch)
    return pl.pallas_call(
        body,
        grid=(HEADS,),
        in_specs=in_specs + [_const_spec(a) for a in extra],
        out_specs=[pl.BlockSpec((1,) + o_shape[1:], lambda h: (h, 0, 0, 0)), st_spec(layer + 1)],
        out_shape=[jax.ShapeDtypeStruct(o_shape, F32),
                   jax.ShapeDtypeStruct((layer + 1, HEADS, dk, HEAD_DIM, batch), F32)],
        compiler_params=_params(("arbitrary",)),
        name=name,
    )(*args, *extra)


def _gla_decode(q, k, g, v, gate, state_t, prev, dk, normw_col, layer, ret):
    nt = q.shape[1]
    body = functools.partial(_gla_decode_kernel, nt=nt, dk=dk, ret=ret, has_norm=normw_col is not None, layer=layer)
    nw = normw_col if normw_col is not None else jnp.ones((layer + 1, HEAD_DIM, 1), F32)
    return _decode_call(body, "ret_decode" if ret else f"gla_decode_dk{dk}", [q, k, g, v, gate], state_t, prev, [nw],
                        dk, layer)


def _gdn_decode(q, k, v, gate, bg, state_t, prev, normw_col, layer):
    body = functools.partial(_gdn_decode_kernel, nt=q.shape[1], layer=layer)
    return _decode_call(body, "gdn_decode", [q, k, v, gate, bg], state_t, prev, [normw_col], HEAD_DIM, layer)


def _rope_tables(pos):
    half = HEAD_DIM // 2
    inv = ROPE_BASE ** (-jnp.arange(half, dtype=F32) / half)
    ang = pos[:, None] * inv[None, :]
    cos, sin = jnp.cos(ang), jnp.sin(ang)
    cos_t = jnp.tile(jnp.concatenate([cos, cos], axis=1), (1, HEADS))
    sin_t = jnp.tile(jnp.concatenate([-sin, sin], axis=1), (1, HEADS))
    return cos_t, sin_t


def _relayout_w_in(w_in):
    parts = []
    for name, w in _SLABS:
        if name == "small":
            sm = jnp.concatenate([w_in[:, :, _SRC["alr"]:_SRC["alr"] + 16], w_in[:, :, _SRC["db"]:_SRC["db"] + 4],
                                  w_in[:, :, _SRC["da"]:_SRC["da"] + 4]], axis=2)
            parts.append(jnp.pad(sm, ((0, 0), (0, 0), (0, w - sm.shape[2]))))
        else:
            parts.append(w_in[:, :, _SRC[name]:_SRC[name] + w])
    return jnp.concatenate(parts, axis=2).astype(BF16)


def kernel(x_prompt, x_sample, state_ret, state_gla, state_hgrn, state_gdn, state_gdn_conv, c_prompt, c_sample, ada_w, ada_b, ln_g, ln_b, ffn1_wi, ffn1_wo, ffn2_wi, ffn2_wo, w_in, gla_wg, gla_bg, hg_lb, gdn_conv, gdn_a_log, gdn_dt_bias, gla_norm, hg_norm, gdn_norm, w_out):
    nl, d = ada_w.shape[0], ada_w.shape[1]
    bp, tp = x_prompt.shape[0], x_prompt.shape[1]
    bs, ts = x_sample.shape[0], x_sample.shape[1]
    alpha = (2.0 * nl) ** 0.25
    assert tp % CHUNK == 0 and ts >= CONV_W - 1

    wi1, wo1, wi2, wo2 = (w.astype(BF16) for w in (ffn1_wi, ffn1_wo, ffn2_wi, ffn2_wo))
    w_out_b = w_out.astype(BF16)
    w_in_p = _relayout_w_in(w_in)
    wg_p = jnp.pad(gla_wg, ((0, 0), (0, 128 - gla_wg.shape[1]), (0, 0))).astype(BF16)
    alog_e = jnp.repeat(gdn_a_log, HEAD_DIM, axis=1)
    dtb_e = jnp.repeat(gdn_dt_bias, HEAD_DIM, axis=1)
    gla_nw = jnp.tile(gla_norm, (1, HEADS))
    hg_nw = jnp.tile(hg_norm, (1, HEADS))
    gdn_nw = jnp.tile(gdn_norm, (1, HEADS))

    mod = _ada(jnp.concatenate([c_prompt, c_sample], axis=0), ada_w, ada_b)
    mod = mod.reshape(nl, bp + bs, N_MOD, d).transpose(0, 2, 1, 3)
    mod_p = mod[:, :, :bp].reshape(nl, N_MOD, bp, 1, d)
    mod_s = mod[:, :, bp:]

    gp = _Rows(bp, tp, decode=False)
    gp_prep = _Rows(bp, tp, decode=False, tile=PREP_TILE)
    gs = _Rows(bs, ts, decode=True)
    assert gs.steps == 1
    cos_p, sin_p = _rope_tables(jnp.arange(tp, dtype=F32))
    cos_s, sin_s = _rope_tables(PAST_LEN + jnp.arange(ts, dtype=F32))
    cos_s, sin_s = jnp.repeat(cos_s, bs, axis=0), jnp.repeat(sin_s, bs, axis=0)

    xp = x_prompt.reshape(bp * tp, d)
    xs = jnp.transpose(x_sample, (1, 0, 2)).reshape(ts * bs, d)

    st_ret, st_gla, st_hg, st_gdn = (jnp.transpose(s, (0, 2, 3, 4, 1))
                                     for s in (state_ret, state_gla, state_hgrn, state_gdn))
    conv_t = jnp.transpose(state_gdn_conv, (0, 2, 1, 3))
    dec_extra = (conv_t, gdn_a_log[:, :, None], gdn_dt_bias[:, :, None])
    ret_g = jnp.asarray(np.broadcast_to(_ret_gamma().astype(np.float32)[:, None, None, None], (HEADS, ts, 1, bs)))
    gla_nc, hg_nc, gdn_nc = gla_norm[:, :, None], hg_norm[:, :, None], gdn_norm[:, :, None]

    p_states, s_convs = [], []
    s_ret = s_gla = s_hg = s_gdn = None
    for l in range(nl):
        xp = _ffn(gp, xp, mod_p, wi1, wo1, ln_g, ln_b, l, alpha)
        ret_f, gla_f, hg_f, gdn_f, p_conv = _prep(gp_prep, xp, mod_p, w_in_p, cos_p, sin_p, wg_p, gla_bg, hg_lb,
                                                  gdn_conv, alog_e, dtb_e, None, l, nl)
        o_ret, p_ret = _gla_prompt(gp, ret_f, HEAD_DIM, None, l, ret=True)
        o_gla, p_gla = _gla_prompt(gp, gla_f, GLA_DK, gla_nw, l, ret=False)
        o_hg, p_hg = _gla_prompt(gp, hg_f, HEAD_DIM, hg_nw, l, ret=False)
        o_gdn, p_gdn = _gdn_prompt(gp, gdn_f, gdn_nw, l)
        xp = _mix_ffn(gp, xp, mod_p, (o_ret, o_gla, o_hg, o_gdn), w_out_b, wi2, wo2, ln_g, ln_b, l, alpha)
        p_states.append((p_ret, p_gla, p_hg, p_gdn, p_conv))

        xs = _ffn(gs, xs, mod_s, wi1, wo1, ln_g, ln_b, l, alpha)
        (rq, rk, rv, rgt, aq, ak, ag, av, agt, hq, hk, hgg, hv, hgt, dq, dk_, dv, dgt, dbg, conv_new) = _prep(
            gs, xs, mod_s, w_in_p, cos_s, sin_s, wg_p, gla_bg, hg_lb, gdn_conv, alog_e, dtb_e, dec_extra, l, nl)
        o_ret, s_ret = _gla_decode(rq, rk, ret_g, rv, rgt, st_ret, s_ret, HEAD_DIM, None, l, True)
        o_gla, s_gla = _gla_decode(aq, ak, ag, av, agt, st_gla, s_gla, GLA_DK, gla_nc, l, False)
        o_hg, s_hg = _gla_decode(hq, hk, hgg, hv, hgt, st_hg, s_hg, HEAD_DIM, hg_nc, l, False)
        o_gdn, s_gdn = _gdn_decode(dq, dk_, dv, dgt, dbg, st_gdn, s_gdn, gdn_nc, l)
        xs = _mix_ffn(gs, xs, mod_s, (o_ret, o_gla, o_hg, o_gdn), w_out_b, wi2, wo2, ln_g, ln_b, l, alpha)
        s_convs.append(conv_new)

    y_prompt = xp.reshape(bp, tp, d)
    y_sample = jnp.transpose(xs.reshape(ts, bs, d), (1, 0, 2))
    p_out = tuple(jnp.stack([st[i] for st in p_states]) for i in range(5))
    s_out = tuple(jnp.transpose(s, (0, 4, 1, 2, 3)) for s in (s_ret, s_gla, s_hg, s_gdn))
    return (y_prompt, y_sample) + p_out + s_out + (jnp.transpose(jnp.stack(s_convs), (0, 2, 1, 3)),)
```

```python
import functools

import numpy as np
import jax
import jax.numpy as jnp
from jax import lax
from jax.experimental import pallas as pl
from jax.experimental.pallas import tpu as pltpu

F32 = jnp.float32
BF16 = jnp.bfloat16

HEAD_DIM = 64
HEADS = 4
GLA_DK = 32
GLA_GATE_NORM = 16.0
CONV_W = 4
CHUNK = 64
ROPE_BASE = 10000.0
LN_EPS = 1e-5
RMS_EPS = 1e-6
PAST_LEN = 16384
N_MOD = 9
HW = HEADS * HEAD_DIM

ROW_TILE = 512
PREP_TILE = 256
MIX_TILE = 1024
DEC_DCHUNK = 8
FF_CHUNK = 512
V7X_VMEM_BYTES = 64 * 1024 * 1024
VMEM_LIMIT = V7X_VMEM_BYTES - 12 * 1024 * 1024

_SLABS = (("rq", 256), ("rk", 256), ("rv", 256), ("rg", 256),
          ("aq", 128), ("ak", 128), ("av", 256), ("ag", 256),
          ("hq", 256), ("hf", 256), ("hi", 256), ("hg", 256),
          ("dqkv", 768), ("dg", 256), ("small", 128))
_OFF = {}
_o = 0
for _n, _w in _SLABS:
    _OFF[_n] = (_o, _w)
    _o += _w
PROJ_W = _o
_SRC = {"rq": 0, "rk": 256, "rv": 512, "rg": 768, "aq": 1024, "ak": 1152, "av": 1280, "alr": 1536,
        "ag": 1552, "hq": 1808, "hf": 2064, "hi": 2320, "hg": 2576, "dqkv": 2832, "db": 3600,
        "da": 3604, "dg": 3608}
SMALL_ALR, SMALL_DB, SMALL_DA = 0, 16, 20


def _bf(x):
    return x.astype(BF16)


def _dot(a, b):
    return jnp.dot(a, b, preferred_element_type=F32)


def _dot_nt(a, b):
    return lax.dot_general(a, b, (((1,), (1,)), ((), ())), preferred_element_type=F32)


def _dot_tn(a, b):
    return lax.dot_general(a, b, (((0,), (0,)), ((), ())), preferred_element_type=F32)


def _split2(x):
    hi = x.astype(BF16)
    lo = (x - hi.astype(F32)).astype(BF16)
    return hi, lo


def _split3(x):
    hi = x.astype(BF16)
    r = x - hi.astype(F32)
    mid = r.astype(BF16)
    lo = (r - mid.astype(F32)).astype(BF16)
    return hi, mid, lo


def _dot_x01(x, m01):
    hi, lo = _split2(x)
    return _dot(hi, m01) + _dot(lo, m01)


def _dot_exact01(m01, x):
    h1, h2, h3 = _split3(x)
    return _dot(m01, h1) + _dot(m01, h2) + _dot(m01, h3)


def _silu(x):
    return x * jax.nn.sigmoid(x)


def _softplus(x):
    return jnp.maximum(x, 0.0) + jnp.log(1.0 + jnp.exp(-jnp.abs(x)))


def _ln(y, g, b):
    mu = jnp.mean(y, axis=-1, keepdims=True)
    yc = y - mu
    var = jnp.mean(yc * yc, axis=-1, keepdims=True)
    return yc * lax.rsqrt(var + LN_EPS) * g + b


def _mod_rows(ref, tm):
    v = ref[...]
    v = v.reshape(v.shape[-2], v.shape[-1])
    nb = v.shape[0]
    if nb == 1:
        return v
    return jnp.concatenate([v] * (tm // nb), axis=0)


def _tile4(x):
    return jnp.concatenate([x, x, x, x], axis=0)


def _per_block(x, const, period):
    n = x.shape[0] // period
    return (x.reshape(n, period, x.shape[1]) * const[None]).reshape(x.shape)


def _cumsum_rows(g, period):
    rows, w = g.shape
    tile = min(period, 8)
    row = lax.broadcasted_iota(jnp.int32, g.shape, 0) % tile
    out = g
    sh = 1
    while sh < tile:
        out = out + jnp.where(row >= sh, pltpu.roll(out, sh, 0), 0.0)
        sh *= 2
    if period == tile:
        return out
    nper = period // tile
    o4 = out.reshape(rows // period, nper, tile, w)
    tot = o4[:, :, tile - 1:tile, :]
    offs = [jnp.zeros_like(tot[:, 0:1])]
    for i in range(1, nper):
        offs.append(offs[-1] + tot[:, i - 1:i])
    return (o4 + jnp.concatenate(offs, axis=1)).reshape(rows, w)


def _block_row(G, period, r):
    c, w = G.shape
    g3 = G.reshape(c // period, period, w)
    return jnp.broadcast_to(g3[:, r:r + 1, :], g3.shape).reshape(c, w)


def _gref(G, m):
    c = G.shape[0]
    if 2 * m >= 8:
        return _block_row(G, 2 * m, m - 1)
    p = lax.broadcasted_iota(jnp.int32, G.shape, 0) % (2 * m)
    if m == 2:
        return jnp.where(p == 0, pltpu.roll(G, c - 1, 0),
                         jnp.where(p == 1, G, jnp.where(p == 2, pltpu.roll(G, 1, 0), pltpu.roll(G, 2, 0))))
    return jnp.where(p == 0, G, pltpu.roll(G, 1, 0))


def _decay_att(q, k, G, mm, diag, lvl_ref, levels, period):
    att = _per_block(diag(q, k), lvl_ref[len(levels)], period)
    for i, m in enumerate(levels):
        if m == 1:
            ql = q * jnp.exp(jnp.minimum(G - pltpu.roll(G, 1, 0), 0.0))
            att = att + _per_block(diag(ql, pltpu.roll(k, 1, 0)), lvl_ref[i], period)
            continue
        gr = _gref(G, m)
        ql = q * jnp.exp(jnp.minimum(G - gr, 0.0))
        kl = k * jnp.exp(jnp.minimum(gr - G, 0.0))
        att = att + _per_block(mm(ql, kl), lvl_ref[i], period)
    return att


def _ret_gamma():
    return np.log(1.0 - 2.0 ** (-5.0 - np.arange(HEADS, dtype=np.float64)))


@functools.lru_cache(maxsize=None)
def _chunk_consts(dk):
    c = CHUNK
    h_att = np.arange(4 * c) // c
    s_att = np.arange(4 * c) % c
    t = np.arange(c)
    kmask = (h_att[:, None] == (np.arange(4 * dk) // dk)[None, :]).astype(np.float32)
    vmask = (h_att[:, None] == (np.arange(HW) // HEAD_DIM)[None, :]).astype(np.float32)
    smask = ((np.arange(HW) // HEAD_DIM)[:, None] == (np.arange(4 * dk) // dk)[None, :]).astype(np.float32)
    levels = (32, 16, 8, 4, 2, 1)
    lv = []
    for m in levels:
        same = (t[:, None] // (2 * m)) == (s_att[None, :] // (2 * m))
        lv.append(same & ((t[:, None] % (2 * m)) >= m) & ((s_att[None, :] % (2 * m)) < m))
    lv.append(t[:, None] == s_att[None, :])
    lvl = np.stack(lv).astype(np.float32)
    incl = (s_att[None, :] <= t[:, None]).astype(np.float32)
    strict = (s_att[None, :] < t[:, None]).astype(np.float32)
    eye = (s_att[None, :] == t[:, None]).astype(np.float32)
    fold = (np.arange(HW)[:, None] % HEAD_DIM == np.arange(HEAD_DIM)[None, :]).astype(np.float32)
    gam = _ret_gamma()
    gh = gam[h_att]
    ret_att = np.where(s_att[None, :] <= t[:, None], np.exp((t[:, None] - s_att[None, :]) * gh[None, :]), 0.0)
    gl = gam[np.arange(HW) // HEAD_DIM]
    ret_q = np.exp((t[:, None] + 1) * gl[None, :])
    ret_k = np.exp((c - 1 - t[:, None]) * gl[None, :])
    ret_s = np.exp(c * gl)[None, :]
    return dict(kmask=kmask, vmask=vmask, smask=smask, lvl=lvl, levels=levels, incl=incl, strict=strict,
                eye=eye, fold=fold, ret_att=ret_att.astype(np.float32), ret_q=ret_q.astype(np.float32),
                ret_k=ret_k.astype(np.float32), ret_s=ret_s.astype(np.float32))


def _bd_ones():
    h = np.arange(HW) // HEAD_DIM
    return (h[:, None] == h[None, :]).astype(np.float32)


def _expand_mat():
    e = np.zeros((128, 2 * HW), np.float32)
    for h in range(HEADS):
        e[SMALL_DB + h, h * HEAD_DIM:(h + 1) * HEAD_DIM] = 1.0
        e[SMALL_DA + h, HW + h * HEAD_DIM:HW + (h + 1) * HEAD_DIM] = 1.0
    return e


def _const_spec(a):
    nd = a.ndim
    return pl.BlockSpec(a.shape, lambda *_: (0,) * nd)


def _params(sem):
    return pltpu.CompilerParams(dimension_semantics=sem, vmem_limit_bytes=VMEM_LIMIT)


def _ada_kernel(c_ref, w_ref, b_ref, o_ref):
    c = c_ref[...]
    o_ref[0] = _dot(_bf(_silu(c)), _bf(w_ref[0])) + b_ref[0]


def _ada(c_all, ada_w, ada_b):
    nl, d, nd = ada_w.shape
    nb = c_all.shape[0]
    tn = d
    return pl.pallas_call(
        _ada_kernel,
        grid=(nl, nd // tn),
        in_specs=[pl.BlockSpec((nb, d), lambda l, j: (0, 0)),
                  pl.BlockSpec((1, d, tn), lambda l, j: (l, 0, j)),
                  pl.BlockSpec((1, 1, tn), lambda l, j: (l, 0, j))],
        out_specs=pl.BlockSpec((1, nb, tn), lambda l, j: (l, 0, j)),
        out_shape=jax.ShapeDtypeStruct((nl, nb, nd), F32),
        compiler_params=_params(("arbitrary", "arbitrary")),
        name="ada_mod",
    )(c_all, ada_w, ada_b.reshape(nl, 1, nd))


class _Rows:
    def __init__(self, batch, t, decode, tile=ROW_TILE):
        self.batch, self.t, self.decode = batch, t, decode
        self.rows = batch * t
        self.tm = min(tile, self.rows)
        assert self.rows % self.tm == 0
        if decode:
            assert self.tm % batch == 0
        else:
            assert t % self.tm == 0
        self.steps = self.rows // self.tm


def _mod_spec(g, layer, k, d):
    if g.decode:
        return pl.BlockSpec((1, 1, g.batch, d), lambda i: (layer, k, 0, 0))
    per = g.t // g.tm
    return pl.BlockSpec((1, 1, 1, 1, d), lambda i: (layer, k, i // per, 0, 0))


def _ffn_block(x, sh_ref, sc_ref, gt_ref, wi_ref, wo_ref, lng_ref, lnb_ref, *, alpha, ln_idx, layer):
    tm, d = x.shape
    dff = wo_ref.shape[1]
    sh, sc, gt = _mod_rows(sh_ref, tm), _mod_rows(sc_ref, tm), _mod_rows(gt_ref, tm)
    h = _bf(x * (1.0 + sc) + sh)
    acc = jnp.zeros((tm, d), F32)
    off = 0
    while off < dff:
        fc = min(FF_CHUNK, dff - off)
        a = _dot(h, wi_ref[0, :, off:off + fc])
        b = _dot(h, wi_ref[0, :, dff + off:dff + off + fc])
        acc = acc + _dot(_bf(_silu(a) * b), wo_ref[0, off:off + fc, :])
        off += fc
    y = alpha * x + 0.5 * (1.0 + gt) * acc
    return _ln(y, lng_ref[layer, ln_idx:ln_idx + 1, :], lnb_ref[layer, ln_idx:ln_idx + 1, :])


def _mix_block(x, gt_ref, o_refs, w_ref, lng_ref, lnb_ref, *, alpha, layer, decode, batch):
    tm, d = x.shape
    gt = _mod_rows(gt_ref, tm)
    if decode:
        accs = [jnp.zeros((batch, d), F32) for _ in range(tm // batch)]
        for mi, o_ref in enumerate(o_refs):
            for h in range(HEADS):
                r0 = mi * HW + h * HEAD_DIM
                wh = w_ref[0, r0:r0 + HEAD_DIM, :]
                for t in range(tm // batch):
                    accs[t] = accs[t] + _dot_tn(_bf(o_ref[h, t]), wh)
        acc = jnp.concatenate(accs, axis=0)
    else:
        acc = jnp.zeros((tm, d), F32)
        for mi, o_ref in enumerate(o_refs):
            acc = acc + _dot(o_ref[...], w_ref[0, mi * HW:(mi + 1) * HW, :])
    y = alpha * x + (1.0 + gt) * acc
    return _ln(y, lng_ref[layer, 1:2, :], lnb_ref[layer, 1:2, :])


def _ffn_kernel(x_ref, sh_ref, sc_ref, gt_ref, wi_ref, wo_ref, lng_ref, lnb_ref, o_ref, **kw):
    o_ref[...] = _ffn_block(x_ref[...], sh_ref, sc_ref, gt_ref, wi_ref, wo_ref, lng_ref, lnb_ref, **kw)


def _mix_ffn_kernel(x_ref, gtm_ref, o0_ref, o1_ref, o2_ref, o3_ref, w_ref, sh_ref, sc_ref, gt_ref, wi_ref, wo_ref,
                    lng_ref, lnb_ref, y_ref, *, alpha, layer, decode, batch):
    x1 = _mix_block(x_ref[...], gtm_ref, (o0_ref, o1_ref, o2_ref, o3_ref), w_ref, lng_ref, lnb_ref, alpha=alpha,
                    layer=layer, decode=decode, batch=batch)
    y_ref[...] = _ffn_block(x1, sh_ref, sc_ref, gt_ref, wi_ref, wo_ref, lng_ref, lnb_ref, alpha=alpha, ln_idx=2,
                            layer=layer)


def _ffn_specs(g, layer, mods, d, dff, ln_g, ln_b):
    return [_mod_spec(g, layer, k, d) for k in mods] + [
        pl.BlockSpec((1, d, 2 * dff), lambda i: (layer, 0, 0), pipeline_mode=pl.Buffered(1)),
        pl.BlockSpec((1, dff, d), lambda i: (layer, 0, 0), pipeline_mode=pl.Buffered(1)),
        _const_spec(ln_g), _const_spec(ln_b)]


def _ffn(g, x, mod, wi, wo, ln_g, ln_b, layer, alpha):
    d = x.shape[1]
    row = pl.BlockSpec((g.tm, d), lambda i: (i, 0))
    return pl.pallas_call(
        functools.partial(_ffn_kernel, alpha=alpha, ln_idx=0, layer=layer),
        grid=(g.steps,),
        in_specs=[row] + _ffn_specs(g, layer, (0, 1, 2), d, wo.shape[1], ln_g, ln_b),
        out_specs=row,
        out_shape=jax.ShapeDtypeStruct(x.shape, F32),
        compiler_params=_params(("arbitrary",)),
        name="ffn0",
    )(x, mod, mod, mod, wi, wo, ln_g, ln_b)


def _mix_ffn(g, x, mod, outs, w_out, wi, wo, ln_g, ln_b, layer, alpha):
    d = x.shape[1]
    row = pl.BlockSpec((g.tm, d), lambda i: (i, 0))
    if g.decode:
        assert g.steps == 1
        ospec = _const_spec(outs[0])
    else:
        ospec = pl.BlockSpec((g.tm, HW), lambda i: (i, 0))
    return pl.pallas_call(
        functools.partial(_mix_ffn_kernel, alpha=alpha, layer=layer, decode=g.decode, batch=g.batch),
        grid=(g.steps,),
        in_specs=[row, _mod_spec(g, layer, 5, d), ospec, ospec, ospec, ospec,
                  pl.BlockSpec((1, w_out.shape[1], d), lambda i: (layer, 0, 0), pipeline_mode=pl.Buffered(1))]
        + _ffn_specs(g, layer, (6, 7, 8), d, wo.shape[1], ln_g, ln_b),
        out_specs=row,
        out_shape=jax.ShapeDtypeStruct(x.shape, F32),
        compiler_params=_params(("arbitrary",)),
        name="mix_ffn2",
    )(x, mod, *outs, w_out, mod, mod, mod, wi, wo, ln_g, ln_b)


def _prep_kernel(*refs, layer, nlayers, g_t, tm, decode, batch):
    (x_ref, sh_ref, sc_ref, w_ref, cos_ref, sin_ref, wg_ref, bg_ref, lb_ref, cw_ref, alog_ref, dtb_ref,
     bd_ref, ex_ref) = refs[:14]
    rest = refs[14:]
    if decode:
        cs_ref, alogc_ref, dtbc_ref = rest[:3]
        outs = rest[3:]
        cbuf = None
    else:
        outs = rest[:-1]
        cbuf = rest[-1]

    x = x_ref[...]
    hb = _bf(x * (1.0 + _mod_rows(sc_ref, tm)) + _mod_rows(sh_ref, tm))

    def proj(name):
        o, w = _OFF[name]
        return _dot(hb, w_ref[0, :, o:o + w])

    bd = bd_ref[...]

    dest = {}
    if decode:
        it = iter(outs)
        for mix, keys in (("ret", ("q", "k", "v", "gate")), ("gla", ("q", "k", "g", "v", "gate")),
                          ("hg", ("q", "k", "g", "v", "gate")), ("gdn", ("q", "k", "v", "gate"))):
            for kk in keys:
                dest[f"{mix}_{kk}"] = ("heads", next(it), 0)
        bgo_ref, convo_ref = next(it), next(it)
    else:
        ret_ref, gla_ref, hg_ref, gdn_ref, conv_ref = outs
        for dst, keys, widths in ((ret_ref, ("ret_q", "ret_k", "ret_v", "ret_gate"), (HW,) * 4),
                                  (gla_ref, ("gla_q", "gla_k", "gla_g", "gla_v", "gla_gate"),
                                   (HEADS * GLA_DK,) * 3 + (HW, HW)),
                                  (hg_ref, ("hg_q", "hg_k", "hg_g", "hg_v", "hg_gate"), (HW,) * 5),
                                  (gdn_ref, ("gdn_q", "gdn_k", "gdn_b", "gdn_g", "gdn_v", "gdn_gate"), (HW,) * 6)):
            off = 0
            for kk, w in zip(keys, widths):
                dest[kk] = ("rows", dst, off)
                off += w

    def put(name, val):
        kind, ref, off = dest[name]
        if kind == "rows":
            ref[:, off:off + val.shape[1]] = val
        else:
            dh = val.shape[1] // HEADS
            for t in range(tm // batch):
                vt = val[t * batch:(t + 1) * batch].T
                for h in range(HEADS):
                    ref[h, t] = vt[h * dh:(h + 1) * dh]

    small = proj("small")

    dq = proj("dqkv")
    cw = cw_ref[0]
    if decode:
        cs = cs_ref[0]
        nt = tm // batch
        blocks = [dq[t * batch:(t + 1) * batch] for t in range(nt)]
        convs = []
        for t in range(nt):
            acc = blocks[t] * cw[CONV_W - 1:CONV_W, :]
            for s in range(1, CONV_W):
                prev = blocks[t - s] if t >= s else cs[CONV_W - 1 - s + t]
                acc = acc + prev * cw[CONV_W - 1 - s:CONV_W - s, :]
            convs.append(acc)
        conv = jnp.concatenate(convs, axis=0)
        convo_ref[...] = dq[(nt - (CONV_W - 1)) * batch:].reshape(CONV_W - 1, batch, dq.shape[1])
    else:
        i = pl.program_id(0)
        per = g_t // tm

        @pl.when(i % per == 0)
        def _():
            cbuf[0:8, :] = jnp.zeros((8, dq.shape[1]), F32)

        cbuf[8:8 + tm, :] = dq
        conv = cbuf[pl.ds(8, tm), :] * cw[CONV_W - 1:CONV_W, :]
        for s in range(1, CONV_W):
            conv = conv + cbuf[pl.ds(8 - s, tm), :] * cw[CONV_W - 1 - s:CONV_W - s, :]
        tail = cbuf[tm:tm + 8, :]
        cbuf[0:8, :] = tail

        @pl.when(i % per == per - 1)
        def _():
            conv_ref[0] = tail[8 - (CONV_W - 1):8, :]

    u = _silu(conv)
    uq, uk = u[:, 0:HW], u[:, HW:2 * HW]
    put("gdn_q", uq * lax.rsqrt(_dot(_bf(uq * uq), bd) + RMS_EPS) * (HEAD_DIM ** -0.5))
    put("gdn_k", uk * lax.rsqrt(_dot(_bf(uk * uk), bd) + RMS_EPS))
    put("gdn_v", u[:, 2 * HW:3 * HW])
    put("gdn_gate", _silu(proj("dg")))
    if decode:
        for t in range(tm // batch):
            smt = small[t * batch:(t + 1) * batch].T
            beta = jax.nn.sigmoid(smt[SMALL_DB:SMALL_DB + HEADS])
            gd = -jnp.exp(alogc_ref[layer]) * _softplus(smt[SMALL_DA:SMALL_DA + HEADS] + dtbc_ref[layer])
            for h in range(HEADS):
                bgo_ref[h, t, 0:1, :] = beta[h:h + 1]
                bgo_ref[h, t, 1:2, :] = gd[h:h + 1]
    else:
        dbda = _dot_x01(small, ex_ref[...])
        put("gdn_b", jax.nn.sigmoid(dbda[:, 0:HW]))
        put("gdn_g", -jnp.exp(alog_ref[layer:layer + 1, :]) * _softplus(dbda[:, HW:2 * HW] + dtb_ref[layer:layer + 1, :]))

    rows = [lb_ref[i:i + 1, :] for i in range(nlayers)]
    mx = functools.reduce(jnp.maximum, rows)
    ex = [jnp.exp(r - mx) for r in rows]
    tot = functools.reduce(lambda a, b: a + b, ex)
    plb = [e / tot for e in ex]
    lb = functools.reduce(lambda a, b: a + b, plb[:layer + 1]) - plb[0]
    zf = proj("hf")
    put("hg_q", _silu(proj("hq")) * (HEAD_DIM ** -0.5))
    put("hg_k", (1.0 - lb) * jax.nn.sigmoid(-zf))
    put("hg_g", jnp.log(lb + (1.0 - lb) * jax.nn.sigmoid(zf)))
    put("hg_v", proj("hi"))
    put("hg_gate", _silu(proj("hg")))

    gpre = _dot(_bf(small), wg_ref[0]) + bg_ref[layer:layer + 1, :]
    put("gla_q", proj("aq") * (GLA_DK ** -0.5))
    put("gla_k", proj("ak"))
    put("gla_g", -_softplus(-gpre) * (1.0 / GLA_GATE_NORM))
    put("gla_v", proj("av"))
    put("gla_gate", _silu(proj("ag")))

    cos, sin = cos_ref[...], sin_ref[...]
    lane = lax.broadcasted_iota(jnp.int32, (tm, HW), 1)
    first_half = (lane % HEAD_DIM) < (HEAD_DIM // 2)

    def rope(v):
        sw = jnp.where(first_half, pltpu.roll(v, HW - HEAD_DIM // 2, 1), pltpu.roll(v, HEAD_DIM // 2, 1))
        return v * cos + sw * sin

    put("ret_q", rope(proj("rq")))
    put("ret_k", rope(proj("rk")) * (HEAD_DIM ** -0.5))
    put("ret_v", proj("rv"))
    put("ret_gate", _silu(proj("rg")))


def _prep(g, x, mod, w_in_p, cos_t, sin_t, wg_p, gla_bg, hg_lb, gdn_conv, alog_e, dtb_e, dec, layer, nlayers):
    d = x.shape[1]
    bd = jnp.asarray(_bd_ones(), BF16)
    exm = jnp.asarray(_expand_mat(), BF16)
    row = pl.BlockSpec((g.tm, d), lambda i: (i, 0))
    if g.decode:
        tab = pl.BlockSpec((g.tm, HW), lambda i: (i, 0))
    else:
        per = g.t // g.tm
        tab = pl.BlockSpec((g.tm, HW), lambda i: (i % per, 0))
    in_specs = [row, _mod_spec(g, layer, 3, d), _mod_spec(g, layer, 4, d),
                pl.BlockSpec((1, d, PROJ_W), lambda i: (layer, 0, 0), pipeline_mode=pl.Buffered(1)),
                tab, tab,
                pl.BlockSpec((1,) + wg_p.shape[1:], lambda i: (layer, 0, 0)),
                _const_spec(gla_bg), _const_spec(hg_lb),
                pl.BlockSpec((1,) + gdn_conv.shape[1:], lambda i: (layer, 0, 0)),
                _const_spec(alog_e), _const_spec(dtb_e), _const_spec(bd), _const_spec(exm)]
    args = [x, mod, mod, w_in_p, cos_t, sin_t, wg_p, gla_bg, hg_lb, gdn_conv, alog_e, dtb_e, bd, exm]
    cch = gdn_conv.shape[2]

    def rows_out(w):
        return jax.ShapeDtypeStruct((g.rows, w), F32), pl.BlockSpec((g.tm, w), lambda i: (i, 0))

    def whole(shape):
        return jax.ShapeDtypeStruct(shape, F32), pl.BlockSpec(shape, lambda i: (0,) * len(shape))

    if g.decode:
        assert g.steps == 1 and g.t >= CONV_W - 1
        cs, alog_c, dtb_c = dec
        in_specs += [pl.BlockSpec((1,) + cs.shape[1:], lambda i: (layer, 0, 0, 0)), _const_spec(alog_c),
                     _const_spec(dtb_c)]
        args += [cs, alog_c, dtb_c]
        outs = []
        for dhs in ((HEAD_DIM,) * 4, (GLA_DK,) * 3 + (HEAD_DIM,) * 2, (HEAD_DIM,) * 5, (HEAD_DIM,) * 4):
            outs += [whole((HEADS, g.t, dh, g.batch)) for dh in dhs]
        outs += [whole((HEADS, g.t, 2, g.batch)), whole((CONV_W - 1, g.batch, cch))]
        scratch = []
    else:
        outs = [rows_out(4 * HW), rows_out(3 * HEADS * GLA_DK + 2 * HW), rows_out(5 * HW), rows_out(6 * HW),
                (jax.ShapeDtypeStruct((g.batch, CONV_W - 1, cch), F32),
                 pl.BlockSpec((1, CONV_W - 1, cch), lambda i: (i // (g.t // g.tm), 0, 0)))]
        scratch = [pltpu.VMEM((g.tm + 8, cch), F32)]
    return pl.pallas_call(
        functools.partial(_prep_kernel, layer=layer, nlayers=nlayers, g_t=g.t, tm=g.tm, decode=g.decode,
                          batch=g.batch),
        grid=(g.steps,),
        in_specs=in_specs,
        out_specs=[o[1] for o in outs],
        out_shape=[o[0] for o in outs],
        scratch_shapes=scratch,
        compiler_params=_params(("arbitrary",)),
        name="prep",
    )(*args)


def _chunk(x, c):
    return x[c * CHUNK:(c + 1) * CHUNK]


def _per_chunk(fn, nch):
    return jnp.concatenate([fn(c) for c in range(nch)], axis=0)


def _finish_prompt(o, gate, normw, bd, o_ref):
    ms = _dot(_bf(o * o), bd) * (1.0 / HEAD_DIM)
    on = o * lax.rsqrt(ms + RMS_EPS)
    if normw is not None:
        on = on * normw
    o_ref[...] = _bf(on * gate)


def _state_out(s_t, fold_ref, st_ref):
    fold = fold_ref[...]
    h1, h2, h3 = _split3(s_t)
    st_ref[0] = _dot_tn(h1, fold) + _dot_tn(h2, fold) + _dot_tn(h3, fold)


def _gla_prompt_kernel(f_ref, kmask_ref, vmask_ref, smask_ref, lvl_ref, fold_ref, bd_ref, nw_ref, ratt_ref, rq_ref,
                       rk_ref, rs_ref, o_ref, st_ref, s_scr, *, dk, nch, ret, levels, has_norm, layer):
    j = pl.program_id(1)
    dkk = HEADS * dk

    @pl.when(j == 0)
    def _():
        s_scr[...] = jnp.zeros(s_scr.shape, F32)

    kmask, vmask, smask, bd = kmask_ref[...], vmask_ref[...], smask_ref[...], bd_ref[...]
    normw = nw_ref[layer:layer + 1, :] if has_norm else None

    def mm(x, y):
        xb, yb = _bf(x), _bf(y)
        return _per_chunk(lambda c: _dot_nt(_chunk(xb, c), _tile4(_chunk(yb, c)) * kmask), nch)

    q = f_ref[:, 0:dkk]
    k = f_ref[:, dkk:2 * dkk]
    if ret:
        v = f_ref[:, 2 * dkk:2 * dkk + HW]
        gate = f_ref[:, 2 * dkk + HW:2 * dkk + 2 * HW]
        att = _per_block(mm(q, k), ratt_ref[...], CHUNK)
        qg = _per_block(q, rq_ref[...], CHUNK)
        kd = _per_block(k, rk_ref[...], CHUNK)
        sdec = [rs_ref[...]] * nch
    else:
        g = f_ref[:, 2 * dkk:3 * dkk]
        v = f_ref[:, 3 * dkk:3 * dkk + HW]
        gate = f_ref[:, 3 * dkk + HW:3 * dkk + 2 * HW]
        G = _cumsum_rows(g, CHUNK)
        att = _decay_att(q, k, G, mm, lambda x, y: _dot_nt(_bf(x * y), kmask), lvl_ref, levels, CHUNK)
        qg = q * jnp.exp(G)
        glb = _block_row(G, CHUNK, CHUNK - 1)
        kd = k * jnp.exp(glb - G)
        sdec = [jnp.exp(G[(c + 1) * CHUNK - 1:(c + 1) * CHUNK, :]) for c in range(nch)]
    vb, qgb, kdb, attb = _bf(v), _bf(qg), _bf(kd), _bf(att)
    o_intra = _per_chunk(lambda c: _dot(_chunk(attb, c), _tile4(_chunk(vb, c)) * vmask), nch)
    upd = [_dot_tn(_chunk(vb, c), _chunk(kdb, c)) * smask for c in range(nch)]

    s = s_scr[...]
    o_inter = []
    for c in range(nch):
        o_inter.append(_dot_nt(_chunk(qgb, c), _bf(s)))
        s = s * sdec[c] + upd[c]
    s_scr[...] = s
    _finish_prompt(o_intra + jnp.concatenate(o_inter, axis=0), gate, normw, bd, o_ref)

    @pl.when(j == pl.num_programs(1) - 1)
    def _():
        _state_out(s, fold_ref, st_ref)


def _gla_prompt(g, feat, dk, normw, layer, ret):
    c = _chunk_consts(dk)
    dkk = HEADS * dk
    blk = min(MIX_TILE, g.t)
    nch = blk // CHUNK
    per = g.t // blk
    consts = [jnp.asarray(c["kmask"], BF16), jnp.asarray(c["vmask"], BF16), jnp.asarray(c["smask"]),
              jnp.asarray(c["lvl"]), jnp.asarray(c["fold"], BF16), jnp.asarray(_bd_ones(), BF16),
              normw if normw is not None else jnp.ones((1, HW), F32),
              jnp.asarray(c["ret_att"]), jnp.asarray(c["ret_q"]), jnp.asarray(c["ret_k"]), jnp.asarray(c["ret_s"])]
    w = feat.shape[1]
    o, st = pl.pallas_call(
        functools.partial(_gla_prompt_kernel, dk=dk, nch=nch, ret=ret, levels=c["levels"],
                          has_norm=normw is not None, layer=layer),
        grid=(g.batch, per),
        in_specs=[pl.BlockSpec((blk, w), lambda b, j: (b * per + j, 0))] + [_const_spec(a) for a in consts],
        out_specs=[pl.BlockSpec((blk, HW), lambda b, j: (b * per + j, 0)),
                   pl.BlockSpec((1, dkk, HEAD_DIM), lambda b, j: (b, 0, 0))],
        out_shape=[jax.ShapeDtypeStruct((g.rows, HW), BF16),
                   jax.ShapeDtypeStruct((g.batch, dkk, HEAD_DIM), F32)],
        scratch_shapes=[pltpu.VMEM((HW, dkk), F32)],
        compiler_params=_params(("arbitrary", "arbitrary")),
        name="ret_prompt" if ret else f"gla_prompt_dk{dk}",
    )(feat, *consts)
    return o, st.reshape(g.batch, HEADS, dk, HEAD_DIM)


def _gdn_prompt_kernel(f_ref, bdm_ref, smask_ref, incl_ref, strict_ref, eye_ref, fold_ref, bd_ref, nw_ref, o_ref,
                       st_ref, s_scr, *, nch, layer):
    j = pl.program_id(1)

    @pl.when(j == 0)
    def _():
        s_scr[...] = jnp.zeros(s_scr.shape, F32)

    bdm, smask, incl, strict, eye, bd = (bdm_ref[...], smask_ref[...], incl_ref[...], strict_ref[...], eye_ref[...],
                                         bd_ref[...])
    normw = nw_ref[layer:layer + 1, :]
    ones_c = jnp.ones((CHUNK, CHUNK), BF16)

    def bdw(yb, c):
        return _tile4(_chunk(yb, c)) * bdm

    def prod1(x, y):
        xb, yb = _bf(x), _bf(y)
        return _per_chunk(lambda c: _dot(_chunk(xb, c), bdw(yb, c)), nch)

    def prod1_pair(x1, x2, y):
        x1b, x2b, yb = _bf(x1), _bf(x2), _bf(y)
        both = [_dot(jnp.concatenate([_chunk(x1b, c), _chunk(x2b, c)], axis=0), bdw(yb, c)) for c in range(nch)]
        return (jnp.concatenate([m[:CHUNK] for m in both], axis=0),
                jnp.concatenate([m[CHUNK:] for m in both], axis=0))

    def prod2(x, y):
        xh, xl = _split2(x)
        yb = _bf(y)

        def one(c):
            top = _dot(jnp.concatenate([_chunk(xh, c), _chunk(xl, c)], axis=0), bdw(yb, c))
            return top[:CHUNK] + top[CHUNK:]

        return _per_chunk(one, nch)

    def prod3(x, y):
        xh, xl = _split2(x)
        yh, yl = _split2(y)

        def one(c):
            top = _dot(jnp.concatenate([_chunk(xh, c), _chunk(xl, c)], axis=0), bdw(yh, c))
            return top[:CHUNK] + top[CHUNK:] + _dot(_chunk(xh, c), bdw(yl, c))

        return _per_chunk(one, nch)

    q, k = f_ref[:, 0:HW], f_ref[:, HW:2 * HW]
    be, ge = f_ref[:, 2 * HW:3 * HW], f_ref[:, 3 * HW:4 * HW]
    v, gate = f_ref[:, 4 * HW:5 * HW], f_ref[:, 5 * HW:6 * HW]
    G = _cumsum_rows(ge, CHUNK)
    zd = _per_block(G, eye, CHUNK)
    grow = _per_chunk(lambda c: _dot_exact01(ones_c, _chunk(zd, c)), nch)
    L = _per_block(jnp.exp(jnp.minimum(G - grow, 0.0)), incl, CHUNK)
    qb, kb = _bf(q), _bf(k)

    def qkk(c):
        return _dot_nt(jnp.concatenate([_chunk(qb, c), _chunk(kb, c)], axis=0), bdw(kb, c))

    qkk_all = [qkk(c) for c in range(nch)]
    qk = jnp.concatenate([m[:CHUNK] for m in qkk_all], axis=0)
    kk = jnp.concatenate([m[CHUNK:] for m in qkk_all], axis=0)
    a = _per_block(be * kk * L, strict, CHUNK)
    eye_b = _per_block(jnp.ones_like(a), eye, CHUNK)
    p = -a
    tinv = eye_b + p
    for i in range(5):
        if i == 0:
            p = prod1(p, p)
        else:
            p, dt = prod1_pair(p, tinv, p)
            tinv = tinv + dt
    tinv = tinv + prod1(tinv, p)
    resid = eye_b - tinv - prod3(a, tinv)
    tinv = tinv + prod1(tinv, resid)
    eg = jnp.exp(G)
    u = prod2(tinv, be * v)
    w = prod2(tinv, be * eg * k)
    glb = _block_row(G, CHUNK, CHUNK - 1)
    wb, qeb, qklb = _bf(w), _bf(q * eg), _bf(qk * L)
    kdb = _bf(k * jnp.exp(glb - G))
    sdec = [jnp.exp(G[(c + 1) * CHUNK - 1:(c + 1) * CHUNK, :]) for c in range(nch)]

    ub = _bf(u)
    s_add = [_dot_tn(_chunk(ub, c), _chunk(kdb, c)) * smask for c in range(nch)]
    s_mul = [_bf(_dot_tn(_chunk(wb, c), _chunk(kdb, c)) * smask) for c in range(nch)]

    s = s_scr[...]
    snaps = []
    for c in range(nch):
        sb = _bf(s)
        snaps.append(sb)
        s = s * sdec[c] + s_add[c] - _dot(sb, s_mul[c])
    s_scr[...] = s

    outs = []
    for c in range(nch):
        ws = _dot_nt(jnp.concatenate([_chunk(wb, c), _chunk(qeb, c)], axis=0), snaps[c])
        db = _bf(_chunk(u, c) - ws[:CHUNK])
        outs.append(_dot(_chunk(qklb, c), _tile4(db) * bdm) + ws[CHUNK:])
    _finish_prompt(jnp.concatenate(outs, axis=0), gate, normw, bd, o_ref)

    @pl.when(j == pl.num_programs(1) - 1)
    def _():
        _state_out(s, fold_ref, st_ref)


def _gdn_prompt(g, feat, normw, layer):
    c = _chunk_consts(HEAD_DIM)
    blk = min(MIX_TILE, g.t)
    nch = blk // CHUNK
    per = g.t // blk
    consts = [jnp.asarray(c["vmask"], BF16), jnp.asarray(c["smask"]), jnp.asarray(c["incl"]), jnp.asarray(c["strict"]),
              jnp.asarray(c["eye"]), jnp.asarray(c["fold"], BF16), jnp.asarray(_bd_ones(), BF16), normw]
    w = feat.shape[1]
    o, st = pl.pallas_call(
        functools.partial(_gdn_prompt_kernel, nch=nch, layer=layer),
        grid=(g.batch, per),
        in_specs=[pl.BlockSpec((blk, w), lambda b, j: (b * per + j, 0))] + [_const_spec(a) for a in consts],
        out_specs=[pl.BlockSpec((blk, HW), lambda b, j: (b * per + j, 0)),
                   pl.BlockSpec((1, HW, HEAD_DIM), lambda b, j: (b, 0, 0))],
        out_shape=[jax.ShapeDtypeStruct((g.rows, HW), BF16),
                   jax.ShapeDtypeStruct((g.batch, HW, HEAD_DIM), F32)],
        scratch_shapes=[pltpu.VMEM((HW, HW), F32)],
        compiler_params=_params(("arbitrary", "arbitrary")),
        name="gdn_prompt",
    )(feat, *consts)
    return o, st.reshape(g.batch, HEADS, HEAD_DIM, HEAD_DIM)


def _state_io(refs, layer):
    if layer == 0:
        return refs
    prev_ref, rest = refs[0], refs[1:]
    rest[-1][0:layer] = prev_ref[...]
    return rest


def _dec_finish(o, gate, normw):
    ms = jnp.mean(o * o, axis=0, keepdims=True)
    on = o * lax.rsqrt(ms + RMS_EPS)
    if normw is not None:
        on = on * normw
    return on * gate


def _gla_decode_kernel(q_ref, k_ref, g_ref, v_ref, gate_ref, s_ref, *refs, nt, dk, ret, has_norm, layer):
    nw_ref, o_ref, so_ref = _state_io(refs, layer)
    normw = nw_ref[layer] if has_norm else None
    qs = [q_ref[0, t] for t in range(nt)]
    ks = [k_ref[0, t] for t in range(nt)]
    vs = [v_ref[0, t] for t in range(nt)]
    decay = [jnp.exp(g_ref[0, t]) for t in range(nt)]
    o_acc = [jnp.zeros(vs[0].shape, F32) for _ in range(nt)]
    for dc in range(dk // DEC_DCHUNK):
        r = slice(dc * DEC_DCHUNK, (dc + 1) * DEC_DCHUNK)
        sc = s_ref[0, 0, r]
        for t in range(nt):
            a = decay[t][None] if ret else decay[t][r][:, None, :]
            sc = sc * a + ks[t][r][:, None, :] * vs[t][None, :, :]
            o_acc[t] = o_acc[t] + jnp.sum(qs[t][r][:, None, :] * sc, axis=0)
        so_ref[layer, 0, r] = sc
    for t in range(nt):
        o_ref[0, t] = _dec_finish(o_acc[t], gate_ref[0, t], normw)


def _gdn_decode_kernel(q_ref, k_ref, v_ref, gate_ref, bg_ref, s_ref, *refs, nt, layer):
    nw_ref, o_ref, so_ref = _state_io(refs, layer)
    normw = nw_ref[layer]
    chunks = [slice(dc * DEC_DCHUNK, (dc + 1) * DEC_DCHUNK) for dc in range(HEAD_DIM // DEC_DCHUNK)]
    for t in range(nt):
        q, k, v = q_ref[0, t], k_ref[0, t], v_ref[0, t]
        beta = bg_ref[0, t, 0:1, :]
        ag = jnp.exp(bg_ref[0, t, 1:2, :])

        def state(r):
            return s_ref[0, 0, r] if t == 0 else so_ref[layer, 0, r]

        ks = jnp.zeros(v.shape, F32)
        for r in chunks:
            ks = ks + jnp.sum(k[r][:, None, :] * state(r), axis=0)
        delta = beta * (v - ag * ks)
        o = jnp.zeros(v.shape, F32)
        for r in chunks:
            sc = state(r) * ag[None] + k[r][:, None, :] * delta[None, :, :]
            so_ref[layer, 0, r] = sc
            o = o + jnp.sum(q[r][:, None, :] * sc, axis=0)
        o_ref[0, t] = _dec_finish(o, gate_ref[0, t], normw)


def _decode_call(body, name, feats, state_t, prev, extra, dk, layer):
    nt, batch = feats[0].shape[1], feats[0].shape[3]

    def head_spec(a):
        return pl.BlockSpec((1,) + a.shape[1:], lambda h: (h,) + (0,) * (a.ndim - 1))

    def st_spec(nl):
        return pl.BlockSpec((nl, 1, dk, HEAD_DIM, batch), lambda h: (0, h, 0, 0, 0))

    in_specs = [head_spec(a) for a in feats]
    in_specs.append(pl.BlockSpec((1, 1, dk, HEAD_DIM, batch), lambda h: (layer, h, 0, 0, 0)))
    args = list(feats) + [state_t]
    if layer > 0:
        in_specs.append(st_spec(layer))
        args.append(prev)
    o_shape = (HEADS, nt, HEAD_DIM, batch)
    return pl.pallas_call(
        body,
        grid=(HEADS,),
        in_specs=in_specs + [_const_spec(a) for a in extra],
        out_specs=[pl.BlockSpec((1,) + o_shape[1:], lambda h: (h, 0, 0, 0)), st_spec(layer + 1)],
        out_shape=[jax.ShapeDtypeStruct(o_shape, F32),
                   jax.ShapeDtypeStruct((layer + 1, HEADS, dk, HEAD_DIM, batch), F32)],
        compiler_params=_params(("arbitrary",)),
        name=name,
    )(*args, *extra)


def _gla_decode(q, k, g, v, gate, state_t, prev, dk, normw_col, layer, ret):
    nt = q.shape[1]
    body = functools.partial(_gla_decode_kernel, nt=nt, dk=dk, ret=ret, has_norm=normw_col is not None, layer=layer)
    nw = normw_col if normw_col is not None else jnp.ones((layer + 1, HEAD_DIM, 1), F32)
    return _decode_call(body, "ret_decode" if ret else f"gla_decode_dk{dk}", [q, k, g, v, gate], state_t, prev, [nw],
                        dk, layer)


def _gdn_decode(q, k, v, gate, bg, state_t, prev, normw_col, layer):
    body = functools.partial(_gdn_decode_kernel, nt=q.shape[1], layer=layer)
    return _decode_call(body, "gdn_decode", [q, k, v, gate, bg], state_t, prev, [normw_col], HEAD_DIM, layer)


def _rope_tables(pos):
    half = HEAD_DIM // 2
    inv = ROPE_BASE ** (-jnp.arange(half, dtype=F32) / half)
    ang = pos[:, None] * inv[None, :]
    cos, sin = jnp.cos(ang), jnp.sin(ang)
    cos_t = jnp.tile(jnp.concatenate([cos, cos], axis=1), (1, HEADS))
    sin_t = jnp.tile(jnp.concatenate([-sin, sin], axis=1), (1, HEADS))
    return cos_t, sin_t


def _relayout_w_in(w_in):
    parts = []
    for name, w in _SLABS:
        if name == "small":
            sm = jnp.concatenate([w_in[:, :, _SRC["alr"]:_SRC["alr"] + 16], w_in[:, :, _SRC["db"]:_SRC["db"] + 4],
                                  w_in[:, :, _SRC["da"]:_SRC["da"] + 4]], axis=2)
            parts.append(jnp.pad(sm, ((0, 0), (0, 0), (0, w - sm.shape[2]))))
        else:
            parts.append(w_in[:, :, _SRC[name]:_SRC[name] + w])
    return jnp.concatenate(parts, axis=2).astype(BF16)


def kernel(x_prompt, x_sample, state_ret, state_gla, state_hgrn, state_gdn, state_gdn_conv, c_prompt, c_sample, ada_w, ada_b, ln_g, ln_b, ffn1_wi, ffn1_wo, ffn2_wi, ffn2_wo, w_in, gla_wg, gla_bg, hg_lb, gdn_conv, gdn_a_log, gdn_dt_bias, gla_norm, hg_norm, gdn_norm, w_out):
    nl, d = ada_w.shape[0], ada_w.shape[1]
    bp, tp = x_prompt.shape[0], x_prompt.shape[1]
    bs, ts = x_sample.shape[0], x_sample.shape[1]
    alpha = (2.0 * nl) ** 0.25
    assert tp % CHUNK == 0 and ts >= CONV_W - 1

    wi1, wo1, wi2, wo2 = (w.astype(BF16) for w in (ffn1_wi, ffn1_wo, ffn2_wi, ffn2_wo))
    w_out_b = w_out.astype(BF16)
    w_in_p = _relayout_w_in(w_in)
    wg_p = jnp.pad(gla_wg, ((0, 0), (0, 128 - gla_wg.shape[1]), (0, 0))).astype(BF16)
    alog_e = jnp.repeat(gdn_a_log, HEAD_DIM, axis=1)
    dtb_e = jnp.repeat(gdn_dt_bias, HEAD_DIM, axis=1)
    gla_nw = jnp.tile(gla_norm, (1, HEADS))
    hg_nw = jnp.tile(hg_norm, (1, HEADS))
    gdn_nw = jnp.tile(gdn_norm, (1, HEADS))

    mod = _ada(jnp.concatenate([c_prompt, c_sample], axis=0), ada_w, ada_b)
    mod = mod.reshape(nl, bp + bs, N_MOD, d).transpose(0, 2, 1, 3)
    mod_p = mod[:, :, :bp].reshape(nl, N_MOD, bp, 1, d)
    mod_s = mod[:, :, bp:]

    gp = _Rows(bp, tp, decode=False)
    gp_prep = _Rows(bp, tp, decode=False, tile=PREP_TILE)
    gs = _Rows(bs, ts, decode=True)
    assert gs.steps == 1
    cos_p, sin_p = _rope_tables(jnp.arange(tp, dtype=F32))
    cos_s, sin_s = _rope_tables(PAST_LEN + jnp.arange(ts, dtype=F32))
    cos_s, sin_s = jnp.repeat(cos_s, bs, axis=0), jnp.repeat(sin_s, bs, axis=0)

    xp = x_prompt.reshape(bp * tp, d)
    xs = jnp.transpose(x_sample, (1, 0, 2)).reshape(ts * bs, d)

    st_ret, st_gla, st_hg, st_gdn = (jnp.transpose(s, (0, 2, 3, 4, 1))
                                     for s in (state_ret, state_gla, state_hgrn, state_gdn))
    conv_t = jnp.transpose(state_gdn_conv, (0, 2, 1, 3))
    dec_extra = (conv_t, gdn_a_log[:, :, None], gdn_dt_bias[:, :, None])
    ret_g = jnp.asarray(np.broadcast_to(_ret_gamma().astype(np.float32)[:, None, None, None], (HEADS, ts, 1, bs)))
    gla_nc, hg_nc, gdn_nc = gla_norm[:, :, None], hg_norm[:, :, None], gdn_norm[:, :, None]

    p_states, s_convs = [], []
    s_ret = s_gla = s_hg = s_gdn = None
    for l in range(nl):
        xp = _ffn(gp, xp, mod_p, wi1, wo1, ln_g, ln_b, l, alpha)
        ret_f, gla_f, hg_f, gdn_f, p_conv = _prep(gp_prep, xp, mod_p, w_in_p, cos_p, sin_p, wg_p, gla_bg, hg_lb,
                                                  gdn_conv, alog_e, dtb_e, None, l, nl)
        o_ret, p_ret = _gla_prompt(gp, ret_f, HEAD_DIM, None, l, ret=True)
        o_gla, p_gla = _gla_prompt(gp, gla_f, GLA_DK, gla_nw, l, ret=False)
        o_hg, p_hg = _gla_prompt(gp, hg_f, HEAD_DIM, hg_nw, l, ret=False)
        o_gdn, p_gdn = _gdn_prompt(gp, gdn_f, gdn_nw, l)
        xp = _mix_ffn(gp, xp, mod_p, (o_ret, o_gla, o_hg, o_gdn), w_out_b, wi2, wo2, ln_g, ln_b, l, alpha)
        p_states.append((p_ret, p_gla, p_hg, p_gdn, p_conv))

        xs = _ffn(gs, xs, mod_s, wi1, wo1, ln_g, ln_b, l, alpha)
        (rq, rk, rv, rgt, aq, ak, ag, av, agt, hq, hk, hgg, hv, hgt, dq, dk_, dv, dgt, dbg, conv_new) = _prep(
            gs, xs, mod_s, w_in_p, cos_s, sin_s, wg_p, gla_bg, hg_lb, gdn_conv, alog_e, dtb_e, dec_extra, l, nl)
        o_ret, s_ret = _gla_decode(rq, rk, ret_g, rv, rgt, st_ret, s_ret, HEAD_DIM, None, l, True)
        o_gla, s_gla = _gla_decode(aq, ak, ag, av, agt, st_gla, s_gla, GLA_DK, gla_nc, l, False)
        o_hg, s_hg = _gla_decode(hq, hk, hgg, hv, hgt, st_hg, s_hg, HEAD_DIM, hg_nc, l, False)
        o_gdn, s_gdn = _gdn_decode(dq, dk_, dv, dgt, dbg, st_gdn, s_gdn, gdn_nc, l)
        xs = _mix_ffn(gs, xs, mod_s, (o_ret, o_gla, o_hg, o_gdn), w_out_b, wi2, wo2, ln_g, ln_b, l, alpha)
        s_convs.append(conv_new)

    y_prompt = xp.reshape(bp, tp, d)
    y_sample = jnp.transpose(xs.reshape(ts, bs, d), (1, 0, 2))
    p_out = tuple(jnp.stack([st[i] for st in p_states]) for i in range(5))
    s_out = tuple(jnp.transpose(s, (0, 4, 1, 2, 3)) for s in (s_ret, s_gla, s_hg, s_gdn))
    return (y_prompt, y_sample) + p_out + s_out + (jnp.transpose(jnp.stack(s_convs), (0, 2, 1, 3)),)
```

```python
import functools

import numpy as np
import jax
import jax.numpy as jnp
from jax import lax
from jax.experimental import pallas as pl
from jax.experimental.pallas import tpu as pltpu

F32 = jnp.float32
BF16 = jnp.bfloat16

HEAD_DIM = 64
HEADS = 4
GLA_DK = 32
GLA_GATE_NORM = 16.0
CONV_W = 4
CHUNK = 64
ROPE_BASE = 10000.0
LN_EPS = 1e-5
RMS_EPS = 1e-6
PAST_LEN = 16384
N_MOD = 9
HW = HEADS * HEAD_DIM

ROW_TILE = 512
PREP_TILE = 256
MIX_TILE = 2048
DEC_DCHUNK = 8
FF_CHUNK = 256
V7X_VMEM_BYTES = 64 * 1024 * 1024
VMEM_LIMIT = V7X_VMEM_BYTES - 12 * 1024 * 1024

_SLABS = (("rq", 256), ("rk", 256), ("rv", 256), ("rg", 256),
          ("aq", 128), ("ak", 128), ("av", 256), ("ag", 256),
          ("hq", 256), ("hf", 256), ("hi", 256), ("hg", 256),
          ("dqkv", 768), ("dg", 256), ("small", 128))
_OFF = {}
_o = 0
for _n, _w in _SLABS:
    _OFF[_n] = (_o, _w)
    _o += _w
PROJ_W = _o
_SRC = {"rq": 0, "rk": 256, "rv": 512, "rg": 768, "aq": 1024, "ak": 1152, "av": 1280, "alr": 1536,
        "ag": 1552, "hq": 1808, "hf": 2064, "hi": 2320, "hg": 2576, "dqkv": 2832, "db": 3600,
        "da": 3604, "dg": 3608}
SMALL_ALR, SMALL_DB, SMALL_DA = 0, 16, 20


def _bf(x):
    return x.astype(BF16)


def _dot(a, b):
    return jnp.dot(a, b, preferred_element_type=F32)


def _dot_nt(a, b):
    return lax.dot_general(a, b, (((1,), (1,)), ((), ())), preferred_element_type=F32)


def _dot_tn(a, b):
    return lax.dot_general(a, b, (((0,), (0,)), ((), ())), preferred_element_type=F32)


def _split2(x):
    hi = x.astype(BF16)
    lo = (x - hi.astype(F32)).astype(BF16)
    return hi, lo


def _split3(x):
    hi = x.astype(BF16)
    r = x - hi.astype(F32)
    mid = r.astype(BF16)
    lo = (r - mid.astype(F32)).astype(BF16)
    return hi, mid, lo


def _dot_x01(x, m01):
    hi, lo = _split2(x)
    return _dot(hi, m01) + _dot(lo, m01)


def _dot_exact01(m01, x):
    h1, h2, h3 = _split3(x)
    return _dot(m01, h1) + _dot(m01, h2) + _dot(m01, h3)


def _silu(x):
    return x * jax.nn.sigmoid(x)


def _softplus(x):
    return jnp.maximum(x, 0.0) + jnp.log(1.0 + jnp.exp(-jnp.abs(x)))


def _ln(y, g, b):
    mu = jnp.mean(y, axis=-1, keepdims=True)
    yc = y - mu
    var = jnp.mean(yc * yc, axis=-1, keepdims=True)
    return yc * lax.rsqrt(var + LN_EPS) * g + b


def _mod_rows(ref, tm):
    v = ref[...]
    v = v.reshape(v.shape[-2], v.shape[-1])
    nb = v.shape[0]
    if nb == 1:
        return v
    return jnp.concatenate([v] * (tm // nb), axis=0)


def _tile4(x):
    return jnp.concatenate([x, x, x, x], axis=0)


def _per_block(x, const, period):
    n = x.shape[0] // period
    return (x.reshape(n, period, x.shape[1]) * const[None]).reshape(x.shape)


def _cumsum_rows(g, period):
    rows, w = g.shape
    tile = min(period, 8)
    row = lax.broadcasted_iota(jnp.int32, g.shape, 0) % tile
    out = g
    sh = 1
    while sh < tile:
        out = out + jnp.where(row >= sh, pltpu.roll(out, sh, 0), 0.0)
        sh *= 2
    if period == tile:
        return out
    nper = period // tile
    o4 = out.reshape(rows // period, nper, tile, w)
    tot = o4[:, :, tile - 1:tile, :]
    offs = [jnp.zeros_like(tot[:, 0:1])]
    for i in range(1, nper):
        offs.append(offs[-1] + tot[:, i - 1:i])
    return (o4 + jnp.concatenate(offs, axis=1)).reshape(rows, w)


def _block_row(G, period, r):
    c, w = G.shape
    g3 = G.reshape(c // period, period, w)
    return jnp.broadcast_to(g3[:, r:r + 1, :], g3.shape).reshape(c, w)


def _gref(G, m):
    c = G.shape[0]
    if 2 * m >= 8:
        return _block_row(G, 2 * m, m - 1)
    p = lax.broadcasted_iota(jnp.int32, G.shape, 0) % (2 * m)
    if m == 2:
        return jnp.where(p == 0, pltpu.roll(G, c - 1, 0),
                         jnp.where(p == 1, G, jnp.where(p == 2, pltpu.roll(G, 1, 0), pltpu.roll(G, 2, 0))))
    return jnp.where(p == 0, G, pltpu.roll(G, 1, 0))


def _decay_att(q, k, G, mm, diag, lvl_ref, levels, period):
    att = _per_block(diag(q, k), lvl_ref[len(levels)], period)
    for i, m in enumerate(levels):
        if m == 1:
            ql = q * jnp.exp(jnp.minimum(G - pltpu.roll(G, 1, 0), 0.0))
            att = att + _per_block(diag(ql, pltpu.roll(k, 1, 0)), lvl_ref[i], period)
            continue
        gr = _gref(G, m)
        ql = q * jnp.exp(jnp.minimum(G - gr, 0.0))
        kl = k * jnp.exp(jnp.minimum(gr - G, 0.0))
        att = att + _per_block(mm(ql, kl), lvl_ref[i], period)
    return att


def _ret_gamma():
    return np.log(1.0 - 2.0 ** (-5.0 - np.arange(HEADS, dtype=np.float64)))


@functools.lru_cache(maxsize=None)
def _chunk_consts(dk):
    c = CHUNK
    h_att = np.arange(4 * c) // c
    s_att = np.arange(4 * c) % c
    t = np.arange(c)
    kmask = (h_att[:, None] == (np.arange(4 * dk) // dk)[None, :]).astype(np.float32)
    vmask = (h_att[:, None] == (np.arange(HW) // HEAD_DIM)[None, :]).astype(np.float32)
    smask = ((np.arange(HW) // HEAD_DIM)[:, None] == (np.arange(4 * dk) // dk)[None, :]).astype(np.float32)
    levels = (32, 16, 8, 4, 2, 1)
    lv = []
    for m in levels:
        same = (t[:, None] // (2 * m)) == (s_att[None, :] // (2 * m))
        lv.append(same & ((t[:, None] % (2 * m)) >= m) & ((s_att[None, :] % (2 * m)) < m))
    lv.append(t[:, None] == s_att[None, :])
    lvl = np.stack(lv).astype(np.float32)
    incl = (s_att[None, :] <= t[:, None]).astype(np.float32)
    strict = (s_att[None, :] < t[:, None]).astype(np.float32)
    eye = (s_att[None, :] == t[:, None]).astype(np.float32)
    fold = (np.arange(HW)[:, None] % HEAD_DIM == np.arange(HEAD_DIM)[None, :]).astype(np.float32)
    gam = _ret_gamma()
    gh = gam[h_att]
    ret_att = np.where(s_att[None, :] <= t[:, None], np.exp((t[:, None] - s_att[None, :]) * gh[None, :]), 0.0)
    gl = gam[np.arange(HW) // HEAD_DIM]
    ret_q = np.exp((t[:, None] + 1) * gl[None, :])
    ret_k = np.exp((c - 1 - t[:, None]) * gl[None, :])
    ret_s = np.exp(c * gl)[None, :]
    return dict(kmask=kmask, vmask=vmask, smask=smask, lvl=lvl, levels=levels, incl=incl, strict=strict,
                eye=eye, fold=fold, ret_att=ret_att.astype(np.float32), ret_q=ret_q.astype(np.float32),
                ret_k=ret_k.astype(np.float32), ret_s=ret_s.astype(np.float32))


def _bd_ones():
    h = np.arange(HW) // HEAD_DIM
    return (h[:, None] == h[None, :]).astype(np.float32)


def _expand_mat():
    e = np.zeros((128, 2 * HW), np.float32)
    for h in range(HEADS):
        e[SMALL_DB + h, h * HEAD_DIM:(h + 1) * HEAD_DIM] = 1.0
        e[SMALL_DA + h, HW + h * HEAD_DIM:HW + (h + 1) * HEAD_DIM] = 1.0
    return e


def _const_spec(a):
    nd = a.ndim
    return pl.BlockSpec(a.shape, lambda *_: (0,) * nd)


def _params(sem):
    return pltpu.CompilerParams(dimension_semantics=sem, vmem_limit_bytes=VMEM_LIMIT)


def _ada_kernel(c_ref, w_ref, b_ref, o_ref):
    c = c_ref[...]
    o_ref[0] = _dot(_bf(_silu(c)), _bf(w_ref[0])) + b_ref[0]


def _ada(c_all, ada_w, ada_b):
    nl, d, nd = ada_w.shape
    nb = c_all.shape[0]
    tn = d
    return pl.pallas_call(
        _ada_kernel,
        grid=(nl, nd // tn),
        in_specs=[pl.BlockSpec((nb, d), lambda l, j: (0, 0)),
                  pl.BlockSpec((1, d, tn), lambda l, j: (l, 0, j)),
                  pl.BlockSpec((1, 1, tn), lambda l, j: (l, 0, j))],
        out_specs=pl.BlockSpec((1, nb, tn), lambda l, j: (l, 0, j)),
        out_shape=jax.ShapeDtypeStruct((nl, nb, nd), F32),
        compiler_params=_params(("arbitrary", "arbitrary")),
        name="ada_mod",
    )(c_all, ada_w, ada_b.reshape(nl, 1, nd))


class _Rows:
    def __init__(self, batch, t, decode, tile=ROW_TILE):
        self.batch, self.t, self.decode = batch, t, decode
        self.rows = batch * t
        self.tm = min(tile, self.rows)
        assert self.rows % self.tm == 0
        if decode:
            assert self.tm % batch == 0
        else:
            assert t % self.tm == 0
        self.steps = self.rows // self.tm


def _mod_spec(g, layer, k, d):
    if g.decode:
        return pl.BlockSpec((1, 1, g.batch, d), lambda i: (layer, k, 0, 0))
    per = g.t // g.tm
    return pl.BlockSpec((1, 1, 1, 1, d), lambda i: (layer, k, i // per, 0, 0))


def _ffn_block(x, sh_ref, sc_ref, gt_ref, wi_ref, wo_ref, lng_ref, lnb_ref, *, alpha, ln_idx, layer):
    tm, d = x.shape
    dff = wo_ref.shape[1]
    sh, sc, gt = _mod_rows(sh_ref, tm), _mod_rows(sc_ref, tm), _mod_rows(gt_ref, tm)
    h = _bf(x * (1.0 + sc) + sh)
    acc = jnp.zeros((tm, d), F32)
    off = 0
    while off < dff:
        fc = min(FF_CHUNK, dff - off)
        a = _dot(h, wi_ref[0, :, off:off + fc])
        b = _dot(h, wi_ref[0, :, dff + off:dff + off + fc])
        acc = acc + _dot(_bf(_silu(a) * b), wo_ref[0, off:off + fc, :])
        off += fc
    y = alpha * x + 0.5 * (1.0 + gt) * acc
    return _ln(y, lng_ref[layer, ln_idx:ln_idx + 1, :], lnb_ref[layer, ln_idx:ln_idx + 1, :])


def _mix_block(x, gt_ref, o_refs, w_ref, lng_ref, lnb_ref, *, alpha, layer, decode, batch):
    tm, d = x.shape
    gt = _mod_rows(gt_ref, tm)
    if decode:
        accs = [jnp.zeros((batch, d), F32) for _ in range(tm // batch)]
        for mi, o_ref in enumerate(o_refs):
            for h in range(HEADS):
                r0 = mi * HW + h * HEAD_DIM
                wh = w_ref[0, r0:r0 + HEAD_DIM, :]
                for t in range(tm // batch):
                    accs[t] = accs[t] + _dot_tn(_bf(o_ref[h, t]), wh)
        acc = jnp.concatenate(accs, axis=0)
    else:
        acc = jnp.zeros((tm, d), F32)
        for mi, o_ref in enumerate(o_refs):
            acc = acc + _dot(o_ref[...], w_ref[0, mi * HW:(mi + 1) * HW, :])
    y = alpha * x + (1.0 + gt) * acc
    return _ln(y, lng_ref[layer, 1:2, :], lnb_ref[layer, 1:2, :])


def _ffn_kernel(x_ref, sh_ref, sc_ref, gt_ref, wi_ref, wo_ref, lng_ref, lnb_ref, o_ref, **kw):
    o_ref[...] = _ffn_block(x_ref[...], sh_ref, sc_ref, gt_ref, wi_ref, wo_ref, lng_ref, lnb_ref, **kw)


def _mix_ffn_kernel(x_ref, gtm_ref, o0_ref, o1_ref, o2_ref, o3_ref, w_ref, sh_ref, sc_ref, gt_ref, wi_ref, wo_ref,
                    lng_ref, lnb_ref, y_ref, *, alpha, layer, decode, batch):
    x1 = _mix_block(x_ref[...], gtm_ref, (o0_ref, o1_ref, o2_ref, o3_ref), w_ref, lng_ref, lnb_ref, alpha=alpha,
                    layer=layer, decode=decode, batch=batch)
    y_ref[...] = _ffn_block(x1, sh_ref, sc_ref, gt_ref, wi_ref, wo_ref, lng_ref, lnb_ref, alpha=alpha, ln_idx=2,
                            layer=layer)


def _ffn_specs(g, layer, mods, d, dff, ln_g, ln_b):
    return [_mod_spec(g, layer, k, d) for k in mods] + [
        pl.BlockSpec((1, d, 2 * dff), lambda i: (layer, 0, 0), pipeline_mode=pl.Buffered(1)),
        pl.BlockSpec((1, dff, d), lambda i: (layer, 0, 0), pipeline_mode=pl.Buffered(1)),
        _const_spec(ln_g), _const_spec(ln_b)]


def _ffn(g, x, mod, wi, wo, ln_g, ln_b, layer, alpha):
    d = x.shape[1]
    row = pl.BlockSpec((g.tm, d), lambda i: (i, 0))
    return pl.pallas_call(
        functools.partial(_ffn_kernel, alpha=alpha, ln_idx=0, layer=layer),
        grid=(g.steps,),
        in_specs=[row] + _ffn_specs(g, layer, (0, 1, 2), d, wo.shape[1], ln_g, ln_b),
        out_specs=row,
        out_shape=jax.ShapeDtypeStruct(x.shape, F32),
        compiler_params=_params(("arbitrary",)),
        name="ffn0",
    )(x, mod, mod, mod, wi, wo, ln_g, ln_b)


def _mix_ffn(g, x, mod, outs, w_out, wi, wo, ln_g, ln_b, layer, alpha):
    d = x.shape[1]
    row = pl.BlockSpec((g.tm, d), lambda i: (i, 0))
    if g.decode:
        assert g.steps == 1
        ospec = _const_spec(outs[0])
    else:
        ospec = pl.BlockSpec((g.tm, HW), lambda i: (i, 0))
    return pl.pallas_call(
        functools.partial(_mix_ffn_kernel, alpha=alpha, layer=layer, decode=g.decode, batch=g.batch),
        grid=(g.steps,),
        in_specs=[row, _mod_spec(g, layer, 5, d), ospec, ospec, ospec, ospec,
                  pl.BlockSpec((1, w_out.shape[1], d), lambda i: (layer, 0, 0), pipeline_mode=pl.Buffered(1))]
        + _ffn_specs(g, layer, (6, 7, 8), d, wo.shape[1], ln_g, ln_b),
        out_specs=row,
        out_shape=jax.ShapeDtypeStruct(x.shape, F32),
        compiler_params=_params(("arbitrary",)),
        name="mix_ffn2",
    )(x, mod, *outs, w_out, mod, mod, mod, wi, wo, ln_g, ln_b)


def _prep_kernel(*refs, layer, nlayers, g_t, tm, decode, batch):
    (x_ref, sh_ref, sc_ref, w_ref, cos_ref, sin_ref, wg_ref, bg_ref, lb_ref, cw_ref, alog_ref, dtb_ref,
     bd_ref, ex_ref) = refs[:14]
    rest = refs[14:]
    if decode:
        cs_ref, alogc_ref, dtbc_ref = rest[:3]
        outs = rest[3:]
        cbuf = None
    else:
        outs = rest[:-1]
        cbuf = rest[-1]

    x = x_ref[...]
    hb = _bf(x * (1.0 + _mod_rows(sc_ref, tm)) + _mod_rows(sh_ref, tm))

    def proj(name):
        o, w = _OFF[name]
        return _dot(hb, w_ref[0, :, o:o + w])

    bd = bd_ref[...]

    dest = {}
    if decode:
        it = iter(outs)
        for mix, keys in (("ret", ("q", "k", "v", "gate")), ("gla", ("q", "k", "g", "v", "gate")),
                          ("hg", ("q", "k", "g", "v", "gate")), ("gdn", ("q", "k", "v", "gate"))):
            for kk in keys:
                dest[f"{mix}_{kk}"] = ("heads", next(it), 0)
        bgo_ref, convo_ref = next(it), next(it)
    else:
        ret_ref, gla_ref, hg_ref, gdn_ref, conv_ref = outs
        for dst, keys, widths in ((ret_ref, ("ret_q", "ret_k", "ret_v", "ret_gate"), (HW,) * 4),
                                  (gla_ref, ("gla_q", "gla_k", "gla_g", "gla_v", "gla_gate"),
                                   (HEADS * GLA_DK,) * 3 + (HW, HW)),
                                  (hg_ref, ("hg_q", "hg_k", "hg_g", "hg_v", "hg_gate"), (HW,) * 5),
                                  (gdn_ref, ("gdn_q", "gdn_k", "gdn_b", "gdn_g", "gdn_v", "gdn_gate"), (HW,) * 6)):
            off = 0
            for kk, w in zip(keys, widths):
                dest[kk] = ("rows", dst, off)
                off += w

    def put(name, val):
        kind, ref, off = dest[name]
        if kind == "rows":
            ref[:, off:off + val.shape[1]] = val
        else:
            dh = val.shape[1] // HEADS
            for t in range(tm // batch):
                vt = val[t * batch:(t + 1) * batch].T
                for h in range(HEADS):
                    ref[h, t] = vt[h * dh:(h + 1) * dh]

    small = proj("small")

    dq = proj("dqkv")
    cw = cw_ref[0]
    if decode:
        cs = cs_ref[0]
        nt = tm // batch
        blocks = [dq[t * batch:(t + 1) * batch] for t in range(nt)]
        convs = []
        for t in range(nt):
            acc = blocks[t] * cw[CONV_W - 1:CONV_W, :]
            for s in range(1, CONV_W):
                prev = blocks[t - s] if t >= s else cs[CONV_W - 1 - s + t]
                acc = acc + prev * cw[CONV_W - 1 - s:CONV_W - s, :]
            convs.append(acc)
        conv = jnp.concatenate(convs, axis=0)
        convo_ref[...] = dq[(nt - (CONV_W - 1)) * batch:].reshape(CONV_W - 1, batch, dq.shape[1])
    else:
        i = pl.program_id(0)
        per = g_t // tm

        @pl.when(i % per == 0)
        def _():
            cbuf[0:8, :] = jnp.zeros((8, dq.shape[1]), F32)

        cbuf[8:8 + tm, :] = dq
        conv = cbuf[pl.ds(8, tm), :] * cw[CONV_W - 1:CONV_W, :]
        for s in range(1, CONV_W):
            conv = conv + cbuf[pl.ds(8 - s, tm), :] * cw[CONV_W - 1 - s:CONV_W - s, :]
        tail = cbuf[tm:tm + 8, :]
        cbuf[0:8, :] = tail

        @pl.when(i % per == per - 1)
        def _():
            conv_ref[0] = tail[8 - (CONV_W - 1):8, :]

    u = _silu(conv)
    uq, uk = u[:, 0:HW], u[:, HW:2 * HW]
    put("gdn_q", uq * lax.rsqrt(_dot(_bf(uq * uq), bd) + RMS_EPS) * (HEAD_DIM ** -0.5))
    put("gdn_k", uk * lax.rsqrt(_dot(_bf(uk * uk), bd) + RMS_EPS))
    put("gdn_v", u[:, 2 * HW:3 * HW])
    put("gdn_gate", _silu(proj("dg")))
    if decode:
        for t in range(tm // batch):
            smt = small[t * batch:(t + 1) * batch].T
            beta = jax.nn.sigmoid(smt[SMALL_DB:SMALL_DB + HEADS])
            gd = -jnp.exp(alogc_ref[layer]) * _softplus(smt[SMALL_DA:SMALL_DA + HEADS] + dtbc_ref[layer])
            for h in range(HEADS):
                bgo_ref[h, t, 0:1, :] = beta[h:h + 1]
                bgo_ref[h, t, 1:2, :] = gd[h:h + 1]
    else:
        dbda = _dot_x01(small, ex_ref[...])
        put("gdn_b", jax.nn.sigmoid(dbda[:, 0:HW]))
        put("gdn_g", -jnp.exp(alog_ref[layer:layer + 1, :]) * _softplus(dbda[:, HW:2 * HW] + dtb_ref[layer:layer + 1, :]))

    rows = [lb_ref[i:i + 1, :] for i in range(nlayers)]
    mx = functools.reduce(jnp.maximum, rows)
    ex = [jnp.exp(r - mx) for r in rows]
    tot = functools.reduce(lambda a, b: a + b, ex)
    plb = [e / tot for e in ex]
    lb = functools.reduce(lambda a, b: a + b, plb[:layer + 1]) - plb[0]
    zf = proj("hf")
    put("hg_q", _silu(proj("hq")) * (HEAD_DIM ** -0.5))
    put("hg_k", (1.0 - lb) * jax.nn.sigmoid(-zf))
    put("hg_g", jnp.log(lb + (1.0 - lb) * jax.nn.sigmoid(zf)))
    put("hg_v", proj("hi"))
    put("hg_gate", _silu(proj("hg")))

    gpre = _dot(_bf(small), wg_ref[0]) + bg_ref[layer:layer + 1, :]
    put("gla_q", proj("aq") * (GLA_DK ** -0.5))
    put("gla_k", proj("ak"))
    put("gla_g", -_softplus(-gpre) * (1.0 / GLA_GATE_NORM))
    put("gla_v", proj("av"))
    put("gla_gate", _silu(proj("ag")))

    cos, sin = cos_ref[...], sin_ref[...]
    lane = lax.broadcasted_iota(jnp.int32, (tm, HW), 1)
    first_half = (lane % HEAD_DIM) < (HEAD_DIM // 2)

    def rope(v):
        sw = jnp.where(first_half, pltpu.roll(v, HW - HEAD_DIM // 2, 1), pltpu.roll(v, HEAD_DIM // 2, 1))
        return v * cos + sw * sin

    put("ret_q", rope(proj("rq")))
    put("ret_k", rope(proj("rk")) * (HEAD_DIM ** -0.5))
    put("ret_v", proj("rv"))
    put("ret_gate", _silu(proj("rg")))


def _prep(g, x, mod, w_in_p, cos_t, sin_t, wg_p, gla_bg, hg_lb, gdn_conv, alog_e, dtb_e, dec, layer, nlayers):
    d = x.shape[1]
    bd = jnp.asarray(_bd_ones(), BF16)
    exm = jnp.asarray(_expand_mat(), BF16)
    row = pl.BlockSpec((g.tm, d), lambda i: (i, 0))
    if g.decode:
        tab = pl.BlockSpec((g.tm, HW), lambda i: (i, 0))
    else:
        per = g.t // g.tm
        tab = pl.BlockSpec((g.tm, HW), lambda i: (i % per, 0))
    in_specs = [row, _mod_spec(g, layer, 3, d), _mod_spec(g, layer, 4, d),
                pl.BlockSpec((1, d, PROJ_W), lambda i: (layer, 0, 0), pipeline_mode=pl.Buffered(1)),
                tab, tab,
                pl.BlockSpec((1,) + wg_p.shape[1:], lambda i: (layer, 0, 0)),
                _const_spec(gla_bg), _const_spec(hg_lb),
                pl.BlockSpec((1,) + gdn_conv.shape[1:], lambda i: (layer, 0, 0)),
                _const_spec(alog_e), _const_spec(dtb_e), _const_spec(bd), _const_spec(exm)]
    args = [x, mod, mod, w_in_p, cos_t, sin_t, wg_p, gla_bg, hg_lb, gdn_conv, alog_e, dtb_e, bd, exm]
    cch = gdn_conv.shape[2]

    def rows_out(w):
        return jax.ShapeDtypeStruct((g.rows, w), F32), pl.BlockSpec((g.tm, w), lambda i: (i, 0))

    def whole(shape):
        return jax.ShapeDtypeStruct(shape, F32), pl.BlockSpec(shape, lambda i: (0,) * len(shape))

    if g.decode:
        assert g.steps == 1 and g.t >= CONV_W - 1
        cs, alog_c, dtb_c = dec
        in_specs += [pl.BlockSpec((1,) + cs.shape[1:], lambda i: (layer, 0, 0, 0)), _const_spec(alog_c),
                     _const_spec(dtb_c)]
        args += [cs, alog_c, dtb_c]
        outs = []
        for dhs in ((HEAD_DIM,) * 4, (GLA_DK,) * 3 + (HEAD_DIM,) * 2, (HEAD_DIM,) * 5, (HEAD_DIM,) * 4):
            outs += [whole((HEADS, g.t, dh, g.batch)) for dh in dhs]
        outs += [whole((HEADS, g.t, 2, g.batch)), whole((CONV_W - 1, g.batch, cch))]
        scratch = []
    else:
        outs = [rows_out(4 * HW), rows_out(3 * HEADS * GLA_DK + 2 * HW), rows_out(5 * HW), rows_out(6 * HW),
                (jax.ShapeDtypeStruct((g.batch, CONV_W - 1, cch), F32),
                 pl.BlockSpec((1, CONV_W - 1, cch), lambda i: (i // (g.t // g.tm), 0, 0)))]
        scratch = [pltpu.VMEM((g.tm + 8, cch), F32)]
    return pl.pallas_call(
        functools.partial(_prep_kernel, layer=layer, nlayers=nlayers, g_t=g.t, tm=g.tm, decode=g.decode,
                          batch=g.batch),
        grid=(g.steps,),
        in_specs=in_specs,
        out_specs=[o[1] for o in outs],
        out_shape=[o[0] for o in outs],
        scratch_shapes=scratch,
        compiler_params=_params(("arbitrary",)),
        name="prep",
    )(*args)


def _chunk(x, c):
    return x[c * CHUNK:(c + 1) * CHUNK]


def _per_chunk(fn, nch):
    return jnp.concatenate([fn(c) for c in range(nch)], axis=0)


def _finish_prompt(o, gate, normw, bd, o_ref):
    ms = _dot(_bf(o * o), bd) * (1.0 / HEAD_DIM)
    on = o * lax.rsqrt(ms + RMS_EPS)
    if normw is not None:
        on = on * normw
    o_ref[...] = _bf(on * gate)


def _state_out(s_t, fold_ref, st_ref):
    fold = fold_ref[...]
    h1, h2, h3 = _split3(s_t)
    st_ref[0] = _dot_tn(h1, fold) + _dot_tn(h2, fold) + _dot_tn(h3, fold)


def _gla_prompt_kernel(f_ref, kmask_ref, vmask_ref, smask_ref, lvl_ref, fold_ref, bd_ref, nw_ref, ratt_ref, rq_ref,
                       rk_ref, rs_ref, o_ref, st_ref, s_scr, *, dk, nch, ret, levels, has_norm, layer):
    j = pl.program_id(1)
    dkk = HEADS * dk

    @pl.when(j == 0)
    def _():
        s_scr[...] = jnp.zeros(s_scr.shape, F32)

    kmask, vmask, smask, bd = kmask_ref[...], vmask_ref[...], smask_ref[...], bd_ref[...]
    normw = nw_ref[layer:layer + 1, :] if has_norm else None

    def mm(x, y):
        xb, yb = _bf(x), _bf(y)
        return _per_chunk(lambda c: _dot_nt(_chunk(xb, c), _tile4(_chunk(yb, c)) * kmask), nch)

    q = f_ref[:, 0:dkk]
    k = f_ref[:, dkk:2 * dkk]
    if ret:
        v = f_ref[:, 2 * dkk:2 * dkk + HW]
        gate = f_ref[:, 2 * dkk + HW:2 * dkk + 2 * HW]
        att = _per_block(mm(q, k), ratt_ref[...], CHUNK)
        qg = _per_block(q, rq_ref[...], CHUNK)
        kd = _per_block(k, rk_ref[...], CHUNK)
        sdec = [rs_ref[...]] * nch
    else:
        g = f_ref[:, 2 * dkk:3 * dkk]
        v = f_ref[:, 3 * dkk:3 * dkk + HW]
        gate = f_ref[:, 3 * dkk + HW:3 * dkk + 2 * HW]
        G = _cumsum_rows(g, CHUNK)
        att = _decay_att(q, k, G, mm, lambda x, y: _dot_nt(_bf(x * y), kmask), lvl_ref, levels, CHUNK)
        qg = q * jnp.exp(G)
        glb = _block_row(G, CHUNK, CHUNK - 1)
        kd = k * jnp.exp(glb - G)
        sdec = [jnp.exp(G[(c + 1) * CHUNK - 1:(c + 1) * CHUNK, :]) for c in range(nch)]
    vb, qgb, kdb, attb = _bf(v), _bf(qg), _bf(kd), _bf(att)
    o_intra = _per_chunk(lambda c: _dot(_chunk(attb, c), _tile4(_chunk(vb, c)) * vmask), nch)
    upd = [_dot_tn(_chunk(vb, c), _chunk(kdb, c)) * smask for c in range(nch)]

    s = s_scr[...]
    o_inter = []
    for c in range(nch):
        o_inter.append(_dot_nt(_chunk(qgb, c), _bf(s)))
        s = s * sdec[c] + upd[c]
    s_scr[...] = s
    _finish_prompt(o_intra + jnp.concatenate(o_inter, axis=0), gate, normw, bd, o_ref)

    @pl.when(j == pl.num_programs(1) - 1)
    def _():
        _state_out(s, fold_ref, st_ref)


def _gla_prompt(g, feat, dk, normw, layer, ret):
    c = _chunk_consts(dk)
    dkk = HEADS * dk
    blk = min(MIX_TILE, g.t)
    nch = blk // CHUNK
    per = g.t // blk
    consts = [jnp.asarray(c["kmask"], BF16), jnp.asarray(c["vmask"], BF16), jnp.asarray(c["smask"]),
              jnp.asarray(c["lvl"]), jnp.asarray(c["fold"], BF16), jnp.asarray(_bd_ones(), BF16),
              normw if normw is not None else jnp.ones((1, HW), F32),
              jnp.asarray(c["ret_att"]), jnp.asarray(c["ret_q"]), jnp.asarray(c["ret_k"]), jnp.asarray(c["ret_s"])]
    w = feat.shape[1]
    o, st = pl.pallas_call(
        functools.partial(_gla_prompt_kernel, dk=dk, nch=nch, ret=ret, levels=c["levels"],
                          has_norm=normw is not None, layer=layer),
        grid=(g.batch, per),
        in_specs=[pl.BlockSpec((blk, w), lambda b, j: (b * per + j, 0))] + [_const_spec(a) for a in consts],
        out_specs=[pl.BlockSpec((blk, HW), lambda b, j: (b * per + j, 0)),
                   pl.BlockSpec((1, dkk, HEAD_DIM), lambda b, j: (b, 0, 0))],
        out_shape=[jax.ShapeDtypeStruct((g.rows, HW), BF16),
                   jax.ShapeDtypeStruct((g.batch, dkk, HEAD_DIM), F32)],
        scratch_shapes=[pltpu.VMEM((HW, dkk), F32)],
        compiler_params=_params(("arbitrary", "arbitrary")),
        name="ret_prompt" if ret else f"gla_prompt_dk{dk}",
    )(feat, *consts)
    return o, st.reshape(g.batch, HEADS, dk, HEAD_DIM)


def _gdn_prompt_kernel(f_ref, bdm_ref, smask_ref, incl_ref, strict_ref, eye_ref, fold_ref, bd_ref, nw_ref, o_ref,
                       st_ref, s_scr, *, nch, layer):
    j = pl.program_id(1)

    @pl.when(j == 0)
    def _():
        s_scr[...] = jnp.zeros(s_scr.shape, F32)

    bdm, smask, incl, strict, eye, bd = (bdm_ref[...], smask_ref[...], incl_ref[...], strict_ref[...], eye_ref[...],
                                         bd_ref[...])
    normw = nw_ref[layer:layer + 1, :]
    ones_c = jnp.ones((CHUNK, CHUNK), BF16)

    def bdw(yb, c):
        return _tile4(_chunk(yb, c)) * bdm

    def prod1(x, y):
        xb, yb = _bf(x), _bf(y)
        return _per_chunk(lambda c: _dot(_chunk(xb, c), bdw(yb, c)), nch)

    def prod1_pair(x1, x2, y):
        x1b, x2b, yb = _bf(x1), _bf(x2), _bf(y)
        both = [_dot(jnp.concatenate([_chunk(x1b, c), _chunk(x2b, c)], axis=0), bdw(yb, c)) for c in range(nch)]
        return (jnp.concatenate([m[:CHUNK] for m in both], axis=0),
                jnp.concatenate([m[CHUNK:] for m in both], axis=0))

    def prod2(x, y):
        xh, xl = _split2(x)
        yb = _bf(y)

        def one(c):
            top = _dot(jnp.concatenate([_chunk(xh, c), _chunk(xl, c)], axis=0), bdw(yb, c))
            return top[:CHUNK] + top[CHUNK:]

        return _per_chunk(one, nch)

    def prod3(x, y):
        xh, xl = _split2(x)
        yh, yl = _split2(y)

        def one(c):
            top = _dot(jnp.concatenate([_chunk(xh, c), _chunk(xl, c)], axis=0), bdw(yh, c))
            return top[:CHUNK] + top[CHUNK:] + _dot(_chunk(xh, c), bdw(yl, c))

        return _per_chunk(one, nch)

    q, k = f_ref[:, 0:HW], f_ref[:, HW:2 * HW]
    be, ge = f_ref[:, 2 * HW:3 * HW], f_ref[:, 3 * HW:4 * HW]
    v, gate = f_ref[:, 4 * HW:5 * HW], f_ref[:, 5 * HW:6 * HW]
    G = _cumsum_rows(ge, CHUNK)
    zd = _per_block(G, eye, CHUNK)
    grow = _per_chunk(lambda c: _dot_exact01(ones_c, _chunk(zd, c)), nch)
    L = _per_block(jnp.exp(jnp.minimum(G - grow, 0.0)), incl, CHUNK)
    qb, kb = _bf(q), _bf(k)

    def qkk(c):
        return _dot_nt(jnp.concatenate([_chunk(qb, c), _chunk(kb, c)], axis=0), bdw(kb, c))

    qkk_all = [qkk(c) for c in range(nch)]
    qk = jnp.concatenate([m[:CHUNK] for m in qkk_all], axis=0)
    kk = jnp.concatenate([m[CHUNK:] for m in qkk_all], axis=0)
    a = _per_block(be * kk * L, strict, CHUNK)
    eye_b = _per_block(jnp.ones_like(a), eye, CHUNK)
    p = -a
    tinv = eye_b + p
    for i in range(5):
        if i == 0:
            p = prod1(p, p)
        else:
            p, dt = prod1_pair(p, tinv, p)
            tinv = tinv + dt
    tinv = tinv + prod1(tinv, p)
    resid = eye_b - tinv - prod3(a, tinv)
    tinv = tinv + prod1(tinv, resid)
    eg = jnp.exp(G)
    u = prod2(tinv, be * v)
    w = prod2(tinv, be * eg * k)
    glb = _block_row(G, CHUNK, CHUNK - 1)
    wb, qeb, qklb = _bf(w), _bf(q * eg), _bf(qk * L)
    kdb = _bf(k * jnp.exp(glb - G))
    sdec = [jnp.exp(G[(c + 1) * CHUNK - 1:(c + 1) * CHUNK, :]) for c in range(nch)]

    ub = _bf(u)
    s_add = [_dot_tn(_chunk(ub, c), _chunk(kdb, c)) * smask for c in range(nch)]
    s_mul = [_bf(_dot_tn(_chunk(wb, c), _chunk(kdb, c)) * smask) for c in range(nch)]

    s = s_scr[...]
    snaps = []
    for c in range(nch):
        sb = _bf(s)
        snaps.append(sb)
        s = s * sdec[c] + s_add[c] - _dot(sb, s_mul[c])
    s_scr[...] = s

    outs = []
    for c in range(nch):
        ws = _dot_nt(jnp.concatenate([_chunk(wb, c), _chunk(qeb, c)], axis=0), snaps[c])
        db = _bf(_chunk(u, c) - ws[:CHUNK])
        outs.append(_dot(_chunk(qklb, c), _tile4(db) * bdm) + ws[CHUNK:])
    _finish_prompt(jnp.concatenate(outs, axis=0), gate, normw, bd, o_ref)

    @pl.when(j == pl.num_programs(1) - 1)
    def _():
        _state_out(s, fold_ref, st_ref)


def _gdn_prompt(g, feat, normw, layer):
    c = _chunk_consts(HEAD_DIM)
    blk = min(MIX_TILE, g.t)
    nch = blk // CHUNK
    per = g.t // blk
    consts = [jnp.asarray(c["vmask"], BF16), jnp.asarray(c["smask"]), jnp.asarray(c["incl"]), jnp.asarray(c["strict"]),
              jnp.asarray(c["eye"]), jnp.asarray(c["fold"], BF16), jnp.asarray(_bd_ones(), BF16), normw]
    w = feat.shape[1]
    o, st = pl.pallas_call(
        functools.partial(_gdn_prompt_kernel, nch=nch, layer=layer),
        grid=(g.batch, per),
        in_specs=[pl.BlockSpec((blk, w), lambda b, j: (b * per + j, 0))] + [_const_spec(a) for a in consts],
        out_specs=[pl.BlockSpec((blk, HW), lambda b, j: (b * per + j, 0)),
                   pl.BlockSpec((1, HW, HEAD_DIM), lambda b, j: (b, 0, 0))],
        out_shape=[jax.ShapeDtypeStruct((g.rows, HW), BF16),
                   jax.ShapeDtypeStruct((g.batch, HW, HEAD_DIM), F32)],
        scratch_shapes=[pltpu.VMEM((HW, HW), F32)],
        compiler_params=_params(("arbitrary", "arbitrary")),
        name="gdn_prompt",
    )(feat, *consts)
    return o, st.reshape(g.batch, HEADS, HEAD_DIM, HEAD_DIM)


def _state_io(refs, layer):
    if layer == 0:
        return refs
    prev_ref, rest = refs[0], refs[1:]
    rest[-1][0:layer] = prev_ref[...]
    return rest


def _dec_finish(o, gate, normw):
    ms = jnp.mean(o * o, axis=0, keepdims=True)
    on = o * lax.rsqrt(ms + RMS_EPS)
    if normw is not None:
        on = on * normw
    return on * gate


def _gla_decode_kernel(q_ref, k_ref, g_ref, v_ref, gate_ref, s_ref, *refs, nt, dk, ret, has_norm, layer):
    nw_ref, o_ref, so_ref = _state_io(refs, layer)
    normw = nw_ref[layer] if has_norm else None
    qs = [q_ref[0, t] for t in range(nt)]
    ks = [k_ref[0, t] for t in range(nt)]
    vs = [v_ref[0, t] for t in range(nt)]
    decay = [jnp.exp(g_ref[0, t]) for t in range(nt)]
    o_acc = [jnp.zeros(vs[0].shape, F32) for _ in range(nt)]
    for dc in range(dk // DEC_DCHUNK):
        r = slice(dc * DEC_DCHUNK, (dc + 1) * DEC_DCHUNK)
        sc = s_ref[0, 0, r]
        for t in range(nt):
            a = decay[t][None] if ret else decay[t][r][:, None, :]
            sc = sc * a + ks[t][r][:, None, :] * vs[t][None, :, :]
            o_acc[t] = o_acc[t] + jnp.sum(qs[t][r][:, None, :] * sc, axis=0)
        so_ref[layer, 0, r] = sc
    for t in range(nt):
        o_ref[0, t] = _dec_finish(o_acc[t], gate_ref[0, t], normw)


def _gdn_decode_kernel(q_ref, k_ref, v_ref, gate_ref, bg_ref, s_ref, *refs, nt, layer):
    nw_ref, o_ref, so_ref = _state_io(refs, layer)
    normw = nw_ref[layer]
    chunks = [slice(dc * DEC_DCHUNK, (dc + 1) * DEC_DCHUNK) for dc in range(HEAD_DIM // DEC_DCHUNK)]
    for t in range(nt):
        q, k, v = q_ref[0, t], k_ref[0, t], v_ref[0, t]
        beta = bg_ref[0, t, 0:1, :]
        ag = jnp.exp(bg_ref[0, t, 1:2, :])

        def state(r):
            return s_ref[0, 0, r] if t == 0 else so_ref[layer, 0, r]

        ks = jnp.zeros(v.shape, F32)
        for r in chunks:
            ks = ks + jnp.sum(k[r][:, None, :] * state(r), axis=0)
        delta = beta * (v - ag * ks)
        o = jnp.zeros(v.shape, F32)
        for r in chunks:
            sc = state(r) * ag[None] + k[r][:, None, :] * delta[None, :, :]
            so_ref[layer, 0, r] = sc
            o = o + jnp.sum(q[r][:, None, :] * sc, axis=0)
        o_ref[0, t] = _dec_finish(o, gate_ref[0, t], normw)


def _decode_call(body, name, feats, state_t, prev, extra, dk, layer):
    nt, batch = feats[0].shape[1], feats[0].shape[3]

    def head_spec(a):
        return pl.BlockSpec((1,) + a.shape[1:], lambda h: (h,) + (0,) * (a.ndim - 1))

    def st_spec(nl):
        return pl.BlockSpec((nl, 1, dk, HEAD_DIM, batch), lambda h: (0, h, 0, 0, 0))

    in_specs = [head_spec(a) for a in feats]
    in_specs.append(pl.BlockSpec((1, 1, dk, HEAD_DIM, batch), lambda h: (layer, h, 0, 0, 0)))
    args = list(feats) + [state_t]
    if layer > 0:
        in_specs.append(st_spec(layer))
        args.append(prev)
    o_shape = (HEADS, nt, HEAD_DIM, batch)
    return pl.pallas_call(
        body,
        grid=(HEADS,),
        in_specs=in_specs + [_const_spec(a) for a in extra],
        out_specs=[pl.BlockSpec((1,) + o_shape[1:], lambda h: (h, 0, 0, 0)), st_spec(layer + 1)],
        out_shape=[jax.ShapeDtypeStruct(o_shape, F32),
                   jax.ShapeDtypeStruct((layer + 1, HEADS, dk, HEAD_DIM, batch), F32)],
        compiler_params=_params(("arbitrary",)),
        name=name,
    )(*args, *extra)


def _gla_decode(q, k, g, v, gate, state_t, prev, dk, normw_col, layer, ret):
    nt = q.shape[1]
    body = functools.partial(_gla_decode_kernel, nt=nt, dk=dk, ret=ret, has_norm=normw_col is not None, layer=layer)
    nw = normw_col if normw_col is not None else jnp.ones((layer + 1, HEAD_DIM, 1), F32)
    return _decode_call(body, "ret_decode" if ret else f"gla_decode_dk{dk}", [q, k, g, v, gate], state_t, prev, [nw],
                        dk, layer)


def _gdn_decode(q, k, v, gate, bg, state_t, prev, normw_col, layer):
    body = functools.partial(_gdn_decode_kernel, nt=q.shape[1], layer=layer)
    return _decode_call(body, "gdn_decode", [q, k, v, gate, bg], state_t, prev, [normw_col], HEAD_DIM, layer)


def _rope_tables(pos):
    half = HEAD_DIM // 2
    inv = ROPE_BASE ** (-jnp.arange(half, dtype=F32) / half)
    ang = pos[:, None] * inv[None, :]
    cos, sin = jnp.cos(ang), jnp.sin(ang)
    cos_t = jnp.tile(jnp.concatenate([cos, cos], axis=1), (1, HEADS))
    sin_t = jnp.tile(jnp.concatenate([-sin, sin], axis=1), (1, HEADS))
    return cos_t, sin_t


def _relayout_w_in(w_in):
    parts = []
    for name, w in _SLABS:
        if name == "small":
            sm = jnp.concatenate([w_in[:, :, _SRC["alr"]:_SRC["alr"] + 16], w_in[:, :, _SRC["db"]:_SRC["db"] + 4],
                                  w_in[:, :, _SRC["da"]:_SRC["da"] + 4]], axis=2)
            parts.append(jnp.pad(sm, ((0, 0), (0, 0), (0, w - sm.shape[2]))))
        else:
            parts.append(w_in[:, :, _SRC[name]:_SRC[name] + w])
    return jnp.concatenate(parts, axis=2).astype(BF16)


def kernel(x_prompt, x_sample, state_ret, state_gla, state_hgrn, state_gdn, state_gdn_conv, c_prompt, c_sample, ada_w, ada_b, ln_g, ln_b, ffn1_wi, ffn1_wo, ffn2_wi, ffn2_wo, w_in, gla_wg, gla_bg, hg_lb, gdn_conv, gdn_a_log, gdn_dt_bias, gla_norm, hg_norm, gdn_norm, w_out):
    nl, d = ada_w.shape[0], ada_w.shape[1]
    bp, tp = x_prompt.shape[0], x_prompt.shape[1]
    bs, ts = x_sample.shape[0], x_sample.shape[1]
    alpha = (2.0 * nl) ** 0.25
    assert tp % CHUNK == 0 and ts >= CONV_W - 1

    wi1, wo1, wi2, wo2 = (w.astype(BF16) for w in (ffn1_wi, ffn1_wo, ffn2_wi, ffn2_wo))
    w_out_b = w_out.astype(BF16)
    w_in_p = _relayout_w_in(w_in)
    wg_p = jnp.pad(gla_wg, ((0, 0), (0, 128 - gla_wg.shape[1]), (0, 0))).astype(BF16)
    alog_e = jnp.repeat(gdn_a_log, HEAD_DIM, axis=1)
    dtb_e = jnp.repeat(gdn_dt_bias, HEAD_DIM, axis=1)
    gla_nw = jnp.tile(gla_norm, (1, HEADS))
    hg_nw = jnp.tile(hg_norm, (1, HEADS))
    gdn_nw = jnp.tile(gdn_norm, (1, HEADS))

    mod = _ada(jnp.concatenate([c_prompt, c_sample], axis=0), ada_w, ada_b)
    mod = mod.reshape(nl, bp + bs, N_MOD, d).transpose(0, 2, 1, 3)
    mod_p = mod[:, :, :bp].reshape(nl, N_MOD, bp, 1, d)
    mod_s = mod[:, :, bp:]

    gp = _Rows(bp, tp, decode=False)
    gp_prep = _Rows(bp, tp, decode=False, tile=PREP_TILE)
    gs = _Rows(bs, ts, decode=True)
    assert gs.steps == 1
    cos_p, sin_p = _rope_tables(jnp.arange(tp, dtype=F32))
    cos_s, sin_s = _rope_tables(PAST_LEN + jnp.arange(ts, dtype=F32))
    cos_s, sin_s = jnp.repeat(cos_s, bs, axis=0), jnp.repeat(sin_s, bs, axis=0)

    xp = x_prompt.reshape(bp * tp, d)
    xs = jnp.transpose(x_sample, (1, 0, 2)).reshape(ts * bs, d)

    st_ret, st_gla, st_hg, st_gdn = (jnp.transpose(s, (0, 2, 3, 4, 1))
                                     for s in (state_ret, state_gla, state_hgrn, state_gdn))
    conv_t = jnp.transpose(state_gdn_conv, (0, 2, 1, 3))
    dec_extra = (conv_t, gdn_a_log[:, :, None], gdn_dt_bias[:, :, None])
    ret_g = jnp.asarray(np.broadcast_to(_ret_gamma().astype(np.float32)[:, None, None, None], (HEADS, ts, 1, bs)))
    gla_nc, hg_nc, gdn_nc = gla_norm[:, :, None], hg_norm[:, :, None], gdn_norm[:, :, None]

    p_states, s_convs = [], []
    s_ret = s_gla = s_hg = s_gdn = None
    for l in range(nl):
        xp = _ffn(gp, xp, mod_p, wi1, wo1, ln_g, ln_b, l, alpha)
        ret_f, gla_f, hg_f, gdn_f, p_conv = _prep(gp_prep, xp, mod_p, w_in_p, cos_p, sin_p, wg_p, gla_bg, hg_lb,
                                                  gdn_conv, alog_e, dtb_e, None, l, nl)
        o_ret, p_ret = _gla_prompt(gp, ret_f, HEAD_DIM, None, l, ret=True)
        o_gla, p_gla = _gla_prompt(gp, gla_f, GLA_DK, gla_nw, l, ret=False)
        o_hg, p_hg = _gla_prompt(gp, hg_f, HEAD_DIM, hg_nw, l, ret=False)
        o_gdn, p_gdn = _gdn_prompt(gp, gdn_f, gdn_nw, l)
        xp = _mix_ffn(gp, xp, mod_p, (o_ret, o_gla, o_hg, o_gdn), w_out_b, wi2, wo2, ln_g, ln_b, l, alpha)
        p_states.append((p_ret, p_gla, p_hg, p_gdn, p_conv))

        xs = _ffn(gs, xs, mod_s, wi1, wo1, ln_g, ln_b, l, alpha)
        (rq, rk, rv, rgt, aq, ak, ag, av, agt, hq, hk, hgg, hv, hgt, dq, dk_, dv, dgt, dbg, conv_new) = _prep(
            gs, xs, mod_s, w_in_p, cos_s, sin_s, wg_p, gla_bg, hg_lb, gdn_conv, alog_e, dtb_e, dec_extra, l, nl)
        o_ret, s_ret = _gla_decode(rq, rk, ret_g, rv, rgt, st_ret, s_ret, HEAD_DIM, None, l, True)
        o_gla, s_gla = _gla_decode(aq, ak, ag, av, agt, st_gla, s_gla, GLA_DK, gla_nc, l, False)
        o_hg, s_hg = _gla_decode(hq, hk, hgg, hv, hgt, st_hg, s_hg, HEAD_DIM, hg_nc, l, False)
        o_gdn, s_gdn = _gdn_decode(dq, dk_, dv, dgt, dbg, st_gdn, s_gdn, gdn_nc, l)
        xs = _mix_ffn(gs, xs, mod_s, (o_ret, o_gla, o_hg, o_gdn), w_out_b, wi2, wo2, ln_g, ln_b, l, alpha)
        s_convs.append(conv_new)

    y_prompt = xp.reshape(bp, tp, d)
    y_sample = jnp.transpose(xs.reshape(ts, bs, d), (1, 0, 2))
    p_out = tuple(jnp.stack([st[i] for st in p_states]) for i in range(5))
    s_out = tuple(jnp.transpose(s, (0, 4, 1, 2, 3)) for s in (s_ret, s_gla, s_hg, s_gdn))
    return (y_prompt, y_sample) + p_out + s_out + (jnp.transpose(jnp.stack(s_convs), (0, 2, 1, 3)),)
```

```python
import functools

import numpy as np
import jax
import jax.numpy as jnp
from jax import lax
from jax.experimental import pallas as pl
from jax.experimental.pallas import tpu as pltpu

F32 = jnp.float32
BF16 = jnp.bfloat16

HEAD_DIM = 64
HEADS = 4
GLA_DK = 32
GLA_GATE_RANK = 16
GLA_GATE_NORM = 16.0
CONV_W = 4
CHUNK = 64
ROPE_BASE = 10000.0
LN_EPS = 1e-5
RMS_EPS = 1e-6
PAST_LEN = 16384
N_MOD = 9
HW = HEADS * HEAD_DIM

V7X_LANES = 128
V7X_SUBLANES = 8
V7X_VMEM_BYTES = 64 * 1024 * 1024

ROW_TILE = 512
PREP_TILE = 512
MIX_TILE = 2048
DEC_DCHUNK = V7X_SUBLANES
FF_CHUNK = 256
VMEM_LIMIT = V7X_VMEM_BYTES - 12 * 1024 * 1024

_MIX_COLS = (("rq", HW), ("rk", HW), ("rv", HW), ("rg", HW),
             ("aq", HEADS * GLA_DK), ("ak", HEADS * GLA_DK), ("av", HW), ("alr", GLA_GATE_RANK), ("ag", HW),
             ("hq", HW), ("hf", HW), ("hi", HW), ("hg", HW),
             ("dqkv", 3 * HW), ("db", HEADS), ("da", HEADS), ("dg", HW))
_SRC = {}
_o = 0
for _n, _w in _MIX_COLS:
    _SRC[_n] = (_o, _w)
    _o += _w
_NARROW = ("alr", "db", "da")
_SLABS = tuple((n, w) for n, w in _MIX_COLS if n not in _NARROW) + (("small", V7X_LANES),)
_OFF = {}
_o = 0
for _n, _w in _SLABS:
    assert _o % V7X_LANES == 0
    _OFF[_n] = (_o, _w)
    _o += _w
PROJ_W = _o
SMALL_ALR, SMALL_DB, SMALL_DA = 0, GLA_GATE_RANK, GLA_GATE_RANK + HEADS


def _bf(x):
    return x.astype(BF16)


def _dot(a, b):
    return jnp.dot(a, b, preferred_element_type=F32)


def _dot_nt(a, b):
    return lax.dot_general(a, b, (((1,), (1,)), ((), ())), preferred_element_type=F32)


def _dot_tn(a, b):
    return lax.dot_general(a, b, (((0,), (0,)), ((), ())), preferred_element_type=F32)


def _split2(x):
    hi = x.astype(BF16)
    lo = (x - hi.astype(F32)).astype(BF16)
    return hi, lo


def _split3(x):
    hi = x.astype(BF16)
    r = x - hi.astype(F32)
    mid = r.astype(BF16)
    lo = (r - mid.astype(F32)).astype(BF16)
    return hi, mid, lo


def _dot_x01(x, m01):
    hi, lo = _split2(x)
    return _dot(hi, m01) + _dot(lo, m01)


def _dot_exact01(m01, x):
    h1, h2, h3 = _split3(x)
    return _dot(m01, h1) + _dot(m01, h2) + _dot(m01, h3)


def _silu(x):
    return x * jax.nn.sigmoid(x)


def _softplus(x):
    return jnp.maximum(x, 0.0) + jnp.log(1.0 + jnp.exp(-jnp.abs(x)))


def _ln(y, g, b):
    mu = jnp.mean(y, axis=-1, keepdims=True)
    yc = y - mu
    var = jnp.mean(yc * yc, axis=-1, keepdims=True)
    return yc * lax.rsqrt(var + LN_EPS) * g + b


def _mod_rows(ref, tm):
    v = ref[...]
    v = v.reshape(v.shape[-2], v.shape[-1])
    nb = v.shape[0]
    if nb == 1:
        return v
    return jnp.concatenate([v] * (tm // nb), axis=0)


def _tile4(x):
    return jnp.concatenate([x, x, x, x], axis=0)


def _per_block(x, const, period):
    n = x.shape[0] // period
    return (x.reshape(n, period, x.shape[1]) * const[None]).reshape(x.shape)


def _cumsum_rows(g, period):
    rows, w = g.shape
    tile = min(period, V7X_SUBLANES)
    row = lax.broadcasted_iota(jnp.int32, g.shape, 0) % tile
    out = g
    sh = 1
    while sh < tile:
        out = out + jnp.where(row >= sh, pltpu.roll(out, sh, 0), 0.0)
        sh *= 2
    if period == tile:
        return out
    nper = period // tile
    o4 = out.reshape(rows // period, nper, tile, w)
    tot = o4[:, :, tile - 1:tile, :]
    offs = [jnp.zeros_like(tot[:, 0:1])]
    for i in range(1, nper):
        offs.append(offs[-1] + tot[:, i - 1:i])
    return (o4 + jnp.concatenate(offs, axis=1)).reshape(rows, w)


def _block_row(G, period, r):
    c, w = G.shape
    g3 = G.reshape(c // period, period, w)
    return jnp.broadcast_to(g3[:, r:r + 1, :], g3.shape).reshape(c, w)


def _gref(G, m):
    c = G.shape[0]
    if 2 * m >= V7X_SUBLANES:
        return _block_row(G, 2 * m, m - 1)
    p = lax.broadcasted_iota(jnp.int32, G.shape, 0) % (2 * m)
    if m == 2:
        return jnp.where(p == 0, pltpu.roll(G, c - 1, 0),
                         jnp.where(p == 1, G, jnp.where(p == 2, pltpu.roll(G, 1, 0), pltpu.roll(G, 2, 0))))
    return jnp.where(p == 0, G, pltpu.roll(G, 1, 0))


def _decay_att(q, k, G, mm, diag, lvl_ref, levels, period):
    att = _per_block(diag(q, k), lvl_ref[len(levels)], period)
    for i, m in enumerate(levels):
        if m == 1:
            ql = q * jnp.exp(jnp.minimum(G - pltpu.roll(G, 1, 0), 0.0))
            att = att + _per_block(diag(ql, pltpu.roll(k, 1, 0)), lvl_ref[i], period)
            continue
        gr = _gref(G, m)
        ql = q * jnp.exp(jnp.minimum(G - gr, 0.0))
        kl = k * jnp.exp(jnp.minimum(gr - G, 0.0))
        att = att + _per_block(mm(ql, kl), lvl_ref[i], period)
    return att


def _ret_gamma():
    return np.log(1.0 - 2.0 ** (-5.0 - np.arange(HEADS, dtype=np.float64)))


@functools.lru_cache(maxsize=None)
def _chunk_consts(dk):
    c = CHUNK
    h_att = np.arange(4 * c) // c
    s_att = np.arange(4 * c) % c
    t = np.arange(c)
    kmask = (h_att[:, None] == (np.arange(4 * dk) // dk)[None, :]).astype(np.float32)
    vmask = (h_att[:, None] == (np.arange(HW) // HEAD_DIM)[None, :]).astype(np.float32)
    smask = ((np.arange(HW) // HEAD_DIM)[:, None] == (np.arange(4 * dk) // dk)[None, :]).astype(np.float32)
    levels = (32, 16, 8, 4, 2, 1)
    lv = []
    for m in levels:
        same = (t[:, None] // (2 * m)) == (s_att[None, :] // (2 * m))
        lv.append(same & ((t[:, None] % (2 * m)) >= m) & ((s_att[None, :] % (2 * m)) < m))
    lv.append(t[:, None] == s_att[None, :])
    lvl = np.stack(lv).astype(np.float32)
    incl = (s_att[None, :] <= t[:, None]).astype(np.float32)
    strict = (s_att[None, :] < t[:, None]).astype(np.float32)
    eye = (s_att[None, :] == t[:, None]).astype(np.float32)
    fold = (np.arange(HW)[:, None] % HEAD_DIM == np.arange(HEAD_DIM)[None, :]).astype(np.float32)
    gam = _ret_gamma()
    gh = gam[h_att]
    ret_att = np.where(s_att[None, :] <= t[:, None], np.exp((t[:, None] - s_att[None, :]) * gh[None, :]), 0.0)
    gl = gam[np.arange(HW) // HEAD_DIM]
    ret_q = np.exp((t[:, None] + 1) * gl[None, :])
    ret_k = np.exp((c - 1 - t[:, None]) * gl[None, :])
    ret_s = np.exp(c * gl)[None, :]
    return dict(kmask=kmask, vmask=vmask, smask=smask, lvl=lvl, levels=levels, incl=incl, strict=strict,
                eye=eye, fold=fold, ret_att=ret_att.astype(np.float32), ret_q=ret_q.astype(np.float32),
                ret_k=ret_k.astype(np.float32), ret_s=ret_s.astype(np.float32))


def _bd_ones():
    h = np.arange(HW) // HEAD_DIM
    return (h[:, None] == h[None, :]).astype(np.float32)


def _expand_mat():
    e = np.zeros((_OFF["small"][1], 2 * HW), np.float32)
    for h in range(HEADS):
        e[SMALL_DB + h, h * HEAD_DIM:(h + 1) * HEAD_DIM] = 1.0
        e[SMALL_DA + h, HW + h * HEAD_DIM:HW + (h + 1) * HEAD_DIM] = 1.0
    return e


def _const_spec(a):
    nd = a.ndim
    return pl.BlockSpec(a.shape, lambda *_: (0,) * nd)


def _params(sem):
    return pltpu.CompilerParams(dimension_semantics=sem, vmem_limit_bytes=VMEM_LIMIT)


def _ada_kernel(c_ref, w_ref, b_ref, o_ref):
    c = c_ref[...]
    o_ref[0] = _dot(_bf(_silu(c)), _bf(w_ref[0])) + b_ref[0]


def _ada(c_all, ada_w, ada_b):
    nl, d, nd = ada_w.shape
    nb = c_all.shape[0]
    tn = d
    return pl.pallas_call(
        _ada_kernel,
        grid=(nl, nd // tn),
        in_specs=[pl.BlockSpec((nb, d), lambda l, j: (0, 0)),
                  pl.BlockSpec((1, d, tn), lambda l, j: (l, 0, j)),
                  pl.BlockSpec((1, 1, tn), lambda l, j: (l, 0, j))],
        out_specs=pl.BlockSpec((1, nb, tn), lambda l, j: (l, 0, j)),
        out_shape=jax.ShapeDtypeStruct((nl, nb, nd), F32),
        compiler_params=_params(("arbitrary", "arbitrary")),
        name="ada_mod",
    )(c_all, ada_w, ada_b.reshape(nl, 1, nd))


class _Rows:
    def __init__(self, batch, t, decode, tile=ROW_TILE):
        self.batch, self.t, self.decode = batch, t, decode
        self.rows = batch * t
        self.tm = min(tile, self.rows)
        assert self.rows % self.tm == 0
        if decode:
            assert self.tm % batch == 0
        else:
            assert t % self.tm == 0
        self.steps = self.rows // self.tm


def _mod_spec(g, layer, k, d):
    if g.decode:
        return pl.BlockSpec((1, 1, g.batch, d), lambda i: (layer, k, 0, 0))
    per = g.t // g.tm
    return pl.BlockSpec((1, 1, 1, 1, d), lambda i: (layer, k, i // per, 0, 0))


def _ffn_block(x, sh_ref, sc_ref, gt_ref, wi_ref, wo_ref, lng_ref, lnb_ref, *, alpha, ln_idx, layer):
    tm, d = x.shape
    dff = wo_ref.shape[1]
    sh, sc, gt = _mod_rows(sh_ref, tm), _mod_rows(sc_ref, tm), _mod_rows(gt_ref, tm)
    h = _bf(x * (1.0 + sc) + sh)
    acc = jnp.zeros((tm, d), F32)
    off = 0
    while off < dff:
        fc = min(FF_CHUNK, dff - off)
        a = _dot(h, wi_ref[0, :, off:off + fc])
        b = _dot(h, wi_ref[0, :, dff + off:dff + off + fc])
        acc = acc + _dot(_bf(_silu(a) * b), wo_ref[0, off:off + fc, :])
        off += fc
    y = alpha * x + 0.5 * (1.0 + gt) * acc
    return _ln(y, lng_ref[layer, ln_idx:ln_idx + 1, :], lnb_ref[layer, ln_idx:ln_idx + 1, :])


def _mix_block(x, gt_ref, o_refs, w_ref, lng_ref, lnb_ref, *, alpha, layer, decode, batch):
    tm, d = x.shape
    gt = _mod_rows(gt_ref, tm)
    if decode:
        accs = [jnp.zeros((batch, d), F32) for _ in range(tm // batch)]
        for mi, o_ref in enumerate(o_refs):
            for h in range(HEADS):
                r0 = mi * HW + h * HEAD_DIM
                wh = w_ref[0, r0:r0 + HEAD_DIM, :]
                for t in range(tm // batch):
                    accs[t] = accs[t] + _dot_tn(_bf(o_ref[h, t]), wh)
        acc = jnp.concatenate(accs, axis=0)
    else:
        acc = jnp.zeros((tm, d), F32)
        for mi, o_ref in enumerate(o_refs):
            acc = acc + _dot(o_ref[...], w_ref[0, mi * HW:(mi + 1) * HW, :])
    y = alpha * x + (1.0 + gt) * acc
    return _ln(y, lng_ref[layer, 1:2, :], lnb_ref[layer, 1:2, :])


def _ffn_kernel(x_ref, sh_ref, sc_ref, gt_ref, wi_ref, wo_ref, lng_ref, lnb_ref, o_ref, **kw):
    o_ref[...] = _ffn_block(x_ref[...], sh_ref, sc_ref, gt_ref, wi_ref, wo_ref, lng_ref, lnb_ref, **kw)


def _mix_ffn_kernel(x_ref, gtm_ref, o0_ref, o1_ref, o2_ref, o3_ref, w_ref, sh_ref, sc_ref, gt_ref, wi_ref, wo_ref,
                    lng_ref, lnb_ref, y_ref, *, alpha, layer, decode, batch):
    x1 = _mix_block(x_ref[...], gtm_ref, (o0_ref, o1_ref, o2_ref, o3_ref), w_ref, lng_ref, lnb_ref, alpha=alpha,
                    layer=layer, decode=decode, batch=batch)
    y_ref[...] = _ffn_block(x1, sh_ref, sc_ref, gt_ref, wi_ref, wo_ref, lng_ref, lnb_ref, alpha=alpha, ln_idx=2,
                            layer=layer)


def _ffn_specs(g, layer, mods, d, dff, ln_g, ln_b):
    return [_mod_spec(g, layer, k, d) for k in mods] + [
        pl.BlockSpec((1, d, 2 * dff), lambda i: (layer, 0, 0), pipeline_mode=pl.Buffered(1)),
        pl.BlockSpec((1, dff, d), lambda i: (layer, 0, 0), pipeline_mode=pl.Buffered(1)),
        _const_spec(ln_g), _const_spec(ln_b)]


def _ffn(g, x, mod, wi, wo, ln_g, ln_b, layer, alpha):
    d = x.shape[1]
    row = pl.BlockSpec((g.tm, d), lambda i: (i, 0))
    return pl.pallas_call(
        functools.partial(_ffn_kernel, alpha=alpha, ln_idx=0, layer=layer),
        grid=(g.steps,),
        in_specs=[row] + _ffn_specs(g, layer, (0, 1, 2), d, wo.shape[1], ln_g, ln_b),
        out_specs=row,
        out_shape=jax.ShapeDtypeStruct(x.shape, F32),
        compiler_params=_params(("arbitrary",)),
        name="ffn0",
    )(x, mod, mod, mod, wi, wo, ln_g, ln_b)


def _mix_ffn(g, x, mod, outs, w_out, wi, wo, ln_g, ln_b, layer, alpha):
    d = x.shape[1]
    row = pl.BlockSpec((g.tm, d), lambda i: (i, 0))
    if g.decode:
        assert g.steps == 1
        ospec = _const_spec(outs[0])
    else:
        ospec = pl.BlockSpec((g.tm, HW), lambda i: (i, 0))
    return pl.pallas_call(
        functools.partial(_mix_ffn_kernel, alpha=alpha, layer=layer, decode=g.decode, batch=g.batch),
        grid=(g.steps,),
        in_specs=[row, _mod_spec(g, layer, 5, d), ospec, ospec, ospec, ospec,
                  pl.BlockSpec((1, w_out.shape[1], d), lambda i: (layer, 0, 0), pipeline_mode=pl.Buffered(1))]
        + _ffn_specs(g, layer, (6, 7, 8), d, wo.shape[1], ln_g, ln_b),
        out_specs=row,
        out_shape=jax.ShapeDtypeStruct(x.shape, F32),
        compiler_params=_params(("arbitrary",)),
        name="mix_ffn2",
    )(x, mod, *outs, w_out, mod, mod, mod, wi, wo, ln_g, ln_b)


def _prep_kernel(*refs, layer, nlayers, g_t, tm, decode, batch):
    (x_ref, sh_ref, sc_ref, w_ref, cos_ref, sin_ref, wg_ref, bg_ref, lb_ref, cw_ref, alog_ref, dtb_ref,
     bd_ref, ex_ref) = refs[:14]
    rest = refs[14:]
    if decode:
        cs_ref, alogc_ref, dtbc_ref = rest[:3]
        outs = rest[3:]
        cbuf = None
    else:
        outs = rest[:-1]
        cbuf = rest[-1]

    x = x_ref[...]
    hb = _bf(x * (1.0 + _mod_rows(sc_ref, tm)) + _mod_rows(sh_ref, tm))

    def proj(name):
        o, w = _OFF[name]
        return _dot(hb, w_ref[0, :, o:o + w])

    bd = bd_ref[...]

    dest = {}
    if decode:
        it = iter(outs)
        for mix, keys in (("ret", ("q", "k", "v", "gate")), ("gla", ("q", "k", "g", "v", "gate")),
                          ("hg", ("q", "k", "g", "v", "gate")), ("gdn", ("q", "k", "v", "gate"))):
            for kk in keys:
                dest[f"{mix}_{kk}"] = ("heads", next(it), 0)
        bgo_ref, convo_ref = next(it), next(it)
    else:
        ret_ref, gla_ref, hg_ref, gdn_ref, conv_ref = outs
        for dst, keys, widths in ((ret_ref, ("ret_q", "ret_k", "ret_v", "ret_gate"), (HW,) * 4),
                                  (gla_ref, ("gla_q", "gla_k", "gla_g", "gla_v", "gla_gate"),
                                   (HEADS * GLA_DK,) * 3 + (HW, HW)),
                                  (hg_ref, ("hg_q", "hg_k", "hg_g", "hg_v", "hg_gate"), (HW,) * 5),
                                  (gdn_ref, ("gdn_q", "gdn_k", "gdn_b", "gdn_g", "gdn_v", "gdn_gate"), (HW,) * 6)):
            off = 0
            for kk, w in zip(keys, widths):
                dest[kk] = ("rows", dst, off)
                off += w

    def put(name, val):
        kind, ref, off = dest[name]
        if kind == "rows":
            ref[:, off:off + val.shape[1]] = val
        else:
            dh = val.shape[1] // HEADS
            for t in range(tm // batch):
                vt = val[t * batch:(t + 1) * batch].T
                for h in range(HEADS):
                    ref[h, t] = vt[h * dh:(h + 1) * dh]

    small = proj("small")

    dq = proj("dqkv")
    cw = cw_ref[0]
    if decode:
        cs = cs_ref[0]
        nt = tm // batch
        blocks = [dq[t * batch:(t + 1) * batch] for t in range(nt)]
        convs = []
        for t in range(nt):
            acc = blocks[t] * cw[CONV_W - 1:CONV_W, :]
            for s in range(1, CONV_W):
                prev = blocks[t - s] if t >= s else cs[CONV_W - 1 - s + t]
                acc = acc + prev * cw[CONV_W - 1 - s:CONV_W - s, :]
            convs.append(acc)
        conv = jnp.concatenate(convs, axis=0)
        convo_ref[...] = dq[(nt - (CONV_W - 1)) * batch:].reshape(CONV_W - 1, batch, dq.shape[1])
    else:
        i = pl.program_id(0)
        per = g_t // tm

        halo = V7X_SUBLANES

        @pl.when(i % per == 0)
        def _():
            cbuf[0:halo, :] = jnp.zeros((halo, dq.shape[1]), F32)

        cbuf[halo:halo + tm, :] = dq
        conv = cbuf[pl.ds(halo, tm), :] * cw[CONV_W - 1:CONV_W, :]
        for s in range(1, CONV_W):
            conv = conv + cbuf[pl.ds(halo - s, tm), :] * cw[CONV_W - 1 - s:CONV_W - s, :]
        tail = cbuf[tm:tm + halo, :]
        cbuf[0:halo, :] = tail

        @pl.when(i % per == per - 1)
        def _():
            conv_ref[0] = tail[halo - (CONV_W - 1):halo, :]

    u = _silu(conv)
    uq, uk = u[:, 0:HW], u[:, HW:2 * HW]
    put("gdn_q", uq * lax.rsqrt(_dot(_bf(uq * uq), bd) + RMS_EPS) * (HEAD_DIM ** -0.5))
    put("gdn_k", uk * lax.rsqrt(_dot(_bf(uk * uk), bd) + RMS_EPS))
    put("gdn_v", u[:, 2 * HW:3 * HW])
    put("gdn_gate", _silu(proj("dg")))
    if decode:
        for t in range(tm // batch):
            smt = small[t * batch:(t + 1) * batch].T
            beta = jax.nn.sigmoid(smt[SMALL_DB:SMALL_DB + HEADS])
            gd = -jnp.exp(alogc_ref[layer]) * _softplus(smt[SMALL_DA:SMALL_DA + HEADS] + dtbc_ref[layer])
            for h in range(HEADS):
                bgo_ref[h, t, 0:1, :] = beta[h:h + 1]
                bgo_ref[h, t, 1:2, :] = gd[h:h + 1]
    else:
        dbda = _dot_x01(small, ex_ref[...])
        put("gdn_b", jax.nn.sigmoid(dbda[:, 0:HW]))
        put("gdn_g", -jnp.exp(alog_ref[layer:layer + 1, :]) * _softplus(dbda[:, HW:2 * HW] + dtb_ref[layer:layer + 1, :]))

    rows = [lb_ref[i:i + 1, :] for i in range(nlayers)]
    mx = functools.reduce(jnp.maximum, rows)
    ex = [jnp.exp(r - mx) for r in rows]
    tot = functools.reduce(lambda a, b: a + b, ex)
    plb = [e / tot for e in ex]
    lb = functools.reduce(lambda a, b: a + b, plb[:layer + 1]) - plb[0]
    zf = proj("hf")
    put("hg_q", _silu(proj("hq")) * (HEAD_DIM ** -0.5))
    put("hg_k", (1.0 - lb) * jax.nn.sigmoid(-zf))
    put("hg_g", jnp.log(lb + (1.0 - lb) * jax.nn.sigmoid(zf)))
    put("hg_v", proj("hi"))
    put("hg_gate", _silu(proj("hg")))

    gpre = _dot(_bf(small), wg_ref[0]) + bg_ref[layer:layer + 1, :]
    put("gla_q", proj("aq") * (GLA_DK ** -0.5))
    put("gla_k", proj("ak"))
    put("gla_g", -_softplus(-gpre) * (1.0 / GLA_GATE_NORM))
    put("gla_v", proj("av"))
    put("gla_gate", _silu(proj("ag")))

    cos, sin = cos_ref[...], sin_ref[...]
    lane = lax.broadcasted_iota(jnp.int32, (tm, HW), 1)
    first_half = (lane % HEAD_DIM) < (HEAD_DIM // 2)

    def rope(v):
        sw = jnp.where(first_half, pltpu.roll(v, HW - HEAD_DIM // 2, 1), pltpu.roll(v, HEAD_DIM // 2, 1))
        return v * cos + sw * sin

    put("ret_q", rope(proj("rq")))
    put("ret_k", rope(proj("rk")) * (HEAD_DIM ** -0.5))
    put("ret_v", proj("rv"))
    put("ret_gate", _silu(proj("rg")))


def _prep(g, x, mod, w_in_p, cos_t, sin_t, wg_p, gla_bg, hg_lb, gdn_conv, alog_e, dtb_e, dec, layer, nlayers):
    d = x.shape[1]
    bd = jnp.asarray(_bd_ones(), BF16)
    exm = jnp.asarray(_expand_mat(), BF16)
    row = pl.BlockSpec((g.tm, d), lambda i: (i, 0))
    if g.decode:
        tab = pl.BlockSpec((g.tm, HW), lambda i: (i, 0))
    else:
        per = g.t // g.tm
        tab = pl.BlockSpec((g.tm, HW), lambda i: (i % per, 0))
    in_specs = [row, _mod_spec(g, layer, 3, d), _mod_spec(g, layer, 4, d),
                pl.BlockSpec((1, d, PROJ_W), lambda i: (layer, 0, 0), pipeline_mode=pl.Buffered(1)),
                tab, tab,
                pl.BlockSpec((1,) + wg_p.shape[1:], lambda i: (layer, 0, 0)),
                _const_spec(gla_bg), _const_spec(hg_lb),
                pl.BlockSpec((1,) + gdn_conv.shape[1:], lambda i: (layer, 0, 0)),
                _const_spec(alog_e), _const_spec(dtb_e), _const_spec(bd), _const_spec(exm)]
    args = [x, mod, mod, w_in_p, cos_t, sin_t, wg_p, gla_bg, hg_lb, gdn_conv, alog_e, dtb_e, bd, exm]
    cch = gdn_conv.shape[2]

    def rows_out(w):
        return jax.ShapeDtypeStruct((g.rows, w), F32), pl.BlockSpec((g.tm, w), lambda i: (i, 0))

    def whole(shape):
        return jax.ShapeDtypeStruct(shape, F32), pl.BlockSpec(shape, lambda i: (0,) * len(shape))

    if g.decode:
        assert g.steps == 1 and g.t >= CONV_W - 1
        cs, alog_c, dtb_c = dec
        in_specs += [pl.BlockSpec((1,) + cs.shape[1:], lambda i: (layer, 0, 0, 0)), _const_spec(alog_c),
                     _const_spec(dtb_c)]
        args += [cs, alog_c, dtb_c]
        outs = []
        for dhs in ((HEAD_DIM,) * 4, (GLA_DK,) * 3 + (HEAD_DIM,) * 2, (HEAD_DIM,) * 5, (HEAD_DIM,) * 4):
            outs += [whole((HEADS, g.t, dh, g.batch)) for dh in dhs]
        outs += [whole((HEADS, g.t, 2, g.batch)), whole((CONV_W - 1, g.batch, cch))]
        scratch = []
    else:
        outs = [rows_out(4 * HW), rows_out(3 * HEADS * GLA_DK + 2 * HW), rows_out(5 * HW), rows_out(6 * HW),
                (jax.ShapeDtypeStruct((g.batch, CONV_W - 1, cch), F32),
                 pl.BlockSpec((1, CONV_W - 1, cch), lambda i: (i // (g.t // g.tm), 0, 0)))]
        scratch = [pltpu.VMEM((g.tm + V7X_SUBLANES, cch), F32)]
    return pl.pallas_call(
        functools.partial(_prep_kernel, layer=layer, nlayers=nlayers, g_t=g.t, tm=g.tm, decode=g.decode,
                          batch=g.batch),
        grid=(g.steps,),
        in_specs=in_specs,
        out_specs=[o[1] for o in outs],
        out_shape=[o[0] for o in outs],
        scratch_shapes=scratch,
        compiler_params=_params(("arbitrary",)),
        name="prep",
    )(*args)


def _chunk(x, c):
    return x[c * CHUNK:(c + 1) * CHUNK]


def _per_chunk(fn, nch):
    return jnp.concatenate([fn(c) for c in range(nch)], axis=0)


def _finish_prompt(o, gate, normw, bd, o_ref):
    ms = _dot(_bf(o * o), bd) * (1.0 / HEAD_DIM)
    on = o * lax.rsqrt(ms + RMS_EPS)
    if normw is not None:
        on = on * normw
    o_ref[...] = _bf(on * gate)


def _state_out(s_t, fold_ref, st_ref):
    fold = fold_ref[...]
    h1, h2, h3 = _split3(s_t)
    st_ref[0] = _dot_tn(h1, fold) + _dot_tn(h2, fold) + _dot_tn(h3, fold)


def _gla_prompt_kernel(f_ref, kmask_ref, vmask_ref, smask_ref, lvl_ref, fold_ref, bd_ref, nw_ref, ratt_ref, rq_ref,
                       rk_ref, rs_ref, o_ref, st_ref, s_scr, *, dk, nch, ret, levels, has_norm, layer):
    j = pl.program_id(1)
    dkk = HEADS * dk

    @pl.when(j == 0)
    def _():
        s_scr[...] = jnp.zeros(s_scr.shape, F32)

    kmask, vmask, smask, bd = kmask_ref[...], vmask_ref[...], smask_ref[...], bd_ref[...]
    normw = nw_ref[layer:layer + 1, :] if has_norm else None

    def mm(x, y):
        xb, yb = _bf(x), _bf(y)
        return _per_chunk(lambda c: _dot_nt(_chunk(xb, c), _tile4(_chunk(yb, c)) * kmask), nch)

    q = f_ref[:, 0:dkk]
    k = f_ref[:, dkk:2 * dkk]
    if ret:
        v = f_ref[:, 2 * dkk:2 * dkk + HW]
        gate = f_ref[:, 2 * dkk + HW:2 * dkk + 2 * HW]
        att = _per_block(mm(q, k), ratt_ref[...], CHUNK)
        qg = _per_block(q, rq_ref[...], CHUNK)
        kd = _per_block(k, rk_ref[...], CHUNK)
        sdec = [rs_ref[...]] * nch
    else:
        g = f_ref[:, 2 * dkk:3 * dkk]
        v = f_ref[:, 3 * dkk:3 * dkk + HW]
        gate = f_ref[:, 3 * dkk + HW:3 * dkk + 2 * HW]
        G = _cumsum_rows(g, CHUNK)
        att = _decay_att(q, k, G, mm, lambda x, y: _dot_nt(_bf(x * y), kmask), lvl_ref, levels, CHUNK)
        qg = q * jnp.exp(G)
        glb = _block_row(G, CHUNK, CHUNK - 1)
        kd = k * jnp.exp(glb - G)
        sdec = [jnp.exp(G[(c + 1) * CHUNK - 1:(c + 1) * CHUNK, :]) for c in range(nch)]
    vb, qgb, kdb, attb = _bf(v), _bf(qg), _bf(kd), _bf(att)
    o_intra = _per_chunk(lambda c: _dot(_chunk(attb, c), _tile4(_chunk(vb, c)) * vmask), nch)
    upd = [_dot_tn(_chunk(vb, c), _chunk(kdb, c)) * smask for c in range(nch)]

    s = s_scr[...]
    o_inter = []
    for c in range(nch):
        o_inter.append(_dot_nt(_chunk(qgb, c), _bf(s)))
        s = s * sdec[c] + upd[c]
    s_scr[...] = s
    _finish_prompt(o_intra + jnp.concatenate(o_inter, axis=0), gate, normw, bd, o_ref)

    @pl.when(j == pl.num_programs(1) - 1)
    def _():
        _state_out(s, fold_ref, st_ref)


def _gla_prompt(g, feat, dk, normw, layer, ret):
    c = _chunk_consts(dk)
    dkk = HEADS * dk
    blk = min(MIX_TILE, g.t)
    nch = blk // CHUNK
    per = g.t // blk
    consts = [jnp.asarray(c["kmask"], BF16), jnp.asarray(c["vmask"], BF16), jnp.asarray(c["smask"]),
              jnp.asarray(c["lvl"]), jnp.asarray(c["fold"], BF16), jnp.asarray(_bd_ones(), BF16),
              normw if normw is not None else jnp.ones((1, HW), F32),
              jnp.asarray(c["ret_att"]), jnp.asarray(c["ret_q"]), jnp.asarray(c["ret_k"]), jnp.asarray(c["ret_s"])]
    w = feat.shape[1]
    o, st = pl.pallas_call(
        functools.partial(_gla_prompt_kernel, dk=dk, nch=nch, ret=ret, levels=c["levels"],
                          has_norm=normw is not None, layer=layer),
        grid=(g.batch, per),
        in_specs=[pl.BlockSpec((blk, w), lambda b, j: (b * per + j, 0))] + [_const_spec(a) for a in consts],
        out_specs=[pl.BlockSpec((blk, HW), lambda b, j: (b * per + j, 0)),
                   pl.BlockSpec((1, dkk, HEAD_DIM), lambda b, j: (b, 0, 0))],
        out_shape=[jax.ShapeDtypeStruct((g.rows, HW), BF16),
                   jax.ShapeDtypeStruct((g.batch, dkk, HEAD_DIM), F32)],
        scratch_shapes=[pltpu.VMEM((HW, dkk), F32)],
        compiler_params=_params(("arbitrary", "arbitrary")),
        name="ret_prompt" if ret else f"gla_prompt_dk{dk}",
    )(feat, *consts)
    return o, st.reshape(g.batch, HEADS, dk, HEAD_DIM)


def _gdn_prompt_kernel(f_ref, bdm_ref, smask_ref, incl_ref, strict_ref, eye_ref, fold_ref, bd_ref, nw_ref, o_ref,
                       st_ref, s_scr, *, nch, layer):
    j = pl.program_id(1)

    @pl.when(j == 0)
    def _():
        s_scr[...] = jnp.zeros(s_scr.shape, F32)

    bdm, smask, incl, strict, eye, bd = (bdm_ref[...], smask_ref[...], incl_ref[...], strict_ref[...], eye_ref[...],
                                         bd_ref[...])
    normw = nw_ref[layer:layer + 1, :]
    ones_c = jnp.ones((CHUNK, CHUNK), BF16)

    def bdw(yb, c):
        return _tile4(_chunk(yb, c)) * bdm

    def prod1(x, y):
        xb, yb = _bf(x), _bf(y)
        return _per_chunk(lambda c: _dot(_chunk(xb, c), bdw(yb, c)), nch)

    def prod1_pair(x1, x2, y):
        x1b, x2b, yb = _bf(x1), _bf(x2), _bf(y)
        both = [_dot(jnp.concatenate([_chunk(x1b, c), _chunk(x2b, c)], axis=0), bdw(yb, c)) for c in range(nch)]
        return (jnp.concatenate([m[:CHUNK] for m in both], axis=0),
                jnp.concatenate([m[CHUNK:] for m in both], axis=0))

    def prod2(x, y):
        xh, xl = _split2(x)
        yb = _bf(y)

        def one(c):
            top = _dot(jnp.concatenate([_chunk(xh, c), _chunk(xl, c)], axis=0), bdw(yb, c))
            return top[:CHUNK] + top[CHUNK:]

        return _per_chunk(one, nch)

    def prod3(x, y):
        xh, xl = _split2(x)
        yh, yl = _split2(y)

        def one(c):
            top = _dot(jnp.concatenate([_chunk(xh, c), _chunk(xl, c)], axis=0), bdw(yh, c))
            return top[:CHUNK] + top[CHUNK:] + _dot(_chunk(xh, c), bdw(yl, c))

        return _per_chunk(one, nch)

    q, k = f_ref[:, 0:HW], f_ref[:, HW:2 * HW]
    be, ge = f_ref[:, 2 * HW:3 * HW], f_ref[:, 3 * HW:4 * HW]
    v, gate = f_ref[:, 4 * HW:5 * HW], f_ref[:, 5 * HW:6 * HW]
    G = _cumsum_rows(ge, CHUNK)
    zd = _per_block(G, eye, CHUNK)
    grow = _per_chunk(lambda c: _dot_exact01(ones_c, _chunk(zd, c)), nch)
    L = _per_block(jnp.exp(jnp.minimum(G - grow, 0.0)), incl, CHUNK)
    qb, kb = _bf(q), _bf(k)

    def qkk(c):
        return _dot_nt(jnp.concatenate([_chunk(qb, c), _chunk(kb, c)], axis=0), bdw(kb, c))

    qkk_all = [qkk(c) for c in range(nch)]
    qk = jnp.concatenate([m[:CHUNK] for m in qkk_all], axis=0)
    kk = jnp.concatenate([m[CHUNK:] for m in qkk_all], axis=0)
    a = _per_block(be * kk * L, strict, CHUNK)
    eye_b = _per_block(jnp.ones_like(a), eye, CHUNK)
    p = -a
    tinv = eye_b + p
    for i in range(5):
        if i == 0:
            p = prod1(p, p)
        else:
            p, dt = prod1_pair(p, tinv, p)
            tinv = tinv + dt
    tinv = tinv + prod1(tinv, p)
    resid = eye_b - tinv - prod3(a, tinv)
    tinv = tinv + prod1(tinv, resid)
    eg = jnp.exp(G)
    u = prod2(tinv, be * v)
    w = prod2(tinv, be * eg * k)
    glb = _block_row(G, CHUNK, CHUNK - 1)
    wb, qeb, qklb = _bf(w), _bf(q * eg), _bf(qk * L)
    kdb = _bf(k * jnp.exp(glb - G))
    sdec = [jnp.exp(G[(c + 1) * CHUNK - 1:(c + 1) * CHUNK, :]) for c in range(nch)]

    ub = _bf(u)
    s_add = [_dot_tn(_chunk(ub, c), _chunk(kdb, c)) * smask for c in range(nch)]
    s_mul = [_bf(_dot_tn(_chunk(wb, c), _chunk(kdb, c)) * smask) for c in range(nch)]

    s = s_scr[...]
    snaps = []
    for c in range(nch):
        sb = _bf(s)
        snaps.append(sb)
        s = s * sdec[c] + s_add[c] - _dot(sb, s_mul[c])
    s_scr[...] = s

    outs = []
    for c in range(nch):
        ws = _dot_nt(jnp.concatenate([_chunk(wb, c), _chunk(qeb, c)], axis=0), snaps[c])
        db = _bf(_chunk(u, c) - ws[:CHUNK])
        outs.append(_dot(_chunk(qklb, c), _tile4(db) * bdm) + ws[CHUNK:])
    _finish_prompt(jnp.concatenate(outs, axis=0), gate, normw, bd, o_ref)

    @pl.when(j == pl.num_programs(1) - 1)
    def _():
        _state_out(s, fold_ref, st_ref)


def _gdn_prompt(g, feat, normw, layer):
    c = _chunk_consts(HEAD_DIM)
    blk = min(MIX_TILE, g.t)
    nch = blk // CHUNK
    per = g.t // blk
    consts = [jnp.asarray(c["vmask"], BF16), jnp.asarray(c["smask"]), jnp.asarray(c["incl"]), jnp.asarray(c["strict"]),
              jnp.asarray(c["eye"]), jnp.asarray(c["fold"], BF16), jnp.asarray(_bd_ones(), BF16), normw]
    w = feat.shape[1]
    o, st = pl.pallas_call(
        functools.partial(_gdn_prompt_kernel, nch=nch, layer=layer),
        grid=(g.batch, per),
        in_specs=[pl.BlockSpec((blk, w), lambda b, j: (b * per + j, 0))] + [_const_spec(a) for a in consts],
        out_specs=[pl.BlockSpec((blk, HW), lambda b, j: (b * per + j, 0)),
                   pl.BlockSpec((1, HW, HEAD_DIM), lambda b, j: (b, 0, 0))],
        out_shape=[jax.ShapeDtypeStruct((g.rows, HW), BF16),
                   jax.ShapeDtypeStruct((g.batch, HW, HEAD_DIM), F32)],
        scratch_shapes=[pltpu.VMEM((HW, HW), F32)],
        compiler_params=_params(("arbitrary", "arbitrary")),
        name="gdn_prompt",
    )(feat, *consts)
    return o, st.reshape(g.batch, HEADS, HEAD_DIM, HEAD_DIM)


def _state_io(refs, layer):
    if layer == 0:
        return refs
    prev_ref, rest = refs[0], refs[1:]
    rest[-1][0:layer] = prev_ref[...]
    return rest


def _dec_finish(o, gate, normw):
    ms = jnp.mean(o * o, axis=0, keepdims=True)
    on = o * lax.rsqrt(ms + RMS_EPS)
    if normw is not None:
        on = on * normw
    return on * gate


def _gla_decode_kernel(q_ref, k_ref, g_ref, v_ref, gate_ref, s_ref, *refs, nt, dk, ret, has_norm, layer):
    nw_ref, o_ref, so_ref = _state_io(refs, layer)
    normw = nw_ref[layer] if has_norm else None
    qs = [q_ref[0, t] for t in range(nt)]
    ks = [k_ref[0, t] for t in range(nt)]
    vs = [v_ref[0, t] for t in range(nt)]
    decay = [jnp.exp(g_ref[0, t]) for t in range(nt)]
    o_acc = [jnp.zeros(vs[0].shape, F32) for _ in range(nt)]
    for dc in range(dk // DEC_DCHUNK):
        r = slice(dc * DEC_DCHUNK, (dc + 1) * DEC_DCHUNK)
        sc = s_ref[0, 0, r]
        for t in range(nt):
            a = decay[t][None] if ret else decay[t][r][:, None, :]
            sc = sc * a + ks[t][r][:, None, :] * vs[t][None, :, :]
            o_acc[t] = o_acc[t] + jnp.sum(qs[t][r][:, None, :] * sc, axis=0)
        so_ref[layer, 0, r] = sc
    for t in range(nt):
        o_ref[0, t] = _dec_finish(o_acc[t], gate_ref[0, t], normw)


def _gdn_decode_kernel(q_ref, k_ref, v_ref, gate_ref, bg_ref, s_ref, *refs, nt, layer):
    nw_ref, o_ref, so_ref = _state_io(refs, layer)
    normw = nw_ref[layer]
    chunks = [slice(dc * DEC_DCHUNK, (dc + 1) * DEC_DCHUNK) for dc in range(HEAD_DIM // DEC_DCHUNK)]
    for t in range(nt):
        q, k, v = q_ref[0, t], k_ref[0, t], v_ref[0, t]
        beta = bg_ref[0, t, 0:1, :]
        ag = jnp.exp(bg_ref[0, t, 1:2, :])

        def state(r):
            return s_ref[0, 0, r] if t == 0 else so_ref[layer, 0, r]

        ks = jnp.zeros(v.shape, F32)
        for r in chunks:
            ks = ks + jnp.sum(k[r][:, None, :] * state(r), axis=0)
        delta = beta * (v - ag * ks)
        o = jnp.zeros(v.shape, F32)
        for r in chunks:
            sc = state(r) * ag[None] + k[r][:, None, :] * delta[None, :, :]
            so_ref[layer, 0, r] = sc
            o = o + jnp.sum(q[r][:, None, :] * sc, axis=0)
        o_ref[0, t] = _dec_finish(o, gate_ref[0, t], normw)


def _decode_call(body, name, feats, state_t, prev, extra, dk, layer):
    nt, batch = feats[0].shape[1], feats[0].shape[3]

    def head_spec(a):
        return pl.BlockSpec((1,) + a.shape[1:], lambda h: (h,) + (0,) * (a.ndim - 1))

    def st_spec(nl):
        return pl.BlockSpec((nl, 1, dk, HEAD_DIM, batch), lambda h: (0, h, 0, 0, 0))

    in_specs = [head_spec(a) for a in feats]
    in_specs.append(pl.BlockSpec((1, 1, dk, HEAD_DIM, batch), lambda h: (layer, h, 0, 0, 0)))
    args = list(feats) + [state_t]
    if layer > 0:
        in_specs.append(st_spec(layer))
        args.append(prev)
    o_shape = (HEADS, nt, HEAD_DIM, batch)
    return pl.pallas_call(
        body,
        grid=(HEADS,),
        in_specs=in_specs + [_const_spec(a) for a in extra],
        out_specs=[pl.BlockSpec((1,) + o_shape[1:], lambda h: (h, 0, 0, 0)), st_spec(layer + 1)],
        out_shape=[jax.ShapeDtypeStruct(o_shape, F32),
                   jax.ShapeDtypeStruct((layer + 1, HEADS, dk, HEAD_DIM, batch), F32)],
        compiler_params=_params(("arbitrary",)),
        name=name,
    )(*args, *extra)


def _gla_decode(q, k, g, v, gate, state_t, prev, dk, normw_col, layer, ret):
    nt = q.shape[1]
    body = functools.partial(_gla_decode_kernel, nt=nt, dk=dk, ret=ret, has_norm=normw_col is not None, layer=layer)
    nw = normw_col if normw_col is not None else jnp.ones((layer + 1, HEAD_DIM, 1), F32)
    return _decode_call(body, "ret_decode" if ret else f"gla_decode_dk{dk}", [q, k, g, v, gate], state_t, prev, [nw],
                        dk, layer)


def _gdn_decode(q, k, v, gate, bg, state_t, prev, normw_col, layer):
    body = functools.partial(_gdn_decode_kernel, nt=q.shape[1], layer=layer)
    return _decode_call(body, "gdn_decode", [q, k, v, gate, bg], state_t, prev, [normw_col], HEAD_DIM, layer)


def _rope_tables(pos):
    half = HEAD_DIM // 2
    inv = ROPE_BASE ** (-jnp.arange(half, dtype=F32) / half)
    ang = pos[:, None] * inv[None, :]
    cos, sin = jnp.cos(ang), jnp.sin(ang)
    cos_t = jnp.tile(jnp.concatenate([cos, cos], axis=1), (1, HEADS))
    sin_t = jnp.tile(jnp.concatenate([-sin, sin], axis=1), (1, HEADS))
    return cos_t, sin_t


def _relayout_w_in(w_in):
    def cols(name):
        o, w = _SRC[name]
        return w_in[:, :, o:o + w]

    parts = []
    for name, w in _SLABS:
        if name == "small":
            sm = jnp.concatenate([cols(n) for n in _NARROW], axis=2)
            parts.append(jnp.pad(sm, ((0, 0), (0, 0), (0, w - sm.shape[2]))))
        else:
            parts.append(cols(name))
    return jnp.concatenate(parts, axis=2).astype(BF16)


def kernel(x_prompt, x_sample, state_ret, state_gla, state_hgrn, state_gdn, state_gdn_conv, c_prompt, c_sample, ada_w, ada_b, ln_g, ln_b, ffn1_wi, ffn1_wo, ffn2_wi, ffn2_wo, w_in, gla_wg, gla_bg, hg_lb, gdn_conv, gdn_a_log, gdn_dt_bias, gla_norm, hg_norm, gdn_norm, w_out):
    nl, d = ada_w.shape[0], ada_w.shape[1]
    bp, tp = x_prompt.shape[0], x_prompt.shape[1]
    bs, ts = x_sample.shape[0], x_sample.shape[1]
    alpha = (2.0 * nl) ** 0.25
    assert tp % CHUNK == 0 and ts >= CONV_W - 1

    wi1, wo1, wi2, wo2 = (w.astype(BF16) for w in (ffn1_wi, ffn1_wo, ffn2_wi, ffn2_wo))
    w_out_b = w_out.astype(BF16)
    w_in_p = _relayout_w_in(w_in)
    wg_p = jnp.pad(gla_wg, ((0, 0), (0, _OFF["small"][1] - gla_wg.shape[1]), (0, 0))).astype(BF16)
    alog_e = jnp.repeat(gdn_a_log, HEAD_DIM, axis=1)
    dtb_e = jnp.repeat(gdn_dt_bias, HEAD_DIM, axis=1)
    gla_nw = jnp.tile(gla_norm, (1, HEADS))
    hg_nw = jnp.tile(hg_norm, (1, HEADS))
    gdn_nw = jnp.tile(gdn_norm, (1, HEADS))

    mod = _ada(jnp.concatenate([c_prompt, c_sample], axis=0), ada_w, ada_b)
    mod = mod.reshape(nl, bp + bs, N_MOD, d).transpose(0, 2, 1, 3)
    mod_p = mod[:, :, :bp].reshape(nl, N_MOD, bp, 1, d)
    mod_s = mod[:, :, bp:]

    gp = _Rows(bp, tp, decode=False)
    gp_prep = _Rows(bp, tp, decode=False, tile=PREP_TILE)
    gs = _Rows(bs, ts, decode=True)
    assert gs.steps == 1
    cos_p, sin_p = _rope_tables(jnp.arange(tp, dtype=F32))
    cos_s, sin_s = _rope_tables(PAST_LEN + jnp.arange(ts, dtype=F32))
    cos_s, sin_s = jnp.repeat(cos_s, bs, axis=0), jnp.repeat(sin_s, bs, axis=0)

    xp = x_prompt.reshape(bp * tp, d)
    xs = jnp.transpose(x_sample, (1, 0, 2)).reshape(ts * bs, d)

    st_ret, st_gla, st_hg, st_gdn = (jnp.transpose(s, (0, 2, 3, 4, 1))
                                     for s in (state_ret, state_gla, state_hgrn, state_gdn))
    conv_t = jnp.transpose(state_gdn_conv, (0, 2, 1, 3))
    dec_extra = (conv_t, gdn_a_log[:, :, None], gdn_dt_bias[:, :, None])
    ret_g = jnp.asarray(np.broadcast_to(_ret_gamma().astype(np.float32)[:, None, None, None], (HEADS, ts, 1, bs)))
    gla_nc, hg_nc, gdn_nc = gla_norm[:, :, None], hg_norm[:, :, None], gdn_norm[:, :, None]

    p_states, s_convs = [], []
    s_ret = s_gla = s_hg = s_gdn = None
    for l in range(nl):
        xp = _ffn(gp, xp, mod_p, wi1, wo1, ln_g, ln_b, l, alpha)
        ret_f, gla_f, hg_f, gdn_f, p_conv = _prep(gp_prep, xp, mod_p, w_in_p, cos_p, sin_p, wg_p, gla_bg, hg_lb,
                                                  gdn_conv, alog_e, dtb_e, None, l, nl)
        o_ret, p_ret = _gla_prompt(gp, ret_f, HEAD_DIM, None, l, ret=True)
        o_gla, p_gla = _gla_prompt(gp, gla_f, GLA_DK, gla_nw, l, ret=False)
        o_hg, p_hg = _gla_prompt(gp, hg_f, HEAD_DIM, hg_nw, l, ret=False)
        o_gdn, p_gdn = _gdn_prompt(gp, gdn_f, gdn_nw, l)
        xp = _mix_ffn(gp, xp, mod_p, (o_ret, o_gla, o_hg, o_gdn), w_out_b, wi2, wo2, ln_g, ln_b, l, alpha)
        p_states.append((p_ret, p_gla, p_hg, p_gdn, p_conv))

        xs = _ffn(gs, xs, mod_s, wi1, wo1, ln_g, ln_b, l, alpha)
        (rq, rk, rv, rgt, aq, ak, ag, av, agt, hq, hk, hgg, hv, hgt, dq, dk_, dv, dgt, dbg, conv_new) = _prep(
            gs, xs, mod_s, w_in_p, cos_s, sin_s, wg_p, gla_bg, hg_lb, gdn_conv, alog_e, dtb_e, dec_extra, l, nl)
        o_ret, s_ret = _gla_decode(rq, rk, ret_g, rv, rgt, st_ret, s_ret, HEAD_DIM, None, l, True)
        o_gla, s_gla = _gla_decode(aq, ak, ag, av, agt, st_gla, s_gla, GLA_DK, gla_nc, l, False)
        o_hg, s_hg = _gla_decode(hq, hk, hgg, hv, hgt, st_hg, s_hg, HEAD_DIM, hg_nc, l, False)
        o_gdn, s_gdn = _gdn_decode(dq, dk_, dv, dgt, dbg, st_gdn, s_gdn, gdn_nc, l)
        xs = _mix_ffn(gs, xs, mod_s, (o_ret, o_gla, o_hg, o_gdn), w_out_b, wi2, wo2, ln_g, ln_b, l, alpha)
        s_convs.append(conv_new)

    y_prompt = xp.reshape(bp, tp, d)
    y_sample = jnp.transpose(xs.reshape(ts, bs, d), (1, 0, 2))
    p_out = tuple(jnp.stack([st[i] for st in p_states]) for i in range(5))
    s_out = tuple(jnp.transpose(s, (0, 4, 1, 2, 3)) for s in (s_ret, s_gla, s_hg, s_gdn))
    return (y_prompt, y_sample) + p_out + s_out + (jnp.transpose(jnp.stack(s_convs), (0, 2, 1, 3)),)
```

```python
import functools

import numpy as np
import jax
import jax.numpy as jnp
from jax import lax
from jax.experimental import pallas as pl
from jax.experimental.pallas import tpu as pltpu

F32 = jnp.float32
BF16 = jnp.bfloat16

HEAD_DIM = 64
HEADS = 4
GLA_DK = 32
GLA_GATE_RANK = 16
GLA_GATE_NORM = 16.0
CONV_W = 4
CHUNK = 64
ROPE_BASE = 10000.0
LN_EPS = 1e-5
RMS_EPS = 1e-6
PAST_LEN = 16384
N_MOD = 9
HW = HEADS * HEAD_DIM

V7X_LANES = 128
V7X_SUBLANES = 8
V7X_VMEM_BYTES = 64 * 1024 * 1024

ROW_TILE = 512
PREP_TILE = 512
MIX_TILE = 2048
DEC_DCHUNK = V7X_SUBLANES
FF_CHUNK = 256
VMEM_LIMIT = V7X_VMEM_BYTES - 12 * 1024 * 1024

_MIX_COLS = (("rq", HW), ("rk", HW), ("rv", HW), ("rg", HW),
             ("aq", HEADS * GLA_DK), ("ak", HEADS * GLA_DK), ("av", HW), ("alr", GLA_GATE_RANK), ("ag", HW),
             ("hq", HW), ("hf", HW), ("hi", HW), ("hg", HW),
             ("dqkv", 3 * HW), ("db", HEADS), ("da", HEADS), ("dg", HW))
_SRC = {}
_o = 0
for _n, _w in _MIX_COLS:
    _SRC[_n] = (_o, _w)
    _o += _w
_NARROW = ("alr", "db", "da")
_SLABS = tuple((n, w) for n, w in _MIX_COLS if n not in _NARROW) + (("small", V7X_LANES),)
_OFF = {}
_o = 0
for _n, _w in _SLABS:
    assert _o % V7X_LANES == 0
    _OFF[_n] = (_o, _w)
    _o += _w
PROJ_W = _o
SMALL_ALR, SMALL_DB, SMALL_DA = 0, GLA_GATE_RANK, GLA_GATE_RANK + HEADS


def _bf(x):
    return x.astype(BF16)


def _dot(a, b):
    return jnp.dot(a, b, preferred_element_type=F32)


def _dot_nt(a, b):
    return lax.dot_general(a, b, (((1,), (1,)), ((), ())), preferred_element_type=F32)


def _dot_tn(a, b):
    return lax.dot_general(a, b, (((0,), (0,)), ((), ())), preferred_element_type=F32)


def _split2(x):
    hi = x.astype(BF16)
    lo = (x - hi.astype(F32)).astype(BF16)
    return hi, lo


def _split3(x):
    hi = x.astype(BF16)
    r = x - hi.astype(F32)
    mid = r.astype(BF16)
    lo = (r - mid.astype(F32)).astype(BF16)
    return hi, mid, lo


def _dot_x01(x, m01):
    hi, lo = _split2(x)
    return _dot(hi, m01) + _dot(lo, m01)


def _dot_exact01(m01, x):
    h1, h2, h3 = _split3(x)
    return _dot(m01, h1) + _dot(m01, h2) + _dot(m01, h3)


def _silu(x):
    return x * jax.nn.sigmoid(x)


def _softplus(x):
    return jnp.maximum(x, 0.0) + jnp.log(1.0 + jnp.exp(-jnp.abs(x)))


def _ln(y, g, b):
    mu = jnp.mean(y, axis=-1, keepdims=True)
    yc = y - mu
    var = jnp.mean(yc * yc, axis=-1, keepdims=True)
    return yc * lax.rsqrt(var + LN_EPS) * g + b


def _mod_rows(ref, tm):
    v = ref[...]
    v = v.reshape(v.shape[-2], v.shape[-1])
    nb = v.shape[0]
    if nb == 1:
        return v
    return jnp.concatenate([v] * (tm // nb), axis=0)


def _tile4(x):
    return jnp.concatenate([x, x, x, x], axis=0)


def _per_block(x, const, period):
    n = x.shape[0] // period
    return (x.reshape(n, period, x.shape[1]) * const[None]).reshape(x.shape)


def _cumsum_rows(g, period):
    rows, w = g.shape
    tile = min(period, V7X_SUBLANES)
    row = lax.broadcasted_iota(jnp.int32, g.shape, 0) % tile
    out = g
    sh = 1
    while sh < tile:
        out = out + jnp.where(row >= sh, pltpu.roll(out, sh, 0), 0.0)
        sh *= 2
    if period == tile:
        return out
    nper = period // tile
    o4 = out.reshape(rows // period, nper, tile, w)
    tot = o4[:, :, tile - 1:tile, :]
    offs = [jnp.zeros_like(tot[:, 0:1])]
    for i in range(1, nper):
        offs.append(offs[-1] + tot[:, i - 1:i])
    return (o4 + jnp.concatenate(offs, axis=1)).reshape(rows, w)


def _block_row(G, period, r):
    c, w = G.shape
    g3 = G.reshape(c // period, period, w)
    return jnp.broadcast_to(g3[:, r:r + 1, :], g3.shape).reshape(c, w)


def _gref(G, m):
    c = G.shape[0]
    if 2 * m >= V7X_SUBLANES:
        return _block_row(G, 2 * m, m - 1)
    p = lax.broadcasted_iota(jnp.int32, G.shape, 0) % (2 * m)
    if m == 2:
        return jnp.where(p == 0, pltpu.roll(G, c - 1, 0),
                         jnp.where(p == 1, G, jnp.where(p == 2, pltpu.roll(G, 1, 0), pltpu.roll(G, 2, 0))))
    return jnp.where(p == 0, G, pltpu.roll(G, 1, 0))


def _decay_att(q, k, G, mm, diag, lvl_ref, levels, period):
    att = _per_block(diag(q, k), lvl_ref[len(levels)], period)
    for i, m in enumerate(levels):
        if m == 1:
            ql = q * jnp.exp(jnp.minimum(G - pltpu.roll(G, 1, 0), 0.0))
            att = att + _per_block(diag(ql, pltpu.roll(k, 1, 0)), lvl_ref[i], period)
            continue
        gr = _gref(G, m)
        ql = q * jnp.exp(jnp.minimum(G - gr, 0.0))
        kl = k * jnp.exp(jnp.minimum(gr - G, 0.0))
        att = att + _per_block(mm(ql, kl), lvl_ref[i], period)
    return att


def _ret_gamma():
    return np.log(1.0 - 2.0 ** (-5.0 - np.arange(HEADS, dtype=np.float64)))


@functools.lru_cache(maxsize=None)
def _chunk_consts(dk):
    c = CHUNK
    h_att = np.arange(4 * c) // c
    s_att = np.arange(4 * c) % c
    t = np.arange(c)
    kmask = (h_att[:, None] == (np.arange(4 * dk) // dk)[None, :]).astype(np.float32)
    vmask = (h_att[:, None] == (np.arange(HW) // HEAD_DIM)[None, :]).astype(np.float32)
    smask = ((np.arange(HW) // HEAD_DIM)[:, None] == (np.arange(4 * dk) // dk)[None, :]).astype(np.float32)
    levels = (32, 16, 8, 4, 2, 1)
    lv = []
    for m in levels:
        same = (t[:, None] // (2 * m)) == (s_att[None, :] // (2 * m))
        lv.append(same & ((t[:, None] % (2 * m)) >= m) & ((s_att[None, :] % (2 * m)) < m))
    lv.append(t[:, None] == s_att[None, :])
    lvl = np.stack(lv).astype(np.float32)
    incl = (s_att[None, :] <= t[:, None]).astype(np.float32)
    strict = (s_att[None, :] < t[:, None]).astype(np.float32)
    eye = (s_att[None, :] == t[:, None]).astype(np.float32)
    fold = (np.arange(HW)[:, None] % HEAD_DIM == np.arange(HEAD_DIM)[None, :]).astype(np.float32)
    gam = _ret_gamma()
    gh = gam[h_att]
    ret_att = np.where(s_att[None, :] <= t[:, None], np.exp((t[:, None] - s_att[None, :]) * gh[None, :]), 0.0)
    gl = gam[np.arange(HW) // HEAD_DIM]
    ret_q = np.exp((t[:, None] + 1) * gl[None, :])
    ret_k = np.exp((c - 1 - t[:, None]) * gl[None, :])
    ret_s = np.exp(c * gl)[None, :]
    return dict(kmask=kmask, vmask=vmask, smask=smask, lvl=lvl, levels=levels, incl=incl, strict=strict,
                eye=eye, fold=fold, ret_att=ret_att.astype(np.float32), ret_q=ret_q.astype(np.float32),
                ret_k=ret_k.astype(np.float32), ret_s=ret_s.astype(np.float32))


def _bd_ones():
    h = np.arange(HW) // HEAD_DIM
    return (h[:, None] == h[None, :]).astype(np.float32)


def _expand_mat():
    e = np.zeros((_OFF["small"][1], 2 * HW), np.float32)
    for h in range(HEADS):
        e[SMALL_DB + h, h * HEAD_DIM:(h + 1) * HEAD_DIM] = 1.0
        e[SMALL_DA + h, HW + h * HEAD_DIM:HW + (h + 1) * HEAD_DIM] = 1.0
    return e


def _const_spec(a):
    nd = a.ndim
    return pl.BlockSpec(a.shape, lambda *_: (0,) * nd)


def _params(sem):
    return pltpu.CompilerParams(dimension_semantics=sem, vmem_limit_bytes=VMEM_LIMIT)


def _ada_kernel(cp_ref, cs_ref, w_ref, b_ref, op_ref, os_ref):
    w = _bf(w_ref[0])
    mp = _dot(_bf(_silu(cp_ref[...])), w) + b_ref[0]
    for b in range(mp.shape[0]):
        op_ref[0, 0, b] = mp[b:b + 1]
    os_ref[0, 0] = _dot(_bf(_silu(cs_ref[...])), w) + b_ref[0]


def _ada(c_prompt, c_sample, ada_w, ada_b):
    nl, d, nd = ada_w.shape
    bp, bs = c_prompt.shape[0], c_sample.shape[0]
    nmod = nd // d
    assert nmod == N_MOD
    return pl.pallas_call(
        _ada_kernel,
        grid=(nl, nmod),
        in_specs=[_const_spec(c_prompt), _const_spec(c_sample),
                  pl.BlockSpec((1, d, d), lambda l, j: (l, 0, j)),
                  pl.BlockSpec((1, 1, d), lambda l, j: (l, 0, j))],
        out_specs=[pl.BlockSpec((1, 1, bp, 1, d), lambda l, j: (l, j, 0, 0, 0)),
                   pl.BlockSpec((1, 1, bs, d), lambda l, j: (l, j, 0, 0))],
        out_shape=[jax.ShapeDtypeStruct((nl, nmod, bp, 1, d), F32),
                   jax.ShapeDtypeStruct((nl, nmod, bs, d), F32)],
        compiler_params=_params(("arbitrary", "arbitrary")),
        name="ada_mod",
    )(c_prompt, c_sample, ada_w, ada_b.reshape(nl, 1, nd))


class _Rows:
    def __init__(self, batch, t, decode, tile=ROW_TILE):
        self.batch, self.t, self.decode = batch, t, decode
        self.rows = batch * t
        self.tm = min(tile, self.rows)
        assert self.rows % self.tm == 0
        if decode:
            assert self.tm % batch == 0
        else:
            assert t % self.tm == 0
        self.steps = self.rows // self.tm


def _mod_spec(g, layer, k, d):
    if g.decode:
        return pl.BlockSpec((1, 1, g.batch, d), lambda i: (layer, k, 0, 0))
    per = g.t // g.tm
    return pl.BlockSpec((1, 1, 1, 1, d), lambda i: (layer, k, i // per, 0, 0))


def _ffn_block(x, sh_ref, sc_ref, gt_ref, wi_ref, wo_ref, lng_ref, lnb_ref, *, alpha, ln_idx, layer):
    tm, d = x.shape
    dff = wo_ref.shape[1]
    sh, sc, gt = _mod_rows(sh_ref, tm), _mod_rows(sc_ref, tm), _mod_rows(gt_ref, tm)
    h = _bf(x * (1.0 + sc) + sh)
    acc = jnp.zeros((tm, d), F32)
    off = 0
    while off < dff:
        fc = min(FF_CHUNK, dff - off)
        a = _dot(h, wi_ref[0, :, off:off + fc])
        b = _dot(h, wi_ref[0, :, dff + off:dff + off + fc])
        acc = acc + _dot(_bf(_silu(a) * b), wo_ref[0, off:off + fc, :])
        off += fc
    y = alpha * x + 0.5 * (1.0 + gt) * acc
    return _ln(y, lng_ref[layer, ln_idx:ln_idx + 1, :], lnb_ref[layer, ln_idx:ln_idx + 1, :])


def _mix_block(x, gt_ref, o_refs, w_ref, lng_ref, lnb_ref, *, alpha, layer, decode, batch):
    tm, d = x.shape
    gt = _mod_rows(gt_ref, tm)
    if decode:
        accs = [jnp.zeros((batch, d), F32) for _ in range(tm // batch)]
        for mi, o_ref in enumerate(o_refs):
            for h in range(HEADS):
                r0 = mi * HW + h * HEAD_DIM
                wh = w_ref[0, r0:r0 + HEAD_DIM, :]
                for t in range(tm // batch):
                    accs[t] = accs[t] + _dot_tn(_bf(o_ref[h, t]), wh)
        acc = jnp.concatenate(accs, axis=0)
    else:
        acc = jnp.zeros((tm, d), F32)
        for mi, o_ref in enumerate(o_refs):
            acc = acc + _dot(o_ref[...], w_ref[0, mi * HW:(mi + 1) * HW, :])
    y = alpha * x + (1.0 + gt) * acc
    return _ln(y, lng_ref[layer, 1:2, :], lnb_ref[layer, 1:2, :])


def _ffn_kernel(x_ref, sh_ref, sc_ref, gt_ref, wi_ref, wo_ref, lng_ref, lnb_ref, o_ref, **kw):
    o_ref[...] = _ffn_block(x_ref[...], sh_ref, sc_ref, gt_ref, wi_ref, wo_ref, lng_ref, lnb_ref, **kw)


def _mix_ffn_kernel(x_ref, gtm_ref, o0_ref, o1_ref, o2_ref, o3_ref, w_ref, sh_ref, sc_ref, gt_ref, wi_ref, wo_ref,
                    lng_ref, lnb_ref, y_ref, *, alpha, layer, decode, batch):
    x1 = _mix_block(x_ref[...], gtm_ref, (o0_ref, o1_ref, o2_ref, o3_ref), w_ref, lng_ref, lnb_ref, alpha=alpha,
                    layer=layer, decode=decode, batch=batch)
    y_ref[...] = _ffn_block(x1, sh_ref, sc_ref, gt_ref, wi_ref, wo_ref, lng_ref, lnb_ref, alpha=alpha, ln_idx=2,
                            layer=layer)


def _ffn_specs(g, layer, mods, d, dff, ln_g, ln_b):
    return [_mod_spec(g, layer, k, d) for k in mods] + [
        pl.BlockSpec((1, d, 2 * dff), lambda i: (layer, 0, 0), pipeline_mode=pl.Buffered(1)),
        pl.BlockSpec((1, dff, d), lambda i: (layer, 0, 0), pipeline_mode=pl.Buffered(1)),
        _const_spec(ln_g), _const_spec(ln_b)]


def _ffn(g, x, mod, wi, wo, ln_g, ln_b, layer, alpha):
    d = x.shape[1]
    row = pl.BlockSpec((g.tm, d), lambda i: (i, 0))
    return pl.pallas_call(
        functools.partial(_ffn_kernel, alpha=alpha, ln_idx=0, layer=layer),
        grid=(g.steps,),
        in_specs=[row] + _ffn_specs(g, layer, (0, 1, 2), d, wo.shape[1], ln_g, ln_b),
        out_specs=row,
        out_shape=jax.ShapeDtypeStruct(x.shape, F32),
        compiler_params=_params(("arbitrary",)),
        name="ffn0",
    )(x, mod, mod, mod, wi, wo, ln_g, ln_b)


def _mix_ffn(g, x, mod, outs, w_out, wi, wo, ln_g, ln_b, layer, alpha):
    d = x.shape[1]
    row = pl.BlockSpec((g.tm, d), lambda i: (i, 0))
    if g.decode:
        assert g.steps == 1
        ospec = _const_spec(outs[0])
    else:
        ospec = pl.BlockSpec((g.tm, HW), lambda i: (i, 0))
    return pl.pallas_call(
        functools.partial(_mix_ffn_kernel, alpha=alpha, layer=layer, decode=g.decode, batch=g.batch),
        grid=(g.steps,),
        in_specs=[row, _mod_spec(g, layer, 5, d), ospec, ospec, ospec, ospec,
                  pl.BlockSpec((1, w_out.shape[1], d), lambda i: (layer, 0, 0), pipeline_mode=pl.Buffered(1))]
        + _ffn_specs(g, layer, (6, 7, 8), d, wo.shape[1], ln_g, ln_b),
        out_specs=row,
        out_shape=jax.ShapeDtypeStruct(x.shape, F32),
        compiler_params=_params(("arbitrary",)),
        name="mix_ffn2",
    )(x, mod, *outs, w_out, mod, mod, mod, wi, wo, ln_g, ln_b)


def _prep_kernel(*refs, layer, nlayers, g_t, tm, decode, batch):
    (x_ref, sh_ref, sc_ref, w_ref, cos_ref, sin_ref, wg_ref, bg_ref, lb_ref, cw_ref, alog_ref, dtb_ref,
     bd_ref, ex_ref) = refs[:14]
    rest = refs[14:]
    if decode:
        cs_ref, alogc_ref, dtbc_ref = rest[:3]
        outs = rest[3:]
        cbuf = None
    else:
        outs = rest[:-1]
        cbuf = rest[-1]

    x = x_ref[...]
    hb = _bf(x * (1.0 + _mod_rows(sc_ref, tm)) + _mod_rows(sh_ref, tm))

    def proj(name):
        o, w = _OFF[name]
        return _dot(hb, w_ref[0, :, o:o + w])

    bd = bd_ref[...]

    dest = {}
    if decode:
        it = iter(outs)
        for mix, keys in (("ret", ("q", "k", "v", "gate")), ("gla", ("q", "k", "g", "v", "gate")),
                          ("hg", ("q", "k", "g", "v", "gate")), ("gdn", ("q", "k", "v", "gate"))):
            for kk in keys:
                dest[f"{mix}_{kk}"] = ("heads", next(it), 0)
        bgo_ref, convo_ref = next(it), next(it)
    else:
        ret_ref, gla_ref, hg_ref, gdn_ref, conv_ref = outs
        for dst, keys, widths in ((ret_ref, ("ret_q", "ret_k", "ret_v", "ret_gate"), (HW,) * 4),
                                  (gla_ref, ("gla_q", "gla_k", "gla_g", "gla_v", "gla_gate"),
                                   (HEADS * GLA_DK,) * 3 + (HW, HW)),
                                  (hg_ref, ("hg_q", "hg_k", "hg_g", "hg_v", "hg_gate"), (HW,) * 5),
                                  (gdn_ref, ("gdn_q", "gdn_k", "gdn_b", "gdn_g", "gdn_v", "gdn_gate"), (HW,) * 6)):
            off = 0
            for kk, w in zip(keys, widths):
                dest[kk] = ("rows", dst, off)
                off += w

    def put(name, val):
        kind, ref, off = dest[name]
        if kind == "rows":
            ref[:, off:off + val.shape[1]] = val
        else:
            dh = val.shape[1] // HEADS
            for t in range(tm // batch):
                vt = val[t * batch:(t + 1) * batch].T
                for h in range(HEADS):
                    ref[h, t] = vt[h * dh:(h + 1) * dh]

    small = proj("small")

    dq = proj("dqkv")
    cw = cw_ref[0]
    if decode:
        cs = cs_ref[0]
        nt = tm // batch
        blocks = [dq[t * batch:(t + 1) * batch] for t in range(nt)]
        convs = []
        for t in range(nt):
            acc = blocks[t] * cw[CONV_W - 1:CONV_W, :]
            for s in range(1, CONV_W):
                prev = blocks[t - s] if t >= s else cs[CONV_W - 1 - s + t]
                acc = acc + prev * cw[CONV_W - 1 - s:CONV_W - s, :]
            convs.append(acc)
        conv = jnp.concatenate(convs, axis=0)
        convo_ref[...] = dq[(nt - (CONV_W - 1)) * batch:].reshape(CONV_W - 1, batch, dq.shape[1])
    else:
        i = pl.program_id(0)
        per = g_t // tm

        halo = V7X_SUBLANES

        @pl.when(i % per == 0)
        def _():
            cbuf[0:halo, :] = jnp.zeros((halo, dq.shape[1]), F32)

        cbuf[halo:halo + tm, :] = dq
        conv = cbuf[pl.ds(halo, tm), :] * cw[CONV_W - 1:CONV_W, :]
        for s in range(1, CONV_W):
            conv = conv + cbuf[pl.ds(halo - s, tm), :] * cw[CONV_W - 1 - s:CONV_W - s, :]
        tail = cbuf[tm:tm + halo, :]
        cbuf[0:halo, :] = tail

        @pl.when(i % per == per - 1)
        def _():
            conv_ref[0] = tail[halo - (CONV_W - 1):halo, :]

    u = _silu(conv)
    uq, uk = u[:, 0:HW], u[:, HW:2 * HW]
    put("gdn_q", uq * lax.rsqrt(_dot(_bf(uq * uq), bd) + RMS_EPS) * (HEAD_DIM ** -0.5))
    put("gdn_k", uk * lax.rsqrt(_dot(_bf(uk * uk), bd) + RMS_EPS))
    put("gdn_v", u[:, 2 * HW:3 * HW])
    put("gdn_gate", _silu(proj("dg")))
    if decode:
        for t in range(tm // batch):
            smt = small[t * batch:(t + 1) * batch].T
            beta = jax.nn.sigmoid(smt[SMALL_DB:SMALL_DB + HEADS])
            gd = -jnp.exp(alogc_ref[layer]) * _softplus(smt[SMALL_DA:SMALL_DA + HEADS] + dtbc_ref[layer])
            for h in range(HEADS):
                bgo_ref[h, t, 0:1, :] = beta[h:h + 1]
                bgo_ref[h, t, 1:2, :] = gd[h:h + 1]
    else:
        dbda = _dot_x01(small, ex_ref[...])
        put("gdn_b", jax.nn.sigmoid(dbda[:, 0:HW]))
        put("gdn_g", -jnp.exp(alog_ref[layer:layer + 1, :]) * _softplus(dbda[:, HW:2 * HW] + dtb_ref[layer:layer + 1, :]))

    rows = [lb_ref[i:i + 1, :] for i in range(nlayers)]
    mx = functools.reduce(jnp.maximum, rows)
    ex = [jnp.exp(r - mx) for r in rows]
    tot = functools.reduce(lambda a, b: a + b, ex)
    plb = [e / tot for e in ex]
    lb = functools.reduce(lambda a, b: a + b, plb[:layer + 1]) - plb[0]
    zf = proj("hf")
    put("hg_q", _silu(proj("hq")) * (HEAD_DIM ** -0.5))
    put("hg_k", (1.0 - lb) * jax.nn.sigmoid(-zf))
    put("hg_g", jnp.log(lb + (1.0 - lb) * jax.nn.sigmoid(zf)))
    put("hg_v", proj("hi"))
    put("hg_gate", _silu(proj("hg")))

    gpre = _dot(_bf(small), wg_ref[0]) + bg_ref[layer:layer + 1, :]
    put("gla_q", proj("aq") * (GLA_DK ** -0.5))
    put("gla_k", proj("ak"))
    put("gla_g", -_softplus(-gpre) * (1.0 / GLA_GATE_NORM))
    put("gla_v", proj("av"))
    put("gla_gate", _silu(proj("ag")))

    cos, sin = cos_ref[...], sin_ref[...]
    lane = lax.broadcasted_iota(jnp.int32, (tm, HW), 1)
    first_half = (lane % HEAD_DIM) < (HEAD_DIM // 2)

    def rope(v):
        sw = jnp.where(first_half, pltpu.roll(v, HW - HEAD_DIM // 2, 1), pltpu.roll(v, HEAD_DIM // 2, 1))
        return v * cos + sw * sin

    put("ret_q", rope(proj("rq")))
    put("ret_k", rope(proj("rk")) * (HEAD_DIM ** -0.5))
    put("ret_v", proj("rv"))
    put("ret_gate", _silu(proj("rg")))


def _prep(g, x, mod, w_in_p, cos_t, sin_t, wg_p, gla_bg, hg_lb, gdn_conv, alog_e, dtb_e, dec, layer, nlayers):
    d = x.shape[1]
    bd = jnp.asarray(_bd_ones(), BF16)
    exm = jnp.asarray(_expand_mat(), BF16)
    row = pl.BlockSpec((g.tm, d), lambda i: (i, 0))
    if g.decode:
        tab = pl.BlockSpec((g.tm, HW), lambda i: (i, 0))
    else:
        per = g.t // g.tm
        tab = pl.BlockSpec((g.tm, HW), lambda i: (i % per, 0))
    in_specs = [row, _mod_spec(g, layer, 3, d), _mod_spec(g, layer, 4, d),
                pl.BlockSpec((1, d, PROJ_W), lambda i: (layer, 0, 0), pipeline_mode=pl.Buffered(1)),
                tab, tab,
                pl.BlockSpec((1,) + wg_p.shape[1:], lambda i: (layer, 0, 0)),
                _const_spec(gla_bg), _const_spec(hg_lb),
                pl.BlockSpec((1,) + gdn_conv.shape[1:], lambda i: (layer, 0, 0)),
                _const_spec(alog_e), _const_spec(dtb_e), _const_spec(bd), _const_spec(exm)]
    args = [x, mod, mod, w_in_p, cos_t, sin_t, wg_p, gla_bg, hg_lb, gdn_conv, alog_e, dtb_e, bd, exm]
    cch = gdn_conv.shape[2]

    def rows_out(w):
        return jax.ShapeDtypeStruct((g.rows, w), F32), pl.BlockSpec((g.tm, w), lambda i: (i, 0))

    def whole(shape):
        return jax.ShapeDtypeStruct(shape, F32), pl.BlockSpec(shape, lambda i: (0,) * len(shape))

    if g.decode:
        assert g.steps == 1 and g.t >= CONV_W - 1
        cs, alog_c, dtb_c = dec
        in_specs += [pl.BlockSpec((1,) + cs.shape[1:], lambda i: (layer, 0, 0, 0)), _const_spec(alog_c),
                     _const_spec(dtb_c)]
        args += [cs, alog_c, dtb_c]
        outs = []
        for dhs in ((HEAD_DIM,) * 4, (GLA_DK,) * 3 + (HEAD_DIM,) * 2, (HEAD_DIM,) * 5, (HEAD_DIM,) * 4):
            outs += [whole((HEADS, g.t, dh, g.batch)) for dh in dhs]
        outs += [whole((HEADS, g.t, 2, g.batch)), whole((CONV_W - 1, g.batch, cch))]
        scratch = []
    else:
        outs = [rows_out(4 * HW), rows_out(3 * HEADS * GLA_DK + 2 * HW), rows_out(5 * HW), rows_out(6 * HW),
                (jax.ShapeDtypeStruct((g.batch, CONV_W - 1, cch), F32),
                 pl.BlockSpec((1, CONV_W - 1, cch), lambda i: (i // (g.t // g.tm), 0, 0)))]
        scratch = [pltpu.VMEM((g.tm + V7X_SUBLANES, cch), F32)]
    return pl.pallas_call(
        functools.partial(_prep_kernel, layer=layer, nlayers=nlayers, g_t=g.t, tm=g.tm, decode=g.decode,
                          batch=g.batch),
        grid=(g.steps,),
        in_specs=in_specs,
        out_specs=[o[1] for o in outs],
        out_shape=[o[0] for o in outs],
        scratch_shapes=scratch,
        compiler_params=_params(("arbitrary",)),
        name="prep",
    )(*args)


def _chunk(x, c):
    return x[c * CHUNK:(c + 1) * CHUNK]


def _per_chunk(fn, nch):
    return jnp.concatenate([fn(c) for c in range(nch)], axis=0)


def _finish_prompt(o, gate, normw, bd, o_ref):
    ms = _dot(_bf(o * o), bd) * (1.0 / HEAD_DIM)
    on = o * lax.rsqrt(ms + RMS_EPS)
    if normw is not None:
        on = on * normw
    o_ref[...] = _bf(on * gate)


def _state_out(s_t, fold_ref, st_ref):
    fold = fold_ref[...]
    h1, h2, h3 = _split3(s_t)
    st_ref[0] = _dot_tn(h1, fold) + _dot_tn(h2, fold) + _dot_tn(h3, fold)


def _gla_prompt_kernel(f_ref, kmask_ref, vmask_ref, smask_ref, lvl_ref, fold_ref, bd_ref, nw_ref, ratt_ref, rq_ref,
                       rk_ref, rs_ref, o_ref, st_ref, s_scr, *, dk, nch, ret, levels, has_norm, layer):
    j = pl.program_id(1)
    dkk = HEADS * dk

    @pl.when(j == 0)
    def _():
        s_scr[...] = jnp.zeros(s_scr.shape, F32)

    kmask, vmask, smask, bd = kmask_ref[...], vmask_ref[...], smask_ref[...], bd_ref[...]
    normw = nw_ref[layer:layer + 1, :] if has_norm else None

    def mm(x, y):
        xb, yb = _bf(x), _bf(y)
        return _per_chunk(lambda c: _dot_nt(_chunk(xb, c), _tile4(_chunk(yb, c)) * kmask), nch)

    q = f_ref[:, 0:dkk]
    k = f_ref[:, dkk:2 * dkk]
    if ret:
        v = f_ref[:, 2 * dkk:2 * dkk + HW]
        gate = f_ref[:, 2 * dkk + HW:2 * dkk + 2 * HW]
        att = _per_block(mm(q, k), ratt_ref[...], CHUNK)
        qg = _per_block(q, rq_ref[...], CHUNK)
        kd = _per_block(k, rk_ref[...], CHUNK)
        sdec = [rs_ref[...]] * nch
    else:
        g = f_ref[:, 2 * dkk:3 * dkk]
        v = f_ref[:, 3 * dkk:3 * dkk + HW]
        gate = f_ref[:, 3 * dkk + HW:3 * dkk + 2 * HW]
        G = _cumsum_rows(g, CHUNK)
        att = _decay_att(q, k, G, mm, lambda x, y: _dot_nt(_bf(x * y), kmask), lvl_ref, levels, CHUNK)
        qg = q * jnp.exp(G)
        glb = _block_row(G, CHUNK, CHUNK - 1)
        kd = k * jnp.exp(glb - G)
        sdec = [jnp.exp(G[(c + 1) * CHUNK - 1:(c + 1) * CHUNK, :]) for c in range(nch)]
    vb, qgb, kdb, attb = _bf(v), _bf(qg), _bf(kd), _bf(att)
    o_intra = _per_chunk(lambda c: _dot(_chunk(attb, c), _tile4(_chunk(vb, c)) * vmask), nch)
    upd = [_dot_tn(_chunk(vb, c), _chunk(kdb, c)) * smask for c in range(nch)]

    s = s_scr[...]
    o_inter = []
    for c in range(nch):
        o_inter.append(_dot_nt(_chunk(qgb, c), _bf(s)))
        s = s * sdec[c] + upd[c]
    s_scr[...] = s
    _finish_prompt(o_intra + jnp.concatenate(o_inter, axis=0), gate, normw, bd, o_ref)

    @pl.when(j == pl.num_programs(1) - 1)
    def _():
        _state_out(s, fold_ref, st_ref)


def _gla_prompt(g, feat, dk, normw, layer, ret):
    c = _chunk_consts(dk)
    dkk = HEADS * dk
    blk = min(MIX_TILE, g.t)
    nch = blk // CHUNK
    per = g.t // blk
    consts = [jnp.asarray(c["kmask"], BF16), jnp.asarray(c["vmask"], BF16), jnp.asarray(c["smask"]),
              jnp.asarray(c["lvl"]), jnp.asarray(c["fold"], BF16), jnp.asarray(_bd_ones(), BF16),
              normw if normw is not None else jnp.ones((1, HW), F32),
              jnp.asarray(c["ret_att"]), jnp.asarray(c["ret_q"]), jnp.asarray(c["ret_k"]), jnp.asarray(c["ret_s"])]
    w = feat.shape[1]
    o, st = pl.pallas_call(
        functools.partial(_gla_prompt_kernel, dk=dk, nch=nch, ret=ret, levels=c["levels"],
                          has_norm=normw is not None, layer=layer),
        grid=(g.batch, per),
        in_specs=[pl.BlockSpec((blk, w), lambda b, j: (b * per + j, 0))] + [_const_spec(a) for a in consts],
        out_specs=[pl.BlockSpec((blk, HW), lambda b, j: (b * per + j, 0)),
                   pl.BlockSpec((1, dkk, HEAD_DIM), lambda b, j: (b, 0, 0))],
        out_shape=[jax.ShapeDtypeStruct((g.rows, HW), BF16),
                   jax.ShapeDtypeStruct((g.batch, dkk, HEAD_DIM), F32)],
        scratch_shapes=[pltpu.VMEM((HW, dkk), F32)],
        compiler_params=_params(("arbitrary", "arbitrary")),
        name="ret_prompt" if ret else f"gla_prompt_dk{dk}",
    )(feat, *consts)
    return o, st.reshape(g.batch, HEADS, dk, HEAD_DIM)


def _gdn_prompt_kernel(f_ref, bdm_ref, smask_ref, incl_ref, strict_ref, eye_ref, fold_ref, bd_ref, nw_ref, o_ref,
                       st_ref, s_scr, *, nch, layer):
    j = pl.program_id(1)

    @pl.when(j == 0)
    def _():
        s_scr[...] = jnp.zeros(s_scr.shape, F32)

    bdm, smask, incl, strict, eye, bd = (bdm_ref[...], smask_ref[...], incl_ref[...], strict_ref[...], eye_ref[...],
                                         bd_ref[...])
    normw = nw_ref[layer:layer + 1, :]
    ones_c = jnp.ones((CHUNK, CHUNK), BF16)

    def bdw(yb, c):
        return _tile4(_chunk(yb, c)) * bdm

    def prod1(x, y):
        xb, yb = _bf(x), _bf(y)
        return _per_chunk(lambda c: _dot(_chunk(xb, c), bdw(yb, c)), nch)

    def prod1_pair(x1, x2, y):
        x1b, x2b, yb = _bf(x1), _bf(x2), _bf(y)
        both = [_dot(jnp.concatenate([_chunk(x1b, c), _chunk(x2b, c)], axis=0), bdw(yb, c)) for c in range(nch)]
        return (jnp.concatenate([m[:CHUNK] for m in both], axis=0),
                jnp.concatenate([m[CHUNK:] for m in both], axis=0))

    def prod2(x, y):
        xh, xl = _split2(x)
        yb = _bf(y)

        def one(c):
            top = _dot(jnp.concatenate([_chunk(xh, c), _chunk(xl, c)], axis=0), bdw(yb, c))
            return top[:CHUNK] + top[CHUNK:]

        return _per_chunk(one, nch)

    def prod3(x, y):
        xh, xl = _split2(x)
        yh, yl = _split2(y)

        def one(c):
            top = _dot(jnp.concatenate([_chunk(xh, c), _chunk(xl, c)], axis=0), bdw(yh, c))
            return top[:CHUNK] + top[CHUNK:] + _dot(_chunk(xh, c), bdw(yl, c))

        return _per_chunk(one, nch)

    q, k = f_ref[:, 0:HW], f_ref[:, HW:2 * HW]
    be, ge = f_ref[:, 2 * HW:3 * HW], f_ref[:, 3 * HW:4 * HW]
    v, gate = f_ref[:, 4 * HW:5 * HW], f_ref[:, 5 * HW:6 * HW]
    G = _cumsum_rows(ge, CHUNK)
    zd = _per_block(G, eye, CHUNK)
    grow = _per_chunk(lambda c: _dot_exact01(ones_c, _chunk(zd, c)), nch)
    L = _per_block(jnp.exp(jnp.minimum(G - grow, 0.0)), incl, CHUNK)
    qb, kb = _bf(q), _bf(k)

    def qkk(c):
        return _dot_nt(jnp.concatenate([_chunk(qb, c), _chunk(kb, c)], axis=0), bdw(kb, c))

    qkk_all = [qkk(c) for c in range(nch)]
    qk = jnp.concatenate([m[:CHUNK] for m in qkk_all], axis=0)
    kk = jnp.concatenate([m[CHUNK:] for m in qkk_all], axis=0)
    a = _per_block(be * kk * L, strict, CHUNK)
    eye_b = _per_block(jnp.ones_like(a), eye, CHUNK)
    p = -a
    tinv = eye_b + p
    for i in range(5):
        if i == 0:
            p = prod1(p, p)
        else:
            p, dt = prod1_pair(p, tinv, p)
            tinv = tinv + dt
    tinv = tinv + prod1(tinv, p)
    resid = eye_b - tinv - prod3(a, tinv)
    tinv = tinv + prod1(tinv, resid)
    eg = jnp.exp(G)
    u = prod2(tinv, be * v)
    w = prod2(tinv, be * eg * k)
    glb = _block_row(G, CHUNK, CHUNK - 1)
    wb, qeb, qklb = _bf(w), _bf(q * eg), _bf(qk * L)
    kdb = _bf(k * jnp.exp(glb - G))
    sdec = [jnp.exp(G[(c + 1) * CHUNK - 1:(c + 1) * CHUNK, :]) for c in range(nch)]

    ub = _bf(u)
    s_add = [_dot_tn(_chunk(ub, c), _chunk(kdb, c)) * smask for c in range(nch)]
    s_mul = [_bf(_dot_tn(_chunk(wb, c), _chunk(kdb, c)) * smask) for c in range(nch)]

    s = s_scr[...]
    snaps = []
    for c in range(nch):
        sb = _bf(s)
        snaps.append(sb)
        s = s * sdec[c] + s_add[c] - _dot(sb, s_mul[c])
    s_scr[...] = s

    outs = []
    for c in range(nch):
        ws = _dot_nt(jnp.concatenate([_chunk(wb, c), _chunk(qeb, c)], axis=0), snaps[c])
        db = _bf(_chunk(u, c) - ws[:CHUNK])
        outs.append(_dot(_chunk(qklb, c), _tile4(db) * bdm) + ws[CHUNK:])
    _finish_prompt(jnp.concatenate(outs, axis=0), gate, normw, bd, o_ref)

    @pl.when(j == pl.num_programs(1) - 1)
    def _():
        _state_out(s, fold_ref, st_ref)


def _gdn_prompt(g, feat, normw, layer):
    c = _chunk_consts(HEAD_DIM)
    blk = min(MIX_TILE, g.t)
    nch = blk // CHUNK
    per = g.t // blk
    consts = [jnp.asarray(c["vmask"], BF16), jnp.asarray(c["smask"]), jnp.asarray(c["incl"]), jnp.asarray(c["strict"]),
              jnp.asarray(c["eye"]), jnp.asarray(c["fold"], BF16), jnp.asarray(_bd_ones(), BF16), normw]
    w = feat.shape[1]
    o, st = pl.pallas_call(
        functools.partial(_gdn_prompt_kernel, nch=nch, layer=layer),
        grid=(g.batch, per),
        in_specs=[pl.BlockSpec((blk, w), lambda b, j: (b * per + j, 0))] + [_const_spec(a) for a in consts],
        out_specs=[pl.BlockSpec((blk, HW), lambda b, j: (b * per + j, 0)),
                   pl.BlockSpec((1, HW, HEAD_DIM), lambda b, j: (b, 0, 0))],
        out_shape=[jax.ShapeDtypeStruct((g.rows, HW), BF16),
                   jax.ShapeDtypeStruct((g.batch, HW, HEAD_DIM), F32)],
        scratch_shapes=[pltpu.VMEM((HW, HW), F32)],
        compiler_params=_params(("arbitrary", "arbitrary")),
        name="gdn_prompt",
    )(feat, *consts)
    return o, st.reshape(g.batch, HEADS, HEAD_DIM, HEAD_DIM)


def _state_io(refs, layer):
    if layer == 0:
        return refs
    prev_ref, rest = refs[0], refs[1:]
    rest[-1][0:layer] = prev_ref[...]
    return rest


def _dec_finish(o, gate, normw):
    ms = jnp.mean(o * o, axis=0, keepdims=True)
    on = o * lax.rsqrt(ms + RMS_EPS)
    if normw is not None:
        on = on * normw
    return on * gate


def _gla_decode_kernel(q_ref, k_ref, g_ref, v_ref, gate_ref, s_ref, *refs, nt, dk, ret, has_norm, layer):
    nw_ref, o_ref, so_ref = _state_io(refs, layer)
    normw = nw_ref[layer] if has_norm else None
    qs = [q_ref[0, t] for t in range(nt)]
    ks = [k_ref[0, t] for t in range(nt)]
    vs = [v_ref[0, t] for t in range(nt)]
    decay = [jnp.exp(g_ref[0, t]) for t in range(nt)]
    o_acc = [jnp.zeros(vs[0].shape, F32) for _ in range(nt)]
    for dc in range(dk // DEC_DCHUNK):
        r = slice(dc * DEC_DCHUNK, (dc + 1) * DEC_DCHUNK)
        sc = s_ref[0, 0, r]
        for t in range(nt):
            a = decay[t][None] if ret else decay[t][r][:, None, :]
            sc = sc * a + ks[t][r][:, None, :] * vs[t][None, :, :]
            o_acc[t] = o_acc[t] + jnp.sum(qs[t][r][:, None, :] * sc, axis=0)
        so_ref[layer, 0, r] = sc
    for t in range(nt):
        o_ref[0, t] = _dec_finish(o_acc[t], gate_ref[0, t], normw)


def _gdn_decode_kernel(q_ref, k_ref, v_ref, gate_ref, bg_ref, s_ref, *refs, nt, layer):
    nw_ref, o_ref, so_ref = _state_io(refs, layer)
    normw = nw_ref[layer]
    chunks = [slice(dc * DEC_DCHUNK, (dc + 1) * DEC_DCHUNK) for dc in range(HEAD_DIM // DEC_DCHUNK)]
    for t in range(nt):
        q, k, v = q_ref[0, t], k_ref[0, t], v_ref[0, t]
        beta = bg_ref[0, t, 0:1, :]
        ag = jnp.exp(bg_ref[0, t, 1:2, :])

        def state(r):
            return s_ref[0, 0, r] if t == 0 else so_ref[layer, 0, r]

        ks = jnp.zeros(v.shape, F32)
        for r in chunks:
            ks = ks + jnp.sum(k[r][:, None, :] * state(r), axis=0)
        delta = beta * (v - ag * ks)
        o = jnp.zeros(v.shape, F32)
        for r in chunks:
            sc = state(r) * ag[None] + k[r][:, None, :] * delta[None, :, :]
            so_ref[layer, 0, r] = sc
            o = o + jnp.sum(q[r][:, None, :] * sc, axis=0)
        o_ref[0, t] = _dec_finish(o, gate_ref[0, t], normw)


def _decode_call(body, name, feats, state_t, prev, extra, dk, layer):
    nt, batch = feats[0].shape[1], feats[0].shape[3]

    def head_spec(a):
        return pl.BlockSpec((1,) + a.shape[1:], lambda h: (h,) + (0,) * (a.ndim - 1))

    def st_spec(nl):
        return pl.BlockSpec((nl, 1, dk, HEAD_DIM, batch), lambda h: (0, h, 0, 0, 0))

    in_specs = [head_spec(a) for a in feats]
    in_specs.append(pl.BlockSpec((1, 1, dk, HEAD_DIM, batch), lambda h: (layer, h, 0, 0, 0)))
    args = list(feats) + [state_t]
    if layer > 0:
        in_specs.append(st_spec(layer))
        args.append(prev)
    o_shape = (HEADS, nt, HEAD_DIM, batch)
    return pl.pallas_call(
        body,
        grid=(HEADS,),
        in_specs=in_specs + [_const_spec(a) for a in extra],
        out_specs=[pl.BlockSpec((1,) + o_shape[1:], lambda h: (h, 0, 0, 0)), st_spec(layer + 1)],
        out_shape=[jax.ShapeDtypeStruct(o_shape, F32),
                   jax.ShapeDtypeStruct((layer + 1, HEADS, dk, HEAD_DIM, batch), F32)],
        compiler_params=_params(("arbitrary",)),
        name=name,
    )(*args, *extra)


def _gla_decode(q, k, g, v, gate, state_t, prev, dk, normw_col, layer, ret):
    nt = q.shape[1]
    body = functools.partial(_gla_decode_kernel, nt=nt, dk=dk, ret=ret, has_norm=normw_col is not None, layer=layer)
    nw = normw_col if normw_col is not None else jnp.ones((layer + 1, HEAD_DIM, 1), F32)
    return _decode_call(body, "ret_decode" if ret else f"gla_decode_dk{dk}", [q, k, g, v, gate], state_t, prev, [nw],
                        dk, layer)


def _gdn_decode(q, k, v, gate, bg, state_t, prev, normw_col, layer):
    body = functools.partial(_gdn_decode_kernel, nt=q.shape[1], layer=layer)
    return _decode_call(body, "gdn_decode", [q, k, v, gate, bg], state_t, prev, [normw_col], HEAD_DIM, layer)


def _rope_tables(pos):
    half = HEAD_DIM // 2
    inv = ROPE_BASE ** (-jnp.arange(half, dtype=F32) / half)
    ang = pos[:, None] * inv[None, :]
    cos, sin = jnp.cos(ang), jnp.sin(ang)
    cos_t = jnp.tile(jnp.concatenate([cos, cos], axis=1), (1, HEADS))
    sin_t = jnp.tile(jnp.concatenate([-sin, sin], axis=1), (1, HEADS))
    return cos_t, sin_t


def _relayout_w_in(w_in):
    def cols(name):
        o, w = _SRC[name]
        return w_in[:, :, o:o + w]

    parts = []
    for name, w in _SLABS:
        if name == "small":
            sm = jnp.concatenate([cols(n) for n in _NARROW], axis=2)
            parts.append(jnp.pad(sm, ((0, 0), (0, 0), (0, w - sm.shape[2]))))
        else:
            parts.append(cols(name))
    return jnp.concatenate(parts, axis=2).astype(BF16)


def kernel(x_prompt, x_sample, state_ret, state_gla, state_hgrn, state_gdn, state_gdn_conv, c_prompt, c_sample, ada_w, ada_b, ln_g, ln_b, ffn1_wi, ffn1_wo, ffn2_wi, ffn2_wo, w_in, gla_wg, gla_bg, hg_lb, gdn_conv, gdn_a_log, gdn_dt_bias, gla_norm, hg_norm, gdn_norm, w_out):
    nl, d = ada_w.shape[0], ada_w.shape[1]
    bp, tp = x_prompt.shape[0], x_prompt.shape[1]
    bs, ts = x_sample.shape[0], x_sample.shape[1]
    alpha = (2.0 * nl) ** 0.25
    assert tp % CHUNK == 0 and ts >= CONV_W - 1

    wi1, wo1, wi2, wo2 = (w.astype(BF16) for w in (ffn1_wi, ffn1_wo, ffn2_wi, ffn2_wo))
    w_out_b = w_out.astype(BF16)
    w_in_p = _relayout_w_in(w_in)
    wg_p = jnp.pad(gla_wg, ((0, 0), (0, _OFF["small"][1] - gla_wg.shape[1]), (0, 0))).astype(BF16)
    alog_e = jnp.repeat(gdn_a_log, HEAD_DIM, axis=1)
    dtb_e = jnp.repeat(gdn_dt_bias, HEAD_DIM, axis=1)
    gla_nw = jnp.tile(gla_norm, (1, HEADS))
    hg_nw = jnp.tile(hg_norm, (1, HEADS))
    gdn_nw = jnp.tile(gdn_norm, (1, HEADS))

    mod_p, mod_s = _ada(c_prompt, c_sample, ada_w, ada_b)

    gp = _Rows(bp, tp, decode=False)
    gp_prep = _Rows(bp, tp, decode=False, tile=PREP_TILE)
    gs = _Rows(bs, ts, decode=True)
    assert gs.steps == 1
    cos_p, sin_p = _rope_tables(jnp.arange(tp, dtype=F32))
    cos_s, sin_s = _rope_tables(PAST_LEN + jnp.arange(ts, dtype=F32))
    cos_s, sin_s = jnp.repeat(cos_s, bs, axis=0), jnp.repeat(sin_s, bs, axis=0)

    xp = x_prompt.reshape(bp * tp, d)
    xs = jnp.transpose(x_sample, (1, 0, 2)).reshape(ts * bs, d)

    st_ret, st_gla, st_hg, st_gdn = (jnp.transpose(s, (0, 2, 3, 4, 1))
                                     for s in (state_ret, state_gla, state_hgrn, state_gdn))
    conv_t = jnp.transpose(state_gdn_conv, (0, 2, 1, 3))
    dec_extra = (conv_t, gdn_a_log[:, :, None], gdn_dt_bias[:, :, None])
    ret_g = jnp.asarray(np.broadcast_to(_ret_gamma().astype(np.float32)[:, None, None, None], (HEADS, ts, 1, bs)))
    gla_nc, hg_nc, gdn_nc = gla_norm[:, :, None], hg_norm[:, :, None], gdn_norm[:, :, None]

    p_states, s_convs = [], []
    s_ret = s_gla = s_hg = s_gdn = None
    for l in range(nl):
        xp = _ffn(gp, xp, mod_p, wi1, wo1, ln_g, ln_b, l, alpha)
        ret_f, gla_f, hg_f, gdn_f, p_conv = _prep(gp_prep, xp, mod_p, w_in_p, cos_p, sin_p, wg_p, gla_bg, hg_lb,
                                                  gdn_conv, alog_e, dtb_e, None, l, nl)
        o_ret, p_ret = _gla_prompt(gp, ret_f, HEAD_DIM, None, l, ret=True)
        o_gla, p_gla = _gla_prompt(gp, gla_f, GLA_DK, gla_nw, l, ret=False)
        o_hg, p_hg = _gla_prompt(gp, hg_f, HEAD_DIM, hg_nw, l, ret=False)
        o_gdn, p_gdn = _gdn_prompt(gp, gdn_f, gdn_nw, l)
        xp = _mix_ffn(gp, xp, mod_p, (o_ret, o_gla, o_hg, o_gdn), w_out_b, wi2, wo2, ln_g, ln_b, l, alpha)
        p_states.append((p_ret, p_gla, p_hg, p_gdn, p_conv))

        xs = _ffn(gs, xs, mod_s, wi1, wo1, ln_g, ln_b, l, alpha)
        (rq, rk, rv, rgt, aq, ak, ag, av, agt, hq, hk, hgg, hv, hgt, dq, dk_, dv, dgt, dbg, conv_new) = _prep(
            gs, xs, mod_s, w_in_p, cos_s, sin_s, wg_p, gla_bg, hg_lb, gdn_conv, alog_e, dtb_e, dec_extra, l, nl)
        o_ret, s_ret = _gla_decode(rq, rk, ret_g, rv, rgt, st_ret, s_ret, HEAD_DIM, None, l, True)
        o_gla, s_gla = _gla_decode(aq, ak, ag, av, agt, st_gla, s_gla, GLA_DK, gla_nc, l, False)
        o_hg, s_hg = _gla_decode(hq, hk, hgg, hv, hgt, st_hg, s_hg, HEAD_DIM, hg_nc, l, False)
        o_gdn, s_gdn = _gdn_decode(dq, dk_, dv, dgt, dbg, st_gdn, s_gdn, gdn_nc, l)
        xs = _mix_ffn(gs, xs, mod_s, (o_ret, o_gla, o_hg, o_gdn), w_out_b, wi2, wo2, ln_g, ln_b, l, alpha)
        s_convs.append(conv_new)

    y_prompt = xp.reshape(bp, tp, d)
    y_sample = jnp.transpose(xs.reshape(ts, bs, d), (1, 0, 2))
    p_out = tuple(jnp.stack([st[i] for st in p_states]) for i in range(5))
    s_out = tuple(jnp.transpose(s, (0, 4, 1, 2, 3)) for s in (s_ret, s_gla, s_hg, s_gdn))
    return (y_prompt, y_sample) + p_out + s_out + (jnp.transpose(jnp.stack(s_convs), (0, 2, 1, 3)),)
```

```python
import functools

import numpy as np
import jax
import jax.numpy as jnp
from jax import lax
from jax.experimental import pallas as pl
from jax.experimental.pallas import tpu as pltpu

F32 = jnp.float32
BF16 = jnp.bfloat16

HEAD_DIM = 64
HEADS = 4
GLA_DK = 32
GLA_GATE_RANK = 16
GLA_GATE_NORM = 16.0
CONV_W = 4
CHUNK = 64
ROPE_BASE = 10000.0
LN_EPS = 1e-5
RMS_EPS = 1e-6
PAST_LEN = 16384
N_MOD = 9
HW = HEADS * HEAD_DIM

V7X_LANES = 128
V7X_SUBLANES = 8
V7X_VMEM_BYTES = 64 * 1024 * 1024

ROW_TILE = 512
PREP_TILE = 512
MIX_TILE = 2048
DEC_DCHUNK = V7X_SUBLANES
FF_CHUNK = 256
VMEM_LIMIT = V7X_VMEM_BYTES - 12 * 1024 * 1024

_MIX_COLS = (("rq", HW), ("rk", HW), ("rv", HW), ("rg", HW),
             ("aq", HEADS * GLA_DK), ("ak", HEADS * GLA_DK), ("av", HW), ("alr", GLA_GATE_RANK), ("ag", HW),
             ("hq", HW), ("hf", HW), ("hi", HW), ("hg", HW),
             ("dqkv", 3 * HW), ("db", HEADS), ("da", HEADS), ("dg", HW))
_SRC = {}
_o = 0
for _n, _w in _MIX_COLS:
    _SRC[_n] = (_o, _w)
    _o += _w
_NARROW = ("alr", "db", "da")
_SLABS = tuple((n, w) for n, w in _MIX_COLS if n not in _NARROW) + (("small", V7X_LANES),)
_OFF = {}
_o = 0
for _n, _w in _SLABS:
    assert _o % V7X_LANES == 0
    _OFF[_n] = (_o, _w)
    _o += _w
PROJ_W = _o
SMALL_ALR, SMALL_DB, SMALL_DA = 0, GLA_GATE_RANK, GLA_GATE_RANK + HEADS


def _bf(x):
    return x.astype(BF16)


def _dot(a, b):
    return jnp.dot(a, b, preferred_element_type=F32)


def _dot_nt(a, b):
    return lax.dot_general(a, b, (((1,), (1,)), ((), ())), preferred_element_type=F32)


def _dot_tn(a, b):
    return lax.dot_general(a, b, (((0,), (0,)), ((), ())), preferred_element_type=F32)


def _split2(x):
    hi = x.astype(BF16)
    lo = (x - hi.astype(F32)).astype(BF16)
    return hi, lo


def _split3(x):
    hi = x.astype(BF16)
    r = x - hi.astype(F32)
    mid = r.astype(BF16)
    lo = (r - mid.astype(F32)).astype(BF16)
    return hi, mid, lo


def _dot_x01(x, m01):
    hi, lo = _split2(x)
    return _dot(hi, m01) + _dot(lo, m01)


def _dot_exact01(m01, x):
    h1, h2, h3 = _split3(x)
    return _dot(m01, h1) + _dot(m01, h2) + _dot(m01, h3)


def _silu(x):
    return x * jax.nn.sigmoid(x)


def _softplus(x):
    return jnp.maximum(x, 0.0) + jnp.log(1.0 + jnp.exp(-jnp.abs(x)))


def _ln(y, g, b):
    mu = jnp.mean(y, axis=-1, keepdims=True)
    yc = y - mu
    var = jnp.mean(yc * yc, axis=-1, keepdims=True)
    return yc * lax.rsqrt(var + LN_EPS) * g + b


def _mod_rows(ref, tm):
    v = ref[...]
    v = v.reshape(v.shape[-2], v.shape[-1])
    nb = v.shape[0]
    if nb == 1:
        return v
    return jnp.concatenate([v] * (tm // nb), axis=0)


def _tile4(x):
    return jnp.concatenate([x, x, x, x], axis=0)


def _per_block(x, const, period):
    n = x.shape[0] // period
    return (x.reshape(n, period, x.shape[1]) * const[None]).reshape(x.shape)


def _cumsum_rows(g, period):
    rows, w = g.shape
    tile = min(period, V7X_SUBLANES)
    row = lax.broadcasted_iota(jnp.int32, g.shape, 0) % tile
    out = g
    sh = 1
    while sh < tile:
        out = out + jnp.where(row >= sh, pltpu.roll(out, sh, 0), 0.0)
        sh *= 2
    if period == tile:
        return out
    nper = period // tile
    o4 = out.reshape(rows // period, nper, tile, w)
    tot = o4[:, :, tile - 1:tile, :]
    offs = [jnp.zeros_like(tot[:, 0:1])]
    for i in range(1, nper):
        offs.append(offs[-1] + tot[:, i - 1:i])
    return (o4 + jnp.concatenate(offs, axis=1)).reshape(rows, w)


def _block_row(G, period, r):
    c, w = G.shape
    g3 = G.reshape(c // period, period, w)
    return jnp.broadcast_to(g3[:, r:r + 1, :], g3.shape).reshape(c, w)


def _gref(G, m):
    c = G.shape[0]
    if 2 * m >= V7X_SUBLANES:
        return _block_row(G, 2 * m, m - 1)
    p = lax.broadcasted_iota(jnp.int32, G.shape, 0) % (2 * m)
    if m == 2:
        return jnp.where(p == 0, pltpu.roll(G, c - 1, 0),
                         jnp.where(p == 1, G, jnp.where(p == 2, pltpu.roll(G, 1, 0), pltpu.roll(G, 2, 0))))
    return jnp.where(p == 0, G, pltpu.roll(G, 1, 0))


def _decay_att(q, k, G, mm, diag, lvl_ref, levels, period):
    att = _per_block(diag(q, k), lvl_ref[len(levels)], period)
    for i, m in enumerate(levels):
        if m == 1:
            ql = q * jnp.exp(jnp.minimum(G - pltpu.roll(G, 1, 0), 0.0))
            att = att + _per_block(diag(ql, pltpu.roll(k, 1, 0)), lvl_ref[i], period)
            continue
        gr = _gref(G, m)
        ql = q * jnp.exp(jnp.minimum(G - gr, 0.0))
        kl = k * jnp.exp(jnp.minimum(gr - G, 0.0))
        att = att + _per_block(mm(ql, kl), lvl_ref[i], period)
    return att


def _ret_gamma():
    return np.log(1.0 - 2.0 ** (-5.0 - np.arange(HEADS, dtype=np.float64)))


@functools.lru_cache(maxsize=None)
def _chunk_consts(dk):
    c = CHUNK
    h_att = np.arange(4 * c) // c
    s_att = np.arange(4 * c) % c
    t = np.arange(c)
    kmask = (h_att[:, None] == (np.arange(4 * dk) // dk)[None, :]).astype(np.float32)
    vmask = (h_att[:, None] == (np.arange(HW) // HEAD_DIM)[None, :]).astype(np.float32)
    smask = ((np.arange(HW) // HEAD_DIM)[:, None] == (np.arange(4 * dk) // dk)[None, :]).astype(np.float32)
    levels = (32, 16, 8, 4, 2, 1)
    lv = []
    for m in levels:
        same = (t[:, None] // (2 * m)) == (s_att[None, :] // (2 * m))
        lv.append(same & ((t[:, None] % (2 * m)) >= m) & ((s_att[None, :] % (2 * m)) < m))
    lv.append(t[:, None] == s_att[None, :])
    lvl = np.stack(lv).astype(np.float32)
    incl = (s_att[None, :] <= t[:, None]).astype(np.float32)
    strict = (s_att[None, :] < t[:, None]).astype(np.float32)
    eye = (s_att[None, :] == t[:, None]).astype(np.float32)
    fold = (np.arange(HW)[:, None] % HEAD_DIM == np.arange(HEAD_DIM)[None, :]).astype(np.float32)
    gam = _ret_gamma()
    gh = gam[h_att]
    ret_att = np.where(s_att[None, :] <= t[:, None], np.exp((t[:, None] - s_att[None, :]) * gh[None, :]), 0.0)
    gl = gam[np.arange(HW) // HEAD_DIM]
    ret_q = np.exp((t[:, None] + 1) * gl[None, :])
    ret_k = np.exp((c - 1 - t[:, None]) * gl[None, :])
    ret_s = np.exp(c * gl)[None, :]
    return dict(kmask=kmask, vmask=vmask, smask=smask, lvl=lvl, levels=levels, incl=incl, strict=strict,
                eye=eye, fold=fold, ret_att=ret_att.astype(np.float32), ret_q=ret_q.astype(np.float32),
                ret_k=ret_k.astype(np.float32), ret_s=ret_s.astype(np.float32))


def _bd_ones():
    h = np.arange(HW) // HEAD_DIM
    return (h[:, None] == h[None, :]).astype(np.float32)


def _expand_mat():
    e = np.zeros((_OFF["small"][1], 2 * HW), np.float32)
    for h in range(HEADS):
        e[SMALL_DB + h, h * HEAD_DIM:(h + 1) * HEAD_DIM] = 1.0
        e[SMALL_DA + h, HW + h * HEAD_DIM:HW + (h + 1) * HEAD_DIM] = 1.0
    return e


def _const_spec(a):
    nd = a.ndim
    return pl.BlockSpec(a.shape, lambda *_: (0,) * nd)


def _params(sem):
    return pltpu.CompilerParams(dimension_semantics=sem, vmem_limit_bytes=VMEM_LIMIT)


def _ada_kernel(cp_ref, cs_ref, w_ref, b_ref, op_ref, os_ref):
    w = _bf(w_ref[0])
    mp = _dot(_bf(_silu(cp_ref[...])), w) + b_ref[0]
    for b in range(mp.shape[0]):
        op_ref[0, 0, b] = mp[b:b + 1]
    os_ref[0, 0] = _dot(_bf(_silu(cs_ref[...])), w) + b_ref[0]


def _ada(c_prompt, c_sample, ada_w, ada_b):
    nl, d, nd = ada_w.shape
    bp, bs = c_prompt.shape[0], c_sample.shape[0]
    nmod = nd // d
    assert nmod == N_MOD
    return pl.pallas_call(
        _ada_kernel,
        grid=(nl, nmod),
        in_specs=[_const_spec(c_prompt), _const_spec(c_sample),
                  pl.BlockSpec((1, d, d), lambda l, j: (l, 0, j)),
                  pl.BlockSpec((1, 1, d), lambda l, j: (l, 0, j))],
        out_specs=[pl.BlockSpec((1, 1, bp, 1, d), lambda l, j: (l, j, 0, 0, 0)),
                   pl.BlockSpec((1, 1, bs, d), lambda l, j: (l, j, 0, 0))],
        out_shape=[jax.ShapeDtypeStruct((nl, nmod, bp, 1, d), F32),
                   jax.ShapeDtypeStruct((nl, nmod, bs, d), F32)],
        compiler_params=_params(("arbitrary", "arbitrary")),
        name="ada_mod",
    )(c_prompt, c_sample, ada_w, ada_b.reshape(nl, 1, nd))


class _Rows:
    def __init__(self, batch, t, decode, tile=ROW_TILE):
        self.batch, self.t, self.decode = batch, t, decode
        self.rows = batch * t
        self.tm = min(tile, self.rows)
        assert self.rows % self.tm == 0
        if decode:
            assert self.tm % batch == 0
        else:
            assert t % self.tm == 0
        self.steps = self.rows // self.tm


def _mod_spec(g, layer, k, d):
    if g.decode:
        return pl.BlockSpec((1, 1, g.batch, d), lambda i: (layer, k, 0, 0))
    per = g.t // g.tm
    return pl.BlockSpec((1, 1, 1, 1, d), lambda i: (layer, k, jnp.minimum(i, g.steps - 1) // per, 0, 0))


def _row_spec(g, width):
    if g.decode:
        assert g.steps == 1
        return pl.BlockSpec((g.tm, width), lambda i: (0, 0))
    return pl.BlockSpec((g.tm, width), lambda i: (jnp.minimum(i, g.steps - 1), 0))


def _ffn_block(x, sh_ref, sc_ref, gt_ref, wi_ref, wo_ref, lng_ref, lnb_ref, *, alpha, ln_idx, layer):
    tm, d = x.shape
    dff = wo_ref.shape[1]
    sh, sc, gt = _mod_rows(sh_ref, tm), _mod_rows(sc_ref, tm), _mod_rows(gt_ref, tm)
    h = _bf(x * (1.0 + sc) + sh)
    acc = jnp.zeros((tm, d), F32)
    off = 0
    while off < dff:
        fc = min(FF_CHUNK, dff - off)
        a = _dot(h, wi_ref[0, :, off:off + fc])
        b = _dot(h, wi_ref[0, :, dff + off:dff + off + fc])
        acc = acc + _dot(_bf(_silu(a) * b), wo_ref[0, off:off + fc, :])
        off += fc
    y = alpha * x + 0.5 * (1.0 + gt) * acc
    return _ln(y, lng_ref[layer, ln_idx:ln_idx + 1, :], lnb_ref[layer, ln_idx:ln_idx + 1, :])


def _mix_block(x, gt_ref, o_refs, w_ref, lng_ref, lnb_ref, *, alpha, layer, decode, batch):
    tm, d = x.shape
    gt = _mod_rows(gt_ref, tm)
    if decode:
        accs = [jnp.zeros((batch, d), F32) for _ in range(tm // batch)]
        for mi, o_ref in enumerate(o_refs):
            for h in range(HEADS):
                r0 = mi * HW + h * HEAD_DIM
                wh = w_ref[0, r0:r0 + HEAD_DIM, :]
                for t in range(tm // batch):
                    accs[t] = accs[t] + _dot_tn(_bf(o_ref[h, t]), wh)
        acc = jnp.concatenate(accs, axis=0)
    else:
        acc = jnp.zeros((tm, d), F32)
        for mi, o_ref in enumerate(o_refs):
            acc = acc + _dot(o_ref[...], w_ref[0, mi * HW:(mi + 1) * HW, :])
    y = alpha * x + (1.0 + gt) * acc
    return _ln(y, lng_ref[layer, 1:2, :], lnb_ref[layer, 1:2, :])


def _ffn_kernel(xp_ref, xs_ref, *refs, n_prompt, **kw):
    mod_p, mod_s = refs[0:3], refs[3:6]
    wi_ref, wo_ref, lng_ref, lnb_ref, yp_ref, ys_ref = refs[6:]
    i = pl.program_id(0)

    @pl.when(i < n_prompt)
    def _():
        yp_ref[...] = _ffn_block(xp_ref[...], *mod_p, wi_ref, wo_ref, lng_ref, lnb_ref, **kw)

    @pl.when(i == n_prompt)
    def _():
        ys_ref[...] = _ffn_block(xs_ref[...], *mod_s, wi_ref, wo_ref, lng_ref, lnb_ref, **kw)


def _mix_ffn_kernel(*refs, n_prompt, alpha, layer, batch):
    xp_ref, gtp_ref, xs_ref, gts_ref = refs[0:4]
    op_refs, os_refs, w_ref = refs[4:8], refs[8:12], refs[12]
    mod_p, mod_s = refs[13:16], refs[16:19]
    wi_ref, wo_ref, lng_ref, lnb_ref, yp_ref, ys_ref = refs[19:]

    def run(x_ref, gtm_ref, o_refs, mod, y_ref, decode):
        x1 = _mix_block(x_ref[...], gtm_ref, o_refs, w_ref, lng_ref, lnb_ref, alpha=alpha, layer=layer,
                        decode=decode, batch=batch)
        y_ref[...] = _ffn_block(x1, *mod, wi_ref, wo_ref, lng_ref, lnb_ref, alpha=alpha, ln_idx=2, layer=layer)

    i = pl.program_id(0)

    @pl.when(i < n_prompt)
    def _():
        run(xp_ref, gtp_ref, op_refs, mod_p, yp_ref, False)

    @pl.when(i == n_prompt)
    def _():
        run(xs_ref, gts_ref, os_refs, mod_s, ys_ref, True)


def _ffn_specs(gp, gs, layer, mods, d, dff, ln_g, ln_b):
    return [_mod_spec(g, layer, k, d) for g in (gp, gs) for k in mods] + [
        pl.BlockSpec((1, d, 2 * dff), lambda i: (layer, 0, 0), pipeline_mode=pl.Buffered(1)),
        pl.BlockSpec((1, dff, d), lambda i: (layer, 0, 0), pipeline_mode=pl.Buffered(1)),
        _const_spec(ln_g), _const_spec(ln_b)]


def _ffn(gp, gs, xp, xs, mod_p, mod_s, wi, wo, ln_g, ln_b, layer, alpha):
    d = xp.shape[1]
    rows = [_row_spec(gp, d), _row_spec(gs, d)]
    return pl.pallas_call(
        functools.partial(_ffn_kernel, n_prompt=gp.steps, alpha=alpha, ln_idx=0, layer=layer),
        grid=(gp.steps + 1,),
        in_specs=rows + _ffn_specs(gp, gs, layer, (0, 1, 2), d, wo.shape[1], ln_g, ln_b),
        out_specs=rows,
        out_shape=[jax.ShapeDtypeStruct(xp.shape, F32), jax.ShapeDtypeStruct(xs.shape, F32)],
        compiler_params=_params(("arbitrary",)),
        name="ffn0",
    )(xp, xs, *(mod_p,) * 3, *(mod_s,) * 3, wi, wo, ln_g, ln_b)


def _mix_ffn(gp, gs, xp, xs, mod_p, mod_s, outs_p, outs_s, w_out, wi, wo, ln_g, ln_b, layer, alpha):
    d = xp.shape[1]
    rows = [_row_spec(gp, d), _row_spec(gs, d)]
    return pl.pallas_call(
        functools.partial(_mix_ffn_kernel, n_prompt=gp.steps, alpha=alpha, layer=layer, batch=gs.batch),
        grid=(gp.steps + 1,),
        in_specs=[rows[0], _mod_spec(gp, layer, 5, d), rows[1], _mod_spec(gs, layer, 5, d)]
        + [_row_spec(gp, HW)] * 4 + [_const_spec(outs_s[0])] * 4
        + [pl.BlockSpec((1, w_out.shape[1], d), lambda i: (layer, 0, 0), pipeline_mode=pl.Buffered(1))]
        + _ffn_specs(gp, gs, layer, (6, 7, 8), d, wo.shape[1], ln_g, ln_b),
        out_specs=rows,
        out_shape=[jax.ShapeDtypeStruct(xp.shape, F32), jax.ShapeDtypeStruct(xs.shape, F32)],
        compiler_params=_params(("arbitrary",)),
        name="mix_ffn2",
    )(xp, mod_p, xs, mod_s, *outs_p, *outs_s, w_out, *(mod_p,) * 3, *(mod_s,) * 3, wi, wo, ln_g, ln_b)


def _prep_kernel(*refs, layer, nlayers, g_t, tm, decode, batch):
    (x_ref, sh_ref, sc_ref, w_ref, cos_ref, sin_ref, wg_ref, bg_ref, lb_ref, cw_ref, alog_ref, dtb_ref,
     bd_ref, ex_ref) = refs[:14]
    rest = refs[14:]
    if decode:
        cs_ref, alogc_ref, dtbc_ref = rest[:3]
        outs = rest[3:]
        cbuf = None
    else:
        outs = rest[:-1]
        cbuf = rest[-1]

    x = x_ref[...]
    hb = _bf(x * (1.0 + _mod_rows(sc_ref, tm)) + _mod_rows(sh_ref, tm))

    def proj(name):
        o, w = _OFF[name]
        return _dot(hb, w_ref[0, :, o:o + w])

    bd = bd_ref[...]

    dest = {}
    if decode:
        it = iter(outs)
        for mix, keys in (("ret", ("q", "k", "v", "gate")), ("gla", ("q", "k", "g", "v", "gate")),
                          ("hg", ("q", "k", "g", "v", "gate")), ("gdn", ("q", "k", "v", "gate"))):
            for kk in keys:
                dest[f"{mix}_{kk}"] = ("heads", next(it), 0)
        bgo_ref, convo_ref = next(it), next(it)
    else:
        ret_ref, gla_ref, hg_ref, gdn_ref, conv_ref = outs
        for dst, keys, widths in ((ret_ref, ("ret_q", "ret_k", "ret_v", "ret_gate"), (HW,) * 4),
                                  (gla_ref, ("gla_q", "gla_k", "gla_g", "gla_v", "gla_gate"),
                                   (HEADS * GLA_DK,) * 3 + (HW, HW)),
                                  (hg_ref, ("hg_q", "hg_k", "hg_g", "hg_v", "hg_gate"), (HW,) * 5),
                                  (gdn_ref, ("gdn_q", "gdn_k", "gdn_b", "gdn_g", "gdn_v", "gdn_gate"), (HW,) * 6)):
            off = 0
            for kk, w in zip(keys, widths):
                dest[kk] = ("rows", dst, off)
                off += w

    def put(name, val):
        kind, ref, off = dest[name]
        if kind == "rows":
            ref[:, off:off + val.shape[1]] = val
        else:
            dh = val.shape[1] // HEADS
            for t in range(tm // batch):
                vt = val[t * batch:(t + 1) * batch].T
                for h in range(HEADS):
                    ref[h, t] = vt[h * dh:(h + 1) * dh]

    small = proj("small")

    dq = proj("dqkv")
    cw = cw_ref[0]
    if decode:
        cs = cs_ref[0]
        nt = tm // batch
        blocks = [dq[t * batch:(t + 1) * batch] for t in range(nt)]
        convs = []
        for t in range(nt):
            acc = blocks[t] * cw[CONV_W - 1:CONV_W, :]
            for s in range(1, CONV_W):
                prev = blocks[t - s] if t >= s else cs[CONV_W - 1 - s + t]
                acc = acc + prev * cw[CONV_W - 1 - s:CONV_W - s, :]
            convs.append(acc)
        conv = jnp.concatenate(convs, axis=0)
        convo_ref[...] = dq[(nt - (CONV_W - 1)) * batch:].reshape(CONV_W - 1, batch, dq.shape[1])
    else:
        i = pl.program_id(0)
        per = g_t // tm

        halo = V7X_SUBLANES

        @pl.when(i % per == 0)
        def _():
            cbuf[0:halo, :] = jnp.zeros((halo, dq.shape[1]), F32)

        cbuf[halo:halo + tm, :] = dq
        conv = cbuf[pl.ds(halo, tm), :] * cw[CONV_W - 1:CONV_W, :]
        for s in range(1, CONV_W):
            conv = conv + cbuf[pl.ds(halo - s, tm), :] * cw[CONV_W - 1 - s:CONV_W - s, :]
        tail = cbuf[tm:tm + halo, :]
        cbuf[0:halo, :] = tail

        @pl.when(i % per == per - 1)
        def _():
            conv_ref[0] = tail[halo - (CONV_W - 1):halo, :]

    u = _silu(conv)
    uq, uk = u[:, 0:HW], u[:, HW:2 * HW]
    put("gdn_q", uq * lax.rsqrt(_dot(_bf(uq * uq), bd) + RMS_EPS) * (HEAD_DIM ** -0.5))
    put("gdn_k", uk * lax.rsqrt(_dot(_bf(uk * uk), bd) + RMS_EPS))
    put("gdn_v", u[:, 2 * HW:3 * HW])
    put("gdn_gate", _silu(proj("dg")))
    if decode:
        for t in range(tm // batch):
            smt = small[t * batch:(t + 1) * batch].T
            beta = jax.nn.sigmoid(smt[SMALL_DB:SMALL_DB + HEADS])
            gd = -jnp.exp(alogc_ref[layer]) * _softplus(smt[SMALL_DA:SMALL_DA + HEADS] + dtbc_ref[layer])
            for h in range(HEADS):
                bgo_ref[h, t, 0:1, :] = beta[h:h + 1]
                bgo_ref[h, t, 1:2, :] = gd[h:h + 1]
    else:
        dbda = _dot_x01(small, ex_ref[...])
        put("gdn_b", jax.nn.sigmoid(dbda[:, 0:HW]))
        put("gdn_g", -jnp.exp(alog_ref[layer:layer + 1, :]) * _softplus(dbda[:, HW:2 * HW] + dtb_ref[layer:layer + 1, :]))

    rows = [lb_ref[i:i + 1, :] for i in range(nlayers)]
    mx = functools.reduce(jnp.maximum, rows)
    ex = [jnp.exp(r - mx) for r in rows]
    tot = functools.reduce(lambda a, b: a + b, ex)
    plb = [e / tot for e in ex]
    lb = functools.reduce(lambda a, b: a + b, plb[:layer + 1]) - plb[0]
    zf = proj("hf")
    put("hg_q", _silu(proj("hq")) * (HEAD_DIM ** -0.5))
    put("hg_k", (1.0 - lb) * jax.nn.sigmoid(-zf))
    put("hg_g", jnp.log(lb + (1.0 - lb) * jax.nn.sigmoid(zf)))
    put("hg_v", proj("hi"))
    put("hg_gate", _silu(proj("hg")))

    gpre = _dot(_bf(small), wg_ref[0]) + bg_ref[layer:layer + 1, :]
    put("gla_q", proj("aq") * (GLA_DK ** -0.5))
    put("gla_k", proj("ak"))
    put("gla_g", -_softplus(-gpre) * (1.0 / GLA_GATE_NORM))
    put("gla_v", proj("av"))
    put("gla_gate", _silu(proj("ag")))

    cos, sin = cos_ref[...], sin_ref[...]
    lane = lax.broadcasted_iota(jnp.int32, (tm, HW), 1)
    first_half = (lane % HEAD_DIM) < (HEAD_DIM // 2)

    def rope(v):
        sw = jnp.where(first_half, pltpu.roll(v, HW - HEAD_DIM // 2, 1), pltpu.roll(v, HEAD_DIM // 2, 1))
        return v * cos + sw * sin

    put("ret_q", rope(proj("rq")))
    put("ret_k", rope(proj("rk")) * (HEAD_DIM ** -0.5))
    put("ret_v", proj("rv"))
    put("ret_gate", _silu(proj("rg")))


def _prep(g, x, mod, w_in_p, cos_t, sin_t, wg_p, gla_bg, hg_lb, gdn_conv, alog_e, dtb_e, dec, layer, nlayers):
    d = x.shape[1]
    bd = jnp.asarray(_bd_ones(), BF16)
    exm = jnp.asarray(_expand_mat(), BF16)
    row = pl.BlockSpec((g.tm, d), lambda i: (i, 0))
    if g.decode:
        tab = pl.BlockSpec((g.tm, HW), lambda i: (i, 0))
    else:
        per = g.t // g.tm
        tab = pl.BlockSpec((g.tm, HW), lambda i: (i % per, 0))
    in_specs = [row, _mod_spec(g, layer, 3, d), _mod_spec(g, layer, 4, d),
                pl.BlockSpec((1, d, PROJ_W), lambda i: (layer, 0, 0), pipeline_mode=pl.Buffered(1)),
                tab, tab,
                pl.BlockSpec((1,) + wg_p.shape[1:], lambda i: (layer, 0, 0)),
                _const_spec(gla_bg), _const_spec(hg_lb),
                pl.BlockSpec((1,) + gdn_conv.shape[1:], lambda i: (layer, 0, 0)),
                _const_spec(alog_e), _const_spec(dtb_e), _const_spec(bd), _const_spec(exm)]
    args = [x, mod, mod, w_in_p, cos_t, sin_t, wg_p, gla_bg, hg_lb, gdn_conv, alog_e, dtb_e, bd, exm]
    cch = gdn_conv.shape[2]

    def rows_out(w):
        return jax.ShapeDtypeStruct((g.rows, w), F32), pl.BlockSpec((g.tm, w), lambda i: (i, 0))

    def whole(shape):
        return jax.ShapeDtypeStruct(shape, F32), pl.BlockSpec(shape, lambda i: (0,) * len(shape))

    if g.decode:
        assert g.steps == 1 and g.t >= CONV_W - 1
        cs, alog_c, dtb_c = dec
        in_specs += [pl.BlockSpec((1,) + cs.shape[1:], lambda i: (layer, 0, 0, 0)), _const_spec(alog_c),
                     _const_spec(dtb_c)]
        args += [cs, alog_c, dtb_c]
        outs = []
        for dhs in ((HEAD_DIM,) * 4, (GLA_DK,) * 3 + (HEAD_DIM,) * 2, (HEAD_DIM,) * 5, (HEAD_DIM,) * 4):
            outs += [whole((HEADS, g.t, dh, g.batch)) for dh in dhs]
        outs += [whole((HEADS, g.t, 2, g.batch)), whole((CONV_W - 1, g.batch, cch))]
        scratch = []
    else:
        outs = [rows_out(4 * HW), rows_out(3 * HEADS * GLA_DK + 2 * HW), rows_out(5 * HW), rows_out(6 * HW),
                (jax.ShapeDtypeStruct((g.batch, CONV_W - 1, cch), F32),
                 pl.BlockSpec((1, CONV_W - 1, cch), lambda i: (i // (g.t // g.tm), 0, 0)))]
        scratch = [pltpu.VMEM((g.tm + V7X_SUBLANES, cch), F32)]
    return pl.pallas_call(
        functools.partial(_prep_kernel, layer=layer, nlayers=nlayers, g_t=g.t, tm=g.tm, decode=g.decode,
                          batch=g.batch),
        grid=(g.steps,),
        in_specs=in_specs,
        out_specs=[o[1] for o in outs],
        out_shape=[o[0] for o in outs],
        scratch_shapes=scratch,
        compiler_params=_params(("arbitrary",)),
        name="prep",
    )(*args)


def _chunk(x, c):
    return x[c * CHUNK:(c + 1) * CHUNK]


def _per_chunk(fn, nch):
    return jnp.concatenate([fn(c) for c in range(nch)], axis=0)


def _finish_prompt(o, gate, normw, bd, o_ref):
    ms = _dot(_bf(o * o), bd) * (1.0 / HEAD_DIM)
    on = o * lax.rsqrt(ms + RMS_EPS)
    if normw is not None:
        on = on * normw
    o_ref[...] = _bf(on * gate)


def _state_out(s_t, fold_ref, st_ref):
    fold = fold_ref[...]
    h1, h2, h3 = _split3(s_t)
    st_ref[0] = _dot_tn(h1, fold) + _dot_tn(h2, fold) + _dot_tn(h3, fold)


def _gla_prompt_kernel(f_ref, kmask_ref, vmask_ref, smask_ref, lvl_ref, fold_ref, bd_ref, nw_ref, ratt_ref, rq_ref,
                       rk_ref, rs_ref, o_ref, st_ref, s_scr, *, dk, nch, ret, levels, has_norm, layer):
    j = pl.program_id(1)
    dkk = HEADS * dk

    @pl.when(j == 0)
    def _():
        s_scr[...] = jnp.zeros(s_scr.shape, F32)

    kmask, vmask, smask, bd = kmask_ref[...], vmask_ref[...], smask_ref[...], bd_ref[...]
    normw = nw_ref[layer:layer + 1, :] if has_norm else None

    def mm(x, y):
        xb, yb = _bf(x), _bf(y)
        return _per_chunk(lambda c: _dot_nt(_chunk(xb, c), _tile4(_chunk(yb, c)) * kmask), nch)

    q = f_ref[:, 0:dkk]
    k = f_ref[:, dkk:2 * dkk]
    if ret:
        v = f_ref[:, 2 * dkk:2 * dkk + HW]
        gate = f_ref[:, 2 * dkk + HW:2 * dkk + 2 * HW]
        att = _per_block(mm(q, k), ratt_ref[...], CHUNK)
        qg = _per_block(q, rq_ref[...], CHUNK)
        kd = _per_block(k, rk_ref[...], CHUNK)
        sdec = [rs_ref[...]] * nch
    else:
        g = f_ref[:, 2 * dkk:3 * dkk]
        v = f_ref[:, 3 * dkk:3 * dkk + HW]
        gate = f_ref[:, 3 * dkk + HW:3 * dkk + 2 * HW]
        G = _cumsum_rows(g, CHUNK)
        att = _decay_att(q, k, G, mm, lambda x, y: _dot_nt(_bf(x * y), kmask), lvl_ref, levels, CHUNK)
        qg = q * jnp.exp(G)
        glb = _block_row(G, CHUNK, CHUNK - 1)
        kd = k * jnp.exp(glb - G)
        sdec = [jnp.exp(G[(c + 1) * CHUNK - 1:(c + 1) * CHUNK, :]) for c in range(nch)]
    vb, qgb, kdb, attb = _bf(v), _bf(qg), _bf(kd), _bf(att)
    o_intra = _per_chunk(lambda c: _dot(_chunk(attb, c), _tile4(_chunk(vb, c)) * vmask), nch)
    upd = [_dot_tn(_chunk(vb, c), _chunk(kdb, c)) * smask for c in range(nch)]

    s = s_scr[...]
    o_inter = []
    for c in range(nch):
        o_inter.append(_dot_nt(_chunk(qgb, c), _bf(s)))
        s = s * sdec[c] + upd[c]
    s_scr[...] = s
    _finish_prompt(o_intra + jnp.concatenate(o_inter, axis=0), gate, normw, bd, o_ref)

    @pl.when(j == pl.num_programs(1) - 1)
    def _():
        _state_out(s, fold_ref, st_ref)


def _gla_prompt(g, feat, dk, normw, layer, ret):
    c = _chunk_consts(dk)
    dkk = HEADS * dk
    blk = min(MIX_TILE, g.t)
    nch = blk // CHUNK
    per = g.t // blk
    consts = [jnp.asarray(c["kmask"], BF16), jnp.asarray(c["vmask"], BF16), jnp.asarray(c["smask"]),
              jnp.asarray(c["lvl"]), jnp.asarray(c["fold"], BF16), jnp.asarray(_bd_ones(), BF16),
              normw if normw is not None else jnp.ones((1, HW), F32),
              jnp.asarray(c["ret_att"]), jnp.asarray(c["ret_q"]), jnp.asarray(c["ret_k"]), jnp.asarray(c["ret_s"])]
    w = feat.shape[1]
    o, st = pl.pallas_call(
        functools.partial(_gla_prompt_kernel, dk=dk, nch=nch, ret=ret, levels=c["levels"],
                          has_norm=normw is not None, layer=layer),
        grid=(g.batch, per),
        in_specs=[pl.BlockSpec((blk, w), lambda b, j: (b * per + j, 0))] + [_const_spec(a) for a in consts],
        out_specs=[pl.BlockSpec((blk, HW), lambda b, j: (b * per + j, 0)),
                   pl.BlockSpec((1, dkk, HEAD_DIM), lambda b, j: (b, 0, 0))],
        out_shape=[jax.ShapeDtypeStruct((g.rows, HW), BF16),
                   jax.ShapeDtypeStruct((g.batch, dkk, HEAD_DIM), F32)],
        scratch_shapes=[pltpu.VMEM((HW, dkk), F32)],
        compiler_params=_params(("arbitrary", "arbitrary")),
        name="ret_prompt" if ret else f"gla_prompt_dk{dk}",
    )(feat, *consts)
    return o, st.reshape(g.batch, HEADS, dk, HEAD_DIM)


def _gdn_prompt_kernel(f_ref, bdm_ref, smask_ref, incl_ref, strict_ref, eye_ref, fold_ref, bd_ref, nw_ref, o_ref,
                       st_ref, s_scr, *, nch, layer):
    j = pl.program_id(1)

    @pl.when(j == 0)
    def _():
        s_scr[...] = jnp.zeros(s_scr.shape, F32)

    bdm, smask, incl, strict, eye, bd = (bdm_ref[...], smask_ref[...], incl_ref[...], strict_ref[...], eye_ref[...],
                                         bd_ref[...])
    normw = nw_ref[layer:layer + 1, :]
    ones_c = jnp.ones((CHUNK, CHUNK), BF16)

    def bdw(yb, c):
        return _tile4(_chunk(yb, c)) * bdm

    def prod1(x, y):
        xb, yb = _bf(x), _bf(y)
        return _per_chunk(lambda c: _dot(_chunk(xb, c), bdw(yb, c)), nch)

    def prod1_pair(x1, x2, y):
        x1b, x2b, yb = _bf(x1), _bf(x2), _bf(y)
        both = [_dot(jnp.concatenate([_chunk(x1b, c), _chunk(x2b, c)], axis=0), bdw(yb, c)) for c in range(nch)]
        return (jnp.concatenate([m[:CHUNK] for m in both], axis=0),
                jnp.concatenate([m[CHUNK:] for m in both], axis=0))

    def prod2(x, y):
        xh, xl = _split2(x)
        yb = _bf(y)

        def one(c):
            top = _dot(jnp.concatenate([_chunk(xh, c), _chunk(xl, c)], axis=0), bdw(yb, c))
            return top[:CHUNK] + top[CHUNK:]

        return _per_chunk(one, nch)

    def prod3(x, y):
        xh, xl = _split2(x)
        yh, yl = _split2(y)

        def one(c):
            top = _dot(jnp.concatenate([_chunk(xh, c), _chunk(xl, c)], axis=0), bdw(yh, c))
            return top[:CHUNK] + top[CHUNK:] + _dot(_chunk(xh, c), bdw(yl, c))

        return _per_chunk(one, nch)

    q, k = f_ref[:, 0:HW], f_ref[:, HW:2 * HW]
    be, ge = f_ref[:, 2 * HW:3 * HW], f_ref[:, 3 * HW:4 * HW]
    v, gate = f_ref[:, 4 * HW:5 * HW], f_ref[:, 5 * HW:6 * HW]
    G = _cumsum_rows(ge, CHUNK)
    zd = _per_block(G, eye, CHUNK)
    grow = _per_chunk(lambda c: _dot_exact01(ones_c, _chunk(zd, c)), nch)
    L = _per_block(jnp.exp(jnp.minimum(G - grow, 0.0)), incl, CHUNK)
    qb, kb = _bf(q), _bf(k)

    def qkk(c):
        return _dot_nt(jnp.concatenate([_chunk(qb, c), _chunk(kb, c)], axis=0), bdw(kb, c))

    qkk_all = [qkk(c) for c in range(nch)]
    qk = jnp.concatenate([m[:CHUNK] for m in qkk_all], axis=0)
    kk = jnp.concatenate([m[CHUNK:] for m in qkk_all], axis=0)
    a = _per_block(be * kk * L, strict, CHUNK)
    eye_b = _per_block(jnp.ones_like(a), eye, CHUNK)
    p = -a
    tinv = eye_b + p
    for i in range(5):
        if i == 0:
            p = prod1(p, p)
        else:
            p, dt = prod1_pair(p, tinv, p)
            tinv = tinv + dt
    tinv = tinv + prod1(tinv, p)
    resid = eye_b - tinv - prod3(a, tinv)
    tinv = tinv + prod1(tinv, resid)
    eg = jnp.exp(G)
    u = prod2(tinv, be * v)
    w = prod2(tinv, be * eg * k)
    glb = _block_row(G, CHUNK, CHUNK - 1)
    wb, qeb, qklb = _bf(w), _bf(q * eg), _bf(qk * L)
    kdb = _bf(k * jnp.exp(glb - G))
    sdec = [jnp.exp(G[(c + 1) * CHUNK - 1:(c + 1) * CHUNK, :]) for c in range(nch)]

    ub = _bf(u)
    s_add = [_dot_tn(_chunk(ub, c), _chunk(kdb, c)) * smask for c in range(nch)]
    s_mul = [_bf(_dot_tn(_chunk(wb, c), _chunk(kdb, c)) * smask) for c in range(nch)]

    s = s_scr[...]
    snaps = []
    for c in range(nch):
        sb = _bf(s)
        snaps.append(sb)
        s = s * sdec[c] + s_add[c] - _dot(sb, s_mul[c])
    s_scr[...] = s

    outs = []
    for c in range(nch):
        ws = _dot_nt(jnp.concatenate([_chunk(wb, c), _chunk(qeb, c)], axis=0), snaps[c])
        db = _bf(_chunk(u, c) - ws[:CHUNK])
        outs.append(_dot(_chunk(qklb, c), _tile4(db) * bdm) + ws[CHUNK:])
    _finish_prompt(jnp.concatenate(outs, axis=0), gate, normw, bd, o_ref)

    @pl.when(j == pl.num_programs(1) - 1)
    def _():
        _state_out(s, fold_ref, st_ref)


def _gdn_prompt(g, feat, normw, layer):
    c = _chunk_consts(HEAD_DIM)
    blk = min(MIX_TILE, g.t)
    nch = blk // CHUNK
    per = g.t // blk
    consts = [jnp.asarray(c["vmask"], BF16), jnp.asarray(c["smask"]), jnp.asarray(c["incl"]), jnp.asarray(c["strict"]),
              jnp.asarray(c["eye"]), jnp.asarray(c["fold"], BF16), jnp.asarray(_bd_ones(), BF16), normw]
    w = feat.shape[1]
    o, st = pl.pallas_call(
        functools.partial(_gdn_prompt_kernel, nch=nch, layer=layer),
        grid=(g.batch, per),
        in_specs=[pl.BlockSpec((blk, w), lambda b, j: (b * per + j, 0))] + [_const_spec(a) for a in consts],
        out_specs=[pl.BlockSpec((blk, HW), lambda b, j: (b * per + j, 0)),
                   pl.BlockSpec((1, HW, HEAD_DIM), lambda b, j: (b, 0, 0))],
        out_shape=[jax.ShapeDtypeStruct((g.rows, HW), BF16),
                   jax.ShapeDtypeStruct((g.batch, HW, HEAD_DIM), F32)],
        scratch_shapes=[pltpu.VMEM((HW, HW), F32)],
        compiler_params=_params(("arbitrary", "arbitrary")),
        name="gdn_prompt",
    )(feat, *consts)
    return o, st.reshape(g.batch, HEADS, HEAD_DIM, HEAD_DIM)


def _state_io(refs, layer):
    if layer == 0:
        return refs
    prev_ref, rest = refs[0], refs[1:]
    rest[-1][0:layer] = prev_ref[...]
    return rest


def _dec_finish(o, gate, normw):
    ms = jnp.mean(o * o, axis=0, keepdims=True)
    on = o * lax.rsqrt(ms + RMS_EPS)
    if normw is not None:
        on = on * normw
    return on * gate


def _gla_decode_kernel(q_ref, k_ref, g_ref, v_ref, gate_ref, s_ref, *refs, nt, dk, ret, has_norm, layer):
    nw_ref, o_ref, so_ref = _state_io(refs, layer)
    normw = nw_ref[layer] if has_norm else None
    qs = [q_ref[0, t] for t in range(nt)]
    ks = [k_ref[0, t] for t in range(nt)]
    vs = [v_ref[0, t] for t in range(nt)]
    decay = [jnp.exp(g_ref[0, t]) for t in range(nt)]
    o_acc = [jnp.zeros(vs[0].shape, F32) for _ in range(nt)]
    for dc in range(dk // DEC_DCHUNK):
        r = slice(dc * DEC_DCHUNK, (dc + 1) * DEC_DCHUNK)
        sc = s_ref[0, 0, r]
        for t in range(nt):
            a = decay[t][None] if ret else decay[t][r][:, None, :]
            sc = sc * a + ks[t][r][:, None, :] * vs[t][None, :, :]
            o_acc[t] = o_acc[t] + jnp.sum(qs[t][r][:, None, :] * sc, axis=0)
        so_ref[layer, 0, r] = sc
    for t in range(nt):
        o_ref[0, t] = _dec_finish(o_acc[t], gate_ref[0, t], normw)


def _gdn_decode_kernel(q_ref, k_ref, v_ref, gate_ref, bg_ref, s_ref, *refs, nt, layer):
    nw_ref, o_ref, so_ref = _state_io(refs, layer)
    normw = nw_ref[layer]
    chunks = [slice(dc * DEC_DCHUNK, (dc + 1) * DEC_DCHUNK) for dc in range(HEAD_DIM // DEC_DCHUNK)]
    for t in range(nt):
        q, k, v = q_ref[0, t], k_ref[0, t], v_ref[0, t]
        beta = bg_ref[0, t, 0:1, :]
        ag = jnp.exp(bg_ref[0, t, 1:2, :])

        def state(r):
            return s_ref[0, 0, r] if t == 0 else so_ref[layer, 0, r]

        ks = jnp.zeros(v.shape, F32)
        for r in chunks:
            ks = ks + jnp.sum(k[r][:, None, :] * state(r), axis=0)
        delta = beta * (v - ag * ks)
        o = jnp.zeros(v.shape, F32)
        for r in chunks:
            sc = state(r) * ag[None] + k[r][:, None, :] * delta[None, :, :]
            so_ref[layer, 0, r] = sc
            o = o + jnp.sum(q[r][:, None, :] * sc, axis=0)
        o_ref[0, t] = _dec_finish(o, gate_ref[0, t], normw)


def _decode_call(body, name, feats, state_t, prev, extra, dk, layer):
    nt, batch = feats[0].shape[1], feats[0].shape[3]

    def head_spec(a):
        return pl.BlockSpec((1,) + a.shape[1:], lambda h: (h,) + (0,) * (a.ndim - 1))

    def st_spec(nl):
        return pl.BlockSpec((nl, 1, dk, HEAD_DIM, batch), lambda h: (0, h, 0, 0, 0))

    in_specs = [head_spec(a) for a in feats]
    in_specs.append(pl.BlockSpec((1, 1, dk, HEAD_DIM, batch), lambda h: (layer, h, 0, 0, 0)))
    args = list(feats) + [state_t]
    if layer > 0:
        in_specs.append(st_spec(layer))
        args.append(prev)
    o_shape = (HEADS, nt, HEAD_DIM, batch)
    return pl.pallas_call(
        body,
        grid=(HEADS,),
        in_specs=in_specs + [_const_spec(a) for a in extra],
        out_specs=[pl.BlockSpec((1,) + o_shape[1:], lambda h: (h, 0, 0, 0)), st_spec(layer + 1)],
        out_shape=[jax.ShapeDtypeStruct(o_shape, F32),
                   jax.ShapeDtypeStruct((layer + 1, HEADS, dk, HEAD_DIM, batch), F32)],
        compiler_params=_params(("arbitrary",)),
        name=name,
    )(*args, *extra)


def _gla_decode(q, k, g, v, gate, state_t, prev, dk, normw_col, layer, ret):
    nt = q.shape[1]
    body = functools.partial(_gla_decode_kernel, nt=nt, dk=dk, ret=ret, has_norm=normw_col is not None, layer=layer)
    nw = normw_col if normw_col is not None else jnp.ones((layer + 1, HEAD_DIM, 1), F32)
    return _decode_call(body, "ret_decode" if ret else f"gla_decode_dk{dk}", [q, k, g, v, gate], state_t, prev, [nw],
                        dk, layer)


def _gdn_decode(q, k, v, gate, bg, state_t, prev, normw_col, layer):
    body = functools.partial(_gdn_decode_kernel, nt=q.shape[1], layer=layer)
    return _decode_call(body, "gdn_decode", [q, k, v, gate, bg], state_t, prev, [normw_col], HEAD_DIM, layer)


def _rope_tables(pos):
    half = HEAD_DIM // 2
    inv = ROPE_BASE ** (-jnp.arange(half, dtype=F32) / half)
    ang = pos[:, None] * inv[None, :]
    cos, sin = jnp.cos(ang), jnp.sin(ang)
    cos_t = jnp.tile(jnp.concatenate([cos, cos], axis=1), (1, HEADS))
    sin_t = jnp.tile(jnp.concatenate([-sin, sin], axis=1), (1, HEADS))
    return cos_t, sin_t


def _relayout_w_in(w_in):
    def cols(name):
        o, w = _SRC[name]
        return w_in[:, :, o:o + w]

    parts = []
    for name, w in _SLABS:
        if name == "small":
            sm = jnp.concatenate([cols(n) for n in _NARROW], axis=2)
            parts.append(jnp.pad(sm, ((0, 0), (0, 0), (0, w - sm.shape[2]))))
        else:
            parts.append(cols(name))
    return jnp.concatenate(parts, axis=2).astype(BF16)


def kernel(x_prompt, x_sample, state_ret, state_gla, state_hgrn, state_gdn, state_gdn_conv, c_prompt, c_sample, ada_w, ada_b, ln_g, ln_b, ffn1_wi, ffn1_wo, ffn2_wi, ffn2_wo, w_in, gla_wg, gla_bg, hg_lb, gdn_conv, gdn_a_log, gdn_dt_bias, gla_norm, hg_norm, gdn_norm, w_out):
    nl, d = ada_w.shape[0], ada_w.shape[1]
    bp, tp = x_prompt.shape[0], x_prompt.shape[1]
    bs, ts = x_sample.shape[0], x_sample.shape[1]
    alpha = (2.0 * nl) ** 0.25
    assert tp % CHUNK == 0 and ts >= CONV_W - 1

    wi1, wo1, wi2, wo2 = (w.astype(BF16) for w in (ffn1_wi, ffn1_wo, ffn2_wi, ffn2_wo))
    w_out_b = w_out.astype(BF16)
    w_in_p = _relayout_w_in(w_in)
    wg_p = jnp.pad(gla_wg, ((0, 0), (0, _OFF["small"][1] - gla_wg.shape[1]), (0, 0))).astype(BF16)
    alog_e = jnp.repeat(gdn_a_log, HEAD_DIM, axis=1)
    dtb_e = jnp.repeat(gdn_dt_bias, HEAD_DIM, axis=1)
    gla_nw = jnp.tile(gla_norm, (1, HEADS))
    hg_nw = jnp.tile(hg_norm, (1, HEADS))
    gdn_nw = jnp.tile(gdn_norm, (1, HEADS))

    mod_p, mod_s = _ada(c_prompt, c_sample, ada_w, ada_b)

    gp = _Rows(bp, tp, decode=False)
    gp_prep = _Rows(bp, tp, decode=False, tile=PREP_TILE)
    gs = _Rows(bs, ts, decode=True)
    assert gs.steps == 1
    cos_p, sin_p = _rope_tables(jnp.arange(tp, dtype=F32))
    cos_s, sin_s = _rope_tables(PAST_LEN + jnp.arange(ts, dtype=F32))
    cos_s, sin_s = jnp.repeat(cos_s, bs, axis=0), jnp.repeat(sin_s, bs, axis=0)

    xp = x_prompt.reshape(bp * tp, d)
    xs = jnp.transpose(x_sample, (1, 0, 2)).reshape(ts * bs, d)

    st_ret, st_gla, st_hg, st_gdn = (jnp.transpose(s, (0, 2, 3, 4, 1))
                                     for s in (state_ret, state_gla, state_hgrn, state_gdn))
    conv_t = jnp.transpose(state_gdn_conv, (0, 2, 1, 3))
    dec_extra = (conv_t, gdn_a_log[:, :, None], gdn_dt_bias[:, :, None])
    ret_g = jnp.asarray(np.broadcast_to(_ret_gamma().astype(np.float32)[:, None, None, None], (HEADS, ts, 1, bs)))
    gla_nc, hg_nc, gdn_nc = gla_norm[:, :, None], hg_norm[:, :, None], gdn_norm[:, :, None]

    p_states, s_convs = [], []
    s_ret = s_gla = s_hg = s_gdn = None
    for l in range(nl):
        xp, xs = _ffn(gp, gs, xp, xs, mod_p, mod_s, wi1, wo1, ln_g, ln_b, l, alpha)

        ret_f, gla_f, hg_f, gdn_f, p_conv = _prep(gp_prep, xp, mod_p, w_in_p, cos_p, sin_p, wg_p, gla_bg, hg_lb,
                                                  gdn_conv, alog_e, dtb_e, None, l, nl)
        o_ret, p_ret = _gla_prompt(gp, ret_f, HEAD_DIM, None, l, ret=True)
        o_gla, p_gla = _gla_prompt(gp, gla_f, GLA_DK, gla_nw, l, ret=False)
        o_hg, p_hg = _gla_prompt(gp, hg_f, HEAD_DIM, hg_nw, l, ret=False)
        o_gdn, p_gdn = _gdn_prompt(gp, gdn_f, gdn_nw, l)
        outs_p = (o_ret, o_gla, o_hg, o_gdn)
        p_states.append((p_ret, p_gla, p_hg, p_gdn, p_conv))

        (rq, rk, rv, rgt, aq, ak, ag, av, agt, hq, hk, hgg, hv, hgt, dq, dk_, dv, dgt, dbg, conv_new) = _prep(
            gs, xs, mod_s, w_in_p, cos_s, sin_s, wg_p, gla_bg, hg_lb, gdn_conv, alog_e, dtb_e, dec_extra, l, nl)
        o_ret, s_ret = _gla_decode(rq, rk, ret_g, rv, rgt, st_ret, s_ret, HEAD_DIM, None, l, True)
        o_gla, s_gla = _gla_decode(aq, ak, ag, av, agt, st_gla, s_gla, GLA_DK, gla_nc, l, False)
        o_hg, s_hg = _gla_decode(hq, hk, hgg, hv, hgt, st_hg, s_hg, HEAD_DIM, hg_nc, l, False)
        o_gdn, s_gdn = _gdn_decode(dq, dk_, dv, dgt, dbg, st_gdn, s_gdn, gdn_nc, l)
        xp, xs = _mix_ffn(gp, gs, xp, xs, mod_p, mod_s, outs_p, (o_ret, o_gla, o_hg, o_gdn), w_out_b, wi2, wo2,
                          ln_g, ln_b, l, alpha)
        s_convs.append(conv_new)

    y_prompt = xp.reshape(bp, tp, d)
    y_sample = jnp.transpose(xs.reshape(ts, bs, d), (1, 0, 2))
    p_out = tuple(jnp.stack([st[i] for st in p_states]) for i in range(5))
    s_out = tuple(jnp.transpose(s, (0, 4, 1, 2, 3)) for s in (s_ret, s_gla, s_hg, s_gdn))
    return (y_prompt, y_sample) + p_out + s_out + (jnp.transpose(jnp.stack(s_convs), (0, 2, 1, 3)),)
```

```python
import functools

import numpy as np
import jax
import jax.numpy as jnp
from jax import lax
from jax.experimental import pallas as pl
from jax.experimental.pallas import tpu as pltpu

F32 = jnp.float32
BF16 = jnp.bfloat16

HEAD_DIM = 64
HEADS = 4
GLA_DK = 32
GLA_GATE_RANK = 16
GLA_GATE_NORM = 16.0
CONV_W = 4
CHUNK = 64
ROPE_BASE = 10000.0
LN_EPS = 1e-5
RMS_EPS = 1e-6
PAST_LEN = 16384
N_MOD = 9
HW = HEADS * HEAD_DIM

V7X_LANES = 128
V7X_SUBLANES = 8
V7X_VMEM_BYTES = 64 * 1024 * 1024

ROW_TILE = 512
PREP_TILE = 512
MIX_TILE = 2048
DEC_DCHUNK = V7X_SUBLANES
FF_CHUNK = 256
VMEM_LIMIT = V7X_VMEM_BYTES - 12 * 1024 * 1024

_MIX_COLS = (("rq", HW), ("rk", HW), ("rv", HW), ("rg", HW),
             ("aq", HEADS * GLA_DK), ("ak", HEADS * GLA_DK), ("av", HW), ("alr", GLA_GATE_RANK), ("ag", HW),
             ("hq", HW), ("hf", HW), ("hi", HW), ("hg", HW),
             ("dqkv", 3 * HW), ("db", HEADS), ("da", HEADS), ("dg", HW))
_SRC = {}
_o = 0
for _n, _w in _MIX_COLS:
    _SRC[_n] = (_o, _w)
    _o += _w
_NARROW = ("alr", "db", "da")
_SLABS = tuple((n, w) for n, w in _MIX_COLS if n not in _NARROW) + (("small", V7X_LANES),)
_OFF = {}
_o = 0
for _n, _w in _SLABS:
    assert _o % V7X_LANES == 0
    _OFF[_n] = (_o, _w)
    _o += _w
PROJ_W = _o
SMALL_ALR, SMALL_DB, SMALL_DA = 0, GLA_GATE_RANK, GLA_GATE_RANK + HEADS


def _bf(x):
    return x.astype(BF16)


def _dot(a, b):
    return jnp.dot(a, b, preferred_element_type=F32)


def _dot_nt(a, b):
    return lax.dot_general(a, b, (((1,), (1,)), ((), ())), preferred_element_type=F32)


def _dot_tn(a, b):
    return lax.dot_general(a, b, (((0,), (0,)), ((), ())), preferred_element_type=F32)


def _split2(x):
    hi = x.astype(BF16)
    lo = (x - hi.astype(F32)).astype(BF16)
    return hi, lo


def _split3(x):
    hi = x.astype(BF16)
    r = x - hi.astype(F32)
    mid = r.astype(BF16)
    lo = (r - mid.astype(F32)).astype(BF16)
    return hi, mid, lo


def _dot_x01(x, m01):
    hi, lo = _split2(x)
    return _dot(hi, m01) + _dot(lo, m01)


def _dot_exact01(m01, x):
    h1, h2, h3 = _split3(x)
    return _dot(m01, h1) + _dot(m01, h2) + _dot(m01, h3)


def _silu(x):
    return x * jax.nn.sigmoid(x)


def _softplus(x):
    return jnp.maximum(x, 0.0) + jnp.log(1.0 + jnp.exp(-jnp.abs(x)))


def _ln(y, g, b):
    mu = jnp.mean(y, axis=-1, keepdims=True)
    yc = y - mu
    var = jnp.mean(yc * yc, axis=-1, keepdims=True)
    return yc * lax.rsqrt(var + LN_EPS) * g + b


def _mod_rows(ref, tm):
    v = ref[...]
    v = v.reshape(v.shape[-2], v.shape[-1])
    nb = v.shape[0]
    if nb == 1:
        return v
    return jnp.concatenate([v] * (tm // nb), axis=0)


def _tile4(x):
    return jnp.concatenate([x, x, x, x], axis=0)


def _per_block(x, const, period):
    n = x.shape[0] // period
    return (x.reshape(n, period, x.shape[1]) * const[None]).reshape(x.shape)


def _cumsum_rows(g, period):
    rows, w = g.shape
    tile = min(period, V7X_SUBLANES)
    row = lax.broadcasted_iota(jnp.int32, g.shape, 0) % tile
    out = g
    sh = 1
    while sh < tile:
        out = out + jnp.where(row >= sh, pltpu.roll(out, sh, 0), 0.0)
        sh *= 2
    if period == tile:
        return out
    nper = period // tile
    o4 = out.reshape(rows // period, nper, tile, w)
    tot = o4[:, :, tile - 1:tile, :]
    offs = [jnp.zeros_like(tot[:, 0:1])]
    for i in range(1, nper):
        offs.append(offs[-1] + tot[:, i - 1:i])
    return (o4 + jnp.concatenate(offs, axis=1)).reshape(rows, w)


def _block_row(G, period, r):
    c, w = G.shape
    g3 = G.reshape(c // period, period, w)
    return jnp.broadcast_to(g3[:, r:r + 1, :], g3.shape).reshape(c, w)


def _gref(G, m):
    c = G.shape[0]
    if 2 * m >= V7X_SUBLANES:
        return _block_row(G, 2 * m, m - 1)
    p = lax.broadcasted_iota(jnp.int32, G.shape, 0) % (2 * m)
    if m == 2:
        return jnp.where(p == 0, pltpu.roll(G, c - 1, 0),
                         jnp.where(p == 1, G, jnp.where(p == 2, pltpu.roll(G, 1, 0), pltpu.roll(G, 2, 0))))
    return jnp.where(p == 0, G, pltpu.roll(G, 1, 0))


def _decay_att(q, k, G, mm, diag, lvl_ref, levels, period):
    att = _per_block(diag(q, k), lvl_ref[len(levels)], period)
    for i, m in enumerate(levels):
        if m == 1:
            ql = q * jnp.exp(jnp.minimum(G - pltpu.roll(G, 1, 0), 0.0))
            att = att + _per_block(diag(ql, pltpu.roll(k, 1, 0)), lvl_ref[i], period)
            continue
        gr = _gref(G, m)
        ql = q * jnp.exp(jnp.minimum(G - gr, 0.0))
        kl = k * jnp.exp(jnp.minimum(gr - G, 0.0))
        att = att + _per_block(mm(ql, kl), lvl_ref[i], period)
    return att


def _ret_gamma():
    return np.log(1.0 - 2.0 ** (-5.0 - np.arange(HEADS, dtype=np.float64)))


@functools.lru_cache(maxsize=None)
def _chunk_consts(dk):
    c = CHUNK
    h_att = np.arange(4 * c) // c
    s_att = np.arange(4 * c) % c
    t = np.arange(c)
    kmask = (h_att[:, None] == (np.arange(4 * dk) // dk)[None, :]).astype(np.float32)
    vmask = (h_att[:, None] == (np.arange(HW) // HEAD_DIM)[None, :]).astype(np.float32)
    smask = ((np.arange(HW) // HEAD_DIM)[:, None] == (np.arange(4 * dk) // dk)[None, :]).astype(np.float32)
    levels = (32, 16, 8, 4, 2, 1)
    lv = []
    for m in levels:
        same = (t[:, None] // (2 * m)) == (s_att[None, :] // (2 * m))
        lv.append(same & ((t[:, None] % (2 * m)) >= m) & ((s_att[None, :] % (2 * m)) < m))
    lv.append(t[:, None] == s_att[None, :])
    lvl = np.stack(lv).astype(np.float32)
    incl = (s_att[None, :] <= t[:, None]).astype(np.float32)
    strict = (s_att[None, :] < t[:, None]).astype(np.float32)
    eye = (s_att[None, :] == t[:, None]).astype(np.float32)
    fold = (np.arange(HW)[:, None] % HEAD_DIM == np.arange(HEAD_DIM)[None, :]).astype(np.float32)
    gam = _ret_gamma()
    gh = gam[h_att]
    ret_att = np.where(s_att[None, :] <= t[:, None], np.exp((t[:, None] - s_att[None, :]) * gh[None, :]), 0.0)
    gl = gam[np.arange(HW) // HEAD_DIM]
    ret_q = np.exp((t[:, None] + 1) * gl[None, :])
    ret_k = np.exp((c - 1 - t[:, None]) * gl[None, :])
    ret_s = np.exp(c * gl)[None, :]
    return dict(kmask=kmask, vmask=vmask, smask=smask, lvl=lvl, levels=levels, incl=incl, strict=strict,
                eye=eye, fold=fold, ret_att=ret_att.astype(np.float32), ret_q=ret_q.astype(np.float32),
                ret_k=ret_k.astype(np.float32), ret_s=ret_s.astype(np.float32))


def _bd_ones():
    h = np.arange(HW) // HEAD_DIM
    return (h[:, None] == h[None, :]).astype(np.float32)


def _expand_mat():
    e = np.zeros((_OFF["small"][1], 2 * HW), np.float32)
    for h in range(HEADS):
        e[SMALL_DB + h, h * HEAD_DIM:(h + 1) * HEAD_DIM] = 1.0
        e[SMALL_DA + h, HW + h * HEAD_DIM:HW + (h + 1) * HEAD_DIM] = 1.0
    return e


def _const_spec(a):
    nd = a.ndim
    return pl.BlockSpec(a.shape, lambda *_: (0,) * nd)


def _params(sem):
    return pltpu.CompilerParams(dimension_semantics=sem, vmem_limit_bytes=VMEM_LIMIT)


def _ada_kernel(cp_ref, cs_ref, w_ref, b_ref, op_ref, os_ref):
    w = _bf(w_ref[0])
    mp = _dot(_bf(_silu(cp_ref[...])), w) + b_ref[0]
    for b in range(mp.shape[0]):
        op_ref[0, 0, b] = mp[b:b + 1]
    os_ref[0, 0] = _dot(_bf(_silu(cs_ref[...])), w) + b_ref[0]


def _ada(c_prompt, c_sample, ada_w, ada_b):
    nl, d, nd = ada_w.shape
    bp, bs = c_prompt.shape[0], c_sample.shape[0]
    nmod = nd // d
    assert nmod == N_MOD
    return pl.pallas_call(
        _ada_kernel,
        grid=(nl, nmod),
        in_specs=[_const_spec(c_prompt), _const_spec(c_sample),
                  pl.BlockSpec((1, d, d), lambda l, j: (l, 0, j)),
                  pl.BlockSpec((1, 1, d), lambda l, j: (l, 0, j))],
        out_specs=[pl.BlockSpec((1, 1, bp, 1, d), lambda l, j: (l, j, 0, 0, 0)),
                   pl.BlockSpec((1, 1, bs, d), lambda l, j: (l, j, 0, 0))],
        out_shape=[jax.ShapeDtypeStruct((nl, nmod, bp, 1, d), F32),
                   jax.ShapeDtypeStruct((nl, nmod, bs, d), F32)],
        compiler_params=_params(("arbitrary", "arbitrary")),
        name="ada_mod",
    )(c_prompt, c_sample, ada_w, ada_b.reshape(nl, 1, nd))


class _Rows:
    def __init__(self, batch, t, decode, tile=ROW_TILE):
        self.batch, self.t, self.decode = batch, t, decode
        self.rows = batch * t
        self.tm = min(tile, self.rows)
        assert self.rows % self.tm == 0
        if decode:
            assert self.tm % batch == 0
        else:
            assert t % self.tm == 0
        self.steps = self.rows // self.tm


def _mod_spec(g, layer, k, d):
    if g.decode:
        return pl.BlockSpec((1, 1, g.batch, d), lambda i: (layer, k, 0, 0))
    per = g.t // g.tm
    return pl.BlockSpec((1, 1, 1, 1, d), lambda i: (layer, k, i // per, 0, 0))


def _ffn_block(x, sh_ref, sc_ref, gt_ref, wi_ref, wo_ref, lng_ref, lnb_ref, *, alpha, ln_idx, layer):
    tm, d = x.shape
    dff = wo_ref.shape[1]
    sh, sc, gt = _mod_rows(sh_ref, tm), _mod_rows(sc_ref, tm), _mod_rows(gt_ref, tm)
    h = _bf(x * (1.0 + sc) + sh)
    acc = jnp.zeros((tm, d), F32)
    off = 0
    while off < dff:
        fc = min(FF_CHUNK, dff - off)
        a = _dot(h, wi_ref[0, :, off:off + fc])
        b = _dot(h, wi_ref[0, :, dff + off:dff + off + fc])
        acc = acc + _dot(_bf(_silu(a) * b), wo_ref[0, off:off + fc, :])
        off += fc
    y = alpha * x + 0.5 * (1.0 + gt) * acc
    return _ln(y, lng_ref[layer, ln_idx:ln_idx + 1, :], lnb_ref[layer, ln_idx:ln_idx + 1, :])


def _mix_block(x, gt_ref, o_refs, w_ref, lng_ref, lnb_ref, *, alpha, layer, decode, batch):
    tm, d = x.shape
    gt = _mod_rows(gt_ref, tm)
    if decode:
        accs = [jnp.zeros((batch, d), F32) for _ in range(tm // batch)]
        for mi, o_ref in enumerate(o_refs):
            for h in range(HEADS):
                r0 = mi * HW + h * HEAD_DIM
                wh = w_ref[0, r0:r0 + HEAD_DIM, :]
                for t in range(tm // batch):
                    accs[t] = accs[t] + _dot_tn(_bf(o_ref[h, t]), wh)
        acc = jnp.concatenate(accs, axis=0)
    else:
        acc = jnp.zeros((tm, d), F32)
        for mi, o_ref in enumerate(o_refs):
            acc = acc + _dot(o_ref[...], w_ref[0, mi * HW:(mi + 1) * HW, :])
    y = alpha * x + (1.0 + gt) * acc
    return _ln(y, lng_ref[layer, 1:2, :], lnb_ref[layer, 1:2, :])


def _ffn_kernel(x_ref, sh_ref, sc_ref, gt_ref, wi_ref, wo_ref, lng_ref, lnb_ref, o_ref, **kw):
    o_ref[...] = _ffn_block(x_ref[...], sh_ref, sc_ref, gt_ref, wi_ref, wo_ref, lng_ref, lnb_ref, **kw)


def _mix_ffn_kernel(x_ref, gtm_ref, o0_ref, o1_ref, o2_ref, o3_ref, w_ref, sh_ref, sc_ref, gt_ref, wi_ref, wo_ref,
                    lng_ref, lnb_ref, y_ref, *, alpha, layer, decode, batch):
    x1 = _mix_block(x_ref[...], gtm_ref, (o0_ref, o1_ref, o2_ref, o3_ref), w_ref, lng_ref, lnb_ref, alpha=alpha,
                    layer=layer, decode=decode, batch=batch)
    y_ref[...] = _ffn_block(x1, sh_ref, sc_ref, gt_ref, wi_ref, wo_ref, lng_ref, lnb_ref, alpha=alpha, ln_idx=2,
                            layer=layer)


def _ffn_specs(g, layer, mods, d, dff, ln_g, ln_b):
    return [_mod_spec(g, layer, k, d) for k in mods] + [
        pl.BlockSpec((1, d, 2 * dff), lambda i: (layer, 0, 0), pipeline_mode=pl.Buffered(1)),
        pl.BlockSpec((1, dff, d), lambda i: (layer, 0, 0), pipeline_mode=pl.Buffered(1)),
        _const_spec(ln_g), _const_spec(ln_b)]


def _ffn(g, x, mod, wi, wo, ln_g, ln_b, layer, alpha):
    d = x.shape[1]
    row = pl.BlockSpec((g.tm, d), lambda i: (i, 0))
    return pl.pallas_call(
        functools.partial(_ffn_kernel, alpha=alpha, ln_idx=0, layer=layer),
        grid=(g.steps,),
        in_specs=[row] + _ffn_specs(g, layer, (0, 1, 2), d, wo.shape[1], ln_g, ln_b),
        out_specs=row,
        out_shape=jax.ShapeDtypeStruct(x.shape, F32),
        compiler_params=_params(("arbitrary",)),
        name="ffn0",
    )(x, mod, mod, mod, wi, wo, ln_g, ln_b)


def _mix_ffn(g, x, mod, outs, w_out, wi, wo, ln_g, ln_b, layer, alpha):
    d = x.shape[1]
    row = pl.BlockSpec((g.tm, d), lambda i: (i, 0))
    if g.decode:
        assert g.steps == 1
        ospec = _const_spec(outs[0])
    else:
        ospec = pl.BlockSpec((g.tm, HW), lambda i: (i, 0))
    return pl.pallas_call(
        functools.partial(_mix_ffn_kernel, alpha=alpha, layer=layer, decode=g.decode, batch=g.batch),
        grid=(g.steps,),
        in_specs=[row, _mod_spec(g, layer, 5, d), ospec, ospec, ospec, ospec,
                  pl.BlockSpec((1, w_out.shape[1], d), lambda i: (layer, 0, 0), pipeline_mode=pl.Buffered(1))]
        + _ffn_specs(g, layer, (6, 7, 8), d, wo.shape[1], ln_g, ln_b),
        out_specs=row,
        out_shape=jax.ShapeDtypeStruct(x.shape, F32),
        compiler_params=_params(("arbitrary",)),
        name="mix_ffn2",
    )(x, mod, *outs, w_out, mod, mod, mod, wi, wo, ln_g, ln_b)


def _prep_kernel(*refs, layer, nlayers, g_t, tm, decode, batch):
    (x_ref, sh_ref, sc_ref, w_ref, cos_ref, sin_ref, wg_ref, bg_ref, lb_ref, cw_ref, alog_ref, dtb_ref,
     bd_ref, ex_ref) = refs[:14]
    rest = refs[14:]
    if decode:
        cs_ref, alogc_ref, dtbc_ref = rest[:3]
        outs = rest[3:]
        cbuf = None
    else:
        outs = rest[:-1]
        cbuf = rest[-1]

    x = x_ref[...]
    hb = _bf(x * (1.0 + _mod_rows(sc_ref, tm)) + _mod_rows(sh_ref, tm))

    def proj(name):
        o, w = _OFF[name]
        return _dot(hb, w_ref[0, :, o:o + w])

    bd = bd_ref[...]

    dest = {}
    if decode:
        it = iter(outs)
        for mix, keys in (("ret", ("q", "k", "v", "gate")), ("gla", ("q", "k", "g", "v", "gate")),
                          ("hg", ("q", "k", "g", "v", "gate")), ("gdn", ("q", "k", "v", "gate"))):
            for kk in keys:
                dest[f"{mix}_{kk}"] = ("heads", next(it), 0)
        bgo_ref, convo_ref = next(it), next(it)
    else:
        ret_ref, gla_ref, hg_ref, gdn_ref, conv_ref = outs
        for dst, keys, widths in ((ret_ref, ("ret_q", "ret_k", "ret_v", "ret_gate"), (HW,) * 4),
                                  (gla_ref, ("gla_q", "gla_k", "gla_g", "gla_v", "gla_gate"),
                                   (HEADS * GLA_DK,) * 3 + (HW, HW)),
                                  (hg_ref, ("hg_q", "hg_k", "hg_g", "hg_v", "hg_gate"), (HW,) * 5),
                                  (gdn_ref, ("gdn_q", "gdn_k", "gdn_b", "gdn_g", "gdn_v", "gdn_gate"), (HW,) * 6)):
            off = 0
            for kk, w in zip(keys, widths):
                dest[kk] = ("rows", dst, off)
                off += w

    def put(name, val):
        kind, ref, off = dest[name]
        if kind == "rows":
            ref[:, off:off + val.shape[1]] = val
        else:
            dh = val.shape[1] // HEADS
            for t in range(tm // batch):
                vt = val[t * batch:(t + 1) * batch].T
                for h in range(HEADS):
                    ref[h, t] = vt[h * dh:(h + 1) * dh]

    small = proj("small")

    dq = proj("dqkv")
    cw = cw_ref[0]
    if decode:
        cs = cs_ref[0]
        nt = tm // batch
        blocks = [dq[t * batch:(t + 1) * batch] for t in range(nt)]
        convs = []
        for t in range(nt):
            acc = blocks[t] * cw[CONV_W - 1:CONV_W, :]
            for s in range(1, CONV_W):
                prev = blocks[t - s] if t >= s else cs[CONV_W - 1 - s + t]
                acc = acc + prev * cw[CONV_W - 1 - s:CONV_W - s, :]
            convs.append(acc)
        conv = jnp.concatenate(convs, axis=0)
        convo_ref[...] = dq[(nt - (CONV_W - 1)) * batch:].reshape(CONV_W - 1, batch, dq.shape[1])
    else:
        i = pl.program_id(0)
        per = g_t // tm

        halo = V7X_SUBLANES

        @pl.when(i % per == 0)
        def _():
            cbuf[0:halo, :] = jnp.zeros((halo, dq.shape[1]), F32)

        cbuf[halo:halo + tm, :] = dq
        conv = cbuf[pl.ds(halo, tm), :] * cw[CONV_W - 1:CONV_W, :]
        for s in range(1, CONV_W):
            conv = conv + cbuf[pl.ds(halo - s, tm), :] * cw[CONV_W - 1 - s:CONV_W - s, :]
        tail = cbuf[tm:tm + halo, :]
        cbuf[0:halo, :] = tail

        @pl.when(i % per == per - 1)
        def _():
            conv_ref[0] = tail[halo - (CONV_W - 1):halo, :]

    u = _silu(conv)
    uq, uk = u[:, 0:HW], u[:, HW:2 * HW]
    put("gdn_q", uq * lax.rsqrt(_dot(_bf(uq * uq), bd) + RMS_EPS) * (HEAD_DIM ** -0.5))
    put("gdn_k", uk * lax.rsqrt(_dot(_bf(uk * uk), bd) + RMS_EPS))
    put("gdn_v", u[:, 2 * HW:3 * HW])
    put("gdn_gate", _silu(proj("dg")))
    if decode:
        for t in range(tm // batch):
            smt = small[t * batch:(t + 1) * batch].T
            beta = jax.nn.sigmoid(smt[SMALL_DB:SMALL_DB + HEADS])
            gd = -jnp.exp(alogc_ref[layer]) * _softplus(smt[SMALL_DA:SMALL_DA + HEADS] + dtbc_ref[layer])
            for h in range(HEADS):
                bgo_ref[h, t, 0:1, :] = beta[h:h + 1]
                bgo_ref[h, t, 1:2, :] = gd[h:h + 1]
    else:
        dbda = _dot_x01(small, ex_ref[...])
        put("gdn_b", jax.nn.sigmoid(dbda[:, 0:HW]))
        put("gdn_g", -jnp.exp(alog_ref[layer:layer + 1, :]) * _softplus(dbda[:, HW:2 * HW] + dtb_ref[layer:layer + 1, :]))

    rows = [lb_ref[i:i + 1, :] for i in range(nlayers)]
    mx = functools.reduce(jnp.maximum, rows)
    ex = [jnp.exp(r - mx) for r in rows]
    tot = functools.reduce(lambda a, b: a + b, ex)
    plb = [e / tot for e in ex]
    lb = functools.reduce(lambda a, b: a + b, plb[:layer + 1]) - plb[0]
    zf = proj("hf")
    put("hg_q", _silu(proj("hq")) * (HEAD_DIM ** -0.5))
    put("hg_k", (1.0 - lb) * jax.nn.sigmoid(-zf))
    put("hg_g", jnp.log(lb + (1.0 - lb) * jax.nn.sigmoid(zf)))
    put("hg_v", proj("hi"))
    put("hg_gate", _silu(proj("hg")))

    gpre = _dot(_bf(small), wg_ref[0]) + bg_ref[layer:layer + 1, :]
    put("gla_q", proj("aq") * (GLA_DK ** -0.5))
    put("gla_k", proj("ak"))
    put("gla_g", -_softplus(-gpre) * (1.0 / GLA_GATE_NORM))
    put("gla_v", proj("av"))
    put("gla_gate", _silu(proj("ag")))

    cos, sin = cos_ref[...], sin_ref[...]
    lane = lax.broadcasted_iota(jnp.int32, (tm, HW), 1)
    first_half = (lane % HEAD_DIM) < (HEAD_DIM // 2)

    def rope(v):
        sw = jnp.where(first_half, pltpu.roll(v, HW - HEAD_DIM // 2, 1), pltpu.roll(v, HEAD_DIM // 2, 1))
        return v * cos + sw * sin

    put("ret_q", rope(proj("rq")))
    put("ret_k", rope(proj("rk")) * (HEAD_DIM ** -0.5))
    put("ret_v", proj("rv"))
    put("ret_gate", _silu(proj("rg")))


def _prep(g, x, mod, w_in_p, cos_t, sin_t, wg_p, gla_bg, hg_lb, gdn_conv, alog_e, dtb_e, dec, layer, nlayers):
    d = x.shape[1]
    bd = jnp.asarray(_bd_ones(), BF16)
    exm = jnp.asarray(_expand_mat(), BF16)
    row = pl.BlockSpec((g.tm, d), lambda i: (i, 0))
    if g.decode:
        tab = pl.BlockSpec((g.tm, HW), lambda i: (i, 0))
    else:
        per = g.t // g.tm
        tab = pl.BlockSpec((g.tm, HW), lambda i: (i % per, 0))
    in_specs = [row, _mod_spec(g, layer, 3, d), _mod_spec(g, layer, 4, d),
                pl.BlockSpec((1, d, PROJ_W), lambda i: (layer, 0, 0), pipeline_mode=pl.Buffered(1)),
                tab, tab,
                pl.BlockSpec((1,) + wg_p.shape[1:], lambda i: (layer, 0, 0)),
                _const_spec(gla_bg), _const_spec(hg_lb),
                pl.BlockSpec((1,) + gdn_conv.shape[1:], lambda i: (layer, 0, 0)),
                _const_spec(alog_e), _const_spec(dtb_e), _const_spec(bd), _const_spec(exm)]
    args = [x, mod, mod, w_in_p, cos_t, sin_t, wg_p, gla_bg, hg_lb, gdn_conv, alog_e, dtb_e, bd, exm]
    cch = gdn_conv.shape[2]

    def rows_out(w):
        return jax.ShapeDtypeStruct((g.rows, w), F32), pl.BlockSpec((g.tm, w), lambda i: (i, 0))

    def whole(shape):
        return jax.ShapeDtypeStruct(shape, F32), pl.BlockSpec(shape, lambda i: (0,) * len(shape))

    if g.decode:
        assert g.steps == 1 and g.t >= CONV_W - 1
        cs, alog_c, dtb_c = dec
        in_specs += [pl.BlockSpec((1,) + cs.shape[1:], lambda i: (layer, 0, 0, 0)), _const_spec(alog_c),
                     _const_spec(dtb_c)]
        args += [cs, alog_c, dtb_c]
        outs = []
        for dhs in ((HEAD_DIM,) * 4, (GLA_DK,) * 3 + (HEAD_DIM,) * 2, (HEAD_DIM,) * 5, (HEAD_DIM,) * 4):
            outs += [whole((HEADS, g.t, dh, g.batch)) for dh in dhs]
        outs += [whole((HEADS, g.t, 2, g.batch)), whole((CONV_W - 1, g.batch, cch))]
        scratch = []
    else:
        outs = [rows_out(4 * HW), rows_out(3 * HEADS * GLA_DK + 2 * HW), rows_out(5 * HW), rows_out(6 * HW),
                (jax.ShapeDtypeStruct((g.batch, CONV_W - 1, cch), F32),
                 pl.BlockSpec((1, CONV_W - 1, cch), lambda i: (i // (g.t // g.tm), 0, 0)))]
        scratch = [pltpu.VMEM((g.tm + V7X_SUBLANES, cch), F32)]
    return pl.pallas_call(
        functools.partial(_prep_kernel, layer=layer, nlayers=nlayers, g_t=g.t, tm=g.tm, decode=g.decode,
                          batch=g.batch),
        grid=(g.steps,),
        in_specs=in_specs,
        out_specs=[o[1] for o in outs],
        out_shape=[o[0] for o in outs],
        scratch_shapes=scratch,
        compiler_params=_params(("arbitrary",)),
        name="prep",
    )(*args)


def _chunk(x, c):
    return x[c * CHUNK:(c + 1) * CHUNK]


def _per_chunk(fn, nch):
    return jnp.concatenate([fn(c) for c in range(nch)], axis=0)


def _finish_prompt(o, gate, normw, bd, o_ref):
    ms = _dot(_bf(o * o), bd) * (1.0 / HEAD_DIM)
    on = o * lax.rsqrt(ms + RMS_EPS)
    if normw is not None:
        on = on * normw
    o_ref[...] = _bf(on * gate)


def _state_out(s_t, fold_ref, st_ref):
    fold = fold_ref[...]
    h1, h2, h3 = _split3(s_t)
    st_ref[0] = _dot_tn(h1, fold) + _dot_tn(h2, fold) + _dot_tn(h3, fold)


def _gla_prompt_kernel(f_ref, kmask_ref, vmask_ref, smask_ref, lvl_ref, fold_ref, bd_ref, nw_ref, ratt_ref, rq_ref,
                       rk_ref, rs_ref, o_ref, st_ref, s_scr, *, dk, nch, ret, levels, has_norm, layer):
    j = pl.program_id(1)
    dkk = HEADS * dk

    @pl.when(j == 0)
    def _():
        s_scr[...] = jnp.zeros(s_scr.shape, F32)

    kmask, vmask, smask, bd = kmask_ref[...], vmask_ref[...], smask_ref[...], bd_ref[...]
    normw = nw_ref[layer:layer + 1, :] if has_norm else None

    def mm(x, y):
        xb, yb = _bf(x), _bf(y)
        return _per_chunk(lambda c: _dot_nt(_chunk(xb, c), _tile4(_chunk(yb, c)) * kmask), nch)

    q = f_ref[:, 0:dkk]
    k = f_ref[:, dkk:2 * dkk]
    if ret:
        v = f_ref[:, 2 * dkk:2 * dkk + HW]
        gate = f_ref[:, 2 * dkk + HW:2 * dkk + 2 * HW]
        att = _per_block(mm(q, k), ratt_ref[...], CHUNK)
        qg = _per_block(q, rq_ref[...], CHUNK)
        kd = _per_block(k, rk_ref[...], CHUNK)
        sdec = [rs_ref[...]] * nch
    else:
        g = f_ref[:, 2 * dkk:3 * dkk]
        v = f_ref[:, 3 * dkk:3 * dkk + HW]
        gate = f_ref[:, 3 * dkk + HW:3 * dkk + 2 * HW]
        G = _cumsum_rows(g, CHUNK)
        att = _decay_att(q, k, G, mm, lambda x, y: _dot_nt(_bf(x * y), kmask), lvl_ref, levels, CHUNK)
        qg = q * jnp.exp(G)
        glb = _block_row(G, CHUNK, CHUNK - 1)
        kd = k * jnp.exp(glb - G)
        sdec = [jnp.exp(G[(c + 1) * CHUNK - 1:(c + 1) * CHUNK, :]) for c in range(nch)]
    vb, qgb, kdb, attb = _bf(v), _bf(qg), _bf(kd), _bf(att)
    o_intra = _per_chunk(lambda c: _dot(_chunk(attb, c), _tile4(_chunk(vb, c)) * vmask), nch)
    upd = [_dot_tn(_chunk(vb, c), _chunk(kdb, c)) * smask for c in range(nch)]

    s = s_scr[...]
    o_inter = []
    for c in range(nch):
        o_inter.append(_dot_nt(_chunk(qgb, c), _bf(s)))
        s = s * sdec[c] + upd[c]
    s_scr[...] = s
    _finish_prompt(o_intra + jnp.concatenate(o_inter, axis=0), gate, normw, bd, o_ref)

    @pl.when(j == pl.num_programs(1) - 1)
    def _():
        _state_out(s, fold_ref, st_ref)


def _gla_prompt(g, feat, dk, normw, layer, ret):
    c = _chunk_consts(dk)
    dkk = HEADS * dk
    blk = min(MIX_TILE, g.t)
    nch = blk // CHUNK
    per = g.t // blk
    consts = [jnp.asarray(c["kmask"], BF16), jnp.asarray(c["vmask"], BF16), jnp.asarray(c["smask"]),
              jnp.asarray(c["lvl"]), jnp.asarray(c["fold"], BF16), jnp.asarray(_bd_ones(), BF16),
              normw if normw is not None else jnp.ones((1, HW), F32),
              jnp.asarray(c["ret_att"]), jnp.asarray(c["ret_q"]), jnp.asarray(c["ret_k"]), jnp.asarray(c["ret_s"])]
    w = feat.shape[1]
    o, st = pl.pallas_call(
        functools.partial(_gla_prompt_kernel, dk=dk, nch=nch, ret=ret, levels=c["levels"],
                          has_norm=normw is not None, layer=layer),
        grid=(g.batch, per),
        in_specs=[pl.BlockSpec((blk, w), lambda b, j: (b * per + j, 0))] + [_const_spec(a) for a in consts],
        out_specs=[pl.BlockSpec((blk, HW), lambda b, j: (b * per + j, 0)),
                   pl.BlockSpec((1, dkk, HEAD_DIM), lambda b, j: (b, 0, 0))],
        out_shape=[jax.ShapeDtypeStruct((g.rows, HW), BF16),
                   jax.ShapeDtypeStruct((g.batch, dkk, HEAD_DIM), F32)],
        scratch_shapes=[pltpu.VMEM((HW, dkk), F32)],
        compiler_params=_params(("arbitrary", "arbitrary")),
        name="ret_prompt" if ret else f"gla_prompt_dk{dk}",
    )(feat, *consts)
    return o, st.reshape(g.batch, HEADS, dk, HEAD_DIM)


def _gdn_prompt_kernel(f_ref, bdm_ref, smask_ref, incl_ref, strict_ref, eye_ref, fold_ref, bd_ref, nw_ref, o_ref,
                       st_ref, s_scr, *, nch, layer):
    j = pl.program_id(1)

    @pl.when(j == 0)
    def _():
        s_scr[...] = jnp.zeros(s_scr.shape, F32)

    bdm, smask, incl, strict, eye, bd = (bdm_ref[...], smask_ref[...], incl_ref[...], strict_ref[...], eye_ref[...],
                                         bd_ref[...])
    normw = nw_ref[layer:layer + 1, :]
    ones_c = jnp.ones((CHUNK, CHUNK), BF16)

    def bdw(yb, c):
        return _tile4(_chunk(yb, c)) * bdm

    def prod1(x, y):
        xb, yb = _bf(x), _bf(y)
        return _per_chunk(lambda c: _dot(_chunk(xb, c), bdw(yb, c)), nch)

    def prod1_pair(x1, x2, y):
        x1b, x2b, yb = _bf(x1), _bf(x2), _bf(y)
        both = [_dot(jnp.concatenate([_chunk(x1b, c), _chunk(x2b, c)], axis=0), bdw(yb, c)) for c in range(nch)]
        return (jnp.concatenate([m[:CHUNK] for m in both], axis=0),
                jnp.concatenate([m[CHUNK:] for m in both], axis=0))

    def prod2(x, y):
        xh, xl = _split2(x)
        yb = _bf(y)

        def one(c):
            top = _dot(jnp.concatenate([_chunk(xh, c), _chunk(xl, c)], axis=0), bdw(yb, c))
            return top[:CHUNK] + top[CHUNK:]

        return _per_chunk(one, nch)

    def prod3(x, y):
        xh, xl = _split2(x)
        yh, yl = _split2(y)

        def one(c):
            top = _dot(jnp.concatenate([_chunk(xh, c), _chunk(xl, c)], axis=0), bdw(yh, c))
            return top[:CHUNK] + top[CHUNK:] + _dot(_chunk(xh, c), bdw(yl, c))

        return _per_chunk(one, nch)

    q, k = f_ref[:, 0:HW], f_ref[:, HW:2 * HW]
    be, ge = f_ref[:, 2 * HW:3 * HW], f_ref[:, 3 * HW:4 * HW]
    v, gate = f_ref[:, 4 * HW:5 * HW], f_ref[:, 5 * HW:6 * HW]
    G = _cumsum_rows(ge, CHUNK)
    zd = _per_block(G, eye, CHUNK)
    grow = _per_chunk(lambda c: _dot_exact01(ones_c, _chunk(zd, c)), nch)
    L = _per_block(jnp.exp(jnp.minimum(G - grow, 0.0)), incl, CHUNK)
    qb, kb = _bf(q), _bf(k)

    def qkk(c):
        return _dot_nt(jnp.concatenate([_chunk(qb, c), _chunk(kb, c)], axis=0), bdw(kb, c))

    qkk_all = [qkk(c) for c in range(nch)]
    qk = jnp.concatenate([m[:CHUNK] for m in qkk_all], axis=0)
    kk = jnp.concatenate([m[CHUNK:] for m in qkk_all], axis=0)
    a = _per_block(be * kk * L, strict, CHUNK)
    eye_b = _per_block(jnp.ones_like(a), eye, CHUNK)
    p = -a
    tinv = eye_b + p
    for i in range(5):
        if i == 0:
            p = prod1(p, p)
        else:
            p, dt = prod1_pair(p, tinv, p)
            tinv = tinv + dt
    tinv = tinv + prod1(tinv, p)
    resid = eye_b - tinv - prod3(a, tinv)
    tinv = tinv + prod1(tinv, resid)
    eg = jnp.exp(G)
    u = prod2(tinv, be * v)
    w = prod2(tinv, be * eg * k)
    glb = _block_row(G, CHUNK, CHUNK - 1)
    wb, qeb, qklb = _bf(w), _bf(q * eg), _bf(qk * L)
    kdb = _bf(k * jnp.exp(glb - G))
    sdec = [jnp.exp(G[(c + 1) * CHUNK - 1:(c + 1) * CHUNK, :]) for c in range(nch)]

    ub = _bf(u)
    s_add = [_dot_tn(_chunk(ub, c), _chunk(kdb, c)) * smask for c in range(nch)]
    s_mul = [_bf(_dot_tn(_chunk(wb, c), _chunk(kdb, c)) * smask) for c in range(nch)]

    s = s_scr[...]
    snaps = []
    for c in range(nch):
        sb = _bf(s)
        snaps.append(sb)
        s = s * sdec[c] + s_add[c] - _dot(sb, s_mul[c])
    s_scr[...] = s

    outs = []
    for c in range(nch):
        ws = _dot_nt(jnp.concatenate([_chunk(wb, c), _chunk(qeb, c)], axis=0), snaps[c])
        db = _bf(_chunk(u, c) - ws[:CHUNK])
        outs.append(_dot(_chunk(qklb, c), _tile4(db) * bdm) + ws[CHUNK:])
    _finish_prompt(jnp.concatenate(outs, axis=0), gate, normw, bd, o_ref)

    @pl.when(j == pl.num_programs(1) - 1)
    def _():
        _state_out(s, fold_ref, st_ref)


def _gdn_prompt(g, feat, normw, layer):
    c = _chunk_consts(HEAD_DIM)
    blk = min(MIX_TILE, g.t)
    nch = blk // CHUNK
    per = g.t // blk
    consts = [jnp.asarray(c["vmask"], BF16), jnp.asarray(c["smask"]), jnp.asarray(c["incl"]), jnp.asarray(c["strict"]),
              jnp.asarray(c["eye"]), jnp.asarray(c["fold"], BF16), jnp.asarray(_bd_ones(), BF16), normw]
    w = feat.shape[1]
    o, st = pl.pallas_call(
        functools.partial(_gdn_prompt_kernel, nch=nch, layer=layer),
        grid=(g.batch, per),
        in_specs=[pl.BlockSpec((blk, w), lambda b, j: (b * per + j, 0))] + [_const_spec(a) for a in consts],
        out_specs=[pl.BlockSpec((blk, HW), lambda b, j: (b * per + j, 0)),
                   pl.BlockSpec((1, HW, HEAD_DIM), lambda b, j: (b, 0, 0))],
        out_shape=[jax.ShapeDtypeStruct((g.rows, HW), BF16),
                   jax.ShapeDtypeStruct((g.batch, HW, HEAD_DIM), F32)],
        scratch_shapes=[pltpu.VMEM((HW, HW), F32)],
        compiler_params=_params(("arbitrary", "arbitrary")),
        name="gdn_prompt",
    )(feat, *consts)
    return o, st.reshape(g.batch, HEADS, HEAD_DIM, HEAD_DIM)


def _state_io(refs, layer):
    if layer == 0:
        return refs
    prev_ref, rest = refs[0], refs[1:]
    rest[-1][0:layer] = prev_ref[...]
    return rest


def _dec_finish(o, gate, normw):
    ms = jnp.mean(o * o, axis=0, keepdims=True)
    on = o * lax.rsqrt(ms + RMS_EPS)
    if normw is not None:
        on = on * normw
    return on * gate


def _gla_decode_kernel(q_ref, k_ref, g_ref, v_ref, gate_ref, s_ref, *refs, nt, dk, ret, has_norm, layer):
    nw_ref, o_ref, so_ref = _state_io(refs, layer)
    normw = nw_ref[layer] if has_norm else None
    qs = [q_ref[0, t] for t in range(nt)]
    ks = [k_ref[0, t] for t in range(nt)]
    vs = [v_ref[0, t] for t in range(nt)]
    decay = [jnp.exp(g_ref[0, t]) for t in range(nt)]
    o_acc = [jnp.zeros(vs[0].shape, F32) for _ in range(nt)]
    for dc in range(dk // DEC_DCHUNK):
        r = slice(dc * DEC_DCHUNK, (dc + 1) * DEC_DCHUNK)
        sc = s_ref[0, 0, r]
        for t in range(nt):
            a = decay[t][None] if ret else decay[t][r][:, None, :]
            sc = sc * a + ks[t][r][:, None, :] * vs[t][None, :, :]
            o_acc[t] = o_acc[t] + jnp.sum(qs[t][r][:, None, :] * sc, axis=0)
        so_ref[layer, 0, r] = sc
    for t in range(nt):
        o_ref[0, t] = _dec_finish(o_acc[t], gate_ref[0, t], normw)


def _gdn_decode_kernel(q_ref, k_ref, v_ref, gate_ref, bg_ref, s_ref, *refs, nt, layer):
    nw_ref, o_ref, so_ref = _state_io(refs, layer)
    normw = nw_ref[layer]
    chunks = [slice(dc * DEC_DCHUNK, (dc + 1) * DEC_DCHUNK) for dc in range(HEAD_DIM // DEC_DCHUNK)]
    for t in range(nt):
        q, k, v = q_ref[0, t], k_ref[0, t], v_ref[0, t]
        beta = bg_ref[0, t, 0:1, :]
        ag = jnp.exp(bg_ref[0, t, 1:2, :])

        def state(r):
            return s_ref[0, 0, r] if t == 0 else so_ref[layer, 0, r]

        ks = jnp.zeros(v.shape, F32)
        for r in chunks:
            ks = ks + jnp.sum(k[r][:, None, :] * state(r), axis=0)
        delta = beta * (v - ag * ks)
        o = jnp.zeros(v.shape, F32)
        for r in chunks:
            sc = state(r) * ag[None] + k[r][:, None, :] * delta[None, :, :]
            so_ref[layer, 0, r] = sc
            o = o + jnp.sum(q[r][:, None, :] * sc, axis=0)
        o_ref[0, t] = _dec_finish(o, gate_ref[0, t], normw)


def _decode_call(body, name, feats, state_t, prev, extra, dk, layer):
    nt, batch = feats[0].shape[1], feats[0].shape[3]

    def head_spec(a):
        return pl.BlockSpec((1,) + a.shape[1:], lambda h: (h,) + (0,) * (a.ndim - 1))

    def st_spec(nl):
        return pl.BlockSpec((nl, 1, dk, HEAD_DIM, batch), lambda h: (0, h, 0, 0, 0))

    in_specs = [head_spec(a) for a in feats]
    in_specs.append(pl.BlockSpec((1, 1, dk, HEAD_DIM, batch), lambda h: (layer, h, 0, 0, 0)))
    args = list(feats) + [state_t]
    if layer > 0:
        in_specs.append(st_spec(layer))
        args.append(prev)
    o_shape = (HEADS, nt, HEAD_DIM, batch)
    return pl.pallas_call(
        body,
        grid=(HEADS,),
        in_specs=in_specs + [_const_spec(a) for a in extra],
        out_specs=[pl.BlockSpec((1,) + o_shape[1:], lambda h: (h, 0, 0, 0)), st_spec(layer + 1)],
        out_shape=[jax.ShapeDtypeStruct(o_shape, F32),
                   jax.ShapeDtypeStruct((layer + 1, HEADS, dk, HEAD_DIM, batch), F32)],
        compiler_params=_params(("arbitrary",)),
        name=name,
    )(*args, *extra)


def _gla_decode(q, k, g, v, gate, state_t, prev, dk, normw_col, layer, ret):
    nt = q.shape[1]
    body = functools.partial(_gla_decode_kernel, nt=nt, dk=dk, ret=ret, has_norm=normw_col is not None, layer=layer)
    nw = normw_col if normw_col is not None else jnp.ones((layer + 1, HEAD_DIM, 1), F32)
    return _decode_call(body, "ret_decode" if ret else f"gla_decode_dk{dk}", [q, k, g, v, gate], state_t, prev, [nw],
                        dk, layer)


def _gdn_decode(q, k, v, gate, bg, state_t, prev, normw_col, layer):
    body = functools.partial(_gdn_decode_kernel, nt=q.shape[1], layer=layer)
    return _decode_call(body, "gdn_decode", [q, k, v, gate, bg], state_t, prev, [normw_col], HEAD_DIM, layer)


def _rope_tables(pos):
    half = HEAD_DIM // 2
    inv = ROPE_BASE ** (-jnp.arange(half, dtype=F32) / half)
    ang = pos[:, None] * inv[None, :]
    cos, sin = jnp.cos(ang), jnp.sin(ang)
    cos_t = jnp.tile(jnp.concatenate([cos, cos], axis=1), (1, HEADS))
    sin_t = jnp.tile(jnp.concatenate([-sin, sin], axis=1), (1, HEADS))
    return cos_t, sin_t


def _relayout_w_in(w_in):
    nl, d, win = w_in.shape
    rows = min(d, ROW_TILE)
    assert d % rows == 0

    def body(w_ref, o_ref):
        for name, w in _SLABS:
            o, _ = _OFF[name]
            if name == "small":
                o_ref[0, :, o:o + w] = jnp.zeros((rows, w), BF16)
                for n in _NARROW:
                    so, sw = _SRC[n]
                    o_ref[0, :, o:o + sw] = _bf(w_ref[0, :, so:so + sw])
                    o += sw
            else:
                so, _ = _SRC[name]
                o_ref[0, :, o:o + w] = _bf(w_ref[0, :, so:so + w])

    return pl.pallas_call(
        body,
        grid=(nl, d // rows),
        in_specs=[pl.BlockSpec((1, rows, win), lambda l, i: (l, i, 0))],
        out_specs=pl.BlockSpec((1, rows, PROJ_W), lambda l, i: (l, i, 0)),
        out_shape=jax.ShapeDtypeStruct((nl, d, PROJ_W), BF16),
        compiler_params=_params(("arbitrary", "arbitrary")),
        name="relayout_w_in",
    )(w_in)


def kernel(x_prompt, x_sample, state_ret, state_gla, state_hgrn, state_gdn, state_gdn_conv, c_prompt, c_sample, ada_w, ada_b, ln_g, ln_b, ffn1_wi, ffn1_wo, ffn2_wi, ffn2_wo, w_in, gla_wg, gla_bg, hg_lb, gdn_conv, gdn_a_log, gdn_dt_bias, gla_norm, hg_norm, gdn_norm, w_out):
    nl, d = ada_w.shape[0], ada_w.shape[1]
    bp, tp = x_prompt.shape[0], x_prompt.shape[1]
    bs, ts = x_sample.shape[0], x_sample.shape[1]
    alpha = (2.0 * nl) ** 0.25
    assert tp % CHUNK == 0 and ts >= CONV_W - 1

    wi1, wo1, wi2, wo2 = (w.astype(BF16) for w in (ffn1_wi, ffn1_wo, ffn2_wi, ffn2_wo))
    w_out_b = w_out.astype(BF16)
    w_in_p = _relayout_w_in(w_in)
    wg_p = jnp.pad(gla_wg, ((0, 0), (0, _OFF["small"][1] - gla_wg.shape[1]), (0, 0))).astype(BF16)
    alog_e = jnp.repeat(gdn_a_log, HEAD_DIM, axis=1)
    dtb_e = jnp.repeat(gdn_dt_bias, HEAD_DIM, axis=1)
    gla_nw = jnp.tile(gla_norm, (1, HEADS))
    hg_nw = jnp.tile(hg_norm, (1, HEADS))
    gdn_nw = jnp.tile(gdn_norm, (1, HEADS))

    mod_p, mod_s = _ada(c_prompt, c_sample, ada_w, ada_b)

    gp = _Rows(bp, tp, decode=False)
    gp_prep = _Rows(bp, tp, decode=False, tile=PREP_TILE)
    gs = _Rows(bs, ts, decode=True)
    assert gs.steps == 1
    cos_p, sin_p = _rope_tables(jnp.arange(tp, dtype=F32))
    cos_s, sin_s = _rope_tables(PAST_LEN + jnp.arange(ts, dtype=F32))
    cos_s, sin_s = jnp.repeat(cos_s, bs, axis=0), jnp.repeat(sin_s, bs, axis=0)

    xp = x_prompt.reshape(bp * tp, d)
    xs = jnp.transpose(x_sample, (1, 0, 2)).reshape(ts * bs, d)

    st_ret, st_gla, st_hg, st_gdn = (jnp.transpose(s, (0, 2, 3, 4, 1))
                                     for s in (state_ret, state_gla, state_hgrn, state_gdn))
    conv_t = jnp.transpose(state_gdn_conv, (0, 2, 1, 3))
    dec_extra = (conv_t, gdn_a_log[:, :, None], gdn_dt_bias[:, :, None])
    ret_g = jnp.asarray(np.broadcast_to(_ret_gamma().astype(np.float32)[:, None, None, None], (HEADS, ts, 1, bs)))
    gla_nc, hg_nc, gdn_nc = gla_norm[:, :, None], hg_norm[:, :, None], gdn_norm[:, :, None]

    p_states, s_convs = [], []
    s_ret = s_gla = s_hg = s_gdn = None
    for l in range(nl):
        xp = _ffn(gp, xp, mod_p, wi1, wo1, ln_g, ln_b, l, alpha)
        ret_f, gla_f, hg_f, gdn_f, p_conv = _prep(gp_prep, xp, mod_p, w_in_p, cos_p, sin_p, wg_p, gla_bg, hg_lb,
                                                  gdn_conv, alog_e, dtb_e, None, l, nl)
        o_ret, p_ret = _gla_prompt(gp, ret_f, HEAD_DIM, None, l, ret=True)
        o_gla, p_gla = _gla_prompt(gp, gla_f, GLA_DK, gla_nw, l, ret=False)
        o_hg, p_hg = _gla_prompt(gp, hg_f, HEAD_DIM, hg_nw, l, ret=False)
        o_gdn, p_gdn = _gdn_prompt(gp, gdn_f, gdn_nw, l)
        xp = _mix_ffn(gp, xp, mod_p, (o_ret, o_gla, o_hg, o_gdn), w_out_b, wi2, wo2, ln_g, ln_b, l, alpha)
        p_states.append((p_ret, p_gla, p_hg, p_gdn, p_conv))

        xs = _ffn(gs, xs, mod_s, wi1, wo1, ln_g, ln_b, l, alpha)
        (rq, rk, rv, rgt, aq, ak, ag, av, agt, hq, hk, hgg, hv, hgt, dq, dk_, dv, dgt, dbg, conv_new) = _prep(
            gs, xs, mod_s, w_in_p, cos_s, sin_s, wg_p, gla_bg, hg_lb, gdn_conv, alog_e, dtb_e, dec_extra, l, nl)
        o_ret, s_ret = _gla_decode(rq, rk, ret_g, rv, rgt, st_ret, s_ret, HEAD_DIM, None, l, True)
        o_gla, s_gla = _gla_decode(aq, ak, ag, av, agt, st_gla, s_gla, GLA_DK, gla_nc, l, False)
        o_hg, s_hg = _gla_decode(hq, hk, hgg, hv, hgt, st_hg, s_hg, HEAD_DIM, hg_nc, l, False)
        o_gdn, s_gdn = _gdn_decode(dq, dk_, dv, dgt, dbg, st_gdn, s_gdn, gdn_nc, l)
        xs = _mix_ffn(gs, xs, mod_s, (o_ret, o_gla, o_hg, o_gdn), w_out_b, wi2, wo2, ln_g, ln_b, l, alpha)
        s_convs.append(conv_new)

    y_prompt = xp.reshape(bp, tp, d)
    y_sample = jnp.transpose(xs.reshape(ts, bs, d), (1, 0, 2))
    p_out = tuple(jnp.stack([st[i] for st in p_states]) for i in range(5))
    s_out = tuple(jnp.transpose(s, (0, 4, 1, 2, 3)) for s in (s_ret, s_gla, s_hg, s_gdn))
    return (y_prompt, y_sample) + p_out + s_out + (jnp.transpose(jnp.stack(s_convs), (0, 2, 1, 3)),)
```

```python
import functools

import numpy as np
import jax
import jax.numpy as jnp
from jax import lax
from jax.experimental import pallas as pl
from jax.experimental.pallas import tpu as pltpu

F32 = jnp.float32
BF16 = jnp.bfloat16

HEAD_DIM = 64
HEADS = 4
GLA_DK = 32
GLA_GATE_RANK = 16
GLA_GATE_NORM = 16.0
CONV_W = 4
CHUNK = 64
ROPE_BASE = 10000.0
LN_EPS = 1e-5
RMS_EPS = 1e-6
PAST_LEN = 16384
N_MOD = 9
HW = HEADS * HEAD_DIM

V7X_LANES = 128
V7X_SUBLANES = 8
V7X_VMEM_BYTES = 64 * 1024 * 1024

ROW_TILE = 512
PREP_TILE = 512
MIX_TILE = 2048
DEC_DCHUNK = V7X_SUBLANES
FF_CHUNK = 256
VMEM_LIMIT = V7X_VMEM_BYTES - 12 * 1024 * 1024

_MIX_COLS = (("rq", HW), ("rk", HW), ("rv", HW), ("rg", HW),
             ("aq", HEADS * GLA_DK), ("ak", HEADS * GLA_DK), ("av", HW), ("alr", GLA_GATE_RANK), ("ag", HW),
             ("hq", HW), ("hf", HW), ("hi", HW), ("hg", HW),
             ("dqkv", 3 * HW), ("db", HEADS), ("da", HEADS), ("dg", HW))
_SRC = {}
_o = 0
for _n, _w in _MIX_COLS:
    _SRC[_n] = (_o, _w)
    _o += _w
_NARROW = ("alr", "db", "da")
_SLABS = tuple((n, w) for n, w in _MIX_COLS if n not in _NARROW) + (("small", V7X_LANES),)
_OFF = {}
_o = 0
for _n, _w in _SLABS:
    assert _o % V7X_LANES == 0
    _OFF[_n] = (_o, _w)
    _o += _w
PROJ_W = _o
SMALL_ALR, SMALL_DB, SMALL_DA = 0, GLA_GATE_RANK, GLA_GATE_RANK + HEADS


def _bf(x):
    return x.astype(BF16)


def _dot(a, b):
    return jnp.dot(a, b, preferred_element_type=F32)


def _dot_nt(a, b):
    return lax.dot_general(a, b, (((1,), (1,)), ((), ())), preferred_element_type=F32)


def _dot_tn(a, b):
    return lax.dot_general(a, b, (((0,), (0,)), ((), ())), preferred_element_type=F32)


def _split2(x):
    hi = x.astype(BF16)
    lo = (x - hi.astype(F32)).astype(BF16)
    return hi, lo


def _split3(x):
    hi = x.astype(BF16)
    r = x - hi.astype(F32)
    mid = r.astype(BF16)
    lo = (r - mid.astype(F32)).astype(BF16)
    return hi, mid, lo


def _dot_x01(x, m01):
    hi, lo = _split2(x)
    return _dot(hi, m01) + _dot(lo, m01)


def _dot_exact01(m01, x):
    h1, h2, h3 = _split3(x)
    return _dot(m01, h1) + _dot(m01, h2) + _dot(m01, h3)


def _silu(x):
    return x * jax.nn.sigmoid(x)


def _softplus(x):
    return jnp.maximum(x, 0.0) + jnp.log(1.0 + jnp.exp(-jnp.abs(x)))


def _ln(y, g, b):
    mu = jnp.mean(y, axis=-1, keepdims=True)
    yc = y - mu
    var = jnp.mean(yc * yc, axis=-1, keepdims=True)
    return yc * lax.rsqrt(var + LN_EPS) * g + b


def _mod_rows(ref, tm):
    v = ref[...]
    v = v.reshape(v.shape[-2], v.shape[-1])
    nb = v.shape[0]
    if nb == 1:
        return v
    return jnp.concatenate([v] * (tm // nb), axis=0)


def _tile4(x):
    return jnp.concatenate([x, x, x, x], axis=0)


def _per_block(x, const, period):
    n = x.shape[0] // period
    return (x.reshape(n, period, x.shape[1]) * const[None]).reshape(x.shape)


def _cumsum_rows(g, period):
    rows, w = g.shape
    tile = min(period, V7X_SUBLANES)
    row = lax.broadcasted_iota(jnp.int32, g.shape, 0) % tile
    out = g
    sh = 1
    while sh < tile:
        out = out + jnp.where(row >= sh, pltpu.roll(out, sh, 0), 0.0)
        sh *= 2
    if period == tile:
        return out
    nper = period // tile
    o4 = out.reshape(rows // period, nper, tile, w)
    tot = o4[:, :, tile - 1:tile, :]
    offs = [jnp.zeros_like(tot[:, 0:1])]
    for i in range(1, nper):
        offs.append(offs[-1] + tot[:, i - 1:i])
    return (o4 + jnp.concatenate(offs, axis=1)).reshape(rows, w)


def _block_row(G, period, r):
    c, w = G.shape
    g3 = G.reshape(c // period, period, w)
    return jnp.broadcast_to(g3[:, r:r + 1, :], g3.shape).reshape(c, w)


def _gref(G, m):
    c = G.shape[0]
    if 2 * m >= V7X_SUBLANES:
        return _block_row(G, 2 * m, m - 1)
    p = lax.broadcasted_iota(jnp.int32, G.shape, 0) % (2 * m)
    if m == 2:
        return jnp.where(p == 0, pltpu.roll(G, c - 1, 0),
                         jnp.where(p == 1, G, jnp.where(p == 2, pltpu.roll(G, 1, 0), pltpu.roll(G, 2, 0))))
    return jnp.where(p == 0, G, pltpu.roll(G, 1, 0))


def _decay_att(q, k, G, mm, diag, lvl_ref, levels, period):
    att = _per_block(diag(q, k), lvl_ref[len(levels)], period)
    for i, m in enumerate(levels):
        if m == 1:
            ql = q * jnp.exp(jnp.minimum(G - pltpu.roll(G, 1, 0), 0.0))
            att = att + _per_block(diag(ql, pltpu.roll(k, 1, 0)), lvl_ref[i], period)
            continue
        gr = _gref(G, m)
        ql = q * jnp.exp(jnp.minimum(G - gr, 0.0))
        kl = k * jnp.exp(jnp.minimum(gr - G, 0.0))
        att = att + _per_block(mm(ql, kl), lvl_ref[i], period)
    return att


def _ret_gamma():
    return np.log(1.0 - 2.0 ** (-5.0 - np.arange(HEADS, dtype=np.float64)))


@functools.lru_cache(maxsize=None)
def _chunk_consts(dk):
    c = CHUNK
    h_att = np.arange(4 * c) // c
    s_att = np.arange(4 * c) % c
    t = np.arange(c)
    kmask = (h_att[:, None] == (np.arange(4 * dk) // dk)[None, :]).astype(np.float32)
    vmask = (h_att[:, None] == (np.arange(HW) // HEAD_DIM)[None, :]).astype(np.float32)
    smask = ((np.arange(HW) // HEAD_DIM)[:, None] == (np.arange(4 * dk) // dk)[None, :]).astype(np.float32)
    levels = (32, 16, 8, 4, 2, 1)
    lv = []
    for m in levels:
        same = (t[:, None] // (2 * m)) == (s_att[None, :] // (2 * m))
        lv.append(same & ((t[:, None] % (2 * m)) >= m) & ((s_att[None, :] % (2 * m)) < m))
    lv.append(t[:, None] == s_att[None, :])
    lvl = np.stack(lv).astype(np.float32)
    incl = (s_att[None, :] <= t[:, None]).astype(np.float32)
    strict = (s_att[None, :] < t[:, None]).astype(np.float32)
    eye = (s_att[None, :] == t[:, None]).astype(np.float32)
    fold = (np.arange(HW)[:, None] % HEAD_DIM == np.arange(HEAD_DIM)[None, :]).astype(np.float32)
    gam = _ret_gamma()
    gh = gam[h_att]
    ret_att = np.where(s_att[None, :] <= t[:, None], np.exp((t[:, None] - s_att[None, :]) * gh[None, :]), 0.0)
    gl = gam[np.arange(HW) // HEAD_DIM]
    ret_q = np.exp((t[:, None] + 1) * gl[None, :])
    ret_k = np.exp((c - 1 - t[:, None]) * gl[None, :])
    ret_s = np.exp(c * gl)[None, :]
    return dict(kmask=kmask, vmask=vmask, smask=smask, lvl=lvl, levels=levels, incl=incl, strict=strict,
                eye=eye, fold=fold, ret_att=ret_att.astype(np.float32), ret_q=ret_q.astype(np.float32),
                ret_k=ret_k.astype(np.float32), ret_s=ret_s.astype(np.float32))


def _bd_ones():
    h = np.arange(HW) // HEAD_DIM
    return (h[:, None] == h[None, :]).astype(np.float32)


def _expand_mat():
    e = np.zeros((_OFF["small"][1], 2 * HW), np.float32)
    for h in range(HEADS):
        e[SMALL_DB + h, h * HEAD_DIM:(h + 1) * HEAD_DIM] = 1.0
        e[SMALL_DA + h, HW + h * HEAD_DIM:HW + (h + 1) * HEAD_DIM] = 1.0
    return e


def _const_spec(a):
    nd = a.ndim
    return pl.BlockSpec(a.shape, lambda *_: (0,) * nd)


def _params(sem):
    return pltpu.CompilerParams(dimension_semantics=sem, vmem_limit_bytes=VMEM_LIMIT)


def _ada_kernel(cp_ref, cs_ref, w_ref, b_ref, op_ref, os_ref):
    w = _bf(w_ref[0])
    mp = _dot(_bf(_silu(cp_ref[...])), w) + b_ref[0]
    for b in range(mp.shape[0]):
        op_ref[0, 0, b] = mp[b:b + 1]
    os_ref[0, 0] = _dot(_bf(_silu(cs_ref[...])), w) + b_ref[0]


def _ada(c_prompt, c_sample, ada_w, ada_b):
    nl, d, nd = ada_w.shape
    bp, bs = c_prompt.shape[0], c_sample.shape[0]
    nmod = nd // d
    assert nmod == N_MOD
    return pl.pallas_call(
        _ada_kernel,
        grid=(nl, nmod),
        in_specs=[_const_spec(c_prompt), _const_spec(c_sample),
                  pl.BlockSpec((1, d, d), lambda l, j: (l, 0, j)),
                  pl.BlockSpec((1, 1, d), lambda l, j: (l, 0, j))],
        out_specs=[pl.BlockSpec((1, 1, bp, 1, d), lambda l, j: (l, j, 0, 0, 0)),
                   pl.BlockSpec((1, 1, bs, d), lambda l, j: (l, j, 0, 0))],
        out_shape=[jax.ShapeDtypeStruct((nl, nmod, bp, 1, d), F32),
                   jax.ShapeDtypeStruct((nl, nmod, bs, d), F32)],
        compiler_params=_params(("arbitrary", "arbitrary")),
        name="ada_mod",
    )(c_prompt, c_sample, ada_w, ada_b.reshape(nl, 1, nd))


class _Rows:
    def __init__(self, batch, t, decode, tile=ROW_TILE):
        self.batch, self.t, self.decode = batch, t, decode
        self.rows = batch * t
        self.tm = min(tile, self.rows)
        assert self.rows % self.tm == 0
        if decode:
            assert self.tm % batch == 0
        else:
            assert t % self.tm == 0
        self.steps = self.rows // self.tm


def _mod_spec(g, layer, k, d):
    if g.decode:
        return pl.BlockSpec((1, 1, g.batch, d), lambda i: (layer, k, 0, 0))
    per = g.t // g.tm
    return pl.BlockSpec((1, 1, 1, 1, d), lambda i: (layer, k, i // per, 0, 0))


def _ffn_block(x, sh_ref, sc_ref, gt_ref, wi_ref, wo_ref, lng_ref, lnb_ref, *, alpha, ln_idx, layer):
    tm, d = x.shape
    dff = wo_ref.shape[1]
    sh, sc, gt = _mod_rows(sh_ref, tm), _mod_rows(sc_ref, tm), _mod_rows(gt_ref, tm)
    h = _bf(x * (1.0 + sc) + sh)
    acc = jnp.zeros((tm, d), F32)
    off = 0
    while off < dff:
        fc = min(FF_CHUNK, dff - off)
        a = _dot(h, _bf(wi_ref[0, :, off:off + fc]))
        b = _dot(h, _bf(wi_ref[0, :, dff + off:dff + off + fc]))
        acc = acc + _dot(_bf(_silu(a) * b), _bf(wo_ref[0, off:off + fc, :]))
        off += fc
    y = alpha * x + 0.5 * (1.0 + gt) * acc
    return _ln(y, lng_ref[layer, ln_idx:ln_idx + 1, :], lnb_ref[layer, ln_idx:ln_idx + 1, :])


def _mix_block(x, gt_ref, o_refs, w_ref, lng_ref, lnb_ref, *, alpha, layer, decode, batch):
    tm, d = x.shape
    gt = _mod_rows(gt_ref, tm)
    if decode:
        accs = [jnp.zeros((batch, d), F32) for _ in range(tm // batch)]
        for mi, o_ref in enumerate(o_refs):
            for h in range(HEADS):
                r0 = mi * HW + h * HEAD_DIM
                wh = w_ref[0, r0:r0 + HEAD_DIM, :]
                for t in range(tm // batch):
                    accs[t] = accs[t] + _dot_tn(_bf(o_ref[h, t]), wh)
        acc = jnp.concatenate(accs, axis=0)
    else:
        acc = jnp.zeros((tm, d), F32)
        for mi, o_ref in enumerate(o_refs):
            acc = acc + _dot(o_ref[...], w_ref[0, mi * HW:(mi + 1) * HW, :])
    y = alpha * x + (1.0 + gt) * acc
    return _ln(y, lng_ref[layer, 1:2, :], lnb_ref[layer, 1:2, :])


def _ffn_kernel(x_ref, sh_ref, sc_ref, gt_ref, wi_ref, wo_ref, lng_ref, lnb_ref, o_ref, **kw):
    o_ref[...] = _ffn_block(x_ref[...], sh_ref, sc_ref, gt_ref, wi_ref, wo_ref, lng_ref, lnb_ref, **kw)


def _mix_ffn_kernel(x_ref, gtm_ref, o0_ref, o1_ref, o2_ref, o3_ref, w_ref, sh_ref, sc_ref, gt_ref, wi_ref, wo_ref,
                    lng_ref, lnb_ref, y_ref, *, alpha, layer, decode, batch):
    x1 = _mix_block(x_ref[...], gtm_ref, (o0_ref, o1_ref, o2_ref, o3_ref), w_ref, lng_ref, lnb_ref, alpha=alpha,
                    layer=layer, decode=decode, batch=batch)
    y_ref[...] = _ffn_block(x1, sh_ref, sc_ref, gt_ref, wi_ref, wo_ref, lng_ref, lnb_ref, alpha=alpha, ln_idx=2,
                            layer=layer)


def _ffn_specs(g, layer, mods, d, dff, ln_g, ln_b):
    return [_mod_spec(g, layer, k, d) for k in mods] + [
        pl.BlockSpec((1, d, 2 * dff), lambda i: (layer, 0, 0), pipeline_mode=pl.Buffered(1)),
        pl.BlockSpec((1, dff, d), lambda i: (layer, 0, 0), pipeline_mode=pl.Buffered(1)),
        _const_spec(ln_g), _const_spec(ln_b)]


def _ffn(g, x, mod, wi, wo, ln_g, ln_b, layer, alpha):
    d = x.shape[1]
    row = pl.BlockSpec((g.tm, d), lambda i: (i, 0))
    return pl.pallas_call(
        functools.partial(_ffn_kernel, alpha=alpha, ln_idx=0, layer=layer),
        grid=(g.steps,),
        in_specs=[row] + _ffn_specs(g, layer, (0, 1, 2), d, wo.shape[1], ln_g, ln_b),
        out_specs=row,
        out_shape=jax.ShapeDtypeStruct(x.shape, F32),
        compiler_params=_params(("arbitrary",)),
        name="ffn0",
    )(x, mod, mod, mod, wi, wo, ln_g, ln_b)


def _mix_ffn(g, x, mod, outs, w_out, wi, wo, ln_g, ln_b, layer, alpha):
    d = x.shape[1]
    row = pl.BlockSpec((g.tm, d), lambda i: (i, 0))
    if g.decode:
        assert g.steps == 1
        ospec = _const_spec(outs[0])
    else:
        ospec = pl.BlockSpec((g.tm, HW), lambda i: (i, 0))
    return pl.pallas_call(
        functools.partial(_mix_ffn_kernel, alpha=alpha, layer=layer, decode=g.decode, batch=g.batch),
        grid=(g.steps,),
        in_specs=[row, _mod_spec(g, layer, 5, d), ospec, ospec, ospec, ospec,
                  pl.BlockSpec((1, w_out.shape[1], d), lambda i: (layer, 0, 0), pipeline_mode=pl.Buffered(1))]
        + _ffn_specs(g, layer, (6, 7, 8), d, wo.shape[1], ln_g, ln_b),
        out_specs=row,
        out_shape=jax.ShapeDtypeStruct(x.shape, F32),
        compiler_params=_params(("arbitrary",)),
        name="mix_ffn2",
    )(x, mod, *outs, w_out, mod, mod, mod, wi, wo, ln_g, ln_b)


def _prep_kernel(*refs, layer, nlayers, g_t, tm, decode, batch):
    (x_ref, sh_ref, sc_ref, w_ref, cos_ref, sin_ref, wg_ref, bg_ref, lb_ref, cw_ref, alog_ref, dtb_ref,
     bd_ref, ex_ref) = refs[:14]
    rest = refs[14:]
    if decode:
        cs_ref, alogc_ref, dtbc_ref = rest[:3]
        outs = rest[3:]
        cbuf = None
    else:
        outs = rest[:-1]
        cbuf = rest[-1]

    x = x_ref[...]
    hb = _bf(x * (1.0 + _mod_rows(sc_ref, tm)) + _mod_rows(sh_ref, tm))

    def proj(name):
        o, w = _OFF[name]
        return _dot(hb, w_ref[0, :, o:o + w])

    bd = bd_ref[...]

    dest = {}
    if decode:
        it = iter(outs)
        for mix, keys in (("ret", ("q", "k", "v", "gate")), ("gla", ("q", "k", "g", "v", "gate")),
                          ("hg", ("q", "k", "g", "v", "gate")), ("gdn", ("q", "k", "v", "gate"))):
            for kk in keys:
                dest[f"{mix}_{kk}"] = ("heads", next(it), 0)
        bgo_ref, convo_ref = next(it), next(it)
    else:
        ret_ref, gla_ref, hg_ref, gdn_ref, conv_ref = outs
        for dst, keys, widths in ((ret_ref, ("ret_q", "ret_k", "ret_v", "ret_gate"), (HW,) * 4),
                                  (gla_ref, ("gla_q", "gla_k", "gla_g", "gla_v", "gla_gate"),
                                   (HEADS * GLA_DK,) * 3 + (HW, HW)),
                                  (hg_ref, ("hg_q", "hg_k", "hg_g", "hg_v", "hg_gate"), (HW,) * 5),
                                  (gdn_ref, ("gdn_q", "gdn_k", "gdn_b", "gdn_g", "gdn_v", "gdn_gate"), (HW,) * 6)):
            off = 0
            for kk, w in zip(keys, widths):
                dest[kk] = ("rows", dst, off)
                off += w

    def put(name, val):
        kind, ref, off = dest[name]
        if kind == "rows":
            ref[:, off:off + val.shape[1]] = val
        else:
            dh = val.shape[1] // HEADS
            for t in range(tm // batch):
                vt = val[t * batch:(t + 1) * batch].T
                for h in range(HEADS):
                    ref[h, t] = vt[h * dh:(h + 1) * dh]

    small = proj("small")

    dq = proj("dqkv")
    cw = cw_ref[0]
    if decode:
        cs = cs_ref[0]
        nt = tm // batch
        blocks = [dq[t * batch:(t + 1) * batch] for t in range(nt)]
        convs = []
        for t in range(nt):
            acc = blocks[t] * cw[CONV_W - 1:CONV_W, :]
            for s in range(1, CONV_W):
                prev = blocks[t - s] if t >= s else cs[CONV_W - 1 - s + t]
                acc = acc + prev * cw[CONV_W - 1 - s:CONV_W - s, :]
            convs.append(acc)
        conv = jnp.concatenate(convs, axis=0)
        convo_ref[...] = dq[(nt - (CONV_W - 1)) * batch:].reshape(CONV_W - 1, batch, dq.shape[1])
    else:
        i = pl.program_id(0)
        per = g_t // tm

        halo = V7X_SUBLANES

        @pl.when(i % per == 0)
        def _():
            cbuf[0:halo, :] = jnp.zeros((halo, dq.shape[1]), F32)

        cbuf[halo:halo + tm, :] = dq
        conv = cbuf[pl.ds(halo, tm), :] * cw[CONV_W - 1:CONV_W, :]
        for s in range(1, CONV_W):
            conv = conv + cbuf[pl.ds(halo - s, tm), :] * cw[CONV_W - 1 - s:CONV_W - s, :]
        tail = cbuf[tm:tm + halo, :]
        cbuf[0:halo, :] = tail

        @pl.when(i % per == per - 1)
        def _():
            conv_ref[0] = tail[halo - (CONV_W - 1):halo, :]

    u = _silu(conv)
    uq, uk = u[:, 0:HW], u[:, HW:2 * HW]
    put("gdn_q", uq * lax.rsqrt(_dot(_bf(uq * uq), bd) + RMS_EPS) * (HEAD_DIM ** -0.5))
    put("gdn_k", uk * lax.rsqrt(_dot(_bf(uk * uk), bd) + RMS_EPS))
    put("gdn_v", u[:, 2 * HW:3 * HW])
    put("gdn_gate", _silu(proj("dg")))
    if decode:
        for t in range(tm // batch):
            smt = small[t * batch:(t + 1) * batch].T
            beta = jax.nn.sigmoid(smt[SMALL_DB:SMALL_DB + HEADS])
            gd = -jnp.exp(alogc_ref[layer]) * _softplus(smt[SMALL_DA:SMALL_DA + HEADS] + dtbc_ref[layer])
            for h in range(HEADS):
                bgo_ref[h, t, 0:1, :] = beta[h:h + 1]
                bgo_ref[h, t, 1:2, :] = gd[h:h + 1]
    else:
        dbda = _dot_x01(small, ex_ref[...])
        put("gdn_b", jax.nn.sigmoid(dbda[:, 0:HW]))
        put("gdn_g", -jnp.exp(alog_ref[layer:layer + 1, :]) * _softplus(dbda[:, HW:2 * HW] + dtb_ref[layer:layer + 1, :]))

    rows = [lb_ref[i:i + 1, :] for i in range(nlayers)]
    mx = functools.reduce(jnp.maximum, rows)
    ex = [jnp.exp(r - mx) for r in rows]
    tot = functools.reduce(lambda a, b: a + b, ex)
    plb = [e / tot for e in ex]
    lb = functools.reduce(lambda a, b: a + b, plb[:layer + 1]) - plb[0]
    zf = proj("hf")
    put("hg_q", _silu(proj("hq")) * (HEAD_DIM ** -0.5))
    put("hg_k", (1.0 - lb) * jax.nn.sigmoid(-zf))
    put("hg_g", jnp.log(lb + (1.0 - lb) * jax.nn.sigmoid(zf)))
    put("hg_v", proj("hi"))
    put("hg_gate", _silu(proj("hg")))

    gpre = _dot(_bf(small), wg_ref[0]) + bg_ref[layer:layer + 1, :]
    put("gla_q", proj("aq") * (GLA_DK ** -0.5))
    put("gla_k", proj("ak"))
    put("gla_g", -_softplus(-gpre) * (1.0 / GLA_GATE_NORM))
    put("gla_v", proj("av"))
    put("gla_gate", _silu(proj("ag")))

    cos, sin = cos_ref[...], sin_ref[...]
    lane = lax.broadcasted_iota(jnp.int32, (tm, HW), 1)
    first_half = (lane % HEAD_DIM) < (HEAD_DIM // 2)

    def rope(v):
        sw = jnp.where(first_half, pltpu.roll(v, HW - HEAD_DIM // 2, 1), pltpu.roll(v, HEAD_DIM // 2, 1))
        return v * cos + sw * sin

    put("ret_q", rope(proj("rq")))
    put("ret_k", rope(proj("rk")) * (HEAD_DIM ** -0.5))
    put("ret_v", proj("rv"))
    put("ret_gate", _silu(proj("rg")))


def _prep(g, x, mod, w_in_p, cos_t, sin_t, wg_p, gla_bg, hg_lb, gdn_conv, alog_e, dtb_e, dec, layer, nlayers):
    d = x.shape[1]
    bd = jnp.asarray(_bd_ones(), BF16)
    exm = jnp.asarray(_expand_mat(), BF16)
    row = pl.BlockSpec((g.tm, d), lambda i: (i, 0))
    if g.decode:
        tab = pl.BlockSpec((g.tm, HW), lambda i: (i, 0))
    else:
        per = g.t // g.tm
        tab = pl.BlockSpec((g.tm, HW), lambda i: (i % per, 0))
    in_specs = [row, _mod_spec(g, layer, 3, d), _mod_spec(g, layer, 4, d),
                pl.BlockSpec((1, d, PROJ_W), lambda i: (layer, 0, 0), pipeline_mode=pl.Buffered(1)),
                tab, tab,
                pl.BlockSpec((1,) + wg_p.shape[1:], lambda i: (layer, 0, 0)),
                _const_spec(gla_bg), _const_spec(hg_lb),
                pl.BlockSpec((1,) + gdn_conv.shape[1:], lambda i: (layer, 0, 0)),
                _const_spec(alog_e), _const_spec(dtb_e), _const_spec(bd), _const_spec(exm)]
    args = [x, mod, mod, w_in_p, cos_t, sin_t, wg_p, gla_bg, hg_lb, gdn_conv, alog_e, dtb_e, bd, exm]
    cch = gdn_conv.shape[2]

    def rows_out(w):
        return jax.ShapeDtypeStruct((g.rows, w), F32), pl.BlockSpec((g.tm, w), lambda i: (i, 0))

    def whole(shape):
        return jax.ShapeDtypeStruct(shape, F32), pl.BlockSpec(shape, lambda i: (0,) * len(shape))

    if g.decode:
        assert g.steps == 1 and g.t >= CONV_W - 1
        cs, alog_c, dtb_c = dec
        in_specs += [pl.BlockSpec((1,) + cs.shape[1:], lambda i: (layer, 0, 0, 0)), _const_spec(alog_c),
                     _const_spec(dtb_c)]
        args += [cs, alog_c, dtb_c]
        outs = []
        for dhs in ((HEAD_DIM,) * 4, (GLA_DK,) * 3 + (HEAD_DIM,) * 2, (HEAD_DIM,) * 5, (HEAD_DIM,) * 4):
            outs += [whole((HEADS, g.t, dh, g.batch)) for dh in dhs]
        outs += [whole((HEADS, g.t, 2, g.batch)), whole((CONV_W - 1, g.batch, cch))]
        scratch = []
    else:
        outs = [rows_out(4 * HW), rows_out(3 * HEADS * GLA_DK + 2 * HW), rows_out(5 * HW), rows_out(6 * HW),
                (jax.ShapeDtypeStruct((g.batch, CONV_W - 1, cch), F32),
                 pl.BlockSpec((1, CONV_W - 1, cch), lambda i: (i // (g.t // g.tm), 0, 0)))]
        scratch = [pltpu.VMEM((g.tm + V7X_SUBLANES, cch), F32)]
    return pl.pallas_call(
        functools.partial(_prep_kernel, layer=layer, nlayers=nlayers, g_t=g.t, tm=g.tm, decode=g.decode,
                          batch=g.batch),
        grid=(g.steps,),
        in_specs=in_specs,
        out_specs=[o[1] for o in outs],
        out_shape=[o[0] for o in outs],
        scratch_shapes=scratch,
        compiler_params=_params(("arbitrary",)),
        name="prep",
    )(*args)


def _chunk(x, c):
    return x[c * CHUNK:(c + 1) * CHUNK]


def _per_chunk(fn, nch):
    return jnp.concatenate([fn(c) for c in range(nch)], axis=0)


def _finish_prompt(o, gate, normw, bd, o_ref):
    ms = _dot(_bf(o * o), bd) * (1.0 / HEAD_DIM)
    on = o * lax.rsqrt(ms + RMS_EPS)
    if normw is not None:
        on = on * normw
    o_ref[...] = _bf(on * gate)


def _state_out(s_t, fold_ref, st_ref):
    fold = fold_ref[...]
    h1, h2, h3 = _split3(s_t)
    st_ref[0] = _dot_tn(h1, fold) + _dot_tn(h2, fold) + _dot_tn(h3, fold)


def _gla_prompt_kernel(f_ref, kmask_ref, vmask_ref, smask_ref, lvl_ref, fold_ref, bd_ref, nw_ref, ratt_ref, rq_ref,
                       rk_ref, rs_ref, o_ref, st_ref, s_scr, *, dk, nch, ret, levels, has_norm, layer):
    j = pl.program_id(1)
    dkk = HEADS * dk

    @pl.when(j == 0)
    def _():
        s_scr[...] = jnp.zeros(s_scr.shape, F32)

    kmask, vmask, smask, bd = kmask_ref[...], vmask_ref[...], smask_ref[...], bd_ref[...]
    normw = nw_ref[layer:layer + 1, :] if has_norm else None

    def mm(x, y):
        xb, yb = _bf(x), _bf(y)
        return _per_chunk(lambda c: _dot_nt(_chunk(xb, c), _tile4(_chunk(yb, c)) * kmask), nch)

    q = f_ref[:, 0:dkk]
    k = f_ref[:, dkk:2 * dkk]
    if ret:
        v = f_ref[:, 2 * dkk:2 * dkk + HW]
        gate = f_ref[:, 2 * dkk + HW:2 * dkk + 2 * HW]
        att = _per_block(mm(q, k), ratt_ref[...], CHUNK)
        qg = _per_block(q, rq_ref[...], CHUNK)
        kd = _per_block(k, rk_ref[...], CHUNK)
        sdec = [rs_ref[...]] * nch
    else:
        g = f_ref[:, 2 * dkk:3 * dkk]
        v = f_ref[:, 3 * dkk:3 * dkk + HW]
        gate = f_ref[:, 3 * dkk + HW:3 * dkk + 2 * HW]
        G = _cumsum_rows(g, CHUNK)
        att = _decay_att(q, k, G, mm, lambda x, y: _dot_nt(_bf(x * y), kmask), lvl_ref, levels, CHUNK)
        qg = q * jnp.exp(G)
        glb = _block_row(G, CHUNK, CHUNK - 1)
        kd = k * jnp.exp(glb - G)
        sdec = [jnp.exp(G[(c + 1) * CHUNK - 1:(c + 1) * CHUNK, :]) for c in range(nch)]
    vb, qgb, kdb, attb = _bf(v), _bf(qg), _bf(kd), _bf(att)
    o_intra = _per_chunk(lambda c: _dot(_chunk(attb, c), _tile4(_chunk(vb, c)) * vmask), nch)
    upd = [_dot_tn(_chunk(vb, c), _chunk(kdb, c)) * smask for c in range(nch)]

    s = s_scr[...]
    o_inter = []
    for c in range(nch):
        o_inter.append(_dot_nt(_chunk(qgb, c), _bf(s)))
        s = s * sdec[c] + upd[c]
    s_scr[...] = s
    _finish_prompt(o_intra + jnp.concatenate(o_inter, axis=0), gate, normw, bd, o_ref)

    @pl.when(j == pl.num_programs(1) - 1)
    def _():
        _state_out(s, fold_ref, st_ref)


def _gla_prompt(g, feat, dk, normw, layer, ret):
    c = _chunk_consts(dk)
    dkk = HEADS * dk
    blk = min(MIX_TILE, g.t)
    nch = blk // CHUNK
    per = g.t // blk
    consts = [jnp.asarray(c["kmask"], BF16), jnp.asarray(c["vmask"], BF16), jnp.asarray(c["smask"]),
              jnp.asarray(c["lvl"]), jnp.asarray(c["fold"], BF16), jnp.asarray(_bd_ones(), BF16),
              normw if normw is not None else jnp.ones((1, HW), F32),
              jnp.asarray(c["ret_att"]), jnp.asarray(c["ret_q"]), jnp.asarray(c["ret_k"]), jnp.asarray(c["ret_s"])]
    w = feat.shape[1]
    o, st = pl.pallas_call(
        functools.partial(_gla_prompt_kernel, dk=dk, nch=nch, ret=ret, levels=c["levels"],
                          has_norm=normw is not None, layer=layer),
        grid=(g.batch, per),
        in_specs=[pl.BlockSpec((blk, w), lambda b, j: (b * per + j, 0))] + [_const_spec(a) for a in consts],
        out_specs=[pl.BlockSpec((blk, HW), lambda b, j: (b * per + j, 0)),
                   pl.BlockSpec((1, dkk, HEAD_DIM), lambda b, j: (b, 0, 0))],
        out_shape=[jax.ShapeDtypeStruct((g.rows, HW), BF16),
                   jax.ShapeDtypeStruct((g.batch, dkk, HEAD_DIM), F32)],
        scratch_shapes=[pltpu.VMEM((HW, dkk), F32)],
        compiler_params=_params(("arbitrary", "arbitrary")),
        name="ret_prompt" if ret else f"gla_prompt_dk{dk}",
    )(feat, *consts)
    return o, st.reshape(g.batch, HEADS, dk, HEAD_DIM)


def _gdn_prompt_kernel(f_ref, bdm_ref, smask_ref, incl_ref, strict_ref, eye_ref, fold_ref, bd_ref, nw_ref, o_ref,
                       st_ref, s_scr, *, nch, layer):
    j = pl.program_id(1)

    @pl.when(j == 0)
    def _():
        s_scr[...] = jnp.zeros(s_scr.shape, F32)

    bdm, smask, incl, strict, eye, bd = (bdm_ref[...], smask_ref[...], incl_ref[...], strict_ref[...], eye_ref[...],
                                         bd_ref[...])
    normw = nw_ref[layer:layer + 1, :]
    ones_c = jnp.ones((CHUNK, CHUNK), BF16)

    def bdw(yb, c):
        return _tile4(_chunk(yb, c)) * bdm

    def prod1(x, y):
        xb, yb = _bf(x), _bf(y)
        return _per_chunk(lambda c: _dot(_chunk(xb, c), bdw(yb, c)), nch)

    def prod1_pair(x1, x2, y):
        x1b, x2b, yb = _bf(x1), _bf(x2), _bf(y)
        both = [_dot(jnp.concatenate([_chunk(x1b, c), _chunk(x2b, c)], axis=0), bdw(yb, c)) for c in range(nch)]
        return (jnp.concatenate([m[:CHUNK] for m in both], axis=0),
                jnp.concatenate([m[CHUNK:] for m in both], axis=0))

    def prod2(x, y):
        xh, xl = _split2(x)
        yb = _bf(y)

        def one(c):
            top = _dot(jnp.concatenate([_chunk(xh, c), _chunk(xl, c)], axis=0), bdw(yb, c))
            return top[:CHUNK] + top[CHUNK:]

        return _per_chunk(one, nch)

    def prod3(x, y):
        xh, xl = _split2(x)
        yh, yl = _split2(y)

        def one(c):
            top = _dot(jnp.concatenate([_chunk(xh, c), _chunk(xl, c)], axis=0), bdw(yh, c))
            return top[:CHUNK] + top[CHUNK:] + _dot(_chunk(xh, c), bdw(yl, c))

        return _per_chunk(one, nch)

    q, k = f_ref[:, 0:HW], f_ref[:, HW:2 * HW]
    be, ge = f_ref[:, 2 * HW:3 * HW], f_ref[:, 3 * HW:4 * HW]
    v, gate = f_ref[:, 4 * HW:5 * HW], f_ref[:, 5 * HW:6 * HW]
    G = _cumsum_rows(ge, CHUNK)
    zd = _per_block(G, eye, CHUNK)
    grow = _per_chunk(lambda c: _dot_exact01(ones_c, _chunk(zd, c)), nch)
    L = _per_block(jnp.exp(jnp.minimum(G - grow, 0.0)), incl, CHUNK)
    qb, kb = _bf(q), _bf(k)

    def qkk(c):
        return _dot_nt(jnp.concatenate([_chunk(qb, c), _chunk(kb, c)], axis=0), bdw(kb, c))

    qkk_all = [qkk(c) for c in range(nch)]
    qk = jnp.concatenate([m[:CHUNK] for m in qkk_all], axis=0)
    kk = jnp.concatenate([m[CHUNK:] for m in qkk_all], axis=0)
    a = _per_block(be * kk * L, strict, CHUNK)
    eye_b = _per_block(jnp.ones_like(a), eye, CHUNK)
    p = -a
    tinv = eye_b + p
    for i in range(5):
        if i == 0:
            p = prod1(p, p)
        else:
            p, dt = prod1_pair(p, tinv, p)
            tinv = tinv + dt
    tinv = tinv + prod1(tinv, p)
    resid = eye_b - tinv - prod3(a, tinv)
    tinv = tinv + prod1(tinv, resid)
    eg = jnp.exp(G)
    u = prod2(tinv, be * v)
    w = prod2(tinv, be * eg * k)
    glb = _block_row(G, CHUNK, CHUNK - 1)
    wb, qeb, qklb = _bf(w), _bf(q * eg), _bf(qk * L)
    kdb = _bf(k * jnp.exp(glb - G))
    sdec = [jnp.exp(G[(c + 1) * CHUNK - 1:(c + 1) * CHUNK, :]) for c in range(nch)]

    ub = _bf(u)
    s_add = [_dot_tn(_chunk(ub, c), _chunk(kdb, c)) * smask for c in range(nch)]
    s_mul = [_bf(_dot_tn(_chunk(wb, c), _chunk(kdb, c)) * smask) for c in range(nch)]

    s = s_scr[...]
    snaps = []
    for c in range(nch):
        sb = _bf(s)
        snaps.append(sb)
        s = s * sdec[c] + s_add[c] - _dot(sb, s_mul[c])
    s_scr[...] = s

    outs = []
    for c in range(nch):
        ws = _dot_nt(jnp.concatenate([_chunk(wb, c), _chunk(qeb, c)], axis=0), snaps[c])
        db = _bf(_chunk(u, c) - ws[:CHUNK])
        outs.append(_dot(_chunk(qklb, c), _tile4(db) * bdm) + ws[CHUNK:])
    _finish_prompt(jnp.concatenate(outs, axis=0), gate, normw, bd, o_ref)

    @pl.when(j == pl.num_programs(1) - 1)
    def _():
        _state_out(s, fold_ref, st_ref)


def _gdn_prompt(g, feat, normw, layer):
    c = _chunk_consts(HEAD_DIM)
    blk = min(MIX_TILE, g.t)
    nch = blk // CHUNK
    per = g.t // blk
    consts = [jnp.asarray(c["vmask"], BF16), jnp.asarray(c["smask"]), jnp.asarray(c["incl"]), jnp.asarray(c["strict"]),
              jnp.asarray(c["eye"]), jnp.asarray(c["fold"], BF16), jnp.asarray(_bd_ones(), BF16), normw]
    w = feat.shape[1]
    o, st = pl.pallas_call(
        functools.partial(_gdn_prompt_kernel, nch=nch, layer=layer),
        grid=(g.batch, per),
        in_specs=[pl.BlockSpec((blk, w), lambda b, j: (b * per + j, 0))] + [_const_spec(a) for a in consts],
        out_specs=[pl.BlockSpec((blk, HW), lambda b, j: (b * per + j, 0)),
                   pl.BlockSpec((1, HW, HEAD_DIM), lambda b, j: (b, 0, 0))],
        out_shape=[jax.ShapeDtypeStruct((g.rows, HW), BF16),
                   jax.ShapeDtypeStruct((g.batch, HW, HEAD_DIM), F32)],
        scratch_shapes=[pltpu.VMEM((HW, HW), F32)],
        compiler_params=_params(("arbitrary", "arbitrary")),
        name="gdn_prompt",
    )(feat, *consts)
    return o, st.reshape(g.batch, HEADS, HEAD_DIM, HEAD_DIM)


def _state_io(refs, layer):
    if layer == 0:
        return refs
    prev_ref, rest = refs[0], refs[1:]
    rest[-1][0:layer] = prev_ref[...]
    return rest


def _dec_finish(o, gate, normw):
    ms = jnp.mean(o * o, axis=0, keepdims=True)
    on = o * lax.rsqrt(ms + RMS_EPS)
    if normw is not None:
        on = on * normw
    return on * gate


def _gla_decode_kernel(q_ref, k_ref, g_ref, v_ref, gate_ref, s_ref, *refs, nt, dk, ret, has_norm, layer):
    nw_ref, o_ref, so_ref = _state_io(refs, layer)
    normw = nw_ref[layer] if has_norm else None
    qs = [q_ref[0, t] for t in range(nt)]
    ks = [k_ref[0, t] for t in range(nt)]
    vs = [v_ref[0, t] for t in range(nt)]
    decay = [jnp.exp(g_ref[0, t]) for t in range(nt)]
    o_acc = [jnp.zeros(vs[0].shape, F32) for _ in range(nt)]
    for dc in range(dk // DEC_DCHUNK):
        r = slice(dc * DEC_DCHUNK, (dc + 1) * DEC_DCHUNK)
        sc = s_ref[0, 0, r]
        for t in range(nt):
            a = decay[t][None] if ret else decay[t][r][:, None, :]
            sc = sc * a + ks[t][r][:, None, :] * vs[t][None, :, :]
            o_acc[t] = o_acc[t] + jnp.sum(qs[t][r][:, None, :] * sc, axis=0)
        so_ref[layer, 0, r] = sc
    for t in range(nt):
        o_ref[0, t] = _dec_finish(o_acc[t], gate_ref[0, t], normw)


def _gdn_decode_kernel(q_ref, k_ref, v_ref, gate_ref, bg_ref, s_ref, *refs, nt, layer):
    nw_ref, o_ref, so_ref = _state_io(refs, layer)
    normw = nw_ref[layer]
    chunks = [slice(dc * DEC_DCHUNK, (dc + 1) * DEC_DCHUNK) for dc in range(HEAD_DIM // DEC_DCHUNK)]
    for t in range(nt):
        q, k, v = q_ref[0, t], k_ref[0, t], v_ref[0, t]
        beta = bg_ref[0, t, 0:1, :]
        ag = jnp.exp(bg_ref[0, t, 1:2, :])

        def state(r):
            return s_ref[0, 0, r] if t == 0 else so_ref[layer, 0, r]

        ks = jnp.zeros(v.shape, F32)
        for r in chunks:
            ks = ks + jnp.sum(k[r][:, None, :] * state(r), axis=0)
        delta = beta * (v - ag * ks)
        o = jnp.zeros(v.shape, F32)
        for r in chunks:
            sc = state(r) * ag[None] + k[r][:, None, :] * delta[None, :, :]
            so_ref[layer, 0, r] = sc
            o = o + jnp.sum(q[r][:, None, :] * sc, axis=0)
        o_ref[0, t] = _dec_finish(o, gate_ref[0, t], normw)


def _decode_call(body, name, feats, state_t, prev, extra, dk, layer):
    nt, batch = feats[0].shape[1], feats[0].shape[3]

    def head_spec(a):
        return pl.BlockSpec((1,) + a.shape[1:], lambda h: (h,) + (0,) * (a.ndim - 1))

    def st_spec(nl):
        return pl.BlockSpec((nl, 1, dk, HEAD_DIM, batch), lambda h: (0, h, 0, 0, 0))

    in_specs = [head_spec(a) for a in feats]
    in_specs.append(pl.BlockSpec((1, 1, dk, HEAD_DIM, batch), lambda h: (layer, h, 0, 0, 0)))
    args = list(feats) + [state_t]
    if layer > 0:
        in_specs.append(st_spec(layer))
        args.append(prev)
    o_shape = (HEADS, nt, HEAD_DIM, batch)
    return pl.pallas_call(
        body,
        grid=(HEADS,),
        in_specs=in_specs + [_const_spec(a) for a in extra],
        out_specs=[pl.BlockSpec((1,) + o_shape[1:], lambda h: (h, 0, 0, 0)), st_spec(layer + 1)],
        out_shape=[jax.ShapeDtypeStruct(o_shape, F32),
                   jax.ShapeDtypeStruct((layer + 1, HEADS, dk, HEAD_DIM, batch), F32)],
        compiler_params=_params(("arbitrary",)),
        name=name,
    )(*args, *extra)


def _gla_decode(q, k, g, v, gate, state_t, prev, dk, normw_col, layer, ret):
    nt = q.shape[1]
    body = functools.partial(_gla_decode_kernel, nt=nt, dk=dk, ret=ret, has_norm=normw_col is not None, layer=layer)
    nw = normw_col if normw_col is not None else jnp.ones((layer + 1, HEAD_DIM, 1), F32)
    return _decode_call(body, "ret_decode" if ret else f"gla_decode_dk{dk}", [q, k, g, v, gate], state_t, prev, [nw],
                        dk, layer)


def _gdn_decode(q, k, v, gate, bg, state_t, prev, normw_col, layer):
    body = functools.partial(_gdn_decode_kernel, nt=q.shape[1], layer=layer)
    return _decode_call(body, "gdn_decode", [q, k, v, gate, bg], state_t, prev, [normw_col], HEAD_DIM, layer)


def _rope_tables(pos):
    half = HEAD_DIM // 2
    inv = ROPE_BASE ** (-jnp.arange(half, dtype=F32) / half)
    ang = pos[:, None] * inv[None, :]
    cos, sin = jnp.cos(ang), jnp.sin(ang)
    cos_t = jnp.tile(jnp.concatenate([cos, cos], axis=1), (1, HEADS))
    sin_t = jnp.tile(jnp.concatenate([-sin, sin], axis=1), (1, HEADS))
    return cos_t, sin_t


def _relayout_w_in(w_in):
    def cols(name):
        o, w = _SRC[name]
        return w_in[:, :, o:o + w]

    parts = []
    for name, w in _SLABS:
        if name == "small":
            sm = jnp.concatenate([cols(n) for n in _NARROW], axis=2)
            parts.append(jnp.pad(sm, ((0, 0), (0, 0), (0, w - sm.shape[2]))))
        else:
            parts.append(cols(name))
    return jnp.concatenate(parts, axis=2).astype(BF16)


def kernel(x_prompt, x_sample, state_ret, state_gla, state_hgrn, state_gdn, state_gdn_conv, c_prompt, c_sample, ada_w, ada_b, ln_g, ln_b, ffn1_wi, ffn1_wo, ffn2_wi, ffn2_wo, w_in, gla_wg, gla_bg, hg_lb, gdn_conv, gdn_a_log, gdn_dt_bias, gla_norm, hg_norm, gdn_norm, w_out):
    nl, d = ada_w.shape[0], ada_w.shape[1]
    bp, tp = x_prompt.shape[0], x_prompt.shape[1]
    bs, ts = x_sample.shape[0], x_sample.shape[1]
    alpha = (2.0 * nl) ** 0.25
    assert tp % CHUNK == 0 and ts >= CONV_W - 1

    wi1, wo1 = ffn1_wi, ffn1_wo
    wi2, wo2 = ffn2_wi.astype(BF16), ffn2_wo.astype(BF16)
    w_out_b = w_out.astype(BF16)
    w_in_p = _relayout_w_in(w_in)
    wg_p = jnp.pad(gla_wg, ((0, 0), (0, _OFF["small"][1] - gla_wg.shape[1]), (0, 0))).astype(BF16)
    alog_e = jnp.repeat(gdn_a_log, HEAD_DIM, axis=1)
    dtb_e = jnp.repeat(gdn_dt_bias, HEAD_DIM, axis=1)
    gla_nw = jnp.tile(gla_norm, (1, HEADS))
    hg_nw = jnp.tile(hg_norm, (1, HEADS))
    gdn_nw = jnp.tile(gdn_norm, (1, HEADS))

    mod_p, mod_s = _ada(c_prompt, c_sample, ada_w, ada_b)

    gp = _Rows(bp, tp, decode=False)
    gp_prep = _Rows(bp, tp, decode=False, tile=PREP_TILE)
    gs = _Rows(bs, ts, decode=True)
    assert gs.steps == 1
    cos_p, sin_p = _rope_tables(jnp.arange(tp, dtype=F32))
    cos_s, sin_s = _rope_tables(PAST_LEN + jnp.arange(ts, dtype=F32))
    cos_s, sin_s = jnp.repeat(cos_s, bs, axis=0), jnp.repeat(sin_s, bs, axis=0)

    xp = x_prompt.reshape(bp * tp, d)
    xs = jnp.transpose(x_sample, (1, 0, 2)).reshape(ts * bs, d)

    st_ret, st_gla, st_hg, st_gdn = (jnp.transpose(s, (0, 2, 3, 4, 1))
                                     for s in (state_ret, state_gla, state_hgrn, state_gdn))
    conv_t = jnp.transpose(state_gdn_conv, (0, 2, 1, 3))
    dec_extra = (conv_t, gdn_a_log[:, :, None], gdn_dt_bias[:, :, None])
    ret_g = jnp.asarray(np.broadcast_to(_ret_gamma().astype(np.float32)[:, None, None, None], (HEADS, ts, 1, bs)))
    gla_nc, hg_nc, gdn_nc = gla_norm[:, :, None], hg_norm[:, :, None], gdn_norm[:, :, None]

    p_states, s_convs = [], []
    s_ret = s_gla = s_hg = s_gdn = None
    for l in range(nl):
        xp = _ffn(gp, xp, mod_p, wi1, wo1, ln_g, ln_b, l, alpha)
        ret_f, gla_f, hg_f, gdn_f, p_conv = _prep(gp_prep, xp, mod_p, w_in_p, cos_p, sin_p, wg_p, gla_bg, hg_lb,
                                                  gdn_conv, alog_e, dtb_e, None, l, nl)
        o_ret, p_ret = _gla_prompt(gp, ret_f, HEAD_DIM, None, l, ret=True)
        o_gla, p_gla = _gla_prompt(gp, gla_f, GLA_DK, gla_nw, l, ret=False)
        o_hg, p_hg = _gla_prompt(gp, hg_f, HEAD_DIM, hg_nw, l, ret=False)
        o_gdn, p_gdn = _gdn_prompt(gp, gdn_f, gdn_nw, l)
        xp = _mix_ffn(gp, xp, mod_p, (o_ret, o_gla, o_hg, o_gdn), w_out_b, wi2, wo2, ln_g, ln_b, l, alpha)
        p_states.append((p_ret, p_gla, p_hg, p_gdn, p_conv))

        xs = _ffn(gs, xs, mod_s, wi1, wo1, ln_g, ln_b, l, alpha)
        (rq, rk, rv, rgt, aq, ak, ag, av, agt, hq, hk, hgg, hv, hgt, dq, dk_, dv, dgt, dbg, conv_new) = _prep(
            gs, xs, mod_s, w_in_p, cos_s, sin_s, wg_p, gla_bg, hg_lb, gdn_conv, alog_e, dtb_e, dec_extra, l, nl)
        o_ret, s_ret = _gla_decode(rq, rk, ret_g, rv, rgt, st_ret, s_ret, HEAD_DIM, None, l, True)
        o_gla, s_gla = _gla_decode(aq, ak, ag, av, agt, st_gla, s_gla, GLA_DK, gla_nc, l, False)
        o_hg, s_hg = _gla_decode(hq, hk, hgg, hv, hgt, st_hg, s_hg, HEAD_DIM, hg_nc, l, False)
        o_gdn, s_gdn = _gdn_decode(dq, dk_, dv, dgt, dbg, st_gdn, s_gdn, gdn_nc, l)
        xs = _mix_ffn(gs, xs, mod_s, (o_ret, o_gla, o_hg, o_gdn), w_out_b, wi2, wo2, ln_g, ln_b, l, alpha)
        s_convs.append(conv_new)

    y_prompt = xp.reshape(bp, tp, d)
    y_sample = jnp.transpose(xs.reshape(ts, bs, d), (1, 0, 2))
    p_out = tuple(jnp.stack([st[i] for st in p_states]) for i in range(5))
    s_out = tuple(jnp.transpose(s, (0, 4, 1, 2, 3)) for s in (s_ret, s_gla, s_hg, s_gdn))
    return (y_prompt, y_sample) + p_out + s_out + (jnp.transpose(jnp.stack(s_convs), (0, 2, 1, 3)),)
```
